```python
import jax, jax.numpy as jnp
from jax import lax
import numpy as np

D_MODEL = 1024
BATCH = 8
SEQ = 4096
DEPTH = 4

CHUNK = 64
Q_BLOCK = 128
N_MIXERS = 3
EXPAND = 2
D_INNER = EXPAND * D_MODEL
N_HEADS = 16
HEAD_DIM = D_INNER // N_HEADS
MLA_Q_RANK = 256
MLA_KV_RANK = 128
MLA_NOPE_DIM = 128
MLA_ROPE_DIM = 64
MLA_V_DIM = D_INNER // N_HEADS
ROPE_BASE = 10000.0
FORGET_BIAS = 3.0
EPS = 1e-6
NEG = -1e30

kernel_name = 'hybrid_stickbreak_mla_forgetting_trunk'


def rmsnorm(x, g):
    x32 = x.astype(jnp.float32)
    y = x32 * lax.rsqrt(jnp.mean(x32 * x32, axis=-1, keepdims=True) + EPS)
    return (y * g.astype(jnp.float32)).astype(x.dtype)


def split_heads(t, n_heads):
    b, s, _ = t.shape
    return t.reshape(b, s, n_heads, -1).transpose(0, 2, 1, 3)


def merge_heads(t):
    b, h, s, d = t.shape
    return t.transpose(0, 2, 1, 3).reshape(b, s, h * d)


def sweep_query_blocks(block_fn, seq):
    outs = [block_fn(start, start + Q_BLOCK) for start in range(0, seq, Q_BLOCK)]
    return jnp.concatenate(outs, axis=2)


def rope(x, positions):
    r = x.shape[-1]
    inv_freq = ROPE_BASE ** (-jnp.arange(0, r, 2, dtype=jnp.float32) / r)
    ang = positions.astype(jnp.float32)[:, :, None, None] * inv_freq
    cos, sin = jnp.cos(ang), jnp.sin(ang)
    x32 = x.astype(jnp.float32)
    x1, x2 = x32[..., : r // 2], x32[..., r // 2:]
    return jnp.concatenate([x1 * cos - x2 * sin, x1 * sin + x2 * cos], axis=-1).astype(x.dtype)


def stick_breaking_mixer(h, w_in, w_out):
    b, s, _ = h.shape
    q, k, v, gate = jnp.split(h @ w_in, 4, axis=-1)
    q, k, v = split_heads(q, N_HEADS), split_heads(k, N_HEADS), split_heads(v, N_HEADS)
    scale = HEAD_DIM ** -0.5

    def block(start, end):
        z = jnp.einsum('bhqd,bhkd->bhqk', q[:, :, start:end], k[:, :, :end],
                       preferred_element_type=jnp.float32) * scale
        strict = jnp.arange(end)[None, :] < jnp.arange(start, end)[:, None]
        log_skip = jnp.where(strict, jax.nn.log_sigmoid(-z), 0.0)
        later = lax.cumsum(log_skip, axis=3, reverse=True) - log_skip
        w = jnp.where(strict, jnp.exp(jax.nn.log_sigmoid(z) + later), 0.0)
        return jnp.einsum('bhqk,bhkd->bhqd', w.astype(v.dtype), v[:, :, :end])

    o = merge_heads(sweep_query_blocks(block, s))
    return (o * jax.nn.silu(gate)) @ w_out


def mla_mixer(h, positions, w_in, q_norm, w_qb, kv_norm, w_kvb, w_out):
    b, s, _ = h.shape
    i1 = MLA_Q_RANK
    i2 = i1 + MLA_KV_RANK
    i3 = i2 + MLA_ROPE_DIM
    proj = h @ w_in
    q_lat, kv_lat, k_rope, gate = proj[..., :i1], proj[..., i1:i2], proj[..., i2:i3], proj[..., i3:]
    q = (rmsnorm(q_lat, q_norm) @ w_qb).reshape(b, s, N_HEADS, MLA_NOPE_DIM + MLA_ROPE_DIM)
    q_nope = q[..., :MLA_NOPE_DIM].transpose(0, 2, 1, 3)
    q_rope = rope(q[..., MLA_NOPE_DIM:], positions).transpose(0, 2, 1, 3)
    kv = (rmsnorm(kv_lat, kv_norm) @ w_kvb).reshape(b, s, N_HEADS, MLA_NOPE_DIM + MLA_V_DIM)
    k_nope = kv[..., :MLA_NOPE_DIM].transpose(0, 2, 1, 3)
    v = kv[..., MLA_NOPE_DIM:].transpose(0, 2, 1, 3)
    k_rope = rope(k_rope[:, :, None, :], positions)[:, :, 0, :]
    scale = (MLA_NOPE_DIM + MLA_ROPE_DIM) ** -0.5

    def block(start, end):
        z = (jnp.einsum('bhqd,bhkd->bhqk', q_nope[:, :, start:end], k_nope[:, :, :end],
                        preferred_element_type=jnp.float32)
             + jnp.einsum('bhqr,bkr->bhqk', q_rope[:, :, start:end], k_rope[:, :end],
                          preferred_element_type=jnp.float32)) * scale
        allowed = (jnp.arange(end)[None, :] // CHUNK) <= (jnp.arange(start, end)[:, None] // CHUNK)
        p = jax.nn.softmax(jnp.where(allowed, z, NEG), axis=-1)
        return jnp.einsum('bhqk,bhkd->bhqd', p.astype(v.dtype), v[:, :, :end])

    o = merge_heads(sweep_query_blocks(block, s))
    return (o * jax.nn.silu(gate)) @ w_out


def forgetting_mixer(h, w_in, b_f, w_out):
    b, s, _ = h.shape
    proj = h @ w_in
    q = split_heads(proj[..., :D_INNER], N_HEADS)
    k = split_heads(proj[..., D_INNER:2 * D_INNER], N_HEADS)
    v = split_heads(proj[..., 2 * D_INNER:3 * D_INNER], N_HEADS)
    gate = proj[..., 3 * D_INNER:4 * D_INNER]
    f_logit = proj[..., 4 * D_INNER:].astype(jnp.float32) + b_f.astype(jnp.float32)
    cum_log_f = lax.cumsum(jax.nn.log_sigmoid(f_logit), axis=1).transpose(0, 2, 1)
    scale = HEAD_DIM ** -0.5

    def block(start, end):
        z = jnp.einsum('bhqd,bhkd->bhqk', q[:, :, start:end], k[:, :, :end],
                       preferred_element_type=jnp.float32) * scale
        z = z + cum_log_f[:, :, start:end, None] - cum_log_f[:, :, None, :end]
        causal = jnp.arange(end)[None, :] <= jnp.arange(start, end)[:, None]
        p = jax.nn.softmax(jnp.where(causal, z, NEG), axis=-1)
        return jnp.einsum('bhqk,bhkd->bhqd', p.astype(v.dtype), v[:, :, :end])

    o = merge_heads(sweep_query_blocks(block, s))
    return (o * jax.nn.silu(gate)) @ w_out


def _fwd_setup_inputs(seed: int = 0) -> dict:
    key = jax.random.key(seed)
    ks = iter(jax.random.split(key, 32))

    def w(shape):
        return jax.random.normal(next(ks), shape, jnp.float32) * shape[0] ** -0.5

    def gain(n):
        return 1.0 + 0.02 * jax.random.normal(next(ks), (n,), jnp.float32)

    x = jax.random.normal(next(ks), (BATCH, SEQ, D_MODEL), jnp.float32)
    offsets = jax.random.randint(next(ks), (BATCH,), 0, 64) * CHUNK
    positions = (offsets[:, None] + jnp.arange(SEQ)[None, :]).astype(jnp.int32)
    mla_in = MLA_Q_RANK + MLA_KV_RANK + MLA_ROPE_DIM + D_INNER
    return {
        'x': x,
        'positions': positions,
        'ln0': gain(D_MODEL),
        'w_in0': w((D_MODEL, 4 * D_INNER)),
        'w_out0': w((D_INNER, D_MODEL)),
        'ln1': gain(D_MODEL),
        'w_in1': w((D_MODEL, mla_in)),
        'q_norm1': gain(MLA_Q_RANK),
        'w_qb1': w((MLA_Q_RANK, N_HEADS * (MLA_NOPE_DIM + MLA_ROPE_DIM))),
        'kv_norm1': gain(MLA_KV_RANK),
        'w_kvb1': w((MLA_KV_RANK, N_HEADS * (MLA_NOPE_DIM + MLA_V_DIM))),
        'w_out1': w((D_INNER, D_MODEL)),
        'ln2': gain(D_MODEL),
        'w_in2': w((D_MODEL, 4 * D_INNER + N_HEADS)),
        'b_f2': FORGET_BIAS + 0.1 * jax.random.normal(next(ks), (N_HEADS,), jnp.float32),
        'w_out2': w((D_INNER, D_MODEL)),
        'ln3': gain(D_MODEL),
        'w_in3': w((D_MODEL, 4 * D_INNER)),
        'w_out3': w((D_INNER, D_MODEL)),
        'final_norm': gain(D_MODEL),
    }


def _fwd_reference(x, positions, ln0, w_in0, w_out0, ln1, w_in1, q_norm1, w_qb1, kv_norm1, w_kvb1,
              w_out1, ln2, w_in2, b_f2, w_out2, ln3, w_in3, w_out3, final_norm):
    layer_params = [
        (ln0, (w_in0, w_out0)),
        (ln1, (w_in1, q_norm1, w_qb1, kv_norm1, w_kvb1, w_out1)),
        (ln2, (w_in2, b_f2, w_out2)),
        (ln3, (w_in3, w_out3)),
    ]
    for i in range(DEPTH):
        ln, p = layer_params[i]
        h = rmsnorm(x, ln)
        kind = i % N_MIXERS
        if kind == 0:
            y = stick_breaking_mixer(h, *p)
        elif kind == 1:
            y = mla_mixer(h, positions, *p)
        else:
            y = forgetting_mixer(h, *p)
        x = x + y
    return rmsnorm(x, final_norm)


import jax as _jax
import jax.numpy as _jnp

TWIN_FORMAT = 'train_step'
FWD_PARAMS = ['x', 'positions', 'ln0', 'w_in0', 'w_out0', 'ln1', 'w_in1', 'q_norm1', 'w_qb1', 'kv_norm1', 'w_kvb1', 'w_out1', 'ln2', 'w_in2', 'b_f2', 'w_out2', 'ln3', 'w_in3', 'w_out3', 'final_norm']
TWIN_WEIGHTS = ['ln0', 'w_in0', 'w_out0', 'ln1', 'w_in1', 'q_norm1', 'w_qb1', 'kv_norm1', 'w_kvb1', 'w_out1', 'ln2', 'w_in2', 'b_f2', 'w_out2', 'ln3', 'w_in3', 'w_out3', 'final_norm']
TWIN_DIFF_INPUT = 'x'
TWIN_INPUTS = ['x', 'positions', 'ln0', 'w_in0', 'w_out0', 'ln1', 'w_in1', 'q_norm1', 'w_qb1', 'kv_norm1', 'w_kvb1', 'w_out1', 'ln2', 'w_in2', 'b_f2', 'w_out2', 'ln3', 'w_in3', 'w_out3', 'final_norm', 'loss_target', 'm_ln0', 'm_w_in0', 'm_w_out0', 'm_ln1', 'm_w_in1', 'm_q_norm1', 'm_w_qb1', 'm_kv_norm1', 'm_w_kvb1', 'm_w_out1', 'm_ln2', 'm_w_in2', 'm_b_f2', 'm_w_out2', 'm_ln3', 'm_w_in3', 'm_w_out3', 'm_final_norm', 'v_ln0', 'v_w_in0', 'v_w_out0', 'v_ln1', 'v_w_in1', 'v_q_norm1', 'v_w_qb1', 'v_kv_norm1', 'v_w_kvb1', 'v_w_out1', 'v_ln2', 'v_w_in2', 'v_b_f2', 'v_w_out2', 'v_ln3', 'v_w_in3', 'v_w_out3', 'v_final_norm']
TWIN_OUTPUTS = ['loss', 'grad_x', 'grad_ln0', 'grad_w_in0', 'grad_w_out0', 'grad_ln1', 'grad_w_in1', 'grad_q_norm1', 'grad_w_qb1', 'grad_kv_norm1', 'grad_w_kvb1', 'grad_w_out1', 'grad_ln2', 'grad_w_in2', 'grad_b_f2', 'grad_w_out2', 'grad_ln3', 'grad_w_in3', 'grad_w_out3', 'grad_final_norm', 'delta_ln0', 'delta_w_in0', 'delta_w_out0', 'delta_ln1', 'delta_w_in1', 'delta_q_norm1', 'delta_w_qb1', 'delta_kv_norm1', 'delta_w_kvb1', 'delta_w_out1', 'delta_ln2', 'delta_w_in2', 'delta_b_f2', 'delta_w_out2', 'delta_ln3', 'delta_w_in3', 'delta_w_out3', 'delta_final_norm', 'new_m_ln0', 'new_m_w_in0', 'new_m_w_out0', 'new_m_ln1', 'new_m_w_in1', 'new_m_q_norm1', 'new_m_w_qb1', 'new_m_kv_norm1', 'new_m_w_kvb1', 'new_m_w_out1', 'new_m_ln2', 'new_m_w_in2', 'new_m_b_f2', 'new_m_w_out2', 'new_m_ln3', 'new_m_w_in3', 'new_m_w_out3', 'new_m_final_norm', 'new_v_ln0', 'new_v_w_in0', 'new_v_w_out0', 'new_v_ln1', 'new_v_w_in1', 'new_v_q_norm1', 'new_v_w_qb1', 'new_v_kv_norm1', 'new_v_w_kvb1', 'new_v_w_out1', 'new_v_ln2', 'new_v_w_in2', 'new_v_b_f2', 'new_v_w_out2', 'new_v_ln3', 'new_v_w_in3', 'new_v_w_out3', 'new_v_final_norm']
TWIN_LEAF_KINDS = {'loss': 'loss', 'grad_x': 'grad_x', 'grad_ln0': 'grad_w', 'grad_w_in0': 'grad_w', 'grad_w_out0': 'grad_w', 'grad_ln1': 'grad_w', 'grad_w_in1': 'grad_w', 'grad_q_norm1': 'grad_w', 'grad_w_qb1': 'grad_w', 'grad_kv_norm1': 'grad_w', 'grad_w_kvb1': 'grad_w', 'grad_w_out1': 'grad_w', 'grad_ln2': 'grad_w', 'grad_w_in2': 'grad_w', 'grad_b_f2': 'grad_w', 'grad_w_out2': 'grad_w', 'grad_ln3': 'grad_w', 'grad_w_in3': 'grad_w', 'grad_w_out3': 'grad_w', 'grad_final_norm': 'grad_w', 'delta_ln0': 'delta_w', 'delta_w_in0': 'delta_w', 'delta_w_out0': 'delta_w', 'delta_ln1': 'delta_w', 'delta_w_in1': 'delta_w', 'delta_q_norm1': 'delta_w', 'delta_w_qb1': 'delta_w', 'delta_kv_norm1': 'delta_w', 'delta_w_kvb1': 'delta_w', 'delta_w_out1': 'delta_w', 'delta_ln2': 'delta_w', 'delta_w_in2': 'delta_w', 'delta_b_f2': 'delta_w', 'delta_w_out2': 'delta_w', 'delta_ln3': 'delta_w', 'delta_w_in3': 'delta_w', 'delta_w_out3': 'delta_w', 'delta_final_norm': 'delta_w', 'new_m_ln0': 'new_m', 'new_m_w_in0': 'new_m', 'new_m_w_out0': 'new_m', 'new_m_ln1': 'new_m', 'new_m_w_in1': 'new_m', 'new_m_q_norm1': 'new_m', 'new_m_w_qb1': 'new_m', 'new_m_kv_norm1': 'new_m', 'new_m_w_kvb1': 'new_m', 'new_m_w_out1': 'new_m', 'new_m_ln2': 'new_m', 'new_m_w_in2': 'new_m', 'new_m_b_f2': 'new_m', 'new_m_w_out2': 'new_m', 'new_m_ln3': 'new_m', 'new_m_w_in3': 'new_m', 'new_m_w_out3': 'new_m', 'new_m_final_norm': 'new_m', 'new_v_ln0': 'new_v', 'new_v_w_in0': 'new_v', 'new_v_w_out0': 'new_v', 'new_v_ln1': 'new_v', 'new_v_w_in1': 'new_v', 'new_v_q_norm1': 'new_v', 'new_v_w_qb1': 'new_v', 'new_v_kv_norm1': 'new_v', 'new_v_w_kvb1': 'new_v', 'new_v_w_out1': 'new_v', 'new_v_ln2': 'new_v', 'new_v_w_in2': 'new_v', 'new_v_b_f2': 'new_v', 'new_v_w_out2': 'new_v', 'new_v_ln3': 'new_v', 'new_v_w_in3': 'new_v', 'new_v_w_out3': 'new_v', 'new_v_final_norm': 'new_v'}


def _forward(args):
    return _fwd_reference(*[args[k] for k in FWD_PARAMS])


def _output_shape():
    def fwd():
        inp = _fwd_setup_inputs(0)
        return _fwd_reference(*[inp[k] for k in FWD_PARAMS])
    out = _jax.eval_shape(fwd)
    return out.shape, out.dtype

N_MICROBATCH = 1
ADAM_LR = 0.001
ADAM_B1 = 0.9
ADAM_B2 = 0.999
ADAM_EPS = 1e-08
ADAM_WD = 0.01
ADAM_STEP = 10
PER_EXAMPLE_BATCH_AXIS = {'x': 0, 'positions': 0, 'loss_target': 0}
SHARED_INPUTS = []
_WEIGHT_DTYPES = {'ln0': _jnp.float32, 'w_in0': _jnp.float32, 'w_out0': _jnp.float32, 'ln1': _jnp.float32, 'w_in1': _jnp.float32, 'q_norm1': _jnp.float32, 'w_qb1': _jnp.float32, 'kv_norm1': _jnp.float32, 'w_kvb1': _jnp.float32, 'w_out1': _jnp.float32, 'ln2': _jnp.float32, 'w_in2': _jnp.float32, 'b_f2': _jnp.float32, 'w_out2': _jnp.float32, 'ln3': _jnp.float32, 'w_in3': _jnp.float32, 'w_out3': _jnp.float32, 'final_norm': _jnp.float32}
MOMENT_SCALE = {'ln0': 1.124244e-01, 'w_in0': 3.988547e-02, 'w_out0': 7.139040e-02, 'ln1': 3.297059e-02, 'w_in1': 2.147721e-02, 'q_norm1': 3.262996e-02, 'w_qb1': 9.798033e-03, 'kv_norm1': 7.365135e-02, 'w_kvb1': 1.156403e-02, 'w_out1': 1.802098e-02, 'ln2': 6.195427e-02, 'w_in2': 2.173053e-02, 'b_f2': 1.449285e-01, 'w_out2': 3.321212e-02, 'ln3': 9.707437e-02, 'w_in3': 3.322508e-02, 'w_out3': 5.950292e-02, 'final_norm': 3.199327e+01}


def _to_microbatches(a, axis):
    t = _jnp.moveaxis(a, axis, 0)
    t = t.reshape((N_MICROBATCH, t.shape[0] // N_MICROBATCH) + t.shape[1:])
    return _jnp.moveaxis(t, 1, axis + 1)


def setup_inputs(seed: int = 0) -> dict:
    inp = _fwd_setup_inputs(seed)
    key = _jax.random.fold_in(_jax.random.key(seed), 7919)
    shape, _ = _output_shape()
    out = dict(inp)
    out["loss_target"] = _jax.random.normal(_jax.random.fold_in(key, 0), shape, _jnp.float32)
    for i, name in enumerate(TWIN_WEIGHTS):
        w = inp[name].astype(_jnp.float32)
        if MOMENT_SCALE is None:
            s = _jnp.sqrt(_jnp.mean(_jnp.square(w)) + 1e-30)
        else:
            s = MOMENT_SCALE[name]
        km, kv = _jax.random.split(_jax.random.fold_in(key, i + 1))
        out[name] = w
        out["m_" + name] = s * _jax.random.normal(km, w.shape, _jnp.float32)
        out["v_" + name] = (s * s) * _jax.random.uniform(kv, w.shape, _jnp.float32, 0.5, 1.5)
    if N_MICROBATCH > 1:
        for name, axis in PER_EXAMPLE_BATCH_AXIS.items():
            out[name] = _to_microbatches(out[name], axis)
    return {'x': out['x'], 'positions': out['positions'], 'ln0': out['ln0'], 'w_in0': out['w_in0'], 'w_out0': out['w_out0'], 'ln1': out['ln1'], 'w_in1': out['w_in1'], 'q_norm1': out['q_norm1'], 'w_qb1': out['w_qb1'], 'kv_norm1': out['kv_norm1'], 'w_kvb1': out['w_kvb1'], 'w_out1': out['w_out1'], 'ln2': out['ln2'], 'w_in2': out['w_in2'], 'b_f2': out['b_f2'], 'w_out2': out['w_out2'], 'ln3': out['ln3'], 'w_in3': out['w_in3'], 'w_out3': out['w_out3'], 'final_norm': out['final_norm'], 'loss_target': out['loss_target'], 'm_ln0': out['m_ln0'], 'm_w_in0': out['m_w_in0'], 'm_w_out0': out['m_w_out0'], 'm_ln1': out['m_ln1'], 'm_w_in1': out['m_w_in1'], 'm_q_norm1': out['m_q_norm1'], 'm_w_qb1': out['m_w_qb1'], 'm_kv_norm1': out['m_kv_norm1'], 'm_w_kvb1': out['m_w_kvb1'], 'm_w_out1': out['m_w_out1'], 'm_ln2': out['m_ln2'], 'm_w_in2': out['m_w_in2'], 'm_b_f2': out['m_b_f2'], 'm_w_out2': out['m_w_out2'], 'm_ln3': out['m_ln3'], 'm_w_in3': out['m_w_in3'], 'm_w_out3': out['m_w_out3'], 'm_final_norm': out['m_final_norm'], 'v_ln0': out['v_ln0'], 'v_w_in0': out['v_w_in0'], 'v_w_out0': out['v_w_out0'], 'v_ln1': out['v_ln1'], 'v_w_in1': out['v_w_in1'], 'v_q_norm1': out['v_q_norm1'], 'v_w_qb1': out['v_w_qb1'], 'v_kv_norm1': out['v_kv_norm1'], 'v_w_kvb1': out['v_w_kvb1'], 'v_w_out1': out['v_w_out1'], 'v_ln2': out['v_ln2'], 'v_w_in2': out['v_w_in2'], 'v_b_f2': out['v_b_f2'], 'v_w_out2': out['v_w_out2'], 'v_ln3': out['v_ln3'], 'v_w_in3': out['v_w_in3'], 'v_w_out3': out['v_w_out3'], 'v_final_norm': out['v_final_norm']}


def _loss(weights, diff, rest, loss_target):
    with _jax.named_scope("forward"):
        args = {**rest, TWIN_DIFF_INPUT: diff, **{k: w.astype(_WEIGHT_DTYPES[k]) for k, w in weights.items()}}
        y = _forward(args)
    with _jax.named_scope("loss_head"):
        err = _jnp.square(y.astype(_jnp.float32) - loss_target)
        return 0.5 * _jnp.sum(_jnp.mean(err, axis=-1)) if err.ndim else 0.5 * err


def _adamw(w, g, m, v):
    m = ADAM_B1 * m + (1.0 - ADAM_B1) * g
    v = ADAM_B2 * v + (1.0 - ADAM_B2) * _jnp.square(g)
    m_hat = m / (1.0 - ADAM_B1 ** ADAM_STEP)
    v_hat = v / (1.0 - ADAM_B2 ** ADAM_STEP)
    delta = -ADAM_LR * (m_hat / (_jnp.sqrt(v_hat) + ADAM_EPS) + ADAM_WD * w)
    return delta, m, v


def reference(x, positions, ln0, w_in0, w_out0, ln1, w_in1, q_norm1, w_qb1, kv_norm1, w_kvb1, w_out1, ln2, w_in2, b_f2, w_out2, ln3, w_in3, w_out3, final_norm, loss_target, m_ln0, m_w_in0, m_w_out0, m_ln1, m_w_in1, m_q_norm1, m_w_qb1, m_kv_norm1, m_w_kvb1, m_w_out1, m_ln2, m_w_in2, m_b_f2, m_w_out2, m_ln3, m_w_in3, m_w_out3, m_final_norm, v_ln0, v_w_in0, v_w_out0, v_ln1, v_w_in1, v_q_norm1, v_w_qb1, v_kv_norm1, v_w_kvb1, v_w_out1, v_ln2, v_w_in2, v_b_f2, v_w_out2, v_ln3, v_w_in3, v_w_out3, v_final_norm):
    given = dict(x=x, positions=positions, ln0=ln0, w_in0=w_in0, w_out0=w_out0, ln1=ln1, w_in1=w_in1, q_norm1=q_norm1, w_qb1=w_qb1, kv_norm1=kv_norm1, w_kvb1=w_kvb1, w_out1=w_out1, ln2=ln2, w_in2=w_in2, b_f2=b_f2, w_out2=w_out2, ln3=ln3, w_in3=w_in3, w_out3=w_out3, final_norm=final_norm, loss_target=loss_target, m_ln0=m_ln0, m_w_in0=m_w_in0, m_w_out0=m_w_out0, m_ln1=m_ln1, m_w_in1=m_w_in1, m_q_norm1=m_q_norm1, m_w_qb1=m_w_qb1, m_kv_norm1=m_kv_norm1, m_w_kvb1=m_w_kvb1, m_w_out1=m_w_out1, m_ln2=m_ln2, m_w_in2=m_w_in2, m_b_f2=m_b_f2, m_w_out2=m_w_out2, m_ln3=m_ln3, m_w_in3=m_w_in3, m_w_out3=m_w_out3, m_final_norm=m_final_norm, v_ln0=v_ln0, v_w_in0=v_w_in0, v_w_out0=v_w_out0, v_ln1=v_ln1, v_w_in1=v_w_in1, v_q_norm1=v_q_norm1, v_w_qb1=v_w_qb1, v_kv_norm1=v_kv_norm1, v_w_kvb1=v_w_kvb1, v_w_out1=v_w_out1, v_ln2=v_ln2, v_w_in2=v_w_in2, v_b_f2=v_b_f2, v_w_out2=v_w_out2, v_ln3=v_ln3, v_w_in3=v_w_in3, v_w_out3=v_w_out3, v_final_norm=v_final_norm)
    weights = {n: given[n] for n in TWIN_WEIGHTS}
    shared = {n: given[n] for n in SHARED_INPUTS}
    per_example = {n: given[n] for n in ['x', 'positions']}
    grad_fn = _jax.value_and_grad(_loss, argnums=(0, 1))

    def one_microbatch(ex, loss_target):
        ex = dict(ex)
        diff = ex.pop(TWIN_DIFF_INPUT)
        return grad_fn(weights, diff, {**shared, **ex}, loss_target)

    if N_MICROBATCH == 1:
        loss, (grad_w, grad_x) = one_microbatch(per_example, given["loss_target"])
    else:
        def body(carry, xs):
            loss_sum, grad_sum = carry
            l_k, (gw_k, gx_k) = one_microbatch(xs[0], xs[1])
            with _jax.named_scope("update"):
                return (loss_sum + l_k, _jax.tree.map(_jnp.add, grad_sum, gw_k)), gx_k

        init = (_jnp.zeros((), _jnp.float32), _jax.tree.map(_jnp.zeros_like, weights))
        (loss, grad_w), grad_x = _jax.lax.scan(body, init, (per_example, given["loss_target"]))
    with _jax.named_scope("update"):
        delta_w, new_m, new_v = {}, {}, {}
        for n in TWIN_WEIGHTS:
            delta_w[n], new_m[n], new_v[n] = _adamw(weights[n], grad_w[n], given["m_" + n], given["v_" + n])
    return (loss, grad_x, *[grad_w[n] for n in TWIN_WEIGHTS], *[delta_w[n] for n in TWIN_WEIGHTS],
            *[new_m[n] for n in TWIN_WEIGHTS], *[new_v[n] for n in TWIN_WEIGHTS])
```

```python
import functools

import jax
import jax.numpy as jnp
from jax import lax
from jax.experimental import pallas as pl
from jax.experimental.pallas import tpu as pltpu

F32 = jnp.float32
BF16 = jnp.bfloat16
EPS = 1e-6
NEG = -1e30
HEAD_DIM = 128
CHUNK_SHIFT = 6
MLA_Q_RANK = 256
MLA_KV_RANK = 128
MLA_ROPE = 64
MLA_QK = 256
ROPE_BASE = 10000.0
ADAM_LR = 0.001
ADAM_B1 = 0.9
ADAM_B2 = 0.999
ADAM_EPS = 1e-08
ADAM_WD = 0.01
ADAM_STEP = 10
VMEM_LIMIT_BYTES = 56 * 2**20
LANES = 128
PACK_W = 1024
PACK_TR = 256
SMALL_SHAPE = (8, 768)
MESH = pl.DeviceIdType.MESH
N_CHIPS = 4


def _pick(n, cands):
    for c in cands:
        if n % c == 0:
            return c
    return n


def _cparams(sem):
    return pltpu.CompilerParams(dimension_semantics=sem, vmem_limit_bytes=VMEM_LIMIT_BYTES)


def _dot(a, b, dims):
    dn = {"nn": (((1,), (0,)), ((), ())), "nt": (((1,), (1,)), ((), ())), "tn": (((0,), (0,)), ((), ()))}[dims]
    return lax.dot_general(a, b, dn, preferred_element_type=F32)


def _matmul(a, b, dims, name, res=None):
    if dims == "nn":
        (m, k), (k2, n) = a.shape, b.shape
    elif dims == "nt":
        (m, k), (n, k2) = a.shape, b.shape
    else:
        (k, m), (k2, n) = a.shape, b.shape
    assert k == k2, (a.shape, b.shape, dims)
    tm = _pick(m, (1024, 512, 256, 128))
    tn = _pick(n, (1024, 640, 512, 384, 256, 128))
    tk = _pick(k, (1024, 640, 512, 256, 128))
    nk = k // tk

    def body(*refs):
        if res is None:
            a_ref, b_ref, o_ref = refs
            r_ref = None
        else:
            a_ref, b_ref, r_ref, o_ref = refs
        kk = pl.program_id(2)
        p = _dot(a_ref[...].astype(BF16), b_ref[...].astype(BF16), dims)

        @pl.when(kk == 0)
        def _():
            o_ref[...] = p if r_ref is None else p + r_ref[...]

        @pl.when(kk > 0)
        def _():
            o_ref[...] += p

    a_spec = pl.BlockSpec((tk, tm), lambda i, j, kk: (kk, i)) if dims == "tn" else pl.BlockSpec((tm, tk), lambda i, j, kk: (i, kk))
    b_spec = pl.BlockSpec((tn, tk), lambda i, j, kk: (j, kk)) if dims == "nt" else pl.BlockSpec((tk, tn), lambda i, j, kk: (kk, j))
    o_spec = pl.BlockSpec((tm, tn), lambda i, j, kk: (i, j))
    in_specs = [a_spec, b_spec] + ([] if res is None else [o_spec])
    args = (a, b) + (() if res is None else (res,))
    return pl.pallas_call(
        body,
        grid=(m // tm, n // tn, nk),
        in_specs=in_specs,
        out_specs=o_spec,
        out_shape=jax.ShapeDtypeStruct((m, n), F32),
        compiler_params=_cparams(("parallel", "parallel", "arbitrary")),
        name=name,
    )(*args)


def _rmsnorm_fwd(x, g, name, col_block=0):
    s = x.shape[0]
    w = g.shape[1]
    tr = _pick(s, (512, 256, 128))

    def body(x_ref, g_ref, h_ref):
        xv = x_ref[...]
        r = lax.rsqrt(jnp.mean(xv * xv, axis=-1, keepdims=True) + EPS)
        h_ref[...] = ((xv * r) * g_ref[...]).astype(BF16)

    return pl.pallas_call(
        body,
        grid=(s // tr,),
        in_specs=[pl.BlockSpec((tr, w), lambda i: (i, col_block)), pl.BlockSpec((1, w), lambda i: (0, 0))],
        out_specs=pl.BlockSpec((tr, w), lambda i: (i, 0)),
        out_shape=jax.ShapeDtypeStruct((s, w), BF16),
        compiler_params=_cparams(("parallel",)),
        name=name,
    )(x, g)


def _rmsnorm_bwd(x, g, dh, name, col_block=0, dres=None):
    s = x.shape[0]
    w = g.shape[1]
    tr = _pick(s, (512, 256, 128))

    def body(*refs):
        if dres is None:
            x_ref, g_ref, dh_ref, dx_ref, dxb_ref, dg_ref = refs
        else:
            x_ref, g_ref, dh_ref, dr_ref, dx_ref, dxb_ref, dg_ref = refs
        i = pl.program_id(0)
        xv = x_ref[...]
        r = lax.rsqrt(jnp.mean(xv * xv, axis=-1, keepdims=True) + EPS)
        xh = xv * r
        dhv = dh_ref[...]
        dyg = dhv * g_ref[...]
        dx = r * (dyg - xh * jnp.mean(dyg * xh, axis=-1, keepdims=True))
        if dres is not None:
            dx = dx + dr_ref[...]
        dx_ref[...] = dx
        dxb_ref[...] = dx.astype(BF16)
        part = jnp.sum(dhv * xh, axis=0, keepdims=True)

        @pl.when(i == 0)
        def _():
            dg_ref[...] = part

        @pl.when(i > 0)
        def _():
            dg_ref[...] += part

    row = pl.BlockSpec((tr, w), lambda i: (i, 0))
    in_specs = [pl.BlockSpec((tr, w), lambda i: (i, col_block)), pl.BlockSpec((1, w), lambda i: (0, 0)), row]
    args = [x, g, dh]
    if dres is not None:
        in_specs.append(row)
        args.append(dres)
    return pl.pallas_call(
        body,
        grid=(s // tr,),
        in_specs=in_specs,
        out_specs=[row, row, pl.BlockSpec((1, w), lambda i: (0, 0))],
        out_shape=[jax.ShapeDtypeStruct((s, w), F32), jax.ShapeDtypeStruct((s, w), BF16), jax.ShapeDtypeStruct((1, w), F32)],
        compiler_params=_cparams(("arbitrary",)),
        name=name,
    )(*args)


def _loss_head(x, g, target, name):
    s, d = x.shape
    tr = _pick(s, (512, 256, 128))

    def body(x_ref, g_ref, t_ref, dx_ref, dxb_ref, dg_ref, loss_ref):
        i = pl.program_id(0)
        xv = x_ref[...]
        gv = g_ref[...]
        r = lax.rsqrt(jnp.mean(xv * xv, axis=-1, keepdims=True) + EPS)
        xh = xv * r
        err = xh * gv - t_ref[...]
        lpart = 0.5 * jnp.sum(jnp.mean(err * err, axis=-1, keepdims=True), axis=0, keepdims=True)
        dy = err / d
        dyg = dy * gv
        dx = r * (dyg - xh * jnp.mean(dyg * xh, axis=-1, keepdims=True))
        dx_ref[...] = dx
        dxb_ref[...] = dx.astype(BF16)
        part = jnp.sum(dy * xh, axis=0, keepdims=True)
        lrow = jnp.broadcast_to(lpart, (1, LANES))

        @pl.when(i == 0)
        def _():
            dg_ref[...] = part
            loss_ref[...] = lrow

        @pl.when(i > 0)
        def _():
            dg_ref[...] += part
            loss_ref[...] += lrow

    row = pl.BlockSpec((tr, d), lambda i: (i, 0))
    vec = pl.BlockSpec((1, d), lambda i: (0, 0))
    return pl.pallas_call(
        body,
        grid=(s // tr,),
        in_specs=[row, vec, row],
        out_specs=[row, row, vec, pl.BlockSpec((1, LANES), lambda i: (0, 0))],
        out_shape=[
            jax.ShapeDtypeStruct((s, d), F32),
            jax.ShapeDtypeStruct((s, d), BF16),
            jax.ShapeDtypeStruct((1, d), F32),
            jax.ShapeDtypeStruct((1, LANES), F32),
        ],
        compiler_params=_cparams(("arbitrary",)),
        name=name,
    )(x, g, target)


def _sigmoid(x):
    return 1.0 / (1.0 + jnp.exp(-x))


def _gate_fwd(o, proj, gate_blk, name):
    s, di = o.shape
    tr = _pick(s, (256, 128))

    def body(o_ref, gate_ref, g_ref):
        gt = gate_ref[...]
        g_ref[...] = (o_ref[...] * (gt * _sigmoid(gt))).astype(BF16)

    row = pl.BlockSpec((tr, di), lambda i: (i, 0))
    return pl.pallas_call(
        body,
        grid=(s // tr,),
        in_specs=[row, pl.BlockSpec((tr, di), lambda i: (i, gate_blk))],
        out_specs=row,
        out_shape=jax.ShapeDtypeStruct((s, di), BF16),
        compiler_params=_cparams(("parallel",)),
        name=name,
    )(o, proj)


def _gate_bwd(dg, o, proj, gate_blk, name):
    s, di = o.shape
    tr = _pick(s, (256, 128))

    def body(dg_ref, o_ref, gate_ref, do_ref, dgate_ref):
        gt = gate_ref[...]
        sg = _sigmoid(gt)
        dgv = dg_ref[...]
        do_ref[...] = dgv * (gt * sg)
        dgate_ref[...] = (dgv * o_ref[...] * (sg * (1.0 + gt * (1.0 - sg)))).astype(BF16)

    row = pl.BlockSpec((tr, di), lambda i: (i, 0))
    return pl.pallas_call(
        body,
        grid=(s // tr,),
        in_specs=[row, row, pl.BlockSpec((tr, di), lambda i: (i, gate_blk))],
        out_specs=[row, row],
        out_shape=[jax.ShapeDtypeStruct((s, di), F32), jax.ShapeDtypeStruct((s, di), BF16)],
        compiler_params=_cparams(("parallel",)),
        name=name,
    )(dg, o, proj)


def _iotas(tq, tk):
    return lax.broadcasted_iota(jnp.int32, (tq, tk), 0), lax.broadcasted_iota(jnp.int32, (tq, tk), 1)


def _softplus(s):
    return jnp.maximum(s, 0.0) + jnp.log1p(jnp.exp(-jnp.abs(s)))


def _split2(v):
    hi = v.astype(BF16)
    lo = (v - hi.astype(F32)).astype(BF16)
    return hi, lo


def _split3(v):
    a = v.astype(BF16)
    r1 = v - a.astype(F32)
    b = r1.astype(BF16)
    c = (r1 - b.astype(F32)).astype(BF16)
    return a, b, c


SB_T = 128


def _sb_fwd(proj, n_heads, name):
    s = proj.shape[0]
    d = HEAD_DIM
    t = min(SB_T, s)
    scale = d**-0.5

    def body(q_ref, k_ref, v_ref, o_ref, lt_ref, kb_ref, vb_ref):
        i = pl.program_id(1)

        @pl.when(i == 0)
        def _():
            kb_ref[...] = k_ref[...].astype(BF16)
            vb_ref[...] = v_ref[...].astype(BF16)

        q = q_ref[...].astype(BF16)
        rows, cols = _iotas(t, t)
        tri = (rows > cols).astype(BF16)

        def block(kb, cl, acc, masked):
            k0 = pl.multiple_of(kb * t, t)
            sc = _dot(q, kb_ref[pl.ds(k0, t), :], "nt") * scale
            sp = _softplus(sc)
            ls = -sp
            if masked:
                strict = cols < rows
                ls = jnp.where(strict, ls, 0.0)
            hi, lo = _split2(ls)
            later = _dot(hi, tri, "nn") + _dot(lo, tri, "nn")
            w = jnp.exp((sc - sp) + later + cl)
            if masked:
                w = jnp.where(strict, w, 0.0)
            acc = acc + _dot(w.astype(BF16), vb_ref[pl.ds(k0, t), :], "nn")
            return cl + jnp.sum(ls, axis=1, keepdims=True), acc

        cl, acc = block(i, jnp.zeros((t, 1), F32), jnp.zeros((t, d), F32), True)

        def loop(j, carry):
            return block(i - 1 - j, carry[0], carry[1], False)

        cl, acc = lax.fori_loop(0, i, loop, (cl, acc))
        o_ref[...] = acc
        lt_ref[...] = cl

    h = n_heads
    return pl.pallas_call(
        body,
        grid=(h, s // t),
        in_specs=[
            pl.BlockSpec((t, d), lambda hh, i: (i, hh)),
            pl.BlockSpec((s, d), lambda hh, i: (0, h + hh)),
            pl.BlockSpec((s, d), lambda hh, i: (0, 2 * h + hh)),
        ],
        out_specs=[pl.BlockSpec((t, d), lambda hh, i: (i, hh)), pl.BlockSpec((None, t, 1), lambda hh, i: (hh, i, 0))],
        out_shape=[jax.ShapeDtypeStruct((s, h * d), F32), jax.ShapeDtypeStruct((h, s, 1), F32)],
        scratch_shapes=[pltpu.VMEM((s, d), BF16), pltpu.VMEM((s, d), BF16)],
        compiler_params=_cparams(("arbitrary", "arbitrary")),
        name=name,
    )(proj, proj, proj)


def _sb_bwd(proj, ltot, do, n_heads, name):
    s = proj.shape[0]
    d = HEAD_DIM
    t = min(SB_T, s)
    nq = s // t
    scale = d**-0.5

    def body(q_ref, k_ref, v_ref, lt_ref, do_ref, dq_ref, dk_ref, dv_ref, kb_ref, vb_ref):
        i = pl.program_id(1)

        @pl.when(i == 0)
        def _():
            kb_ref[...] = k_ref[...].astype(BF16)
            vb_ref[...] = v_ref[...].astype(BF16)
            dk_ref[...] = jnp.zeros_like(dk_ref)
            dv_ref[...] = jnp.zeros_like(dv_ref)

        q = q_ref[...].astype(BF16)
        dob = do_ref[...].astype(BF16)
        ltv = lt_ref[...]
        rows, cols = _iotas(t, t)
        upto = (rows <= cols).astype(BF16)
        before = (rows < cols).astype(BF16)

        def block(kb, cp, cc, dq, masked):
            k0 = pl.multiple_of(kb * t, t)
            kk = kb_ref[pl.ds(k0, t), :]
            sc = _dot(q, kk, "nt") * scale
            sp = _softplus(sc)
            ls = -sp
            if masked:
                strict = cols < rows
                ls = jnp.where(strict, ls, 0.0)
            hi, lo = _split2(ls)
            prefix = _dot(hi, upto, "nn") + _dot(lo, upto, "nn") + cp
            lsig = sc - sp
            w = jnp.exp(lsig + (ltv - prefix))
            if masked:
                w = jnp.where(strict, w, 0.0)
            da = _dot(dob, vb_ref[pl.ds(k0, t), :], "nt") * w
            dhi, dlo = _split2(da)
            csum = _dot(dhi, before, "nn") + _dot(dlo, before, "nn") + cc
            beta = jnp.exp(lsig)
            dz = da * (1.0 - beta) - beta * csum
            if masked:
                dz = jnp.where(strict, dz, 0.0)
            dzb = dz.astype(BF16)
            dq = dq + _dot(dzb, kk, "nn")
            dk_ref[pl.ds(k0, t), :] += _dot(dzb, q, "tn")
            dv_ref[pl.ds(k0, t), :] += _dot(w.astype(BF16), dob, "tn")
            return cp + jnp.sum(ls, axis=1, keepdims=True), cc + jnp.sum(da, axis=1, keepdims=True), dq

        def loop(kb, carry):
            return block(kb, carry[0], carry[1], carry[2], False)

        z1 = jnp.zeros((t, 1), F32)
        cp, cc, dq = lax.fori_loop(0, i, loop, (z1, z1, jnp.zeros((t, d), F32)))
        cp, cc, dq = block(i, cp, cc, dq, True)
        dq_ref[...] = dq * scale

        @pl.when(i == nq - 1)
        def _():
            dk_ref[...] = dk_ref[...] * scale

    h = n_heads
    qblk = pl.BlockSpec((t, d), lambda hh, i: (i, hh))
    full = pl.BlockSpec((s, d), lambda hh, i: (0, hh))
    shp = jax.ShapeDtypeStruct((s, h * d), F32)
    return pl.pallas_call(
        body,
        grid=(h, nq),
        in_specs=[
            qblk,
            pl.BlockSpec((s, d), lambda hh, i: (0, h + hh)),
            pl.BlockSpec((s, d), lambda hh, i: (0, 2 * h + hh)),
            pl.BlockSpec((None, t, 1), lambda hh, i: (hh, i, 0)),
            qblk,
        ],
        out_specs=[qblk, full, full],
        out_shape=[shp, shp, shp],
        scratch_shapes=[pltpu.VMEM((s, d), BF16), pltpu.VMEM((s, d), BF16)],
        compiler_params=_cparams(("arbitrary", "arbitrary")),
        name=name,
    )(proj, proj, proj, ltot, do)


SM_T = 256


def _allowed(mode, rows, cols, q0, k0):
    r = rows + q0
    c = cols + k0
    if mode == "causal":
        return c <= r
    return (c >> CHUNK_SHIFT) <= (r >> CHUNK_SHIFT)


def _sm_fwd(q_arr, k_arr, v_arr, n_heads, dqk, q_blk, k_blk, v_blk, mode, name, crow=None, ccol=None):
    s = q_arr.shape[0]
    dv = HEAD_DIM
    t = min(SM_T, s)
    bias = crow is not None
    scale = (HEAD_DIM if mode == "causal" else HEAD_DIM + MLA_ROPE) ** -0.5

    def body(*refs):
        if bias:
            q_ref, k_ref, v_ref, cr_ref, cc_ref, o_ref, lse_ref, kb_ref, vb_ref = refs
        else:
            q_ref, k_ref, v_ref, o_ref, lse_ref, kb_ref, vb_ref = refs
        i = pl.program_id(1)

        @pl.when(i == 0)
        def _():
            kb_ref[...] = k_ref[...].astype(BF16)
            vb_ref[...] = v_ref[...].astype(BF16)

        q = q_ref[...].astype(BF16)
        q0 = i * t
        rows, cols = _iotas(t, t)
        crv = cr_ref[...] if bias else None

        def block(kb, m, l, acc, masked):
            k0 = pl.multiple_of(kb * t, t)
            sc = _dot(q, kb_ref[pl.ds(k0, t), :], "nt") * scale
            if bias:
                sc = sc + crv - cc_ref[pl.ds(kb, 1), :]
            if masked:
                sc = jnp.where(_allowed(mode, rows, cols, q0, k0), sc, NEG)
            m_new = jnp.maximum(m, jnp.max(sc, axis=1, keepdims=True))
            alpha = jnp.exp(m - m_new)
            p = jnp.exp(sc - m_new)
            l = alpha * l + jnp.sum(p, axis=1, keepdims=True)
            acc = alpha * acc + _dot(p.astype(BF16), vb_ref[pl.ds(k0, t), :], "nn")
            return m_new, l, acc

        def loop(kb, carry):
            return block(kb, carry[0], carry[1], carry[2], False)

        init = (jnp.full((t, 1), NEG, F32), jnp.zeros((t, 1), F32), jnp.zeros((t, dv), F32))
        m, l, acc = lax.fori_loop(0, i, loop, init)
        m, l, acc = block(i, m, l, acc, True)
        o_ref[...] = acc / l
        lse_ref[...] = m + jnp.log(l)

    h = n_heads
    in_specs = [
        pl.BlockSpec((t, dqk), lambda hh, i: (i, q_blk(hh))),
        pl.BlockSpec((s, dqk), lambda hh, i: (0, k_blk(hh))),
        pl.BlockSpec((s, dv), lambda hh, i: (0, v_blk(hh))),
    ]
    args = [q_arr, k_arr, v_arr]
    if bias:
        in_specs += [pl.BlockSpec((None, t, 1), lambda hh, i: (hh, i, 0)), pl.BlockSpec((None, s // t, t), lambda hh, i: (hh, 0, 0))]
        args += [crow, ccol]
    return pl.pallas_call(
        body,
        grid=(h, s // t),
        in_specs=in_specs,
        out_specs=[pl.BlockSpec((t, dv), lambda hh, i: (i, hh)), pl.BlockSpec((None, t, 1), lambda hh, i: (hh, i, 0))],
        out_shape=[jax.ShapeDtypeStruct((s, h * dv), F32), jax.ShapeDtypeStruct((h, s, 1), F32)],
        scratch_shapes=[pltpu.VMEM((s, dqk), BF16), pltpu.VMEM((s, dv), BF16)],
        compiler_params=_cparams(("arbitrary", "arbitrary")),
        name=name,
    )(*args)


def _sm_bwd(q_arr, k_arr, v_arr, o, do, lse, n_heads, dqk, q_blk, k_blk, v_blk, mode, name, crow=None, ccol=None):
    s = q_arr.shape[0]
    dv = HEAD_DIM
    t = min(SM_T, s)
    nq = s // t
    bias = crow is not None
    scale = (HEAD_DIM if mode == "causal" else HEAD_DIM + MLA_ROPE) ** -0.5

    def body(*refs):
        if bias:
            q_ref, k_ref, v_ref, o_ref, do_ref, lse_ref, cr_ref, cc_ref, dq_ref, dk_ref, dv_ref, dcc_ref, dcr_ref, kb_ref, vb_ref = refs
        else:
            q_ref, k_ref, v_ref, o_ref, do_ref, lse_ref, dq_ref, dk_ref, dv_ref, kb_ref, vb_ref = refs
        i = pl.program_id(1)

        @pl.when(i == 0)
        def _():
            kb_ref[...] = k_ref[...].astype(BF16)
            vb_ref[...] = v_ref[...].astype(BF16)
            dk_ref[...] = jnp.zeros_like(dk_ref)
            dv_ref[...] = jnp.zeros_like(dv_ref)
            if bias:
                dcc_ref[...] = jnp.zeros_like(dcc_ref)

        q = q_ref[...].astype(BF16)
        dov = do_ref[...]
        dob = dov.astype(BF16)
        dsum = jnp.sum(dov * o_ref[...], axis=1, keepdims=True)
        lsev = lse_ref[...]
        q0 = i * t
        rows, cols = _iotas(t, t)
        crv = cr_ref[...] if bias else None

        def block(kb, dq, dr, masked):
            k0 = pl.multiple_of(kb * t, t)
            kk = kb_ref[pl.ds(k0, t), :]
            sc = _dot(q, kk, "nt") * scale
            if bias:
                sc = sc + crv - cc_ref[pl.ds(kb, 1), :]
            if masked:
                sc = jnp.where(_allowed(mode, rows, cols, q0, k0), sc, NEG)
            p = jnp.exp(sc - lsev)
            dz = p * (_dot(dob, vb_ref[pl.ds(k0, t), :], "nt") - dsum)
            dzb = dz.astype(BF16)
            dk_ref[pl.ds(k0, t), :] += _dot(dzb, q, "tn")
            dv_ref[pl.ds(k0, t), :] += _dot(p.astype(BF16), dob, "tn")
            if bias:
                dcc_ref[pl.ds(kb, 1), :] -= jnp.sum(dz, axis=0, keepdims=True)
                dr = dr + jnp.sum(dz, axis=1, keepdims=True)
            return dq + _dot(dzb, kk, "nn"), dr

        def loop(kb, carry):
            return block(kb, carry[0], carry[1], False)

        dq, dr = lax.fori_loop(0, i, loop, (jnp.zeros((t, dqk), F32), jnp.zeros((t, 1), F32)))
        dq, dr = block(i, dq, dr, True)
        dq_ref[...] = dq * scale
        if bias:
            dcr_ref[...] = dr

        @pl.when(i == nq - 1)
        def _():
            dk_ref[...] = dk_ref[...] * scale

    h = n_heads
    qblk = pl.BlockSpec((t, dqk), lambda hh, i: (i, q_blk(hh)))
    oblk = pl.BlockSpec((t, dv), lambda hh, i: (i, hh))
    vec = pl.BlockSpec((None, t, 1), lambda hh, i: (hh, i, 0))
    in_specs = [
        qblk,
        pl.BlockSpec((s, dqk), lambda hh, i: (0, k_blk(hh))),
        pl.BlockSpec((s, dv), lambda hh, i: (0, v_blk(hh))),
        oblk,
        oblk,
        vec,
    ]
    args = [q_arr, k_arr, v_arr, o, do, lse]
    out_specs = [
        pl.BlockSpec((t, dqk), lambda hh, i: (i, hh)),
        pl.BlockSpec((s, dqk), lambda hh, i: (0, hh)),
        pl.BlockSpec((s, dv), lambda hh, i: (0, hh)),
    ]
    out_shape = [
        jax.ShapeDtypeStruct((s, h * dqk), F32),
        jax.ShapeDtypeStruct((s, h * dqk), F32),
        jax.ShapeDtypeStruct((s, h * dv), F32),
    ]
    if bias:
        ccs = pl.BlockSpec((None, nq, t), lambda hh, i: (hh, 0, 0))
        in_specs += [vec, ccs]
        args += [crow, ccol]
        out_specs += [ccs, vec]
        out_shape += [jax.ShapeDtypeStruct((h, nq, t), F32), jax.ShapeDtypeStruct((h, s, 1), F32)]
    return pl.pallas_call(
        body,
        grid=(h, nq),
        in_specs=in_specs,
        out_specs=out_specs,
        out_shape=out_shape,
        scratch_shapes=[pltpu.VMEM((s, dqk), BF16), pltpu.VMEM((s, dv), BF16)],
        compiler_params=_cparams(("arbitrary", "arbitrary")),
        name=name,
    )(*args)


def _rope_tables(pos):
    half = MLA_ROPE // 2
    inv_freq = ROPE_BASE ** (-jnp.arange(0, MLA_ROPE, 2, dtype=F32) / MLA_ROPE)
    ang = pos.astype(F32)[:, None] * inv_freq
    cos, sin = jnp.cos(ang), jnp.sin(ang)
    z = lambda n: jnp.zeros((pos.shape[0], n), F32)
    tc = jnp.concatenate([cos, cos, z(LANES - 2 * half)], axis=1)
    ta = jnp.concatenate([-sin, z(LANES - half)], axis=1)
    tb = jnp.concatenate([z(half), sin, z(LANES - 2 * half)], axis=1)
    return tc, ta, tb


def _rot(v, tc, ta, tb, sign):
    half = MLA_ROPE // 2
    return v * tc + sign * (pltpu.roll(v, LANES - half, 1) * ta + pltpu.roll(v, half, 1) * tb)


def _mla_assemble(qpre, kv, proj1, kr_blk, tabs, n_heads, name):
    s = qpre.shape[0]
    tr = _pick(s, (512, 256, 128))

    def body(qp_ref, kn_ref, kr_ref, tc_ref, ta_ref, tb_ref, qc_ref, kc_ref):
        tc, ta, tb = tc_ref[...], ta_ref[...], tb_ref[...]
        qc_ref[:, :LANES] = qp_ref[:, :LANES]
        qc_ref[:, LANES:] = _rot(qp_ref[:, LANES:], tc, ta, tb, 1.0)
        kc_ref[:, :LANES] = kn_ref[...]
        kc_ref[:, LANES:] = _rot(kr_ref[...], tc, ta, tb, 1.0)

    tab = pl.BlockSpec((tr, LANES), lambda i, hh: (i, 0))
    wide = pl.BlockSpec((tr, MLA_QK), lambda i, hh: (i, hh))
    shp = jax.ShapeDtypeStruct((s, n_heads * MLA_QK), F32)
    return pl.pallas_call(
        body,
        grid=(s // tr, n_heads),
        in_specs=[
            wide,
            pl.BlockSpec((tr, LANES), lambda i, hh: (i, 2 * hh)),
            pl.BlockSpec((tr, LANES), lambda i, hh: (i, kr_blk)),
            tab,
            tab,
            tab,
        ],
        out_specs=[wide, wide],
        out_shape=[shp, shp],
        compiler_params=_cparams(("parallel", "parallel")),
        name=name,
    )(qpre, kv, proj1, *tabs)


def _mla_disassemble(dqcat, dkcat, dv, tabs, n_heads, name):
    s = dqcat.shape[0]
    tr = _pick(s, (512, 256, 128))

    def body(dq_ref, dk_ref, dv_ref, tc_ref, ta_ref, tb_ref, dqp_ref, dkv_ref, dkr_ref):
        hh = pl.program_id(1)
        tc, ta, tb = tc_ref[...], ta_ref[...], tb_ref[...]
        dqp_ref[:, :LANES] = dq_ref[:, :LANES].astype(BF16)
        dqp_ref[:, LANES:] = _rot(dq_ref[:, LANES:], tc, ta, tb, -1.0).astype(BF16)
        dkv_ref[:, :LANES] = dk_ref[:, :LANES].astype(BF16)
        dkv_ref[:, LANES:] = dv_ref[...].astype(BF16)
        part = dk_ref[:, LANES:]

        @pl.when(hh == 0)
        def _():
            dkr_ref[...] = part

        @pl.when(hh > 0)
        def _():
            dkr_ref[...] += part

        @pl.when(hh == n_heads - 1)
        def _():
            dkr_ref[...] = _rot(dkr_ref[...], tc, ta, tb, -1.0)

    tab = pl.BlockSpec((tr, LANES), lambda i, hh: (i, 0))
    wide = pl.BlockSpec((tr, MLA_QK), lambda i, hh: (i, hh))
    shp = jax.ShapeDtypeStruct((s, n_heads * MLA_QK), BF16)
    return pl.pallas_call(
        body,
        grid=(s // tr, n_heads),
        in_specs=[wide, wide, pl.BlockSpec((tr, LANES), lambda i, hh: (i, hh)), tab, tab, tab],
        out_specs=[wide, wide, tab],
        out_shape=[shp, shp, jax.ShapeDtypeStruct((s, LANES), F32)],
        compiler_params=_cparams(("parallel", "arbitrary")),
        name=name,
    )(dqcat, dkcat, dv, *tabs)


def _forget_scan(proj2, f_blk, bias, name):
    s = proj2.shape[0]
    n = LANES

    def body(f_ref, b_ref, c_ref):
        rows, cols = _iotas(n, n)
        tri = (cols <= rows).astype(BF16)

        def step(j, carry):
            r0 = pl.multiple_of(j * n, n)
            f = f_ref[pl.ds(r0, n), :] + b_ref[...]
            lf = jnp.minimum(f, 0.0) - jnp.log1p(jnp.exp(-jnp.abs(f)))
            a, b, c = _split3(lf)
            cs = _dot(tri, a, "nn") + _dot(tri, b, "nn") + _dot(tri, c, "nn") + carry
            c_ref[pl.ds(r0, n), :] = cs
            return cs[n - 1 : n, :]

        lax.fori_loop(0, s // n, step, jnp.zeros((1, n), F32))

    return pl.pallas_call(
        body,
        grid=(1,),
        in_specs=[pl.BlockSpec((s, n), lambda i: (0, f_blk)), pl.BlockSpec((1, n), lambda i: (0, 0))],
        out_specs=pl.BlockSpec((s, n), lambda i: (0, 0)),
        out_shape=jax.ShapeDtypeStruct((s, n), F32),
        compiler_params=_cparams(("arbitrary",)),
        name=name,
    )(proj2, bias)


def _forget_scan_bwd(dc_col, dc_row, proj2, f_blk, bias, n_heads, name):
    s = proj2.shape[0]
    n = LANES
    nb = s // n

    def body(dcc_ref, dcr_ref, f_ref, b_ref, df_ref, db_ref):
        rows, cols = _iotas(n, n)
        tri = (cols >= rows).astype(BF16)
        live = cols < n_heads

        def step(j, carry):
            acc, dbv = carry
            r0 = pl.multiple_of((nb - 1 - j) * n, n)
            a, b, c = _split3(dcc_ref[pl.ds(r0, n), :] + dcr_ref[pl.ds(r0, n), :])
            dl = _dot(tri, a, "nn") + _dot(tri, b, "nn") + _dot(tri, c, "nn") + acc
            f = f_ref[pl.ds(r0, n), :] + b_ref[...]
            df = jnp.where(live, dl / (1.0 + jnp.exp(f)), 0.0)
            df_ref[pl.ds(r0, n), :] = df.astype(BF16)
            return dl[0:1, :], dbv + jnp.sum(df, axis=0, keepdims=True)

        z = jnp.zeros((1, n), F32)
        _, dbv = lax.fori_loop(0, nb, step, (z, z))
        db_ref[...] = dbv

    return pl.pallas_call(
        body,
        grid=(1,),
        in_specs=[
            pl.BlockSpec((s, n), lambda i: (0, 0)),
            pl.BlockSpec((s, n), lambda i: (0, 0)),
            pl.BlockSpec((s, n), lambda i: (0, f_blk)),
            pl.BlockSpec((1, n), lambda i: (0, 0)),
        ],
        out_specs=[pl.BlockSpec((s, n), lambda i: (0, 0)), pl.BlockSpec((1, n), lambda i: (0, 0))],
        out_shape=[jax.ShapeDtypeStruct((s, n), BF16), jax.ShapeDtypeStruct((1, n), F32)],
        compiler_params=_cparams(("arbitrary",)),
        name=name,
    )(dc_col, dc_row, proj2, bias)


def _adamw(w, g, m, v, name):
    r, c = w.shape
    tr = _pick(r, (128, 64, 32, 16, 8))
    c1 = 1.0 - ADAM_B1**ADAM_STEP
    c2 = 1.0 - ADAM_B2**ADAM_STEP

    def body(w_ref, g_ref, m_ref, v_ref, d_ref, mo_ref, vo_ref):
        gv = g_ref[...]
        mn = ADAM_B1 * m_ref[...] + (1.0 - ADAM_B1) * gv
        vn = ADAM_B2 * v_ref[...] + (1.0 - ADAM_B2) * (gv * gv)
        mo_ref[...] = mn
        vo_ref[...] = vn
        d_ref[...] = -ADAM_LR * ((mn / c1) / (jnp.sqrt(vn / c2) + ADAM_EPS) + ADAM_WD * w_ref[...])

    blk = pl.BlockSpec((tr, c), lambda i: (i, 0))
    shp = jax.ShapeDtypeStruct((r, c), F32)
    return pl.pallas_call(
        body,
        grid=(r // tr,),
        in_specs=[blk, blk, blk, blk],
        out_specs=[blk, blk, blk],
        out_shape=[shp, shp, shp],
        compiler_params=_cparams(("parallel",)),
        name=name,
    )(w, g, m, v)


def _mesh_pos():
    return lax.axis_index("x"), lax.axis_index("y"), lax.axis_index("c")


def _other_chips(x, y):
    return [(1 - x, y), (x, 1 - y), (1 - x, 1 - y)]


ANY = pl.BlockSpec(memory_space=pl.ANY)


def _gather_weights(wp):
    rp, wd = wp.shape
    half = rp // 2

    def body(w_ref, out_ref, send_sems, recv_sems, local_sem):
        x, y, c = _mesh_pos()
        me = 2 * x + y
        chips = _other_chips(x, y)

        def region(chip, hc):
            return out_ref.at[chip, pl.ds(hc * half, half), :]

        def copy(k, src, dst, to):
            return pltpu.make_async_remote_copy(
                src_ref=src, dst_ref=dst, send_sem=send_sems.at[k], recv_sem=recv_sems.at[k], device_id=to, device_id_type=MESH
            )

        mine = pltpu.make_async_copy(w_ref, out_ref.at[me], local_sem)
        mine.start()
        first = [copy(j, w_ref.at[pl.ds(c * half, half), :], region(me, c), (cx, cy, c)) for j, (cx, cy) in enumerate(chips)]
        for cp in first:
            cp.start()
        passed = [copy(3 + j, region(2 * cx + cy, c), region(2 * cx + cy, c), (x, y, 1 - c)) for j, (cx, cy) in enumerate(chips)]
        for j, (cx, cy) in enumerate(chips):
            copy(j, region(2 * cx + cy, c), region(2 * cx + cy, c), (x, y, c)).wait_recv()
            passed[j].start()
        for j, (cx, cy) in enumerate(chips):
            copy(3 + j, region(2 * cx + cy, 1 - c), region(2 * cx + cy, 1 - c), (x, y, c)).wait_recv()
        for cp in first + passed:
            cp.wait_send()
        mine.wait()

    return pl.pallas_call(
        body,
        in_specs=[ANY],
        out_specs=ANY,
        out_shape=jax.ShapeDtypeStruct((N_CHIPS, rp, wd), wp.dtype),
        scratch_shapes=[pltpu.SemaphoreType.DMA((6,)), pltpu.SemaphoreType.DMA((6,)), pltpu.SemaphoreType.DMA],
        name="gather_weights",
    )(wp)


def _pair_exchange(g):
    _, rp, wd = g.shape
    half = rp // 2

    def body(g_ref, out_ref, send_sem, recv_sem):
        x, y, c = _mesh_pos()
        cp = pltpu.make_async_remote_copy(
            src_ref=g_ref.at[:, pl.ds((1 - c) * half, half), :],
            dst_ref=out_ref,
            send_sem=send_sem,
            recv_sem=recv_sem,
            device_id=(x, y, 1 - c),
            device_id_type=MESH,
        )
        cp.start()
        cp.wait()

    return pl.pallas_call(
        body,
        in_specs=[ANY],
        out_specs=ANY,
        out_shape=jax.ShapeDtypeStruct((N_CHIPS, half, wd), g.dtype),
        scratch_shapes=[pltpu.SemaphoreType.DMA, pltpu.SemaphoreType.DMA],
        name="rs_pair_exchange",
    )(g)


def _pair_add(g, recv, c_idx):
    _, rp, wd = g.shape
    half = rp // 2
    nb = half // PACK_TR

    def body(c_ref, g_ref, r_ref, o_ref):
        o_ref[...] = g_ref[...] + r_ref[...]

    blk = (1, PACK_TR, wd)
    grid_spec = pltpu.PrefetchScalarGridSpec(
        num_scalar_prefetch=1,
        grid=(N_CHIPS, nb),
        in_specs=[pl.BlockSpec(blk, lambda j, i, cr: (j, cr[0] * nb + i, 0)), pl.BlockSpec(blk, lambda j, i, cr: (j, i, 0))],
        out_specs=pl.BlockSpec(blk, lambda j, i, cr: (j, i, 0)),
    )
    return pl.pallas_call(
        body,
        grid_spec=grid_spec,
        out_shape=jax.ShapeDtypeStruct((N_CHIPS, half, wd), F32),
        compiler_params=_cparams(("parallel", "parallel")),
        name="rs_pair_add",
    )(c_idx, g, recv)


def _chip_exchange(sp):
    _, rh, wd = sp.shape

    def body(s_ref, out_ref, send_sems, recv_sems, local_sem):
        x, y, c = _mesh_pos()
        me = 2 * x + y
        chips = _other_chips(x, y)
        mine = pltpu.make_async_copy(s_ref.at[me], out_ref.at[me], local_sem)
        mine.start()
        sends = []
        for j, (cx, cy) in enumerate(chips):
            cp = pltpu.make_async_remote_copy(
                src_ref=s_ref.at[2 * cx + cy],
                dst_ref=out_ref.at[me],
                send_sem=send_sems.at[j],
                recv_sem=recv_sems.at[j],
                device_id=(cx, cy, c),
                device_id_type=MESH,
            )
            cp.start()
            sends.append(cp)
        for j, (cx, cy) in enumerate(chips):
            pltpu.make_async_remote_copy(
                src_ref=s_ref.at[me],
                dst_ref=out_ref.at[2 * cx + cy],
                send_sem=send_sems.at[j],
                recv_sem=recv_sems.at[j],
                device_id=(x, y, c),
                device_id_type=MESH,
            ).wait_recv()
        for cp in sends:
            cp.wait_send()
        mine.wait()

    return pl.pallas_call(
        body,
        in_specs=[ANY],
        out_specs=ANY,
        out_shape=jax.ShapeDtypeStruct(sp.shape, sp.dtype),
        scratch_shapes=[pltpu.SemaphoreType.DMA((3,)), pltpu.SemaphoreType.DMA((3,)), pltpu.SemaphoreType.DMA],
        name="rs_chip_exchange",
    )(sp)


def _sum_slots(r):
    _, rh, wd = r.shape

    def body(r_ref, o_ref):
        o_ref[...] = ((r_ref[0] + r_ref[1]) + r_ref[2]) + r_ref[3]

    return pl.pallas_call(
        body,
        grid=(rh // PACK_TR,),
        in_specs=[pl.BlockSpec((N_CHIPS, PACK_TR, wd), lambda i: (0, i, 0))],
        out_specs=pl.BlockSpec((PACK_TR, wd), lambda i: (i, 0)),
        out_shape=jax.ShapeDtypeStruct((rh, wd), F32),
        compiler_params=_cparams(("parallel",)),
        name="rs_sum_slots",
    )(r)


def _pair_gather(tp):
    rh, wd = tp.shape

    def body(t_ref, out_ref, send_sem, recv_sem, local_sem):
        x, y, c = _mesh_pos()
        mine = pltpu.make_async_copy(t_ref, out_ref.at[pl.ds(c * rh, rh), :], local_sem)
        mine.start()
        cp = pltpu.make_async_remote_copy(
            src_ref=t_ref,
            dst_ref=out_ref.at[pl.ds(c * rh, rh), :],
            send_sem=send_sem,
            recv_sem=recv_sem,
            device_id=(x, y, 1 - c),
            device_id_type=MESH,
        )
        cp.start()
        cp.wait_send()
        pltpu.make_async_remote_copy(
            src_ref=t_ref,
            dst_ref=out_ref.at[pl.ds((1 - c) * rh, rh), :],
            send_sem=send_sem,
            recv_sem=recv_sem,
            device_id=(x, y, c),
            device_id_type=MESH,
        ).wait_recv()
        mine.wait()

    return pl.pallas_call(
        body,
        in_specs=[ANY],
        out_specs=ANY,
        out_shape=jax.ShapeDtypeStruct((2 * rh, wd), tp.dtype),
        scratch_shapes=[pltpu.SemaphoreType.DMA, pltpu.SemaphoreType.DMA, pltpu.SemaphoreType.DMA],
        name="rs_pair_gather",
    )(tp)


def _allreduce_small(v):
    shape = v.shape
    n_dev = 8

    def body(v_ref, o_ref, slots, send_sems, recv_sems):
        x, y, c = _mesh_pos()
        me = 4 * x + 2 * y + c
        slots[me] = v_ref[...]
        sends = []
        for k in range(1, n_dev):
            fx, fy, fc = (k >> 2) & 1, (k >> 1) & 1, k & 1
            to = (x ^ fx, y ^ fy, c ^ fc)
            cp = pltpu.make_async_remote_copy(
                src_ref=v_ref,
                dst_ref=slots.at[me],
                send_sem=send_sems.at[k - 1],
                recv_sem=recv_sems.at[k - 1],
                device_id=to,
                device_id_type=MESH,
            )
            cp.start()
            sends.append(cp)
        for k in range(1, n_dev):
            fx, fy, fc = (k >> 2) & 1, (k >> 1) & 1, k & 1
            frm = 4 * (x ^ fx) + 2 * (y ^ fy) + (c ^ fc)
            pltpu.make_async_remote_copy(
                src_ref=v_ref,
                dst_ref=slots.at[frm],
                send_sem=send_sems.at[k - 1],
                recv_sem=recv_sems.at[k - 1],
                device_id=(x, y, c),
                device_id_type=MESH,
            ).wait_recv()
        for cp in sends:
            cp.wait_send()
        acc = slots[0]
        for k in range(1, n_dev):
            acc = acc + slots[k]
        o_ref[...] = acc

    vm = pl.BlockSpec(memory_space=pltpu.VMEM)
    return pl.pallas_call(
        body,
        in_specs=[vm],
        out_specs=vm,
        out_shape=jax.ShapeDtypeStruct(shape, F32),
        scratch_shapes=[pltpu.VMEM((n_dev,) + shape, F32), pltpu.SemaphoreType.DMA((n_dev - 1,)), pltpu.SemaphoreType.DMA((n_dev - 1,))],
        name="allreduce_small",
    )(v)


def _pack_layout(shard_shapes):
    offs, rows = [], []
    off = 0
    for r, c in shard_shapes:
        assert (r * c) % PACK_W == 0
        n = r * c // PACK_W
        offs.append(off)
        rows.append(n)
        off += -(-n // 16) * 16
    rp = -(-off // (2 * PACK_TR)) * (2 * PACK_TR)
    return offs, rows, rp


def _pack_rows(parts, offs, rows, rp, lead):
    out, at = [], 0
    for p, o, n in zip(parts, offs, rows):
        if o > at:
            out.append(jnp.zeros(lead + (o - at, PACK_W), p.dtype))
        out.append(p)
        at = o + n
    if rp > at:
        out.append(jnp.zeros(lead + (rp - at, PACK_W), parts[0].dtype))
    return jnp.concatenate(out, axis=len(lead))


def kernel(x, positions, ln0, w_in0, w_out0, ln1, w_in1, q_norm1, w_qb1, kv_norm1, w_kvb1, w_out1, ln2, w_in2, b_f2, w_out2, ln3, w_in3, w_out3, final_norm, loss_target, m_ln0, m_w_in0, m_w_out0, m_ln1, m_w_in1, m_q_norm1, m_w_qb1, m_kv_norm1, m_w_kvb1, m_w_out1, m_ln2, m_w_in2, m_b_f2, m_w_out2, m_ln3, m_w_in3, m_w_out3, m_final_norm, v_ln0, v_w_in0, v_w_out0, v_ln1, v_w_in1, v_q_norm1, v_w_qb1, v_kv_norm1, v_w_kvb1, v_w_out1, v_ln2, v_w_in2, v_b_f2, v_w_out2, v_ln3, v_w_in3, v_w_out3, v_final_norm):
    xs = x[0]
    s, d = xs.shape
    di = 4 * w_out0.shape[0]
    nh = di // HEAD_DIM
    c_idx = lax.axis_index("c").astype(jnp.int32).reshape(1)

    big = [w_in0, w_out0, w_in1, w_qb1, w_kvb1, w_out1, w_in2, w_out2, w_in3, w_out3]
    col_sharded = [True, False, True, True, True, False, True, False, True, False]
    shard_shapes = [w.shape for w in big]
    offs, rows, rp = _pack_layout(shard_shapes)
    wp = _pack_rows([w.astype(BF16).reshape(n, PACK_W) for w, n in zip(big, rows)], offs, rows, rp, ())
    wall = _gather_weights(wp)
    full = []
    for (r, c), o, n, cs in zip(shard_shapes, offs, rows, col_sharded):
        slab = wall[:, o : o + n, :].reshape(N_CHIPS, r, c)
        full.append(slab.transpose(1, 0, 2).reshape(r, N_CHIPS * c) if cs else slab.reshape(N_CHIPS * r, c))
    f_in0, f_out0, f_in1, f_qb1, f_kvb1, f_out1, f_in2, f_out2, f_in3, f_out3 = full

    i_kr = MLA_Q_RANK + MLA_KV_RANK + MLA_ROPE
    w1p = jnp.concatenate([f_in1[:, i_kr:], f_in1[:, :i_kr], jnp.zeros((d, LANES - MLA_ROPE), BF16)], axis=1)
    qlat_blk = di // MLA_Q_RANK
    kvlat_blk = (di + MLA_Q_RANK) // MLA_KV_RANK
    kr_blk = (di + MLA_Q_RANK + MLA_KV_RANK) // LANES
    qk_w = HEAD_DIM + MLA_ROPE
    wqbp = jnp.pad(f_qb1.reshape(MLA_Q_RANK, nh, qk_w), ((0, 0), (0, 0), (0, MLA_QK - qk_w))).reshape(MLA_Q_RANK, nh * MLA_QK)
    n2 = f_in2.shape[1]
    w2p = jnp.pad(f_in2, ((0, 0), (0, 4 * di + LANES - n2)))
    b2p = jnp.pad(b_f2, (0, LANES - nh)).reshape(1, LANES)

    row = lambda v: v.reshape(1, -1)
    tabs = _rope_tables(positions[0])

    def sb_layer_fwd(xin, ln, w_in, w_out, tag):
        h = _rmsnorm_fwd(xin, row(ln), f"norm_fwd_{tag}")
        proj = _matmul(h, w_in, "nn", f"proj_in_{tag}")
        o, lt = _sb_fwd(proj, nh, f"sb_fwd_{tag}")
        g = _gate_fwd(o, proj, 3, f"gate_fwd_{tag}")
        xout = _matmul(g, w_out, "nn", f"proj_out_{tag}", res=xin)
        return xout, (xin, h, proj, o, lt, g)

    def sb_layer_bwd(dxn, dxnb, saved, ln, w_in, w_out, tag):
        xin, h, proj, o, lt, g = saved
        dgf = _matmul(dxnb, w_out, "nt", f"dgate_in_{tag}")
        dw_out = _matmul(g, dxnb, "tn", f"dw_out_{tag}")
        do, dgate = _gate_bwd(dgf, o, proj, 3, f"gate_bwd_{tag}")
        dq, dk, dv = _sb_bwd(proj, lt, do, nh, f"sb_bwd_{tag}")
        dproj = jnp.concatenate([dq.astype(BF16), dk.astype(BF16), dv.astype(BF16), dgate], axis=1)
        dh = _matmul(dproj, w_in, "nt", f"dh_{tag}")
        dw_in = _matmul(h, dproj, "tn", f"dw_in_{tag}")
        dx, dxb, dln = _rmsnorm_bwd(xin, row(ln), dh, f"norm_bwd_{tag}", dres=dxn)
        return dx, dxb, dln, dw_in, dw_out

    x1, sv0 = sb_layer_fwd(xs, ln0, f_in0, f_out0, "l0")

    h1 = _rmsnorm_fwd(x1, row(ln1), "norm_fwd_l1")
    proj1 = _matmul(h1, w1p, "nn", "proj_in_l1")
    qn = _rmsnorm_fwd(proj1, row(q_norm1), "qnorm_fwd_l1", col_block=qlat_blk)
    kvn = _rmsnorm_fwd(proj1, row(kv_norm1), "kvnorm_fwd_l1", col_block=kvlat_blk)
    qpre = _matmul(qn, wqbp, "nn", "q_up_l1")
    kv1 = _matmul(kvn, f_kvb1, "nn", "kv_up_l1")
    qcat, kcat = _mla_assemble(qpre, kv1, proj1, kr_blk, tabs, nh, "mla_assemble_l1")
    mla_blk = (lambda hh: hh, lambda hh: hh, lambda hh: 2 * hh + 1)
    o1, lse1 = _sm_fwd(qcat, kcat, kv1, nh, MLA_QK, *mla_blk, "chunk", "mla_fwd_l1")
    g1 = _gate_fwd(o1, proj1, 0, "gate_fwd_l1")
    x2 = _matmul(g1, f_out1, "nn", "proj_out_l1", res=x1)

    h2 = _rmsnorm_fwd(x2, row(ln2), "norm_fwd_l2")
    proj2 = _matmul(h2, w2p, "nn", "proj_in_l2")
    f_blk = 4 * di // LANES
    cum = _forget_scan(proj2, f_blk, b2p, "forget_scan_l2")
    cum_h = cum[:, :nh].T
    t_sm = min(SM_T, s)
    crow = cum_h.reshape(nh, s, 1)
    ccol = cum_h.reshape(nh, s // t_sm, t_sm)
    fg_blk = (lambda hh: hh, lambda hh: nh + hh, lambda hh: 2 * nh + hh)
    o2, lse2 = _sm_fwd(proj2, proj2, proj2, nh, HEAD_DIM, *fg_blk, "causal", "forget_fwd_l2", crow=crow, ccol=ccol)
    g2 = _gate_fwd(o2, proj2, 3, "gate_fwd_l2")
    x3 = _matmul(g2, f_out2, "nn", "proj_out_l2", res=x2)

    x4, sv3 = sb_layer_fwd(x3, ln3, f_in3, f_out3, "l3")

    dx, dxb, d_final, loss_part = _loss_head(x4, row(final_norm), loss_target[0], "loss_head")
    dx, dxb, d_ln3, dw_in3, dw_out3 = sb_layer_bwd(dx, dxb, sv3, ln3, f_in3, f_out3, "l3")

    dgf2 = _matmul(dxb, f_out2, "nt", "dgate_in_l2")
    dw_out2 = _matmul(g2, dxb, "tn", "dw_out_l2")
    do2, dgate2 = _gate_bwd(dgf2, o2, proj2, 3, "gate_bwd_l2")
    dq2, dk2, dv2, dcc2, dcr2 = _sm_bwd(proj2, proj2, proj2, o2, do2, lse2, nh, HEAD_DIM, *fg_blk, "causal", "forget_bwd_l2", crow=crow, ccol=ccol)
    lanes_of = lambda a: jnp.pad(a.reshape(nh, s).T, ((0, 0), (0, LANES - nh)))
    df2, d_bf = _forget_scan_bwd(lanes_of(dcc2), lanes_of(dcr2), proj2, f_blk, b2p, nh, "forget_scan_bwd_l2")
    dproj2 = jnp.concatenate([dq2.astype(BF16), dk2.astype(BF16), dv2.astype(BF16), dgate2, df2], axis=1)
    dh2 = _matmul(dproj2, w2p, "nt", "dh_l2")
    dw_in2 = _matmul(h2, dproj2, "tn", "dw_in_l2")[:, :n2]
    dx, dxb, d_ln2 = _rmsnorm_bwd(x2, row(ln2), dh2, "norm_bwd_l2", dres=dx)

    dgf1 = _matmul(dxb, f_out1, "nt", "dgate_in_l1")
    dw_out1 = _matmul(g1, dxb, "tn", "dw_out_l1")
    do1, dgate1 = _gate_bwd(dgf1, o1, proj1, 0, "gate_bwd_l1")
    dqc, dkc, dv1 = _sm_bwd(qcat, kcat, kv1, o1, do1, lse1, nh, MLA_QK, *mla_blk, "chunk", "mla_bwd_l1")
    dqpre, dkv1, dkr = _mla_disassemble(dqc, dkc, dv1, tabs, nh, "mla_disassemble_l1")
    dqn = _matmul(dqpre, wqbp, "nt", "dqn_l1")
    dw_qbp = _matmul(qn, dqpre, "tn", "dw_qb_l1")
    dw_qb1 = dw_qbp.reshape(MLA_Q_RANK, nh, MLA_QK)[:, :, :qk_w].reshape(MLA_Q_RANK, nh * qk_w)
    dkvn = _matmul(dkv1, f_kvb1, "nt", "dkvn_l1")
    dw_kvb1 = _matmul(kvn, dkv1, "tn", "dw_kvb_l1")
    _, dqlat_b, d_qnorm = _rmsnorm_bwd(proj1, row(q_norm1), dqn, "qnorm_bwd_l1", col_block=qlat_blk)
    _, dkvlat_b, d_kvnorm = _rmsnorm_bwd(proj1, row(kv_norm1), dkvn, "kvnorm_bwd_l1", col_block=kvlat_blk)
    dproj1 = jnp.concatenate([dgate1, dqlat_b, dkvlat_b, dkr.astype(BF16)], axis=1)
    dh1 = _matmul(dproj1, w1p, "nt", "dh_l1")
    dw1p = _matmul(h1, dproj1, "tn", "dw_in_l1")
    dw_in1 = jnp.concatenate([dw1p[:, di : di + i_kr], dw1p[:, :di]], axis=1)
    dx, dxb, d_ln1 = _rmsnorm_bwd(x1, row(ln1), dh1, "norm_bwd_l1", dres=dx)

    dx, dxb, d_ln0, dw_in0, dw_out0 = sb_layer_bwd(dx, dxb, sv0, ln0, f_in0, f_out0, "l0")
    grad_x = dx.reshape(x.shape)

    dws = [dw_in0, dw_out0, dw_in1, dw_qb1, dw_kvb1, dw_out1, dw_in2, dw_out2, dw_in3, dw_out3]
    parts = []
    for g, (r, c), n, cs in zip(dws, shard_shapes, rows, col_sharded):
        g4 = g.reshape(r, N_CHIPS, c).transpose(1, 0, 2) if cs else g.reshape(N_CHIPS, r, c)
        parts.append(g4.reshape(N_CHIPS, n, PACK_W))
    gp = _pack_rows(parts, offs, rows, rp, (N_CHIPS,))
    sib = _pair_exchange(gp)
    pair = _pair_add(gp, sib, c_idx)
    slots = _chip_exchange(pair)
    mine = _sum_slots(slots)
    gred = _pair_gather(mine)
    big_grads = [gred[o : o + n, :].reshape(r, c) for (r, c), o, n in zip(shard_shapes, offs, rows)]

    small = [ln0, ln1, q_norm1, kv_norm1, ln2, b_f2, ln3, final_norm]
    small_g = [d_ln0[0], d_ln1[0], d_qnorm[0], d_kvnorm[0], d_ln2[0], d_bf[0, :nh], d_ln3[0], d_final[0]]
    n_small = SMALL_SHAPE[0] * SMALL_SHAPE[1]
    used = sum(v.shape[0] for v in small) + 1
    assert used <= n_small

    def pack_small(vs, last):
        return jnp.concatenate(list(vs) + [last, jnp.zeros((n_small - used,), F32)]).reshape(SMALL_SHAPE)

    sm_sum = _allreduce_small(pack_small(small_g, loss_part[0, :1]))
    flat = sm_sum.reshape(-1)
    loss = flat[used - 1]

    big_m = [m_w_in0, m_w_out0, m_w_in1, m_w_qb1, m_w_kvb1, m_w_out1, m_w_in2, m_w_out2, m_w_in3, m_w_out3]
    big_v = [v_w_in0, v_w_out0, v_w_in1, v_w_qb1, v_w_kvb1, v_w_out1, v_w_in2, v_w_out2, v_w_in3, v_w_out3]
    big_names = ["w_in0", "w_out0", "w_in1", "w_qb1", "w_kvb1", "w_out1", "w_in2", "w_out2", "w_in3", "w_out3"]
    big_upd = [_adamw(w, g, m, v, f"adamw_{nm}") for w, g, m, v, nm in zip(big, big_grads, big_m, big_v, big_names)]

    small_m = [m_ln0, m_ln1, m_q_norm1, m_kv_norm1, m_ln2, m_b_f2, m_ln3, m_final_norm]
    small_v = [v_ln0, v_ln1, v_q_norm1, v_kv_norm1, v_ln2, v_b_f2, v_ln3, v_final_norm]
    one = jnp.ones((1,), F32)
    sd, smn, svn = _adamw(pack_small(small, one), sm_sum, pack_small(small_m, one), pack_small(small_v, one), "adamw_small")

    def unpack_small(p):
        out, at = [], 0
        fl = p.reshape(-1)
        for v in small:
            out.append(fl[at : at + v.shape[0]])
            at += v.shape[0]
        return out

    sg_l, sd_l, sm_l, sv_l = unpack_small(sm_sum), unpack_small(sd), unpack_small(smn), unpack_small(svn)

    order = ["ln0", "w_in0", "w_out0", "ln1", "w_in1", "q_norm1", "w_qb1", "kv_norm1", "w_kvb1", "w_out1", "ln2", "w_in2", "b_f2", "w_out2", "ln3", "w_in3", "w_out3", "final_norm"]
    small_names = ["ln0", "ln1", "q_norm1", "kv_norm1", "ln2", "b_f2", "ln3", "final_norm"]
    grads, deltas, new_m, new_v = {}, {}, {}, {}
    for nm, g, (dl, mn, vn) in zip(big_names, big_grads, big_upd):
        grads[nm], deltas[nm], new_m[nm], new_v[nm] = g, dl, mn, vn
    for nm, g, dl, mn, vn in zip(small_names, sg_l, sd_l, sm_l, sv_l):
        grads[nm], deltas[nm], new_m[nm], new_v[nm] = g, dl, mn, vn
    return (loss, grad_x, *[grads[n] for n in order], *[deltas[n] for n in order], *[new_m[n] for n in order], *[new_v[n] for n in order])
```

```python
import functools

import jax
import jax.numpy as jnp
from jax import lax
from jax.experimental import pallas as pl
from jax.experimental.pallas import tpu as pltpu

F32 = jnp.float32
BF16 = jnp.bfloat16
EPS = 1e-6
NEG = -1e30
HEAD_DIM = 128
CHUNK_SHIFT = 6
MLA_Q_RANK = 256
MLA_KV_RANK = 128
MLA_ROPE = 64
MLA_QK = 256
ROPE_BASE = 10000.0
ADAM_LR = 0.001
ADAM_B1 = 0.9
ADAM_B2 = 0.999
ADAM_EPS = 1e-08
ADAM_WD = 0.01
ADAM_STEP = 10
VMEM_LIMIT_BYTES = 56 * 2**20
LANES = 128
PACK_W = 1024
PACK_TR = 256
SMALL_SHAPE = (8, 768)
MESH = pl.DeviceIdType.MESH
N_CHIPS = 4


def _pick(n, cands):
    for c in cands:
        if n % c == 0:
            return c
    return n


def _cparams(sem):
    return pltpu.CompilerParams(dimension_semantics=sem, vmem_limit_bytes=VMEM_LIMIT_BYTES)


def _dot(a, b, dims):
    dn = {"nn": (((1,), (0,)), ((), ())), "nt": (((1,), (1,)), ((), ())), "tn": (((0,), (0,)), ((), ()))}[dims]
    return lax.dot_general(a, b, dn, preferred_element_type=F32)


def _matmul(a, b, dims, name, res=None):
    if dims == "nn":
        (m, k), (k2, n) = a.shape, b.shape
    elif dims == "nt":
        (m, k), (n, k2) = a.shape, b.shape
    else:
        (k, m), (k2, n) = a.shape, b.shape
    assert k == k2, (a.shape, b.shape, dims)
    tm = _pick(m, (1024, 512, 256, 128))
    tn = _pick(n, (1024, 640, 512, 384, 256, 128))
    tk = _pick(k, (1024, 640, 512, 256, 128))
    nk = k // tk

    def body(*refs):
        if res is None:
            a_ref, b_ref, o_ref = refs
            r_ref = None
        else:
            a_ref, b_ref, r_ref, o_ref = refs
        kk = pl.program_id(2)
        p = _dot(a_ref[...].astype(BF16), b_ref[...].astype(BF16), dims)

        @pl.when(kk == 0)
        def _():
            o_ref[...] = p if r_ref is None else p + r_ref[...]

        @pl.when(kk > 0)
        def _():
            o_ref[...] += p

    a_spec = pl.BlockSpec((tk, tm), lambda i, j, kk: (kk, i)) if dims == "tn" else pl.BlockSpec((tm, tk), lambda i, j, kk: (i, kk))
    b_spec = pl.BlockSpec((tn, tk), lambda i, j, kk: (j, kk)) if dims == "nt" else pl.BlockSpec((tk, tn), lambda i, j, kk: (kk, j))
    o_spec = pl.BlockSpec((tm, tn), lambda i, j, kk: (i, j))
    in_specs = [a_spec, b_spec] + ([] if res is None else [o_spec])
    args = (a, b) + (() if res is None else (res,))
    return pl.pallas_call(
        body,
        grid=(m // tm, n // tn, nk),
        in_specs=in_specs,
        out_specs=o_spec,
        out_shape=jax.ShapeDtypeStruct((m, n), F32),
        compiler_params=_cparams(("parallel", "parallel", "arbitrary")),
        name=name,
    )(*args)


def _rmsnorm_fwd(x, g, name, col_block=0):
    s = x.shape[0]
    w = g.shape[1]
    tr = _pick(s, (512, 256, 128))

    def body(x_ref, g_ref, h_ref):
        xv = x_ref[...]
        r = lax.rsqrt(jnp.mean(xv * xv, axis=-1, keepdims=True) + EPS)
        h_ref[...] = ((xv * r) * g_ref[...]).astype(BF16)

    return pl.pallas_call(
        body,
        grid=(s // tr,),
        in_specs=[pl.BlockSpec((tr, w), lambda i: (i, col_block)), pl.BlockSpec((1, w), lambda i: (0, 0))],
        out_specs=pl.BlockSpec((tr, w), lambda i: (i, 0)),
        out_shape=jax.ShapeDtypeStruct((s, w), BF16),
        compiler_params=_cparams(("parallel",)),
        name=name,
    )(x, g)


def _rmsnorm_bwd(x, g, dh, name, col_block=0, dres=None):
    s = x.shape[0]
    w = g.shape[1]
    tr = _pick(s, (512, 256, 128))

    def body(*refs):
        if dres is None:
            x_ref, g_ref, dh_ref, dx_ref, dxb_ref, dg_ref = refs
        else:
            x_ref, g_ref, dh_ref, dr_ref, dx_ref, dxb_ref, dg_ref = refs
        i = pl.program_id(0)
        xv = x_ref[...]
        r = lax.rsqrt(jnp.mean(xv * xv, axis=-1, keepdims=True) + EPS)
        xh = xv * r
        dhv = dh_ref[...]
        dyg = dhv * g_ref[...]
        dx = r * (dyg - xh * jnp.mean(dyg * xh, axis=-1, keepdims=True))
        if dres is not None:
            dx = dx + dr_ref[...]
        dx_ref[...] = dx
        dxb_ref[...] = dx.astype(BF16)
        part = jnp.sum(dhv * xh, axis=0, keepdims=True)

        @pl.when(i == 0)
        def _():
            dg_ref[...] = part

        @pl.when(i > 0)
        def _():
            dg_ref[...] += part

    row = pl.BlockSpec((tr, w), lambda i: (i, 0))
    in_specs = [pl.BlockSpec((tr, w), lambda i: (i, col_block)), pl.BlockSpec((1, w), lambda i: (0, 0)), row]
    args = [x, g, dh]
    if dres is not None:
        in_specs.append(row)
        args.append(dres)
    return pl.pallas_call(
        body,
        grid=(s // tr,),
        in_specs=in_specs,
        out_specs=[row, row, pl.BlockSpec((1, w), lambda i: (0, 0))],
        out_shape=[jax.ShapeDtypeStruct((s, w), F32), jax.ShapeDtypeStruct((s, w), BF16), jax.ShapeDtypeStruct((1, w), F32)],
        compiler_params=_cparams(("arbitrary",)),
        name=name,
    )(*args)


def _loss_head(x, g, target, name):
    s, d = x.shape
    tr = _pick(s, (512, 256, 128))

    def body(x_ref, g_ref, t_ref, dx_ref, dxb_ref, dg_ref, loss_ref):
        i = pl.program_id(0)
        xv = x_ref[...]
        gv = g_ref[...]
        r = lax.rsqrt(jnp.mean(xv * xv, axis=-1, keepdims=True) + EPS)
        xh = xv * r
        err = xh * gv - t_ref[...]
        lpart = 0.5 * jnp.sum(jnp.mean(err * err, axis=-1, keepdims=True), axis=0, keepdims=True)
        dy = err / d
        dyg = dy * gv
        dx = r * (dyg - xh * jnp.mean(dyg * xh, axis=-1, keepdims=True))
        dx_ref[...] = dx
        dxb_ref[...] = dx.astype(BF16)
        part = jnp.sum(dy * xh, axis=0, keepdims=True)
        lrow = jnp.broadcast_to(lpart, (1, LANES))

        @pl.when(i == 0)
        def _():
            dg_ref[...] = part
            loss_ref[...] = lrow

        @pl.when(i > 0)
        def _():
            dg_ref[...] += part
            loss_ref[...] += lrow

    row = pl.BlockSpec((tr, d), lambda i: (i, 0))
    vec = pl.BlockSpec((1, d), lambda i: (0, 0))
    return pl.pallas_call(
        body,
        grid=(s // tr,),
        in_specs=[row, vec, row],
        out_specs=[row, row, vec, pl.BlockSpec((1, LANES), lambda i: (0, 0))],
        out_shape=[
            jax.ShapeDtypeStruct((s, d), F32),
            jax.ShapeDtypeStruct((s, d), BF16),
            jax.ShapeDtypeStruct((1, d), F32),
            jax.ShapeDtypeStruct((1, LANES), F32),
        ],
        compiler_params=_cparams(("arbitrary",)),
        name=name,
    )(x, g, target)


def _sigmoid(x):
    return 1.0 / (1.0 + jnp.exp(-x))


def _gate_fwd(o, proj, gate_blk, name):
    s, di = o.shape
    tr = _pick(s, (256, 128))

    def body(o_ref, gate_ref, g_ref):
        gt = gate_ref[...]
        g_ref[...] = (o_ref[...] * (gt * _sigmoid(gt))).astype(BF16)

    row = pl.BlockSpec((tr, di), lambda i: (i, 0))
    return pl.pallas_call(
        body,
        grid=(s // tr,),
        in_specs=[row, pl.BlockSpec((tr, di), lambda i: (i, gate_blk))],
        out_specs=row,
        out_shape=jax.ShapeDtypeStruct((s, di), BF16),
        compiler_params=_cparams(("parallel",)),
        name=name,
    )(o, proj)


def _gate_bwd(dg, o, proj, gate_blk, name):
    s, di = o.shape
    tr = _pick(s, (256, 128))

    def body(dg_ref, o_ref, gate_ref, do_ref, dgate_ref):
        gt = gate_ref[...]
        sg = _sigmoid(gt)
        dgv = dg_ref[...]
        do_ref[...] = dgv * (gt * sg)
        dgate_ref[...] = (dgv * o_ref[...] * (sg * (1.0 + gt * (1.0 - sg)))).astype(BF16)

    row = pl.BlockSpec((tr, di), lambda i: (i, 0))
    return pl.pallas_call(
        body,
        grid=(s // tr,),
        in_specs=[row, row, pl.BlockSpec((tr, di), lambda i: (i, gate_blk))],
        out_specs=[row, row],
        out_shape=[jax.ShapeDtypeStruct((s, di), F32), jax.ShapeDtypeStruct((s, di), BF16)],
        compiler_params=_cparams(("parallel",)),
        name=name,
    )(dg, o, proj)


def _iotas(tq, tk):
    return lax.broadcasted_iota(jnp.int32, (tq, tk), 0), lax.broadcasted_iota(jnp.int32, (tq, tk), 1)


def _softplus(s):
    return jnp.maximum(s, 0.0) + jnp.log1p(jnp.exp(-jnp.abs(s)))


def _split2(v):
    hi = v.astype(BF16)
    lo = (v - hi.astype(F32)).astype(BF16)
    return hi, lo


def _split3(v):
    a = v.astype(BF16)
    r1 = v - a.astype(F32)
    b = r1.astype(BF16)
    c = (r1 - b.astype(F32)).astype(BF16)
    return a, b, c


SB_TQ = 512
SB_TK = 128


def _sb_fwd(proj, n_heads, name):
    s = proj.shape[0]
    d = HEAD_DIM
    t = min(SB_TQ, s)
    tk = min(SB_TK, s)
    r = t // tk
    scale = d**-0.5

    def body(q_ref, k_ref, v_ref, o_ref, lt_ref, kb_ref, vb_ref):
        i = pl.program_id(1)

        @pl.when(i == 0)
        def _():
            kb_ref[...] = k_ref[...].astype(BF16)
            vb_ref[...] = v_ref[...].astype(BF16)

        q = q_ref[...].astype(BF16)
        rows, cols = _iotas(t, tk)
        trows, tcols = _iotas(tk, tk)
        tri = (trows > tcols).astype(BF16)

        def block(kb, cl, acc, diag):
            k0 = pl.multiple_of(kb * tk, tk)
            sc = _dot(q, kb_ref[pl.ds(k0, tk), :], "nt") * scale
            sp = _softplus(sc)
            ls = -sp
            if diag is not None:
                strict = cols + diag * tk < rows
                ls = jnp.where(strict, ls, 0.0)
            hi, lo = _split2(ls)
            later = _dot(hi, tri, "nn") + _dot(lo, tri, "nn")
            w = jnp.exp((sc - sp) + later + cl)
            if diag is not None:
                w = jnp.where(strict, w, 0.0)
            acc = acc + _dot(w.astype(BF16), vb_ref[pl.ds(k0, tk), :], "nn")
            return cl + jnp.sum(ls, axis=1, keepdims=True), acc

        cl, acc = jnp.zeros((t, 1), F32), jnp.zeros((t, d), F32)
        for dd in reversed(range(r)):
            cl, acc = block(i * r + dd, cl, acc, dd)

        def loop(j, carry):
            return block(i * r - 1 - j, carry[0], carry[1], None)

        cl, acc = lax.fori_loop(0, i * r, loop, (cl, acc))
        o_ref[...] = acc
        lt_ref[...] = cl

    h = n_heads
    return pl.pallas_call(
        body,
        grid=(h, s // t),
        in_specs=[
            pl.BlockSpec((t, d), lambda hh, i: (i, hh)),
            pl.BlockSpec((s, d), lambda hh, i: (0, h + hh)),
            pl.BlockSpec((s, d), lambda hh, i: (0, 2 * h + hh)),
        ],
        out_specs=[pl.BlockSpec((t, d), lambda hh, i: (i, hh)), pl.BlockSpec((None, t, 1), lambda hh, i: (hh, i, 0))],
        out_shape=[jax.ShapeDtypeStruct((s, h * d), F32), jax.ShapeDtypeStruct((h, s, 1), F32)],
        scratch_shapes=[pltpu.VMEM((s, d), BF16), pltpu.VMEM((s, d), BF16)],
        compiler_params=_cparams(("arbitrary", "arbitrary")),
        name=name,
    )(proj, proj, proj)


def _sb_bwd(proj, ltot, do, n_heads, name):
    s = proj.shape[0]
    d = HEAD_DIM
    t = min(SB_TQ, s)
    tk = min(SB_TK, s)
    r = t // tk
    nq = s // t
    scale = d**-0.5

    def body(q_ref, k_ref, v_ref, lt_ref, do_ref, dq_ref, dk_ref, dv_ref, kb_ref, vb_ref):
        i = pl.program_id(1)

        @pl.when(i == 0)
        def _():
            kb_ref[...] = k_ref[...].astype(BF16)
            vb_ref[...] = v_ref[...].astype(BF16)
            dk_ref[...] = jnp.zeros_like(dk_ref)
            dv_ref[...] = jnp.zeros_like(dv_ref)

        q = q_ref[...].astype(BF16)
        dob = do_ref[...].astype(BF16)
        ltv = lt_ref[...]
        rows, cols = _iotas(t, tk)
        trows, tcols = _iotas(tk, tk)
        upto = (trows <= tcols).astype(BF16)
        before = (trows < tcols).astype(BF16)

        def block(kb, cp, cc, dq, diag):
            k0 = pl.multiple_of(kb * tk, tk)
            kk = kb_ref[pl.ds(k0, tk), :]
            sc = _dot(q, kk, "nt") * scale
            sp = _softplus(sc)
            ls = -sp
            if diag is not None:
                strict = cols + diag * tk < rows
                ls = jnp.where(strict, ls, 0.0)
            hi, lo = _split2(ls)
            prefix = _dot(hi, upto, "nn") + _dot(lo, upto, "nn") + cp
            lsig = sc - sp
            w = jnp.exp(lsig + (ltv - prefix))
            if diag is not None:
                w = jnp.where(strict, w, 0.0)
            da = _dot(dob, vb_ref[pl.ds(k0, tk), :], "nt") * w
            dhi, dlo = _split2(da)
            csum = _dot(dhi, before, "nn") + _dot(dlo, before, "nn") + cc
            beta = jnp.exp(lsig)
            dz = da * (1.0 - beta) - beta * csum
            if diag is not None:
                dz = jnp.where(strict, dz, 0.0)
            dzb = dz.astype(BF16)
            dq = dq + _dot(dzb, kk, "nn")
            dk_ref[pl.ds(k0, tk), :] += _dot(dzb, q, "tn")
            dv_ref[pl.ds(k0, tk), :] += _dot(w.astype(BF16), dob, "tn")
            return cp + jnp.sum(ls, axis=1, keepdims=True), cc + jnp.sum(da, axis=1, keepdims=True), dq

        def loop(kb, carry):
            return block(kb, carry[0], carry[1], carry[2], None)

        z1 = jnp.zeros((t, 1), F32)
        cp, cc, dq = lax.fori_loop(0, i * r, loop, (z1, z1, jnp.zeros((t, d), F32)))
        for dd in range(r):
            cp, cc, dq = block(i * r + dd, cp, cc, dq, dd)
        dq_ref[...] = dq * scale

        @pl.when(i == nq - 1)
        def _():
            dk_ref[...] = dk_ref[...] * scale

    h = n_heads
    qblk = pl.BlockSpec((t, d), lambda hh, i: (i, hh))
    full = pl.BlockSpec((s, d), lambda hh, i: (0, hh))
    shp = jax.ShapeDtypeStruct((s, h * d), F32)
    return pl.pallas_call(
        body,
        grid=(h, nq),
        in_specs=[
            qblk,
            pl.BlockSpec((s, d), lambda hh, i: (0, h + hh)),
            pl.BlockSpec((s, d), lambda hh, i: (0, 2 * h + hh)),
            pl.BlockSpec((None, t, 1), lambda hh, i: (hh, i, 0)),
            qblk,
        ],
        out_specs=[qblk, full, full],
        out_shape=[shp, shp, shp],
        scratch_shapes=[pltpu.VMEM((s, d), BF16), pltpu.VMEM((s, d), BF16)],
        compiler_params=_cparams(("arbitrary", "arbitrary")),
        name=name,
    )(proj, proj, proj, ltot, do)


SM_TQ = 512
SM_TK = 256


def _allowed(mode, rows, cols, q0, k0):
    r = rows + q0
    c = cols + k0
    if mode == "causal":
        return c <= r
    return (c >> CHUNK_SHIFT) <= (r >> CHUNK_SHIFT)


def _sm_fwd(q_arr, k_arr, v_arr, n_heads, dqk, q_blk, k_blk, v_blk, mode, name, crow=None, ccol=None):
    s = q_arr.shape[0]
    dv = HEAD_DIM
    t = min(SM_TQ, s)
    tk = min(SM_TK, s)
    r = t // tk
    bias = crow is not None
    scale = (HEAD_DIM if mode == "causal" else HEAD_DIM + MLA_ROPE) ** -0.5

    def body(*refs):
        if bias:
            q_ref, k_ref, v_ref, cr_ref, cc_ref, o_ref, lse_ref, kb_ref, vb_ref = refs
        else:
            q_ref, k_ref, v_ref, o_ref, lse_ref, kb_ref, vb_ref = refs
        i = pl.program_id(1)

        @pl.when(i == 0)
        def _():
            kb_ref[...] = k_ref[...].astype(BF16)
            vb_ref[...] = v_ref[...].astype(BF16)

        q = q_ref[...].astype(BF16)
        q0 = i * t
        rows, cols = _iotas(t, tk)
        crv = cr_ref[...] if bias else None

        def block(kb, m, l, acc, masked):
            k0 = pl.multiple_of(kb * tk, tk)
            sc = _dot(q, kb_ref[pl.ds(k0, tk), :], "nt") * scale
            if bias:
                sc = sc + crv - cc_ref[pl.ds(kb, 1), :]
            if masked:
                sc = jnp.where(_allowed(mode, rows, cols, q0, k0), sc, NEG)
            m_new = jnp.maximum(m, jnp.max(sc, axis=1, keepdims=True))
            alpha = jnp.exp(m - m_new)
            p = jnp.exp(sc - m_new)
            l = alpha * l + jnp.sum(p, axis=1, keepdims=True)
            acc = alpha * acc + _dot(p.astype(BF16), vb_ref[pl.ds(k0, tk), :], "nn")
            return m_new, l, acc

        def loop(kb, carry):
            return block(kb, carry[0], carry[1], carry[2], False)

        init = (jnp.full((t, 1), NEG, F32), jnp.zeros((t, 1), F32), jnp.zeros((t, dv), F32))
        m, l, acc = lax.fori_loop(0, i * r, loop, init)
        for dd in range(r):
            m, l, acc = block(i * r + dd, m, l, acc, True)
        o_ref[...] = acc / l
        lse_ref[...] = m + jnp.log(l)

    h = n_heads
    in_specs = [
        pl.BlockSpec((t, dqk), lambda hh, i: (i, q_blk(hh))),
        pl.BlockSpec((s, dqk), lambda hh, i: (0, k_blk(hh))),
        pl.BlockSpec((s, dv), lambda hh, i: (0, v_blk(hh))),
    ]
    args = [q_arr, k_arr, v_arr]
    if bias:
        in_specs += [pl.BlockSpec((None, t, 1), lambda hh, i: (hh, i, 0)), pl.BlockSpec((None, s // tk, tk), lambda hh, i: (hh, 0, 0))]
        args += [crow, ccol]
    return pl.pallas_call(
        body,
        grid=(h, s // t),
        in_specs=in_specs,
        out_specs=[pl.BlockSpec((t, dv), lambda hh, i: (i, hh)), pl.BlockSpec((None, t, 1), lambda hh, i: (hh, i, 0))],
        out_shape=[jax.ShapeDtypeStruct((s, h * dv), F32), jax.ShapeDtypeStruct((h, s, 1), F32)],
        scratch_shapes=[pltpu.VMEM((s, dqk), BF16), pltpu.VMEM((s, dv), BF16)],
        compiler_params=_cparams(("arbitrary", "arbitrary")),
        name=name,
    )(*args)


def _sm_bwd(q_arr, k_arr, v_arr, o, do, lse, n_heads, dqk, q_blk, k_blk, v_blk, mode, name, crow=None, ccol=None):
    s = q_arr.shape[0]
    dv = HEAD_DIM
    t = min(SM_TQ, s)
    tk = min(SM_TK, s)
    r = t // tk
    nq = s // t
    bias = crow is not None
    scale = (HEAD_DIM if mode == "causal" else HEAD_DIM + MLA_ROPE) ** -0.5

    def body(*refs):
        if bias:
            q_ref, k_ref, v_ref, o_ref, do_ref, lse_ref, cr_ref, cc_ref, dq_ref, dk_ref, dv_ref, dcc_ref, dcr_ref, kb_ref, vb_ref = refs
        else:
            q_ref, k_ref, v_ref, o_ref, do_ref, lse_ref, dq_ref, dk_ref, dv_ref, kb_ref, vb_ref = refs
        i = pl.program_id(1)

        @pl.when(i == 0)
        def _():
            kb_ref[...] = k_ref[...].astype(BF16)
            vb_ref[...] = v_ref[...].astype(BF16)
            dk_ref[...] = jnp.zeros_like(dk_ref)
            dv_ref[...] = jnp.zeros_like(dv_ref)
            if bias:
                dcc_ref[...] = jnp.zeros_like(dcc_ref)

        q = q_ref[...].astype(BF16)
        dov = do_ref[...]
        dob = dov.astype(BF16)
        dsum = jnp.sum(dov * o_ref[...], axis=1, keepdims=True)
        lsev = lse_ref[...]
        q0 = i * t
        rows, cols = _iotas(t, tk)
        crv = cr_ref[...] if bias else None

        def block(kb, dq, dr, masked):
            k0 = pl.multiple_of(kb * tk, tk)
            kk = kb_ref[pl.ds(k0, tk), :]
            sc = _dot(q, kk, "nt") * scale
            if bias:
                sc = sc + crv - cc_ref[pl.ds(kb, 1), :]
            if masked:
                sc = jnp.where(_allowed(mode, rows, cols, q0, k0), sc, NEG)
            p = jnp.exp(sc - lsev)
            dz = p * (_dot(dob, vb_ref[pl.ds(k0, tk), :], "nt") - dsum)
            dzb = dz.astype(BF16)
            dk_ref[pl.ds(k0, tk), :] += _dot(dzb, q, "tn")
            dv_ref[pl.ds(k0, tk), :] += _dot(p.astype(BF16), dob, "tn")
            if bias:
                dcc_ref[pl.ds(kb, 1), :] -= jnp.sum(dz, axis=0, keepdims=True)
                dr = dr + jnp.sum(dz, axis=1, keepdims=True)
            return dq + _dot(dzb, kk, "nn"), dr

        def loop(kb, carry):
            return block(kb, carry[0], carry[1], False)

        dq, dr = lax.fori_loop(0, i * r, loop, (jnp.zeros((t, dqk), F32), jnp.zeros((t, 1), F32)))
        for dd in range(r):
            dq, dr = block(i * r + dd, dq, dr, True)
        dq_ref[...] = dq * scale
        if bias:
            dcr_ref[...] = dr

        @pl.when(i == nq - 1)
        def _():
            dk_ref[...] = dk_ref[...] * scale

    h = n_heads
    qblk = pl.BlockSpec((t, dqk), lambda hh, i: (i, q_blk(hh)))
    oblk = pl.BlockSpec((t, dv), lambda hh, i: (i, hh))
    vec = pl.BlockSpec((None, t, 1), lambda hh, i: (hh, i, 0))
    in_specs = [
        qblk,
        pl.BlockSpec((s, dqk), lambda hh, i: (0, k_blk(hh))),
        pl.BlockSpec((s, dv), lambda hh, i: (0, v_blk(hh))),
        oblk,
        oblk,
        vec,
    ]
    args = [q_arr, k_arr, v_arr, o, do, lse]
    out_specs = [
        pl.BlockSpec((t, dqk), lambda hh, i: (i, hh)),
        pl.BlockSpec((s, dqk), lambda hh, i: (0, hh)),
        pl.BlockSpec((s, dv), lambda hh, i: (0, hh)),
    ]
    out_shape = [
        jax.ShapeDtypeStruct((s, h * dqk), F32),
        jax.ShapeDtypeStruct((s, h * dqk), F32),
        jax.ShapeDtypeStruct((s, h * dv), F32),
    ]
    if bias:
        ccs = pl.BlockSpec((None, s // tk, tk), lambda hh, i: (hh, 0, 0))
        in_specs += [vec, ccs]
        args += [crow, ccol]
        out_specs += [ccs, vec]
        out_shape += [jax.ShapeDtypeStruct((h, s // tk, tk), F32), jax.ShapeDtypeStruct((h, s, 1), F32)]
    return pl.pallas_call(
        body,
        grid=(h, nq),
        in_specs=in_specs,
        out_specs=out_specs,
        out_shape=out_shape,
        scratch_shapes=[pltpu.VMEM((s, dqk), BF16), pltpu.VMEM((s, dv), BF16)],
        compiler_params=_cparams(("arbitrary", "arbitrary")),
        name=name,
    )(*args)


def _rope_tables(pos):
    half = MLA_ROPE // 2
    inv_freq = ROPE_BASE ** (-jnp.arange(0, MLA_ROPE, 2, dtype=F32) / MLA_ROPE)
    ang = pos.astype(F32)[:, None] * inv_freq
    cos, sin = jnp.cos(ang), jnp.sin(ang)
    z = lambda n: jnp.zeros((pos.shape[0], n), F32)
    tc = jnp.concatenate([cos, cos, z(LANES - 2 * half)], axis=1)
    ta = jnp.concatenate([-sin, z(LANES - half)], axis=1)
    tb = jnp.concatenate([z(half), sin, z(LANES - 2 * half)], axis=1)
    return tc, ta, tb


def _rot(v, tc, ta, tb, sign):
    half = MLA_ROPE // 2
    return v * tc + sign * (pltpu.roll(v, LANES - half, 1) * ta + pltpu.roll(v, half, 1) * tb)


def _mla_assemble(qpre, kv, proj1, kr_blk, tabs, n_heads, name):
    s = qpre.shape[0]
    tr = _pick(s, (512, 256, 128))

    def body(qp_ref, kn_ref, kr_ref, tc_ref, ta_ref, tb_ref, qc_ref, kc_ref):
        tc, ta, tb = tc_ref[...], ta_ref[...], tb_ref[...]
        qc_ref[:, :LANES] = qp_ref[:, :LANES]
        qc_ref[:, LANES:] = _rot(qp_ref[:, LANES:], tc, ta, tb, 1.0)
        kc_ref[:, :LANES] = kn_ref[...]
        kc_ref[:, LANES:] = _rot(kr_ref[...], tc, ta, tb, 1.0)

    tab = pl.BlockSpec((tr, LANES), lambda i, hh: (i, 0))
    wide = pl.BlockSpec((tr, MLA_QK), lambda i, hh: (i, hh))
    shp = jax.ShapeDtypeStruct((s, n_heads * MLA_QK), F32)
    return pl.pallas_call(
        body,
        grid=(s // tr, n_heads),
        in_specs=[
            wide,
            pl.BlockSpec((tr, LANES), lambda i, hh: (i, 2 * hh)),
            pl.BlockSpec((tr, LANES), lambda i, hh: (i, kr_blk)),
            tab,
            tab,
            tab,
        ],
        out_specs=[wide, wide],
        out_shape=[shp, shp],
        compiler_params=_cparams(("parallel", "parallel")),
        name=name,
    )(qpre, kv, proj1, *tabs)


def _mla_disassemble(dqcat, dkcat, dv, tabs, n_heads, name):
    s = dqcat.shape[0]
    tr = _pick(s, (512, 256, 128))

    def body(dq_ref, dk_ref, dv_ref, tc_ref, ta_ref, tb_ref, dqp_ref, dkv_ref, dkr_ref):
        hh = pl.program_id(1)
        tc, ta, tb = tc_ref[...], ta_ref[...], tb_ref[...]
        dqp_ref[:, :LANES] = dq_ref[:, :LANES].astype(BF16)
        dqp_ref[:, LANES:] = _rot(dq_ref[:, LANES:], tc, ta, tb, -1.0).astype(BF16)
        dkv_ref[:, :LANES] = dk_ref[:, :LANES].astype(BF16)
        dkv_ref[:, LANES:] = dv_ref[...].astype(BF16)
        part = dk_ref[:, LANES:]

        @pl.when(hh == 0)
        def _():
            dkr_ref[...] = part

        @pl.when(hh > 0)
        def _():
            dkr_ref[...] += part

        @pl.when(hh == n_heads - 1)
        def _():
            dkr_ref[...] = _rot(dkr_ref[...], tc, ta, tb, -1.0)

    tab = pl.BlockSpec((tr, LANES), lambda i, hh: (i, 0))
    wide = pl.BlockSpec((tr, MLA_QK), lambda i, hh: (i, hh))
    shp = jax.ShapeDtypeStruct((s, n_heads * MLA_QK), BF16)
    return pl.pallas_call(
        body,
        grid=(s // tr, n_heads),
        in_specs=[wide, wide, pl.BlockSpec((tr, LANES), lambda i, hh: (i, hh)), tab, tab, tab],
        out_specs=[wide, wide, tab],
        out_shape=[shp, shp, jax.ShapeDtypeStruct((s, LANES), F32)],
        compiler_params=_cparams(("parallel", "arbitrary")),
        name=name,
    )(dqcat, dkcat, dv, *tabs)


def _forget_scan(proj2, f_blk, bias, name):
    s = proj2.shape[0]
    n = LANES

    def body(f_ref, b_ref, c_ref):
        rows, cols = _iotas(n, n)
        tri = (cols <= rows).astype(BF16)

        def step(j, carry):
            r0 = pl.multiple_of(j * n, n)
            f = f_ref[pl.ds(r0, n), :] + b_ref[...]
            lf = jnp.minimum(f, 0.0) - jnp.log1p(jnp.exp(-jnp.abs(f)))
            a, b, c = _split3(lf)
            cs = _dot(tri, a, "nn") + _dot(tri, b, "nn") + _dot(tri, c, "nn") + carry
            c_ref[pl.ds(r0, n), :] = cs
            return cs[n - 1 : n, :]

        lax.fori_loop(0, s // n, step, jnp.zeros((1, n), F32))

    return pl.pallas_call(
        body,
        grid=(1,),
        in_specs=[pl.BlockSpec((s, n), lambda i: (0, f_blk)), pl.BlockSpec((1, n), lambda i: (0, 0))],
        out_specs=pl.BlockSpec((s, n), lambda i: (0, 0)),
        out_shape=jax.ShapeDtypeStruct((s, n), F32),
        compiler_params=_cparams(("arbitrary",)),
        name=name,
    )(proj2, bias)


def _forget_scan_bwd(dc_col, dc_row, proj2, f_blk, bias, n_heads, name):
    s = proj2.shape[0]
    n = LANES
    nb = s // n

    def body(dcc_ref, dcr_ref, f_ref, b_ref, df_ref, db_ref):
        rows, cols = _iotas(n, n)
        tri = (cols >= rows).astype(BF16)
        live = cols < n_heads

        def step(j, carry):
            acc, dbv = carry
            r0 = pl.multiple_of((nb - 1 - j) * n, n)
            a, b, c = _split3(dcc_ref[pl.ds(r0, n), :] + dcr_ref[pl.ds(r0, n), :])
            dl = _dot(tri, a, "nn") + _dot(tri, b, "nn") + _dot(tri, c, "nn") + acc
            f = f_ref[pl.ds(r0, n), :] + b_ref[...]
            df = jnp.where(live, dl / (1.0 + jnp.exp(f)), 0.0)
            df_ref[pl.ds(r0, n), :] = df.astype(BF16)
            return dl[0:1, :], dbv + jnp.sum(df, axis=0, keepdims=True)

        z = jnp.zeros((1, n), F32)
        _, dbv = lax.fori_loop(0, nb, step, (z, z))
        db_ref[...] = dbv

    return pl.pallas_call(
        body,
        grid=(1,),
        in_specs=[
            pl.BlockSpec((s, n), lambda i: (0, 0)),
            pl.BlockSpec((s, n), lambda i: (0, 0)),
            pl.BlockSpec((s, n), lambda i: (0, f_blk)),
            pl.BlockSpec((1, n), lambda i: (0, 0)),
        ],
        out_specs=[pl.BlockSpec((s, n), lambda i: (0, 0)), pl.BlockSpec((1, n), lambda i: (0, 0))],
        out_shape=[jax.ShapeDtypeStruct((s, n), BF16), jax.ShapeDtypeStruct((1, n), F32)],
        compiler_params=_cparams(("arbitrary",)),
        name=name,
    )(dc_col, dc_row, proj2, bias)


def _adamw(w, g, m, v, name):
    r, c = w.shape
    tr = _pick(r, (128, 64, 32, 16, 8))
    c1 = 1.0 - ADAM_B1**ADAM_STEP
    c2 = 1.0 - ADAM_B2**ADAM_STEP

    def body(w_ref, g_ref, m_ref, v_ref, d_ref, mo_ref, vo_ref):
        gv = g_ref[...]
        mn = ADAM_B1 * m_ref[...] + (1.0 - ADAM_B1) * gv
        vn = ADAM_B2 * v_ref[...] + (1.0 - ADAM_B2) * (gv * gv)
        mo_ref[...] = mn
        vo_ref[...] = vn
        d_ref[...] = -ADAM_LR * ((mn / c1) / (jnp.sqrt(vn / c2) + ADAM_EPS) + ADAM_WD * w_ref[...])

    blk = pl.BlockSpec((tr, c), lambda i: (i, 0))
    shp = jax.ShapeDtypeStruct((r, c), F32)
    return pl.pallas_call(
        body,
        grid=(r // tr,),
        in_specs=[blk, blk, blk, blk],
        out_specs=[blk, blk, blk],
        out_shape=[shp, shp, shp],
        compiler_params=_cparams(("parallel",)),
        name=name,
    )(w, g, m, v)


def _mesh_pos():
    return lax.axis_index("x"), lax.axis_index("y"), lax.axis_index("c")


def _other_chips(x, y):
    return [(1 - x, y), (x, 1 - y), (1 - x, 1 - y)]


ANY = pl.BlockSpec(memory_space=pl.ANY)


def _gather_weights(wp):
    rp, wd = wp.shape
    half = rp // 2

    def body(w_ref, out_ref, send_sems, recv_sems, local_sem):
        x, y, c = _mesh_pos()
        me = 2 * x + y
        chips = _other_chips(x, y)

        def region(chip, hc):
            return out_ref.at[chip, pl.ds(hc * half, half), :]

        def copy(k, src, dst, to):
            return pltpu.make_async_remote_copy(
                src_ref=src, dst_ref=dst, send_sem=send_sems.at[k], recv_sem=recv_sems.at[k], device_id=to, device_id_type=MESH
            )

        mine = pltpu.make_async_copy(w_ref, out_ref.at[me], local_sem)
        mine.start()
        first = [copy(j, w_ref.at[pl.ds(c * half, half), :], region(me, c), (cx, cy, c)) for j, (cx, cy) in enumerate(chips)]
        for cp in first:
            cp.start()
        passed = [copy(3 + j, region(2 * cx + cy, c), region(2 * cx + cy, c), (x, y, 1 - c)) for j, (cx, cy) in enumerate(chips)]
        for j, (cx, cy) in enumerate(chips):
            copy(j, region(2 * cx + cy, c), region(2 * cx + cy, c), (x, y, c)).wait_recv()
            passed[j].start()
        for j, (cx, cy) in enumerate(chips):
            copy(3 + j, region(2 * cx + cy, 1 - c), region(2 * cx + cy, 1 - c), (x, y, c)).wait_recv()
        for cp in first + passed:
            cp.wait_send()
        mine.wait()

    return pl.pallas_call(
        body,
        in_specs=[ANY],
        out_specs=ANY,
        out_shape=jax.ShapeDtypeStruct((N_CHIPS, rp, wd), wp.dtype),
        scratch_shapes=[pltpu.SemaphoreType.DMA((6,)), pltpu.SemaphoreType.DMA((6,)), pltpu.SemaphoreType.DMA],
        name="gather_weights",
    )(wp)


def _pair_exchange(g):
    _, rp, wd = g.shape
    half = rp // 2

    def body(g_ref, out_ref, send_sem, recv_sem):
        x, y, c = _mesh_pos()
        cp = pltpu.make_async_remote_copy(
            src_ref=g_ref.at[:, pl.ds((1 - c) * half, half), :],
            dst_ref=out_ref,
            send_sem=send_sem,
            recv_sem=recv_sem,
            device_id=(x, y, 1 - c),
            device_id_type=MESH,
        )
        cp.start()
        cp.wait()

    return pl.pallas_call(
        body,
        in_specs=[ANY],
        out_specs=ANY,
        out_shape=jax.ShapeDtypeStruct((N_CHIPS, half, wd), g.dtype),
        scratch_shapes=[pltpu.SemaphoreType.DMA, pltpu.SemaphoreType.DMA],
        name="rs_pair_exchange",
    )(g)


def _pair_add(g, recv, c_idx):
    _, rp, wd = g.shape
    half = rp // 2
    nb = half // PACK_TR

    def body(c_ref, g_ref, r_ref, o_ref):
        o_ref[...] = g_ref[...] + r_ref[...]

    blk = (1, PACK_TR, wd)
    grid_spec = pltpu.PrefetchScalarGridSpec(
        num_scalar_prefetch=1,
        grid=(N_CHIPS, nb),
        in_specs=[pl.BlockSpec(blk, lambda j, i, cr: (j, cr[0] * nb + i, 0)), pl.BlockSpec(blk, lambda j, i, cr: (j, i, 0))],
        out_specs=pl.BlockSpec(blk, lambda j, i, cr: (j, i, 0)),
    )
    return pl.pallas_call(
        body,
        grid_spec=grid_spec,
        out_shape=jax.ShapeDtypeStruct((N_CHIPS, half, wd), F32),
        compiler_params=_cparams(("parallel", "parallel")),
        name="rs_pair_add",
    )(c_idx, g, recv)


def _chip_exchange(sp):
    _, rh, wd = sp.shape

    def body(s_ref, out_ref, send_sems, recv_sems, local_sem):
        x, y, c = _mesh_pos()
        me = 2 * x + y
        chips = _other_chips(x, y)
        mine = pltpu.make_async_copy(s_ref.at[me], out_ref.at[me], local_sem)
        mine.start()
        sends = []
        for j, (cx, cy) in enumerate(chips):
            cp = pltpu.make_async_remote_copy(
                src_ref=s_ref.at[2 * cx + cy],
                dst_ref=out_ref.at[me],
                send_sem=send_sems.at[j],
                recv_sem=recv_sems.at[j],
                device_id=(cx, cy, c),
                device_id_type=MESH,
            )
            cp.start()
            sends.append(cp)
        for j, (cx, cy) in enumerate(chips):
            pltpu.make_async_remote_copy(
                src_ref=s_ref.at[me],
                dst_ref=out_ref.at[2 * cx + cy],
                send_sem=send_sems.at[j],
                recv_sem=recv_sems.at[j],
                device_id=(x, y, c),
                device_id_type=MESH,
            ).wait_recv()
        for cp in sends:
            cp.wait_send()
        mine.wait()

    return pl.pallas_call(
        body,
        in_specs=[ANY],
        out_specs=ANY,
        out_shape=jax.ShapeDtypeStruct(sp.shape, sp.dtype),
        scratch_shapes=[pltpu.SemaphoreType.DMA((3,)), pltpu.SemaphoreType.DMA((3,)), pltpu.SemaphoreType.DMA],
        name="rs_chip_exchange",
    )(sp)


def _sum_slots(r):
    _, rh, wd = r.shape

    def body(r_ref, o_ref):
        o_ref[...] = ((r_ref[0] + r_ref[1]) + r_ref[2]) + r_ref[3]

    return pl.pallas_call(
        body,
        grid=(rh // PACK_TR,),
        in_specs=[pl.BlockSpec((N_CHIPS, PACK_TR, wd), lambda i: (0, i, 0))],
        out_specs=pl.BlockSpec((PACK_TR, wd), lambda i: (i, 0)),
        out_shape=jax.ShapeDtypeStruct((rh, wd), F32),
        compiler_params=_cparams(("parallel",)),
        name="rs_sum_slots",
    )(r)


def _pair_gather(tp):
    rh, wd = tp.shape

    def body(t_ref, out_ref, send_sem, recv_sem, local_sem):
        x, y, c = _mesh_pos()
        mine = pltpu.make_async_copy(t_ref, out_ref.at[pl.ds(c * rh, rh), :], local_sem)
        mine.start()
        cp = pltpu.make_async_remote_copy(
            src_ref=t_ref,
            dst_ref=out_ref.at[pl.ds(c * rh, rh), :],
            send_sem=send_sem,
            recv_sem=recv_sem,
            device_id=(x, y, 1 - c),
            device_id_type=MESH,
        )
        cp.start()
        cp.wait_send()
        pltpu.make_async_remote_copy(
            src_ref=t_ref,
            dst_ref=out_ref.at[pl.ds((1 - c) * rh, rh), :],
            send_sem=send_sem,
            recv_sem=recv_sem,
            device_id=(x, y, c),
            device_id_type=MESH,
        ).wait_recv()
        mine.wait()

    return pl.pallas_call(
        body,
        in_specs=[ANY],
        out_specs=ANY,
        out_shape=jax.ShapeDtypeStruct((2 * rh, wd), tp.dtype),
        scratch_shapes=[pltpu.SemaphoreType.DMA, pltpu.SemaphoreType.DMA, pltpu.SemaphoreType.DMA],
        name="rs_pair_gather",
    )(tp)


def _allreduce_small(v):
    shape = v.shape
    n_dev = 8

    def body(v_ref, o_ref, slots, send_sems, recv_sems):
        x, y, c = _mesh_pos()
        me = 4 * x + 2 * y + c
        slots[me] = v_ref[...]
        sends = []
        for k in range(1, n_dev):
            fx, fy, fc = (k >> 2) & 1, (k >> 1) & 1, k & 1
            to = (x ^ fx, y ^ fy, c ^ fc)
            cp = pltpu.make_async_remote_copy(
                src_ref=v_ref,
                dst_ref=slots.at[me],
                send_sem=send_sems.at[k - 1],
                recv_sem=recv_sems.at[k - 1],
                device_id=to,
                device_id_type=MESH,
            )
            cp.start()
            sends.append(cp)
        for k in range(1, n_dev):
            fx, fy, fc = (k >> 2) & 1, (k >> 1) & 1, k & 1
            frm = 4 * (x ^ fx) + 2 * (y ^ fy) + (c ^ fc)
            pltpu.make_async_remote_copy(
                src_ref=v_ref,
                dst_ref=slots.at[frm],
                send_sem=send_sems.at[k - 1],
                recv_sem=recv_sems.at[k - 1],
                device_id=(x, y, c),
                device_id_type=MESH,
            ).wait_recv()
        for cp in sends:
            cp.wait_send()
        acc = slots[0]
        for k in range(1, n_dev):
            acc = acc + slots[k]
        o_ref[...] = acc

    vm = pl.BlockSpec(memory_space=pltpu.VMEM)
    return pl.pallas_call(
        body,
        in_specs=[vm],
        out_specs=vm,
        out_shape=jax.ShapeDtypeStruct(shape, F32),
        scratch_shapes=[pltpu.VMEM((n_dev,) + shape, F32), pltpu.SemaphoreType.DMA((n_dev - 1,)), pltpu.SemaphoreType.DMA((n_dev - 1,))],
        name="allreduce_small",
    )(v)


def _pack_layout(shard_shapes):
    offs, rows = [], []
    off = 0
    for r, c in shard_shapes:
        assert (r * c) % PACK_W == 0
        n = r * c // PACK_W
        offs.append(off)
        rows.append(n)
        off += -(-n // 16) * 16
    rp = -(-off // (2 * PACK_TR)) * (2 * PACK_TR)
    return offs, rows, rp


def _pack_rows(parts, offs, rows, rp, lead):
    ends = list(offs[1:]) + [rp]
    nolead = ((0, 0),) * len(lead)
    out = [jnp.pad(p, nolead + ((0, e - o - n), (0, 0))) for p, o, n, e in zip(parts, offs, rows, ends)]
    return jnp.concatenate(out, axis=len(lead))


def kernel(x, positions, ln0, w_in0, w_out0, ln1, w_in1, q_norm1, w_qb1, kv_norm1, w_kvb1, w_out1, ln2, w_in2, b_f2, w_out2, ln3, w_in3, w_out3, final_norm, loss_target, m_ln0, m_w_in0, m_w_out0, m_ln1, m_w_in1, m_q_norm1, m_w_qb1, m_kv_norm1, m_w_kvb1, m_w_out1, m_ln2, m_w_in2, m_b_f2, m_w_out2, m_ln3, m_w_in3, m_w_out3, m_final_norm, v_ln0, v_w_in0, v_w_out0, v_ln1, v_w_in1, v_q_norm1, v_w_qb1, v_kv_norm1, v_w_kvb1, v_w_out1, v_ln2, v_w_in2, v_b_f2, v_w_out2, v_ln3, v_w_in3, v_w_out3, v_final_norm):
    xs = x[0]
    s, d = xs.shape
    di = 4 * w_out0.shape[0]
    nh = di // HEAD_DIM
    c_idx = lax.axis_index("c").astype(jnp.int32).reshape(1)

    big = [w_in0, w_out0, w_in1, w_qb1, w_kvb1, w_out1, w_in2, w_out2, w_in3, w_out3]
    col_sharded = [True, False, True, True, True, False, True, False, True, False]
    shard_shapes = [w.shape for w in big]
    offs, rows, rp = _pack_layout(shard_shapes)
    wp = _pack_rows([w.astype(BF16).reshape(n, PACK_W) for w, n in zip(big, rows)], offs, rows, rp, ())
    wall = _gather_weights(wp)
    full = []
    for (r, c), o, n, cs in zip(shard_shapes, offs, rows, col_sharded):
        slab = wall[:, o : o + n, :].reshape(N_CHIPS, r, c)
        full.append(slab.transpose(1, 0, 2).reshape(r, N_CHIPS * c) if cs else slab.reshape(N_CHIPS * r, c))
    f_in0, f_out0, f_in1, f_qb1, f_kvb1, f_out1, f_in2, f_out2, f_in3, f_out3 = full

    i_kr = MLA_Q_RANK + MLA_KV_RANK + MLA_ROPE
    w1p = jnp.concatenate([f_in1[:, i_kr:], f_in1[:, :i_kr], jnp.zeros((d, LANES - MLA_ROPE), BF16)], axis=1)
    qlat_blk = di // MLA_Q_RANK
    kvlat_blk = (di + MLA_Q_RANK) // MLA_KV_RANK
    kr_blk = (di + MLA_Q_RANK + MLA_KV_RANK) // LANES
    qk_w = HEAD_DIM + MLA_ROPE
    wqbp = jnp.pad(f_qb1.reshape(MLA_Q_RANK, nh, qk_w), ((0, 0), (0, 0), (0, MLA_QK - qk_w))).reshape(MLA_Q_RANK, nh * MLA_QK)
    n2 = f_in2.shape[1]
    w2p = jnp.pad(f_in2, ((0, 0), (0, 4 * di + LANES - n2)))
    b2p = jnp.pad(b_f2, (0, LANES - nh)).reshape(1, LANES)

    row = lambda v: v.reshape(1, -1)
    tabs = _rope_tables(positions[0])

    def sb_layer_fwd(xin, ln, w_in, w_out, tag):
        h = _rmsnorm_fwd(xin, row(ln), f"norm_fwd_{tag}")
        proj = _matmul(h, w_in, "nn", f"proj_in_{tag}")
        o, lt = _sb_fwd(proj, nh, f"sb_fwd_{tag}")
        g = _gate_fwd(o, proj, 3, f"gate_fwd_{tag}")
        xout = _matmul(g, w_out, "nn", f"proj_out_{tag}", res=xin)
        return xout, (xin, h, proj, o, lt, g)

    def sb_layer_bwd(dxn, dxnb, saved, ln, w_in, w_out, tag):
        xin, h, proj, o, lt, g = saved
        dgf = _matmul(dxnb, w_out, "nt", f"dgate_in_{tag}")
        dw_out = _matmul(g, dxnb, "tn", f"dw_out_{tag}")
        do, dgate = _gate_bwd(dgf, o, proj, 3, f"gate_bwd_{tag}")
        dq, dk, dv = _sb_bwd(proj, lt, do, nh, f"sb_bwd_{tag}")
        dproj = jnp.concatenate([dq.astype(BF16), dk.astype(BF16), dv.astype(BF16), dgate], axis=1)
        dh = _matmul(dproj, w_in, "nt", f"dh_{tag}")
        dw_in = _matmul(h, dproj, "tn", f"dw_in_{tag}")
        dx, dxb, dln = _rmsnorm_bwd(xin, row(ln), dh, f"norm_bwd_{tag}", dres=dxn)
        return dx, dxb, dln, dw_in, dw_out

    x1, sv0 = sb_layer_fwd(xs, ln0, f_in0, f_out0, "l0")

    h1 = _rmsnorm_fwd(x1, row(ln1), "norm_fwd_l1")
    proj1 = _matmul(h1, w1p, "nn", "proj_in_l1")
    qn = _rmsnorm_fwd(proj1, row(q_norm1), "qnorm_fwd_l1", col_block=qlat_blk)
    kvn = _rmsnorm_fwd(proj1, row(kv_norm1), "kvnorm_fwd_l1", col_block=kvlat_blk)
    qpre = _matmul(qn, wqbp, "nn", "q_up_l1")
    kv1 = _matmul(kvn, f_kvb1, "nn", "kv_up_l1")
    qcat, kcat = _mla_assemble(qpre, kv1, proj1, kr_blk, tabs, nh, "mla_assemble_l1")
    mla_blk = (lambda hh: hh, lambda hh: hh, lambda hh: 2 * hh + 1)
    o1, lse1 = _sm_fwd(qcat, kcat, kv1, nh, MLA_QK, *mla_blk, "chunk", "mla_fwd_l1")
    g1 = _gate_fwd(o1, proj1, 0, "gate_fwd_l1")
    x2 = _matmul(g1, f_out1, "nn", "proj_out_l1", res=x1)

    h2 = _rmsnorm_fwd(x2, row(ln2), "norm_fwd_l2")
    proj2 = _matmul(h2, w2p, "nn", "proj_in_l2")
    f_blk = 4 * di // LANES
    cum = _forget_scan(proj2, f_blk, b2p, "forget_scan_l2")
    cum_h = cum[:, :nh].T
    t_sm = min(SM_TK, s)
    crow = cum_h.reshape(nh, s, 1)
    ccol = cum_h.reshape(nh, s // t_sm, t_sm)
    fg_blk = (lambda hh: hh, lambda hh: nh + hh, lambda hh: 2 * nh + hh)
    o2, lse2 = _sm_fwd(proj2, proj2, proj2, nh, HEAD_DIM, *fg_blk, "causal", "forget_fwd_l2", crow=crow, ccol=ccol)
    g2 = _gate_fwd(o2, proj2, 3, "gate_fwd_l2")
    x3 = _matmul(g2, f_out2, "nn", "proj_out_l2", res=x2)

    x4, sv3 = sb_layer_fwd(x3, ln3, f_in3, f_out3, "l3")

    dx, dxb, d_final, loss_part = _loss_head(x4, row(final_norm), loss_target[0], "loss_head")
    dx, dxb, d_ln3, dw_in3, dw_out3 = sb_layer_bwd(dx, dxb, sv3, ln3, f_in3, f_out3, "l3")

    dgf2 = _matmul(dxb, f_out2, "nt", "dgate_in_l2")
    dw_out2 = _matmul(g2, dxb, "tn", "dw_out_l2")
    do2, dgate2 = _gate_bwd(dgf2, o2, proj2, 3, "gate_bwd_l2")
    dq2, dk2, dv2, dcc2, dcr2 = _sm_bwd(proj2, proj2, proj2, o2, do2, lse2, nh, HEAD_DIM, *fg_blk, "causal", "forget_bwd_l2", crow=crow, ccol=ccol)
    lanes_of = lambda a: jnp.pad(a.reshape(nh, s).T, ((0, 0), (0, LANES - nh)))
    df2, d_bf = _forget_scan_bwd(lanes_of(dcc2), lanes_of(dcr2), proj2, f_blk, b2p, nh, "forget_scan_bwd_l2")
    dproj2 = jnp.concatenate([dq2.astype(BF16), dk2.astype(BF16), dv2.astype(BF16), dgate2, df2], axis=1)
    dh2 = _matmul(dproj2, w2p, "nt", "dh_l2")
    dw_in2 = _matmul(h2, dproj2, "tn", "dw_in_l2")[:, :n2]
    dx, dxb, d_ln2 = _rmsnorm_bwd(x2, row(ln2), dh2, "norm_bwd_l2", dres=dx)

    dgf1 = _matmul(dxb, f_out1, "nt", "dgate_in_l1")
    dw_out1 = _matmul(g1, dxb, "tn", "dw_out_l1")
    do1, dgate1 = _gate_bwd(dgf1, o1, proj1, 0, "gate_bwd_l1")
    dqc, dkc, dv1 = _sm_bwd(qcat, kcat, kv1, o1, do1, lse1, nh, MLA_QK, *mla_blk, "chunk", "mla_bwd_l1")
    dqpre, dkv1, dkr = _mla_disassemble(dqc, dkc, dv1, tabs, nh, "mla_disassemble_l1")
    dqn = _matmul(dqpre, wqbp, "nt", "dqn_l1")
    dw_qbp = _matmul(qn, dqpre, "tn", "dw_qb_l1")
    dw_qb1 = dw_qbp.reshape(MLA_Q_RANK, nh, MLA_QK)[:, :, :qk_w].reshape(MLA_Q_RANK, nh * qk_w)
    dkvn = _matmul(dkv1, f_kvb1, "nt", "dkvn_l1")
    dw_kvb1 = _matmul(kvn, dkv1, "tn", "dw_kvb_l1")
    _, dqlat_b, d_qnorm = _rmsnorm_bwd(proj1, row(q_norm1), dqn, "qnorm_bwd_l1", col_block=qlat_blk)
    _, dkvlat_b, d_kvnorm = _rmsnorm_bwd(proj1, row(kv_norm1), dkvn, "kvnorm_bwd_l1", col_block=kvlat_blk)
    dproj1 = jnp.concatenate([dgate1, dqlat_b, dkvlat_b, dkr.astype(BF16)], axis=1)
    dh1 = _matmul(dproj1, w1p, "nt", "dh_l1")
    dw1p = _matmul(h1, dproj1, "tn", "dw_in_l1")
    dw_in1 = jnp.concatenate([dw1p[:, di : di + i_kr], dw1p[:, :di]], axis=1)
    dx, dxb, d_ln1 = _rmsnorm_bwd(x1, row(ln1), dh1, "norm_bwd_l1", dres=dx)

    dx, dxb, d_ln0, dw_in0, dw_out0 = sb_layer_bwd(dx, dxb, sv0, ln0, f_in0, f_out0, "l0")
    grad_x = dx.reshape(x.shape)

    dws = [dw_in0, dw_out0, dw_in1, dw_qb1, dw_kvb1, dw_out1, dw_in2, dw_out2, dw_in3, dw_out3]
    parts = []
    for g, (r, c), n, cs in zip(dws, shard_shapes, rows, col_sharded):
        g4 = g.reshape(r, N_CHIPS, c).transpose(1, 0, 2) if cs else g.reshape(N_CHIPS, r, c)
        parts.append(g4.reshape(N_CHIPS, n, PACK_W))
    gp = _pack_rows(parts, offs, rows, rp, (N_CHIPS,))
    sib = _pair_exchange(gp)
    pair = _pair_add(gp, sib, c_idx)
    slots = _chip_exchange(pair)
    mine = _sum_slots(slots)
    gred = _pair_gather(mine)
    big_grads = [gred[o : o + n, :].reshape(r, c) for (r, c), o, n in zip(shard_shapes, offs, rows)]

    small = [ln0, ln1, q_norm1, kv_norm1, ln2, b_f2, ln3, final_norm]
    small_g = [d_ln0[0], d_ln1[0], d_qnorm[0], d_kvnorm[0], d_ln2[0], d_bf[0, :nh], d_ln3[0], d_final[0]]
    n_small = SMALL_SHAPE[0] * SMALL_SHAPE[1]
    used = sum(v.shape[0] for v in small) + 1
    assert used <= n_small

    def pack_small(vs, last):
        return jnp.concatenate(list(vs) + [last, jnp.zeros((n_small - used,), F32)]).reshape(SMALL_SHAPE)

    sm_sum = _allreduce_small(pack_small(small_g, loss_part[0, :1]))
    flat = sm_sum.reshape(-1)
    loss = flat[used - 1]

    big_m = [m_w_in0, m_w_out0, m_w_in1, m_w_qb1, m_w_kvb1, m_w_out1, m_w_in2, m_w_out2, m_w_in3, m_w_out3]
    big_v = [v_w_in0, v_w_out0, v_w_in1, v_w_qb1, v_w_kvb1, v_w_out1, v_w_in2, v_w_out2, v_w_in3, v_w_out3]
    big_names = ["w_in0", "w_out0", "w_in1", "w_qb1", "w_kvb1", "w_out1", "w_in2", "w_out2", "w_in3", "w_out3"]
    big_upd = [_adamw(w, g, m, v, f"adamw_{nm}") for w, g, m, v, nm in zip(big, big_grads, big_m, big_v, big_names)]

    small_m = [m_ln0, m_ln1, m_q_norm1, m_kv_norm1, m_ln2, m_b_f2, m_ln3, m_final_norm]
    small_v = [v_ln0, v_ln1, v_q_norm1, v_kv_norm1, v_ln2, v_b_f2, v_ln3, v_final_norm]
    one = jnp.ones((1,), F32)
    sd, smn, svn = _adamw(pack_small(small, one), sm_sum, pack_small(small_m, one), pack_small(small_v, one), "adamw_small")

    def unpack_small(p):
        out, at = [], 0
        fl = p.reshape(-1)
        for v in small:
            out.append(fl[at : at + v.shape[0]])
            at += v.shape[0]
        return out

    sg_l, sd_l, sm_l, sv_l = unpack_small(sm_sum), unpack_small(sd), unpack_small(smn), unpack_small(svn)

    order = ["ln0", "w_in0", "w_out0", "ln1", "w_in1", "q_norm1", "w_qb1", "kv_norm1", "w_kvb1", "w_out1", "ln2", "w_in2", "b_f2", "w_out2", "ln3", "w_in3", "w_out3", "final_norm"]
    small_names = ["ln0", "ln1", "q_norm1", "kv_norm1", "ln2", "b_f2", "ln3", "final_norm"]
    grads, deltas, new_m, new_v = {}, {}, {}, {}
    for nm, g, (dl, mn, vn) in zip(big_names, big_grads, big_upd):
        grads[nm], deltas[nm], new_m[nm], new_v[nm] = g, dl, mn, vn
    for nm, g, dl, mn, vn in zip(small_names, sg_l, sd_l, sm_l, sv_l):
        grads[nm], deltas[nm], new_m[nm], new_v[nm] = g, dl, mn, vn
    return (loss, grad_x, *[grads[n] for n in order], *[deltas[n] for n in order], *[new_m[n] for n in order], *[new_v[n] for n in order])
```

```python
import functools

import jax
import jax.numpy as jnp
from jax import lax
from jax.experimental import pallas as pl
from jax.experimental.pallas import tpu as pltpu

F32 = jnp.float32
BF16 = jnp.bfloat16
EPS = 1e-6
NEG = -1e30
HEAD_DIM = 128
CHUNK_SHIFT = 6
MLA_Q_RANK = 256
MLA_KV_RANK = 128
MLA_ROPE = 64
MLA_QK = 256
ROPE_BASE = 10000.0
ADAM_LR = 0.001
ADAM_B1 = 0.9
ADAM_B2 = 0.999
ADAM_EPS = 1e-08
ADAM_WD = 0.01
ADAM_STEP = 10
VMEM_LIMIT_BYTES = 56 * 2**20
LANES = 128
PACK_W = 1024
PACK_TR = 256
SMALL_SHAPE = (8, 768)
MESH = pl.DeviceIdType.MESH
N_CHIPS = 4


def _pick(n, cands):
    for c in cands:
        if n % c == 0:
            return c
    return n


def _cparams(sem):
    return pltpu.CompilerParams(dimension_semantics=sem, vmem_limit_bytes=VMEM_LIMIT_BYTES)


def _dot(a, b, dims):
    dn = {"nn": (((1,), (0,)), ((), ())), "nt": (((1,), (1,)), ((), ())), "tn": (((0,), (0,)), ((), ()))}[dims]
    return lax.dot_general(a, b, dn, preferred_element_type=F32)


def _matmul(a, b, dims, name, res=None):
    if dims == "nn":
        (m, k), (k2, n) = a.shape, b.shape
    elif dims == "nt":
        (m, k), (n, k2) = a.shape, b.shape
    else:
        (k, m), (k2, n) = a.shape, b.shape
    assert k == k2, (a.shape, b.shape, dims)
    tm = _pick(m, (1024, 512, 256, 128))
    tn = _pick(n, (1024, 640, 512, 384, 256, 128))
    tk = _pick(k, (1024, 640, 512, 256, 128))
    nk = k // tk

    def body(*refs):
        if res is None:
            a_ref, b_ref, o_ref = refs
            r_ref = None
        else:
            a_ref, b_ref, r_ref, o_ref = refs
        kk = pl.program_id(2)
        p = _dot(a_ref[...].astype(BF16), b_ref[...].astype(BF16), dims)

        @pl.when(kk == 0)
        def _():
            o_ref[...] = p if r_ref is None else p + r_ref[...]

        @pl.when(kk > 0)
        def _():
            o_ref[...] += p

    a_spec = pl.BlockSpec((tk, tm), lambda i, j, kk: (kk, i)) if dims == "tn" else pl.BlockSpec((tm, tk), lambda i, j, kk: (i, kk))
    b_spec = pl.BlockSpec((tn, tk), lambda i, j, kk: (j, kk)) if dims == "nt" else pl.BlockSpec((tk, tn), lambda i, j, kk: (kk, j))
    o_spec = pl.BlockSpec((tm, tn), lambda i, j, kk: (i, j))
    in_specs = [a_spec, b_spec] + ([] if res is None else [o_spec])
    args = (a, b) + (() if res is None else (res,))
    return pl.pallas_call(
        body,
        grid=(m // tm, n // tn, nk),
        in_specs=in_specs,
        out_specs=o_spec,
        out_shape=jax.ShapeDtypeStruct((m, n), F32),
        compiler_params=_cparams(("parallel", "parallel", "arbitrary")),
        name=name,
    )(*args)


def _rmsnorm_fwd(x, g, name, col_block=0):
    s = x.shape[0]
    w = g.shape[1]
    tr = _pick(s, (512, 256, 128))

    def body(x_ref, g_ref, h_ref):
        xv = x_ref[...]
        r = lax.rsqrt(jnp.mean(xv * xv, axis=-1, keepdims=True) + EPS)
        h_ref[...] = ((xv * r) * g_ref[...]).astype(BF16)

    return pl.pallas_call(
        body,
        grid=(s // tr,),
        in_specs=[pl.BlockSpec((tr, w), lambda i: (i, col_block)), pl.BlockSpec((1, w), lambda i: (0, 0))],
        out_specs=pl.BlockSpec((tr, w), lambda i: (i, 0)),
        out_shape=jax.ShapeDtypeStruct((s, w), BF16),
        compiler_params=_cparams(("parallel",)),
        name=name,
    )(x, g)


def _rmsnorm_bwd(x, g, dh, name, col_block=0, dres=None):
    s = x.shape[0]
    w = g.shape[1]
    tr = _pick(s, (512, 256, 128))

    def body(*refs):
        if dres is None:
            x_ref, g_ref, dh_ref, dx_ref, dxb_ref, dg_ref = refs
        else:
            x_ref, g_ref, dh_ref, dr_ref, dx_ref, dxb_ref, dg_ref = refs
        i = pl.program_id(0)
        xv = x_ref[...]
        r = lax.rsqrt(jnp.mean(xv * xv, axis=-1, keepdims=True) + EPS)
        xh = xv * r
        dhv = dh_ref[...]
        dyg = dhv * g_ref[...]
        dx = r * (dyg - xh * jnp.mean(dyg * xh, axis=-1, keepdims=True))
        if dres is not None:
            dx = dx + dr_ref[...]
        dx_ref[...] = dx
        dxb_ref[...] = dx.astype(BF16)
        part = jnp.sum(dhv * xh, axis=0, keepdims=True)

        @pl.when(i == 0)
        def _():
            dg_ref[...] = part

        @pl.when(i > 0)
        def _():
            dg_ref[...] += part

    row = pl.BlockSpec((tr, w), lambda i: (i, 0))
    in_specs = [pl.BlockSpec((tr, w), lambda i: (i, col_block)), pl.BlockSpec((1, w), lambda i: (0, 0)), row]
    args = [x, g, dh]
    if dres is not None:
        in_specs.append(row)
        args.append(dres)
    return pl.pallas_call(
        body,
        grid=(s // tr,),
        in_specs=in_specs,
        out_specs=[row, row, pl.BlockSpec((1, w), lambda i: (0, 0))],
        out_shape=[jax.ShapeDtypeStruct((s, w), F32), jax.ShapeDtypeStruct((s, w), BF16), jax.ShapeDtypeStruct((1, w), F32)],
        compiler_params=_cparams(("arbitrary",)),
        name=name,
    )(*args)


def _loss_head(x, g, target, name):
    s, d = x.shape
    tr = _pick(s, (512, 256, 128))

    def body(x_ref, g_ref, t_ref, dx_ref, dxb_ref, dg_ref, loss_ref):
        i = pl.program_id(0)
        xv = x_ref[...]
        gv = g_ref[...]
        r = lax.rsqrt(jnp.mean(xv * xv, axis=-1, keepdims=True) + EPS)
        xh = xv * r
        err = xh * gv - t_ref[...]
        lpart = 0.5 * jnp.sum(jnp.mean(err * err, axis=-1, keepdims=True), axis=0, keepdims=True)
        dy = err / d
        dyg = dy * gv
        dx = r * (dyg - xh * jnp.mean(dyg * xh, axis=-1, keepdims=True))
        dx_ref[...] = dx
        dxb_ref[...] = dx.astype(BF16)
        part = jnp.sum(dy * xh, axis=0, keepdims=True)
        lrow = jnp.broadcast_to(lpart, (1, LANES))

        @pl.when(i == 0)
        def _():
            dg_ref[...] = part
            loss_ref[...] = lrow

        @pl.when(i > 0)
        def _():
            dg_ref[...] += part
            loss_ref[...] += lrow

    row = pl.BlockSpec((tr, d), lambda i: (i, 0))
    vec = pl.BlockSpec((1, d), lambda i: (0, 0))
    return pl.pallas_call(
        body,
        grid=(s // tr,),
        in_specs=[row, vec, row],
        out_specs=[row, row, vec, pl.BlockSpec((1, LANES), lambda i: (0, 0))],
        out_shape=[
            jax.ShapeDtypeStruct((s, d), F32),
            jax.ShapeDtypeStruct((s, d), BF16),
            jax.ShapeDtypeStruct((1, d), F32),
            jax.ShapeDtypeStruct((1, LANES), F32),
        ],
        compiler_params=_cparams(("arbitrary",)),
        name=name,
    )(x, g, target)


def _sigmoid(x):
    return 1.0 / (1.0 + jnp.exp(-x))


def _gate_fwd(o, proj, gate_blk, name):
    s, di = o.shape
    tr = _pick(s, (256, 128))

    def body(o_ref, gate_ref, g_ref):
        gt = gate_ref[...]
        g_ref[...] = (o_ref[...] * (gt * _sigmoid(gt))).astype(BF16)

    row = pl.BlockSpec((tr, di), lambda i: (i, 0))
    return pl.pallas_call(
        body,
        grid=(s // tr,),
        in_specs=[row, pl.BlockSpec((tr, di), lambda i: (i, gate_blk))],
        out_specs=row,
        out_shape=jax.ShapeDtypeStruct((s, di), BF16),
        compiler_params=_cparams(("parallel",)),
        name=name,
    )(o, proj)


def _gate_bwd(dg, o, proj, gate_blk, name):
    s, di = o.shape
    tr = _pick(s, (256, 128))

    def body(dg_ref, o_ref, gate_ref, do_ref, dgate_ref):
        gt = gate_ref[...]
        sg = _sigmoid(gt)
        dgv = dg_ref[...]
        do_ref[...] = dgv * (gt * sg)
        dgate_ref[...] = (dgv * o_ref[...] * (sg * (1.0 + gt * (1.0 - sg)))).astype(BF16)

    row = pl.BlockSpec((tr, di), lambda i: (i, 0))
    return pl.pallas_call(
        body,
        grid=(s // tr,),
        in_specs=[row, row, pl.BlockSpec((tr, di), lambda i: (i, gate_blk))],
        out_specs=[row, row],
        out_shape=[jax.ShapeDtypeStruct((s, di), F32), jax.ShapeDtypeStruct((s, di), BF16)],
        compiler_params=_cparams(("parallel",)),
        name=name,
    )(dg, o, proj)


def _iotas(tq, tk):
    return lax.broadcasted_iota(jnp.int32, (tq, tk), 0), lax.broadcasted_iota(jnp.int32, (tq, tk), 1)


def _softplus(s):
    return jnp.maximum(s, 0.0) + jnp.log1p(jnp.exp(-jnp.abs(s)))


def _split2(v):
    hi = v.astype(BF16)
    lo = (v - hi.astype(F32)).astype(BF16)
    return hi, lo


def _split3(v):
    a = v.astype(BF16)
    r1 = v - a.astype(F32)
    b = r1.astype(BF16)
    c = (r1 - b.astype(F32)).astype(BF16)
    return a, b, c


SB_TQ = 512
SB_TK = 128
SB_UNROLL = 4


def _sb_fwd(proj, n_heads, name):
    s = proj.shape[0]
    d = HEAD_DIM
    t = min(SB_TQ, s)
    tk = min(SB_TK, s)
    r = t // tk
    un = SB_UNROLL if r % SB_UNROLL == 0 else 1
    scale = d**-0.5

    def body(q_ref, k_ref, v_ref, o_ref, lt_ref, kb_ref, vb_ref):
        i = pl.program_id(1)

        @pl.when(i == 0)
        def _():
            kb_ref[...] = k_ref[...].astype(BF16)
            vb_ref[...] = v_ref[...].astype(BF16)

        q = q_ref[...].astype(BF16)
        rows, cols = _iotas(t, tk)
        trows, tcols = _iotas(tk, tk)
        tri = (trows > tcols).astype(BF16)

        def block(kb, cl, acc, diag):
            k0 = pl.multiple_of(kb * tk, tk)
            sc = _dot(q, kb_ref[pl.ds(k0, tk), :], "nt") * scale
            sp = _softplus(sc)
            ls = -sp
            if diag is not None:
                strict = cols + diag * tk < rows
                ls = jnp.where(strict, ls, 0.0)
            hi, lo = _split2(ls)
            later = _dot(hi, tri, "nn") + _dot(lo, tri, "nn")
            w = jnp.exp((sc - sp) + later + cl)
            if diag is not None:
                w = jnp.where(strict, w, 0.0)
            acc = acc + _dot(w.astype(BF16), vb_ref[pl.ds(k0, tk), :], "nn")
            return cl + jnp.sum(ls, axis=1, keepdims=True), acc

        cl, acc = jnp.zeros((t, 1), F32), jnp.zeros((t, d), F32)
        for dd in reversed(range(r)):
            cl, acc = block(i * r + dd, cl, acc, dd)

        def loop(j, carry):
            for u in range(un):
                carry = block(i * r - 1 - (un * j + u), carry[0], carry[1], None)
            return carry

        cl, acc = lax.fori_loop(0, (i * r) // un, loop, (cl, acc))
        o_ref[...] = acc
        lt_ref[...] = cl

    h = n_heads
    return pl.pallas_call(
        body,
        grid=(h, s // t),
        in_specs=[
            pl.BlockSpec((t, d), lambda hh, i: (i, hh)),
            pl.BlockSpec((s, d), lambda hh, i: (0, h + hh)),
            pl.BlockSpec((s, d), lambda hh, i: (0, 2 * h + hh)),
        ],
        out_specs=[pl.BlockSpec((t, d), lambda hh, i: (i, hh)), pl.BlockSpec((None, t, 1), lambda hh, i: (hh, i, 0))],
        out_shape=[jax.ShapeDtypeStruct((s, h * d), F32), jax.ShapeDtypeStruct((h, s, 1), F32)],
        scratch_shapes=[pltpu.VMEM((s, d), BF16), pltpu.VMEM((s, d), BF16)],
        compiler_params=_cparams(("arbitrary", "arbitrary")),
        name=name,
    )(proj, proj, proj)


def _sb_bwd(proj, ltot, do, n_heads, name):
    s = proj.shape[0]
    d = HEAD_DIM
    t = min(SB_TQ, s)
    tk = min(SB_TK, s)
    r = t // tk
    un = SB_UNROLL if r % SB_UNROLL == 0 else 1
    nq = s // t
    scale = d**-0.5

    def body(q_ref, k_ref, v_ref, lt_ref, do_ref, dq_ref, dk_ref, dv_ref, kb_ref, vb_ref):
        i = pl.program_id(1)

        @pl.when(i == 0)
        def _():
            kb_ref[...] = k_ref[...].astype(BF16)
            vb_ref[...] = v_ref[...].astype(BF16)
            dk_ref[...] = jnp.zeros_like(dk_ref)
            dv_ref[...] = jnp.zeros_like(dv_ref)

        q = q_ref[...].astype(BF16)
        dob = do_ref[...].astype(BF16)
        ltv = lt_ref[...]
        rows, cols = _iotas(t, tk)
        trows, tcols = _iotas(tk, tk)
        upto = (trows <= tcols).astype(BF16)
        before = (trows < tcols).astype(BF16)

        def block(kb, cp, cc, dq, diag):
            k0 = pl.multiple_of(kb * tk, tk)
            kk = kb_ref[pl.ds(k0, tk), :]
            sc = _dot(q, kk, "nt") * scale
            sp = _softplus(sc)
            ls = -sp
            if diag is not None:
                strict = cols + diag * tk < rows
                ls = jnp.where(strict, ls, 0.0)
            hi, lo = _split2(ls)
            prefix = _dot(hi, upto, "nn") + _dot(lo, upto, "nn") + cp
            lsig = sc - sp
            w = jnp.exp(lsig + (ltv - prefix))
            if diag is not None:
                w = jnp.where(strict, w, 0.0)
            da = _dot(dob, vb_ref[pl.ds(k0, tk), :], "nt") * w
            dhi, dlo = _split2(da)
            csum = _dot(dhi, before, "nn") + _dot(dlo, before, "nn") + cc
            beta = jnp.exp(lsig)
            dz = da * (1.0 - beta) - beta * csum
            if diag is not None:
                dz = jnp.where(strict, dz, 0.0)
            dzb = dz.astype(BF16)
            dq = dq + _dot(dzb, kk, "nn")
            dk_ref[pl.ds(k0, tk), :] += _dot(dzb, q, "tn")
            dv_ref[pl.ds(k0, tk), :] += _dot(w.astype(BF16), dob, "tn")
            return cp + jnp.sum(ls, axis=1, keepdims=True), cc + jnp.sum(da, axis=1, keepdims=True), dq

        def loop(j, carry):
            for u in range(un):
                carry = block(un * j + u, carry[0], carry[1], carry[2], None)
            return carry

        z1 = jnp.zeros((t, 1), F32)
        cp, cc, dq = lax.fori_loop(0, (i * r) // un, loop, (z1, z1, jnp.zeros((t, d), F32)))
        for dd in range(r):
            cp, cc, dq = block(i * r + dd, cp, cc, dq, dd)
        dq_ref[...] = dq * scale

        @pl.when(i == nq - 1)
        def _():
            dk_ref[...] = dk_ref[...] * scale

    h = n_heads
    qblk = pl.BlockSpec((t, d), lambda hh, i: (i, hh))
    full = pl.BlockSpec((s, d), lambda hh, i: (0, hh))
    shp = jax.ShapeDtypeStruct((s, h * d), F32)
    return pl.pallas_call(
        body,
        grid=(h, nq),
        in_specs=[
            qblk,
            pl.BlockSpec((s, d), lambda hh, i: (0, h + hh)),
            pl.BlockSpec((s, d), lambda hh, i: (0, 2 * h + hh)),
            pl.BlockSpec((None, t, 1), lambda hh, i: (hh, i, 0)),
            qblk,
        ],
        out_specs=[qblk, full, full],
        out_shape=[shp, shp, shp],
        scratch_shapes=[pltpu.VMEM((s, d), BF16), pltpu.VMEM((s, d), BF16)],
        compiler_params=_cparams(("arbitrary", "arbitrary")),
        name=name,
    )(proj, proj, proj, ltot, do)


SM_TQ = 512
SM_TK = 256
SM_UNROLL = 2


def _allowed(mode, rows, cols, q0, k0):
    r = rows + q0
    c = cols + k0
    if mode == "causal":
        return c <= r
    return (c >> CHUNK_SHIFT) <= (r >> CHUNK_SHIFT)


def _sm_fwd(q_arr, k_arr, v_arr, n_heads, dqk, q_blk, k_blk, v_blk, mode, name, crow=None, ccol=None):
    s = q_arr.shape[0]
    dv = HEAD_DIM
    t = min(SM_TQ, s)
    tk = min(SM_TK, s)
    r = t // tk
    un = SM_UNROLL if r % SM_UNROLL == 0 else 1
    bias = crow is not None
    scale = (HEAD_DIM if mode == "causal" else HEAD_DIM + MLA_ROPE) ** -0.5

    def body(*refs):
        if bias:
            q_ref, k_ref, v_ref, cr_ref, cc_ref, o_ref, lse_ref, kb_ref, vb_ref = refs
        else:
            q_ref, k_ref, v_ref, o_ref, lse_ref, kb_ref, vb_ref = refs
        i = pl.program_id(1)

        @pl.when(i == 0)
        def _():
            kb_ref[...] = k_ref[...].astype(BF16)
            vb_ref[...] = v_ref[...].astype(BF16)

        q = q_ref[...].astype(BF16)
        q0 = i * t
        rows, cols = _iotas(t, tk)
        crv = cr_ref[...] if bias else None

        def block(kb, m, l, acc, masked):
            k0 = pl.multiple_of(kb * tk, tk)
            sc = _dot(q, kb_ref[pl.ds(k0, tk), :], "nt") * scale
            if bias:
                sc = sc + crv - cc_ref[pl.ds(kb, 1), :]
            if masked:
                sc = jnp.where(_allowed(mode, rows, cols, q0, k0), sc, NEG)
            m_new = jnp.maximum(m, jnp.max(sc, axis=1, keepdims=True))
            alpha = jnp.exp(m - m_new)
            p = jnp.exp(sc - m_new)
            l = alpha * l + jnp.sum(p, axis=1, keepdims=True)
            acc = alpha * acc + _dot(p.astype(BF16), vb_ref[pl.ds(k0, tk), :], "nn")
            return m_new, l, acc

        def loop(j, carry):
            for u in range(un):
                carry = block(un * j + u, carry[0], carry[1], carry[2], False)
            return carry

        init = (jnp.full((t, 1), NEG, F32), jnp.zeros((t, 1), F32), jnp.zeros((t, dv), F32))
        m, l, acc = lax.fori_loop(0, (i * r) // un, loop, init)
        for dd in range(r):
            m, l, acc = block(i * r + dd, m, l, acc, True)
        o_ref[...] = acc / l
        lse_ref[...] = m + jnp.log(l)

    h = n_heads
    in_specs = [
        pl.BlockSpec((t, dqk), lambda hh, i: (i, q_blk(hh))),
        pl.BlockSpec((s, dqk), lambda hh, i: (0, k_blk(hh))),
        pl.BlockSpec((s, dv), lambda hh, i: (0, v_blk(hh))),
    ]
    args = [q_arr, k_arr, v_arr]
    if bias:
        in_specs += [pl.BlockSpec((None, t, 1), lambda hh, i: (hh, i, 0)), pl.BlockSpec((None, s // tk, tk), lambda hh, i: (hh, 0, 0))]
        args += [crow, ccol]
    return pl.pallas_call(
        body,
        grid=(h, s // t),
        in_specs=in_specs,
        out_specs=[pl.BlockSpec((t, dv), lambda hh, i: (i, hh)), pl.BlockSpec((None, t, 1), lambda hh, i: (hh, i, 0))],
        out_shape=[jax.ShapeDtypeStruct((s, h * dv), F32), jax.ShapeDtypeStruct((h, s, 1), F32)],
        scratch_shapes=[pltpu.VMEM((s, dqk), BF16), pltpu.VMEM((s, dv), BF16)],
        compiler_params=_cparams(("arbitrary", "arbitrary")),
        name=name,
    )(*args)


def _sm_bwd(q_arr, k_arr, v_arr, o, do, lse, n_heads, dqk, q_blk, k_blk, v_blk, mode, name, crow=None, ccol=None):
    s = q_arr.shape[0]
    dv = HEAD_DIM
    t = min(SM_TQ, s)
    tk = min(SM_TK, s)
    r = t // tk
    un = SM_UNROLL if r % SM_UNROLL == 0 else 1
    nq = s // t
    bias = crow is not None
    scale = (HEAD_DIM if mode == "causal" else HEAD_DIM + MLA_ROPE) ** -0.5

    def body(*refs):
        if bias:
            q_ref, k_ref, v_ref, o_ref, do_ref, lse_ref, cr_ref, cc_ref, dq_ref, dk_ref, dv_ref, dcc_ref, dcr_ref, kb_ref, vb_ref = refs
        else:
            q_ref, k_ref, v_ref, o_ref, do_ref, lse_ref, dq_ref, dk_ref, dv_ref, kb_ref, vb_ref = refs
        i = pl.program_id(1)

        @pl.when(i == 0)
        def _():
            kb_ref[...] = k_ref[...].astype(BF16)
            vb_ref[...] = v_ref[...].astype(BF16)
            dk_ref[...] = jnp.zeros_like(dk_ref)
            dv_ref[...] = jnp.zeros_like(dv_ref)
            if bias:
                dcc_ref[...] = jnp.zeros_like(dcc_ref)

        q = q_ref[...].astype(BF16)
        dov = do_ref[...]
        dob = dov.astype(BF16)
        dsum = jnp.sum(dov * o_ref[...], axis=1, keepdims=True)
        lsev = lse_ref[...]
        q0 = i * t
        rows, cols = _iotas(t, tk)
        crv = cr_ref[...] if bias else None

        def block(kb, dq, dr, masked):
            k0 = pl.multiple_of(kb * tk, tk)
            kk = kb_ref[pl.ds(k0, tk), :]
            sc = _dot(q, kk, "nt") * scale
            if bias:
                sc = sc + crv - cc_ref[pl.ds(kb, 1), :]
            if masked:
                sc = jnp.where(_allowed(mode, rows, cols, q0, k0), sc, NEG)
            p = jnp.exp(sc - lsev)
            dz = p * (_dot(dob, vb_ref[pl.ds(k0, tk), :], "nt") - dsum)
            dzb = dz.astype(BF16)
            dk_ref[pl.ds(k0, tk), :] += _dot(dzb, q, "tn")
            dv_ref[pl.ds(k0, tk), :] += _dot(p.astype(BF16), dob, "tn")
            if bias:
                dcc_ref[pl.ds(kb, 1), :] -= jnp.sum(dz, axis=0, keepdims=True)
                dr = dr + jnp.sum(dz, axis=1, keepdims=True)
            return dq + _dot(dzb, kk, "nn"), dr

        def loop(j, carry):
            for u in range(un):
                carry = block(un * j + u, carry[0], carry[1], False)
            return carry

        dq, dr = lax.fori_loop(0, (i * r) // un, loop, (jnp.zeros((t, dqk), F32), jnp.zeros((t, 1), F32)))
        for dd in range(r):
            dq, dr = block(i * r + dd, dq, dr, True)
        dq_ref[...] = dq * scale
        if bias:
            dcr_ref[...] = dr

        @pl.when(i == nq - 1)
        def _():
            dk_ref[...] = dk_ref[...] * scale

    h = n_heads
    qblk = pl.BlockSpec((t, dqk), lambda hh, i: (i, q_blk(hh)))
    oblk = pl.BlockSpec((t, dv), lambda hh, i: (i, hh))
    vec = pl.BlockSpec((None, t, 1), lambda hh, i: (hh, i, 0))
    in_specs = [
        qblk,
        pl.BlockSpec((s, dqk), lambda hh, i: (0, k_blk(hh))),
        pl.BlockSpec((s, dv), lambda hh, i: (0, v_blk(hh))),
        oblk,
        oblk,
        vec,
    ]
    args = [q_arr, k_arr, v_arr, o, do, lse]
    out_specs = [
        pl.BlockSpec((t, dqk), lambda hh, i: (i, hh)),
        pl.BlockSpec((s, dqk), lambda hh, i: (0, hh)),
        pl.BlockSpec((s, dv), lambda hh, i: (0, hh)),
    ]
    out_shape = [
        jax.ShapeDtypeStruct((s, h * dqk), F32),
        jax.ShapeDtypeStruct((s, h * dqk), F32),
        jax.ShapeDtypeStruct((s, h * dv), F32),
    ]
    if bias:
        ccs = pl.BlockSpec((None, s // tk, tk), lambda hh, i: (hh, 0, 0))
        in_specs += [vec, ccs]
        args += [crow, ccol]
        out_specs += [ccs, vec]
        out_shape += [jax.ShapeDtypeStruct((h, s // tk, tk), F32), jax.ShapeDtypeStruct((h, s, 1), F32)]
    return pl.pallas_call(
        body,
        grid=(h, nq),
        in_specs=in_specs,
        out_specs=out_specs,
        out_shape=out_shape,
        scratch_shapes=[pltpu.VMEM((s, dqk), BF16), pltpu.VMEM((s, dv), BF16)],
        compiler_params=_cparams(("arbitrary", "arbitrary")),
        name=name,
    )(*args)


def _rope_tables(pos):
    half = MLA_ROPE // 2
    inv_freq = ROPE_BASE ** (-jnp.arange(0, MLA_ROPE, 2, dtype=F32) / MLA_ROPE)
    ang = pos.astype(F32)[:, None] * inv_freq
    cos, sin = jnp.cos(ang), jnp.sin(ang)
    z = lambda n: jnp.zeros((pos.shape[0], n), F32)
    tc = jnp.concatenate([cos, cos, z(LANES - 2 * half)], axis=1)
    ta = jnp.concatenate([-sin, z(LANES - half)], axis=1)
    tb = jnp.concatenate([z(half), sin, z(LANES - 2 * half)], axis=1)
    return tc, ta, tb


def _rot(v, tc, ta, tb, sign):
    half = MLA_ROPE // 2
    return v * tc + sign * (pltpu.roll(v, LANES - half, 1) * ta + pltpu.roll(v, half, 1) * tb)


def _mla_assemble(qpre, kv, proj1, kr_blk, tabs, n_heads, name):
    s = qpre.shape[0]
    tr = _pick(s, (512, 256, 128))

    def body(qp_ref, kn_ref, kr_ref, tc_ref, ta_ref, tb_ref, qc_ref, kc_ref):
        tc, ta, tb = tc_ref[...], ta_ref[...], tb_ref[...]
        qc_ref[:, :LANES] = qp_ref[:, :LANES]
        qc_ref[:, LANES:] = _rot(qp_ref[:, LANES:], tc, ta, tb, 1.0)
        kc_ref[:, :LANES] = kn_ref[...]
        kc_ref[:, LANES:] = _rot(kr_ref[...], tc, ta, tb, 1.0)

    tab = pl.BlockSpec((tr, LANES), lambda i, hh: (i, 0))
    wide = pl.BlockSpec((tr, MLA_QK), lambda i, hh: (i, hh))
    shp = jax.ShapeDtypeStruct((s, n_heads * MLA_QK), F32)
    return pl.pallas_call(
        body,
        grid=(s // tr, n_heads),
        in_specs=[
            wide,
            pl.BlockSpec((tr, LANES), lambda i, hh: (i, 2 * hh)),
            pl.BlockSpec((tr, LANES), lambda i, hh: (i, kr_blk)),
            tab,
            tab,
            tab,
        ],
        out_specs=[wide, wide],
        out_shape=[shp, shp],
        compiler_params=_cparams(("parallel", "parallel")),
        name=name,
    )(qpre, kv, proj1, *tabs)


def _mla_disassemble(dqcat, dkcat, dv, tabs, n_heads, name):
    s = dqcat.shape[0]
    tr = _pick(s, (512, 256, 128))

    def body(dq_ref, dk_ref, dv_ref, tc_ref, ta_ref, tb_ref, dqp_ref, dkv_ref, dkr_ref):
        hh = pl.program_id(1)
        tc, ta, tb = tc_ref[...], ta_ref[...], tb_ref[...]
        dqp_ref[:, :LANES] = dq_ref[:, :LANES].astype(BF16)
        dqp_ref[:, LANES:] = _rot(dq_ref[:, LANES:], tc, ta, tb, -1.0).astype(BF16)
        dkv_ref[:, :LANES] = dk_ref[:, :LANES].astype(BF16)
        dkv_ref[:, LANES:] = dv_ref[...].astype(BF16)
        part = dk_ref[:, LANES:]

        @pl.when(hh == 0)
        def _():
            dkr_ref[...] = part

        @pl.when(hh > 0)
        def _():
            dkr_ref[...] += part

        @pl.when(hh == n_heads - 1)
        def _():
            dkr_ref[...] = _rot(dkr_ref[...], tc, ta, tb, -1.0)

    tab = pl.BlockSpec((tr, LANES), lambda i, hh: (i, 0))
    wide = pl.BlockSpec((tr, MLA_QK), lambda i, hh: (i, hh))
    shp = jax.ShapeDtypeStruct((s, n_heads * MLA_QK), BF16)
    return pl.pallas_call(
        body,
        grid=(s // tr, n_heads),
        in_specs=[wide, wide, pl.BlockSpec((tr, LANES), lambda i, hh: (i, hh)), tab, tab, tab],
        out_specs=[wide, wide, tab],
        out_shape=[shp, shp, jax.ShapeDtypeStruct((s, LANES), F32)],
        compiler_params=_cparams(("parallel", "arbitrary")),
        name=name,
    )(dqcat, dkcat, dv, *tabs)


def _forget_scan(proj2, f_blk, bias, name):
    s = proj2.shape[0]
    n = LANES

    def body(f_ref, b_ref, c_ref):
        rows, cols = _iotas(n, n)
        tri = (cols <= rows).astype(BF16)

        def step(j, carry):
            r0 = pl.multiple_of(j * n, n)
            f = f_ref[pl.ds(r0, n), :] + b_ref[...]
            lf = jnp.minimum(f, 0.0) - jnp.log1p(jnp.exp(-jnp.abs(f)))
            a, b, c = _split3(lf)
            cs = _dot(tri, a, "nn") + _dot(tri, b, "nn") + _dot(tri, c, "nn") + carry
            c_ref[pl.ds(r0, n), :] = cs
            return cs[n - 1 : n, :]

        lax.fori_loop(0, s // n, step, jnp.zeros((1, n), F32))

    return pl.pallas_call(
        body,
        grid=(1,),
        in_specs=[pl.BlockSpec((s, n), lambda i: (0, f_blk)), pl.BlockSpec((1, n), lambda i: (0, 0))],
        out_specs=pl.BlockSpec((s, n), lambda i: (0, 0)),
        out_shape=jax.ShapeDtypeStruct((s, n), F32),
        compiler_params=_cparams(("arbitrary",)),
        name=name,
    )(proj2, bias)


def _forget_scan_bwd(dc_col, dc_row, proj2, f_blk, bias, n_heads, name):
    s = proj2.shape[0]
    n = LANES
    nb = s // n

    def body(dcc_ref, dcr_ref, f_ref, b_ref, df_ref, db_ref):
        rows, cols = _iotas(n, n)
        tri = (cols >= rows).astype(BF16)
        live = cols < n_heads

        def step(j, carry):
            acc, dbv = carry
            r0 = pl.multiple_of((nb - 1 - j) * n, n)
            a, b, c = _split3(dcc_ref[pl.ds(r0, n), :] + dcr_ref[pl.ds(r0, n), :])
            dl = _dot(tri, a, "nn") + _dot(tri, b, "nn") + _dot(tri, c, "nn") + acc
            f = f_ref[pl.ds(r0, n), :] + b_ref[...]
            df = jnp.where(live, dl / (1.0 + jnp.exp(f)), 0.0)
            df_ref[pl.ds(r0, n), :] = df.astype(BF16)
            return dl[0:1, :], dbv + jnp.sum(df, axis=0, keepdims=True)

        z = jnp.zeros((1, n), F32)
        _, dbv = lax.fori_loop(0, nb, step, (z, z))
        db_ref[...] = dbv

    return pl.pallas_call(
        body,
        grid=(1,),
        in_specs=[
            pl.BlockSpec((s, n), lambda i: (0, 0)),
            pl.BlockSpec((s, n), lambda i: (0, 0)),
            pl.BlockSpec((s, n), lambda i: (0, f_blk)),
            pl.BlockSpec((1, n), lambda i: (0, 0)),
        ],
        out_specs=[pl.BlockSpec((s, n), lambda i: (0, 0)), pl.BlockSpec((1, n), lambda i: (0, 0))],
        out_shape=[jax.ShapeDtypeStruct((s, n), BF16), jax.ShapeDtypeStruct((1, n), F32)],
        compiler_params=_cparams(("arbitrary",)),
        name=name,
    )(dc_col, dc_row, proj2, bias)


def _adamw(w, g, m, v, name):
    r, c = w.shape
    tr = _pick(r, (128, 64, 32, 16, 8))
    c1 = 1.0 - ADAM_B1**ADAM_STEP
    c2 = 1.0 - ADAM_B2**ADAM_STEP

    def body(w_ref, g_ref, m_ref, v_ref, d_ref, mo_ref, vo_ref):
        gv = g_ref[...]
        mn = ADAM_B1 * m_ref[...] + (1.0 - ADAM_B1) * gv
        vn = ADAM_B2 * v_ref[...] + (1.0 - ADAM_B2) * (gv * gv)
        mo_ref[...] = mn
        vo_ref[...] = vn
        d_ref[...] = -ADAM_LR * ((mn / c1) / (jnp.sqrt(vn / c2) + ADAM_EPS) + ADAM_WD * w_ref[...])

    blk = pl.BlockSpec((tr, c), lambda i: (i, 0))
    shp = jax.ShapeDtypeStruct((r, c), F32)
    return pl.pallas_call(
        body,
        grid=(r // tr,),
        in_specs=[blk, blk, blk, blk],
        out_specs=[blk, blk, blk],
        out_shape=[shp, shp, shp],
        compiler_params=_cparams(("parallel",)),
        name=name,
    )(w, g, m, v)


def _mesh_pos():
    return lax.axis_index("x"), lax.axis_index("y"), lax.axis_index("c")


def _other_chips(x, y):
    return [(1 - x, y), (x, 1 - y), (1 - x, 1 - y)]


ANY = pl.BlockSpec(memory_space=pl.ANY)


def _gather_weights(wp):
    rp, wd = wp.shape
    half = rp // 2

    def body(w_ref, out_ref, send_sems, recv_sems):
        x, y, c = _mesh_pos()
        me = 2 * x + y
        chips = _other_chips(x, y)

        def region(chip, hc):
            return out_ref.at[chip, pl.ds(hc * half, half), :]

        def copy(k, src, dst, to):
            return pltpu.make_async_remote_copy(
                src_ref=src, dst_ref=dst, send_sem=send_sems.at[k], recv_sem=recv_sems.at[k], device_id=to, device_id_type=MESH
            )

        first = [copy(j, w_ref.at[pl.ds(c * half, half), :], region(me, c), (cx, cy, c)) for j, (cx, cy) in enumerate(chips)]
        for cp in first:
            cp.start()
        passed = [copy(3 + j, region(2 * cx + cy, c), region(2 * cx + cy, c), (x, y, 1 - c)) for j, (cx, cy) in enumerate(chips)]
        for j, (cx, cy) in enumerate(chips):
            copy(j, region(2 * cx + cy, c), region(2 * cx + cy, c), (x, y, c)).wait_recv()
            passed[j].start()
        for j, (cx, cy) in enumerate(chips):
            copy(3 + j, region(2 * cx + cy, 1 - c), region(2 * cx + cy, 1 - c), (x, y, c)).wait_recv()
        for cp in first + passed:
            cp.wait_send()

    return pl.pallas_call(
        body,
        in_specs=[ANY],
        out_specs=ANY,
        out_shape=jax.ShapeDtypeStruct((N_CHIPS, rp, wd), wp.dtype),
        scratch_shapes=[pltpu.SemaphoreType.DMA((6,)), pltpu.SemaphoreType.DMA((6,))],
        name="gather_weights",
    )(wp)


def _place_own(wall, wp, pos):
    rp, wd = wp.shape

    def body(x_ref, y_ref, c_ref, wall_ref, w_ref, o_ref):
        o_ref[0] = w_ref[...]

    grid_spec = pltpu.PrefetchScalarGridSpec(
        num_scalar_prefetch=3,
        grid=(rp // PACK_TR,),
        in_specs=[ANY, pl.BlockSpec((PACK_TR, wd), lambda i, xr, yr, cr: (i, 0))],
        out_specs=pl.BlockSpec((1, PACK_TR, wd), lambda i, xr, yr, cr: (2 * xr[0] + yr[0], i, 0)),
    )
    return pl.pallas_call(
        body,
        grid_spec=grid_spec,
        out_shape=jax.ShapeDtypeStruct(wall.shape, wall.dtype),
        input_output_aliases={3: 0},
        compiler_params=_cparams(("parallel",)),
        name="place_own_shard",
    )(*pos, wall, wp)


def _pair_exchange(g):
    _, rp, wd = g.shape
    half = rp // 2

    def body(g_ref, out_ref, send_sem, recv_sem):
        x, y, c = _mesh_pos()
        cp = pltpu.make_async_remote_copy(
            src_ref=g_ref.at[:, pl.ds((1 - c) * half, half), :],
            dst_ref=out_ref,
            send_sem=send_sem,
            recv_sem=recv_sem,
            device_id=(x, y, 1 - c),
            device_id_type=MESH,
        )
        cp.start()
        cp.wait()

    return pl.pallas_call(
        body,
        in_specs=[ANY],
        out_specs=ANY,
        out_shape=jax.ShapeDtypeStruct((N_CHIPS, half, wd), g.dtype),
        scratch_shapes=[pltpu.SemaphoreType.DMA, pltpu.SemaphoreType.DMA],
        name="rs_pair_exchange",
    )(g)


def _pair_add(g, recv, pos):
    _, rp, wd = g.shape
    half = rp // 2
    nb = half // PACK_TR

    def body(x_ref, y_ref, c_ref, g_ref, r_ref, o_ref):
        o_ref[...] = (g_ref[...] + r_ref[...]).astype(BF16)

    blk = (1, PACK_TR, wd)
    grid_spec = pltpu.PrefetchScalarGridSpec(
        num_scalar_prefetch=3,
        grid=(N_CHIPS, nb),
        in_specs=[
            pl.BlockSpec(blk, lambda j, i, xr, yr, cr: (j, cr[0] * nb + i, 0)),
            pl.BlockSpec(blk, lambda j, i, xr, yr, cr: (j, i, 0)),
        ],
        out_specs=pl.BlockSpec(blk, lambda j, i, xr, yr, cr: (j, i, 0)),
    )
    return pl.pallas_call(
        body,
        grid_spec=grid_spec,
        out_shape=jax.ShapeDtypeStruct((N_CHIPS, half, wd), BF16),
        compiler_params=_cparams(("parallel", "parallel")),
        name="rs_pair_add",
    )(*pos, g, recv)


def _chip_exchange(sp):
    _, rh, wd = sp.shape

    def body(s_ref, out_ref, send_sems, recv_sems):
        x, y, c = _mesh_pos()
        me = 2 * x + y
        chips = _other_chips(x, y)
        sends = []
        for j, (cx, cy) in enumerate(chips):
            cp = pltpu.make_async_remote_copy(
                src_ref=s_ref.at[2 * cx + cy],
                dst_ref=out_ref.at[me],
                send_sem=send_sems.at[j],
                recv_sem=recv_sems.at[j],
                device_id=(cx, cy, c),
                device_id_type=MESH,
            )
            cp.start()
            sends.append(cp)
        for j, (cx, cy) in enumerate(chips):
            pltpu.make_async_remote_copy(
                src_ref=s_ref.at[me],
                dst_ref=out_ref.at[2 * cx + cy],
                send_sem=send_sems.at[j],
                recv_sem=recv_sems.at[j],
                device_id=(x, y, c),
                device_id_type=MESH,
            ).wait_recv()
        for cp in sends:
            cp.wait_send()

    return pl.pallas_call(
        body,
        in_specs=[ANY],
        out_specs=ANY,
        out_shape=jax.ShapeDtypeStruct(sp.shape, sp.dtype),
        scratch_shapes=[pltpu.SemaphoreType.DMA((3,)), pltpu.SemaphoreType.DMA((3,))],
        name="rs_chip_exchange",
    )(sp)


def _sum_slots(own, slots, pos):
    _, rh, wd = slots.shape
    nb = rh // PACK_TR

    def body(x_ref, y_ref, c_ref, own_ref, a_ref, b_ref, d_ref, o_ref):
        f = lambda r: r[0].astype(F32)
        o_ref[...] = ((f(own_ref) + f(a_ref)) + f(b_ref)) + f(d_ref)

    blk = (1, PACK_TR, wd)

    def other(k):
        return pl.BlockSpec(blk, lambda i, xr, yr, cr: (k + (k >= 2 * xr[0] + yr[0]).astype(jnp.int32), i, 0))

    grid_spec = pltpu.PrefetchScalarGridSpec(
        num_scalar_prefetch=3,
        grid=(nb,),
        in_specs=[pl.BlockSpec(blk, lambda i, xr, yr, cr: (2 * xr[0] + yr[0], i, 0)), other(0), other(1), other(2)],
        out_specs=pl.BlockSpec((PACK_TR, wd), lambda i, xr, yr, cr: (cr[0] * nb + i, 0)),
    )
    return pl.pallas_call(
        body,
        grid_spec=grid_spec,
        out_shape=jax.ShapeDtypeStruct((2 * rh, wd), F32),
        compiler_params=_cparams(("parallel",)),
        name="rs_sum_slots",
    )(*pos, own, slots, slots, slots)


def _pair_gather(t):
    rh = t.shape[0] // 2

    def body(t_ref, out_ref, send_sem, recv_sem):
        x, y, c = _mesh_pos()
        cp = pltpu.make_async_remote_copy(
            src_ref=t_ref.at[pl.ds(c * rh, rh), :],
            dst_ref=out_ref.at[pl.ds(c * rh, rh), :],
            send_sem=send_sem,
            recv_sem=recv_sem,
            device_id=(x, y, 1 - c),
            device_id_type=MESH,
        )
        cp.start()
        cp.wait_send()
        pltpu.make_async_remote_copy(
            src_ref=t_ref.at[pl.ds((1 - c) * rh, rh), :],
            dst_ref=out_ref.at[pl.ds((1 - c) * rh, rh), :],
            send_sem=send_sem,
            recv_sem=recv_sem,
            device_id=(x, y, c),
            device_id_type=MESH,
        ).wait_recv()

    return pl.pallas_call(
        body,
        in_specs=[ANY],
        out_specs=ANY,
        out_shape=jax.ShapeDtypeStruct(t.shape, t.dtype),
        input_output_aliases={0: 0},
        scratch_shapes=[pltpu.SemaphoreType.DMA, pltpu.SemaphoreType.DMA],
        name="rs_pair_gather",
    )(t)


def _allreduce_small(v):
    shape = v.shape
    n_dev = 8

    def body(v_ref, o_ref, slots, send_sems, recv_sems):
        x, y, c = _mesh_pos()
        me = 4 * x + 2 * y + c
        slots[me] = v_ref[...]
        sends = []
        for k in range(1, n_dev):
            fx, fy, fc = (k >> 2) & 1, (k >> 1) & 1, k & 1
            to = (x ^ fx, y ^ fy, c ^ fc)
            cp = pltpu.make_async_remote_copy(
                src_ref=v_ref,
                dst_ref=slots.at[me],
                send_sem=send_sems.at[k - 1],
                recv_sem=recv_sems.at[k - 1],
                device_id=to,
                device_id_type=MESH,
            )
            cp.start()
            sends.append(cp)
        for k in range(1, n_dev):
            fx, fy, fc = (k >> 2) & 1, (k >> 1) & 1, k & 1
            frm = 4 * (x ^ fx) + 2 * (y ^ fy) + (c ^ fc)
            pltpu.make_async_remote_copy(
                src_ref=v_ref,
                dst_ref=slots.at[frm],
                send_sem=send_sems.at[k - 1],
                recv_sem=recv_sems.at[k - 1],
                device_id=(x, y, c),
                device_id_type=MESH,
            ).wait_recv()
        for cp in sends:
            cp.wait_send()
        acc = slots[0]
        for k in range(1, n_dev):
            acc = acc + slots[k]
        o_ref[...] = acc

    vm = pl.BlockSpec(memory_space=pltpu.VMEM)
    return pl.pallas_call(
        body,
        in_specs=[vm],
        out_specs=vm,
        out_shape=jax.ShapeDtypeStruct(shape, F32),
        scratch_shapes=[pltpu.VMEM((n_dev,) + shape, F32), pltpu.SemaphoreType.DMA((n_dev - 1,)), pltpu.SemaphoreType.DMA((n_dev - 1,))],
        name="allreduce_small",
    )(v)


def _pack_layout(shard_shapes):
    offs, rows = [], []
    off = 0
    for r, c in shard_shapes:
        assert (r * c) % PACK_W == 0
        n = r * c // PACK_W
        offs.append(off)
        rows.append(n)
        off += -(-n // 16) * 16
    rp = -(-off // (2 * PACK_TR)) * (2 * PACK_TR)
    return offs, rows, rp


def _pack_rows(parts, offs, rows, rp, lead):
    ends = list(offs[1:]) + [rp]
    nolead = ((0, 0),) * len(lead)
    out = [jnp.pad(p, nolead + ((0, e - o - n), (0, 0))) for p, o, n, e in zip(parts, offs, rows, ends)]
    return jnp.concatenate(out, axis=len(lead))


def kernel(x, positions, ln0, w_in0, w_out0, ln1, w_in1, q_norm1, w_qb1, kv_norm1, w_kvb1, w_out1, ln2, w_in2, b_f2, w_out2, ln3, w_in3, w_out3, final_norm, loss_target, m_ln0, m_w_in0, m_w_out0, m_ln1, m_w_in1, m_q_norm1, m_w_qb1, m_kv_norm1, m_w_kvb1, m_w_out1, m_ln2, m_w_in2, m_b_f2, m_w_out2, m_ln3, m_w_in3, m_w_out3, m_final_norm, v_ln0, v_w_in0, v_w_out0, v_ln1, v_w_in1, v_q_norm1, v_w_qb1, v_kv_norm1, v_w_kvb1, v_w_out1, v_ln2, v_w_in2, v_b_f2, v_w_out2, v_ln3, v_w_in3, v_w_out3, v_final_norm):
    xs = x[0]
    s, d = xs.shape
    di = 4 * w_out0.shape[0]
    nh = di // HEAD_DIM
    idx = tuple(lax.axis_index(a).astype(jnp.int32).reshape(1) for a in ("x", "y", "c"))

    big = [w_in0, w_out0, w_in1, w_qb1, w_kvb1, w_out1, w_in2, w_out2, w_in3, w_out3]
    col_sharded = [True, False, True, True, True, False, True, False, True, False]
    shard_shapes = [w.shape for w in big]
    offs, rows, rp = _pack_layout(shard_shapes)
    wp = _pack_rows([w.astype(BF16).reshape(n, PACK_W) for w, n in zip(big, rows)], offs, rows, rp, ())
    wall = _place_own(_gather_weights(wp), wp, idx)
    full = []
    for (r, c), o, n, cs in zip(shard_shapes, offs, rows, col_sharded):
        slab = wall[:, o : o + n, :].reshape(N_CHIPS, r, c)
        full.append(slab.transpose(1, 0, 2).reshape(r, N_CHIPS * c) if cs else slab.reshape(N_CHIPS * r, c))
    f_in0, f_out0, f_in1, f_qb1, f_kvb1, f_out1, f_in2, f_out2, f_in3, f_out3 = full

    i_kr = MLA_Q_RANK + MLA_KV_RANK + MLA_ROPE
    w1p = jnp.concatenate([f_in1[:, i_kr:], f_in1[:, :i_kr], jnp.zeros((d, LANES - MLA_ROPE), BF16)], axis=1)
    qlat_blk = di // MLA_Q_RANK
    kvlat_blk = (di + MLA_Q_RANK) // MLA_KV_RANK
    kr_blk = (di + MLA_Q_RANK + MLA_KV_RANK) // LANES
    qk_w = HEAD_DIM + MLA_ROPE
    wqbp = jnp.pad(f_qb1.reshape(MLA_Q_RANK, nh, qk_w), ((0, 0), (0, 0), (0, MLA_QK - qk_w))).reshape(MLA_Q_RANK, nh * MLA_QK)
    n2 = f_in2.shape[1]
    w2p = jnp.pad(f_in2, ((0, 0), (0, 4 * di + LANES - n2)))
    b2p = jnp.pad(b_f2, (0, LANES - nh)).reshape(1, LANES)

    row = lambda v: v.reshape(1, -1)
    tabs = _rope_tables(positions[0])

    def sb_layer_fwd(xin, ln, w_in, w_out, tag):
        h = _rmsnorm_fwd(xin, row(ln), f"norm_fwd_{tag}")
        proj = _matmul(h, w_in, "nn", f"proj_in_{tag}")
        o, lt = _sb_fwd(proj, nh, f"sb_fwd_{tag}")
        g = _gate_fwd(o, proj, 3, f"gate_fwd_{tag}")
        xout = _matmul(g, w_out, "nn", f"proj_out_{tag}", res=xin)
        return xout, (xin, h, proj, o, lt, g)

    def sb_layer_bwd(dxn, dxnb, saved, ln, w_in, w_out, tag):
        xin, h, proj, o, lt, g = saved
        dgf = _matmul(dxnb, w_out, "nt", f"dgate_in_{tag}")
        dw_out = _matmul(g, dxnb, "tn", f"dw_out_{tag}")
        do, dgate = _gate_bwd(dgf, o, proj, 3, f"gate_bwd_{tag}")
        dq, dk, dv = _sb_bwd(proj, lt, do, nh, f"sb_bwd_{tag}")
        dproj = jnp.concatenate([dq.astype(BF16), dk.astype(BF16), dv.astype(BF16), dgate], axis=1)
        dh = _matmul(dproj, w_in, "nt", f"dh_{tag}")
        dw_in = _matmul(h, dproj, "tn", f"dw_in_{tag}")
        dx, dxb, dln = _rmsnorm_bwd(xin, row(ln), dh, f"norm_bwd_{tag}", dres=dxn)
        return dx, dxb, dln, dw_in, dw_out

    x1, sv0 = sb_layer_fwd(xs, ln0, f_in0, f_out0, "l0")

    h1 = _rmsnorm_fwd(x1, row(ln1), "norm_fwd_l1")
    proj1 = _matmul(h1, w1p, "nn", "proj_in_l1")
    qn = _rmsnorm_fwd(proj1, row(q_norm1), "qnorm_fwd_l1", col_block=qlat_blk)
    kvn = _rmsnorm_fwd(proj1, row(kv_norm1), "kvnorm_fwd_l1", col_block=kvlat_blk)
    qpre = _matmul(qn, wqbp, "nn", "q_up_l1")
    kv1 = _matmul(kvn, f_kvb1, "nn", "kv_up_l1")
    qcat, kcat = _mla_assemble(qpre, kv1, proj1, kr_blk, tabs, nh, "mla_assemble_l1")
    mla_blk = (lambda hh: hh, lambda hh: hh, lambda hh: 2 * hh + 1)
    o1, lse1 = _sm_fwd(qcat, kcat, kv1, nh, MLA_QK, *mla_blk, "chunk", "mla_fwd_l1")
    g1 = _gate_fwd(o1, proj1, 0, "gate_fwd_l1")
    x2 = _matmul(g1, f_out1, "nn", "proj_out_l1", res=x1)

    h2 = _rmsnorm_fwd(x2, row(ln2), "norm_fwd_l2")
    proj2 = _matmul(h2, w2p, "nn", "proj_in_l2")
    f_blk = 4 * di // LANES
    cum = _forget_scan(proj2, f_blk, b2p, "forget_scan_l2")
    cum_h = cum[:, :nh].T
    t_sm = min(SM_TK, s)
    crow = cum_h.reshape(nh, s, 1)
    ccol = cum_h.reshape(nh, s // t_sm, t_sm)
    fg_blk = (lambda hh: hh, lambda hh: nh + hh, lambda hh: 2 * nh + hh)
    o2, lse2 = _sm_fwd(proj2, proj2, proj2, nh, HEAD_DIM, *fg_blk, "causal", "forget_fwd_l2", crow=crow, ccol=ccol)
    g2 = _gate_fwd(o2, proj2, 3, "gate_fwd_l2")
    x3 = _matmul(g2, f_out2, "nn", "proj_out_l2", res=x2)

    x4, sv3 = sb_layer_fwd(x3, ln3, f_in3, f_out3, "l3")

    dx, dxb, d_final, loss_part = _loss_head(x4, row(final_norm), loss_target[0], "loss_head")
    dx, dxb, d_ln3, dw_in3, dw_out3 = sb_layer_bwd(dx, dxb, sv3, ln3, f_in3, f_out3, "l3")

    dgf2 = _matmul(dxb, f_out2, "nt", "dgate_in_l2")
    dw_out2 = _matmul(g2, dxb, "tn", "dw_out_l2")
    do2, dgate2 = _gate_bwd(dgf2, o2, proj2, 3, "gate_bwd_l2")
    dq2, dk2, dv2, dcc2, dcr2 = _sm_bwd(proj2, proj2, proj2, o2, do2, lse2, nh, HEAD_DIM, *fg_blk, "causal", "forget_bwd_l2", crow=crow, ccol=ccol)
    lanes_of = lambda a: jnp.pad(a.reshape(nh, s).T, ((0, 0), (0, LANES - nh)))
    df2, d_bf = _forget_scan_bwd(lanes_of(dcc2), lanes_of(dcr2), proj2, f_blk, b2p, nh, "forget_scan_bwd_l2")
    dproj2 = jnp.concatenate([dq2.astype(BF16), dk2.astype(BF16), dv2.astype(BF16), dgate2, df2], axis=1)
    dh2 = _matmul(dproj2, w2p, "nt", "dh_l2")
    dw_in2 = _matmul(h2, dproj2, "tn", "dw_in_l2")[:, :n2]
    dx, dxb, d_ln2 = _rmsnorm_bwd(x2, row(ln2), dh2, "norm_bwd_l2", dres=dx)

    dgf1 = _matmul(dxb, f_out1, "nt", "dgate_in_l1")
    dw_out1 = _matmul(g1, dxb, "tn", "dw_out_l1")
    do1, dgate1 = _gate_bwd(dgf1, o1, proj1, 0, "gate_bwd_l1")
    dqc, dkc, dv1 = _sm_bwd(qcat, kcat, kv1, o1, do1, lse1, nh, MLA_QK, *mla_blk, "chunk", "mla_bwd_l1")
    dqpre, dkv1, dkr = _mla_disassemble(dqc, dkc, dv1, tabs, nh, "mla_disassemble_l1")
    dqn = _matmul(dqpre, wqbp, "nt", "dqn_l1")
    dw_qbp = _matmul(qn, dqpre, "tn", "dw_qb_l1")
    dw_qb1 = dw_qbp.reshape(MLA_Q_RANK, nh, MLA_QK)[:, :, :qk_w].reshape(MLA_Q_RANK, nh * qk_w)
    dkvn = _matmul(dkv1, f_kvb1, "nt", "dkvn_l1")
    dw_kvb1 = _matmul(kvn, dkv1, "tn", "dw_kvb_l1")
    _, dqlat_b, d_qnorm = _rmsnorm_bwd(proj1, row(q_norm1), dqn, "qnorm_bwd_l1", col_block=qlat_blk)
    _, dkvlat_b, d_kvnorm = _rmsnorm_bwd(proj1, row(kv_norm1), dkvn, "kvnorm_bwd_l1", col_block=kvlat_blk)
    dproj1 = jnp.concatenate([dgate1, dqlat_b, dkvlat_b, dkr.astype(BF16)], axis=1)
    dh1 = _matmul(dproj1, w1p, "nt", "dh_l1")
    dw1p = _matmul(h1, dproj1, "tn", "dw_in_l1")
    dw_in1 = jnp.concatenate([dw1p[:, di : di + i_kr], dw1p[:, :di]], axis=1)
    dx, dxb, d_ln1 = _rmsnorm_bwd(x1, row(ln1), dh1, "norm_bwd_l1", dres=dx)

    dx, dxb, d_ln0, dw_in0, dw_out0 = sb_layer_bwd(dx, dxb, sv0, ln0, f_in0, f_out0, "l0")
    grad_x = dx.reshape(x.shape)

    dws = [dw_in0, dw_out0, dw_in1, dw_qb1, dw_kvb1, dw_out1, dw_in2, dw_out2, dw_in3, dw_out3]
    parts = []
    for g, (r, c), n, cs in zip(dws, shard_shapes, rows, col_sharded):
        g4 = g.reshape(r, N_CHIPS, c).transpose(1, 0, 2) if cs else g.reshape(N_CHIPS, r, c)
        parts.append(g4.reshape(N_CHIPS, n, PACK_W))
    gp = _pack_rows(parts, offs, rows, rp, (N_CHIPS,))
    sib = _pair_exchange(gp)
    pair = _pair_add(gp, sib, idx)
    slots = _chip_exchange(pair)
    gred = _pair_gather(_sum_slots(pair, slots, idx))
    big_grads = [gred[o : o + n, :].reshape(r, c) for (r, c), o, n in zip(shard_shapes, offs, rows)]

    small = [ln0, ln1, q_norm1, kv_norm1, ln2, b_f2, ln3, final_norm]
    small_g = [d_ln0[0], d_ln1[0], d_qnorm[0], d_kvnorm[0], d_ln2[0], d_bf[0, :nh], d_ln3[0], d_final[0]]
    n_small = SMALL_SHAPE[0] * SMALL_SHAPE[1]
    used = sum(v.shape[0] for v in small) + 1
    assert used <= n_small

    def pack_small(vs, last):
        return jnp.concatenate(list(vs) + [last, jnp.zeros((n_small - used,), F32)]).reshape(SMALL_SHAPE)

    sm_sum = _allreduce_small(pack_small(small_g, loss_part[0, :1]))
    flat = sm_sum.reshape(-1)
    loss = flat[used - 1]

    big_m = [m_w_in0, m_w_out0, m_w_in1, m_w_qb1, m_w_kvb1, m_w_out1, m_w_in2, m_w_out2, m_w_in3, m_w_out3]
    big_v = [v_w_in0, v_w_out0, v_w_in1, v_w_qb1, v_w_kvb1, v_w_out1, v_w_in2, v_w_out2, v_w_in3, v_w_out3]
    big_names = ["w_in0", "w_out0", "w_in1", "w_qb1", "w_kvb1", "w_out1", "w_in2", "w_out2", "w_in3", "w_out3"]
    big_upd = [_adamw(w, g, m, v, f"adamw_{nm}") for w, g, m, v, nm in zip(big, big_grads, big_m, big_v, big_names)]

    small_m = [m_ln0, m_ln1, m_q_norm1, m_kv_norm1, m_ln2, m_b_f2, m_ln3, m_final_norm]
    small_v = [v_ln0, v_ln1, v_q_norm1, v_kv_norm1, v_ln2, v_b_f2, v_ln3, v_final_norm]
    one = jnp.ones((1,), F32)
    sd, smn, svn = _adamw(pack_small(small, one), sm_sum, pack_small(small_m, one), pack_small(small_v, one), "adamw_small")

    def unpack_small(p):
        out, at = [], 0
        fl = p.reshape(-1)
        for v in small:
            out.append(fl[at : at + v.shape[0]])
            at += v.shape[0]
        return out

    sg_l, sd_l, sm_l, sv_l = unpack_small(sm_sum), unpack_small(sd), unpack_small(smn), unpack_small(svn)

    order = ["ln0", "w_in0", "w_out0", "ln1", "w_in1", "q_norm1", "w_qb1", "kv_norm1", "w_kvb1", "w_out1", "ln2", "w_in2", "b_f2", "w_out2", "ln3", "w_in3", "w_out3", "final_norm"]
    small_names = ["ln0", "ln1", "q_norm1", "kv_norm1", "ln2", "b_f2", "ln3", "final_norm"]
    grads, deltas, new_m, new_v = {}, {}, {}, {}
    for nm, g, (dl, mn, vn) in zip(big_names, big_grads, big_upd):
        grads[nm], deltas[nm], new_m[nm], new_v[nm] = g, dl, mn, vn
    for nm, g, dl, mn, vn in zip(small_names, sg_l, sd_l, sm_l, sv_l):
        grads[nm], deltas[nm], new_m[nm], new_v[nm] = g, dl, mn, vn
    return (loss, grad_x, *[grads[n] for n in order], *[deltas[n] for n in order], *[new_m[n] for n in order], *[new_v[n] for n in order])
```

```python
import functools

import jax
import jax.numpy as jnp
from jax import lax
from jax.experimental import pallas as pl
from jax.experimental.pallas import tpu as pltpu

F32 = jnp.float32
BF16 = jnp.bfloat16
EPS = 1e-6
NEG = -1e30
HEAD_DIM = 128
CHUNK_SHIFT = 6
MLA_Q_RANK = 256
MLA_KV_RANK = 128
MLA_ROPE = 64
MLA_QK = 256
ROPE_BASE = 10000.0
ADAM_LR = 0.001
ADAM_B1 = 0.9
ADAM_B2 = 0.999
ADAM_EPS = 1e-08
ADAM_WD = 0.01
ADAM_STEP = 10
VMEM_LIMIT_BYTES = 56 * 2**20
LANES = 128
PACK_W = 1024
PACK_TR = 256
SMALL_SHAPE = (8, 768)
MESH = pl.DeviceIdType.MESH
N_CHIPS = 4


def _pick(n, cands):
    for c in cands:
        if n % c == 0:
            return c
    return n


def _cparams(sem):
    return pltpu.CompilerParams(dimension_semantics=sem, vmem_limit_bytes=VMEM_LIMIT_BYTES)


def _dot(a, b, dims):
    dn = {"nn": (((1,), (0,)), ((), ())), "nt": (((1,), (1,)), ((), ())), "tn": (((0,), (0,)), ((), ()))}[dims]
    return lax.dot_general(a, b, dn, preferred_element_type=F32)


def _matmul(a, b, dims, name, res=None):
    if dims == "nn":
        (m, k), (k2, n) = a.shape, b.shape
    elif dims == "nt":
        (m, k), (n, k2) = a.shape, b.shape
    else:
        (k, m), (k2, n) = a.shape, b.shape
    assert k == k2, (a.shape, b.shape, dims)
    tm = _pick(m, (1024, 512, 256, 128))
    tn = _pick(n, (1024, 640, 512, 384, 256, 128))
    tk = _pick(k, (1024, 640, 512, 256, 128))
    nk = k // tk

    def body(*refs):
        if res is None:
            a_ref, b_ref, o_ref = refs
            r_ref = None
        else:
            a_ref, b_ref, r_ref, o_ref = refs
        kk = pl.program_id(2)
        p = _dot(a_ref[...].astype(BF16), b_ref[...].astype(BF16), dims)

        @pl.when(kk == 0)
        def _():
            o_ref[...] = p if r_ref is None else p + r_ref[...]

        @pl.when(kk > 0)
        def _():
            o_ref[...] += p

    a_spec = pl.BlockSpec((tk, tm), lambda i, j, kk: (kk, i)) if dims == "tn" else pl.BlockSpec((tm, tk), lambda i, j, kk: (i, kk))
    b_spec = pl.BlockSpec((tn, tk), lambda i, j, kk: (j, kk)) if dims == "nt" else pl.BlockSpec((tk, tn), lambda i, j, kk: (kk, j))
    o_spec = pl.BlockSpec((tm, tn), lambda i, j, kk: (i, j))
    in_specs = [a_spec, b_spec] + ([] if res is None else [o_spec])
    args = (a, b) + (() if res is None else (res,))
    return pl.pallas_call(
        body,
        grid=(m // tm, n // tn, nk),
        in_specs=in_specs,
        out_specs=o_spec,
        out_shape=jax.ShapeDtypeStruct((m, n), F32),
        compiler_params=_cparams(("parallel", "parallel", "arbitrary")),
        name=name,
    )(*args)


def _rmsnorm_fwd(x, g, name, col_block=0):
    s = x.shape[0]
    w = g.shape[1]
    tr = _pick(s, (512, 256, 128))

    def body(x_ref, g_ref, h_ref):
        xv = x_ref[...]
        r = lax.rsqrt(jnp.mean(xv * xv, axis=-1, keepdims=True) + EPS)
        h_ref[...] = ((xv * r) * g_ref[...]).astype(BF16)

    return pl.pallas_call(
        body,
        grid=(s // tr,),
        in_specs=[pl.BlockSpec((tr, w), lambda i: (i, col_block)), pl.BlockSpec((1, w), lambda i: (0, 0))],
        out_specs=pl.BlockSpec((tr, w), lambda i: (i, 0)),
        out_shape=jax.ShapeDtypeStruct((s, w), BF16),
        compiler_params=_cparams(("parallel",)),
        name=name,
    )(x, g)


def _rmsnorm_bwd(x, g, dh, name, col_block=0, dres=None):
    s = x.shape[0]
    w = g.shape[1]
    tr = _pick(s, (512, 256, 128))

    def body(*refs):
        if dres is None:
            x_ref, g_ref, dh_ref, dx_ref, dxb_ref, dg_ref = refs
        else:
            x_ref, g_ref, dh_ref, dr_ref, dx_ref, dxb_ref, dg_ref = refs
        i = pl.program_id(0)
        xv = x_ref[...]
        r = lax.rsqrt(jnp.mean(xv * xv, axis=-1, keepdims=True) + EPS)
        xh = xv * r
        dhv = dh_ref[...]
        dyg = dhv * g_ref[...]
        dx = r * (dyg - xh * jnp.mean(dyg * xh, axis=-1, keepdims=True))
        if dres is not None:
            dx = dx + dr_ref[...]
        dx_ref[...] = dx
        dxb_ref[...] = dx.astype(BF16)
        part = jnp.sum(dhv * xh, axis=0, keepdims=True)

        @pl.when(i == 0)
        def _():
            dg_ref[...] = part

        @pl.when(i > 0)
        def _():
            dg_ref[...] += part

    row = pl.BlockSpec((tr, w), lambda i: (i, 0))
    in_specs = [pl.BlockSpec((tr, w), lambda i: (i, col_block)), pl.BlockSpec((1, w), lambda i: (0, 0)), row]
    args = [x, g, dh]
    if dres is not None:
        in_specs.append(row)
        args.append(dres)
    return pl.pallas_call(
        body,
        grid=(s // tr,),
        in_specs=in_specs,
        out_specs=[row, row, pl.BlockSpec((1, w), lambda i: (0, 0))],
        out_shape=[jax.ShapeDtypeStruct((s, w), F32), jax.ShapeDtypeStruct((s, w), BF16), jax.ShapeDtypeStruct((1, w), F32)],
        compiler_params=_cparams(("arbitrary",)),
        name=name,
    )(*args)


def _loss_head(x, g, target, name):
    s, d = x.shape
    tr = _pick(s, (512, 256, 128))

    def body(x_ref, g_ref, t_ref, dx_ref, dxb_ref, dg_ref, loss_ref):
        i = pl.program_id(0)
        xv = x_ref[...]
        gv = g_ref[...]
        r = lax.rsqrt(jnp.mean(xv * xv, axis=-1, keepdims=True) + EPS)
        xh = xv * r
        err = xh * gv - t_ref[...]
        lpart = 0.5 * jnp.sum(jnp.mean(err * err, axis=-1, keepdims=True), axis=0, keepdims=True)
        dy = err / d
        dyg = dy * gv
        dx = r * (dyg - xh * jnp.mean(dyg * xh, axis=-1, keepdims=True))
        dx_ref[...] = dx
        dxb_ref[...] = dx.astype(BF16)
        part = jnp.sum(dy * xh, axis=0, keepdims=True)
        lrow = jnp.broadcast_to(lpart, (1, LANES))

        @pl.when(i == 0)
        def _():
            dg_ref[...] = part
            loss_ref[...] = lrow

        @pl.when(i > 0)
        def _():
            dg_ref[...] += part
            loss_ref[...] += lrow

    row = pl.BlockSpec((tr, d), lambda i: (i, 0))
    vec = pl.BlockSpec((1, d), lambda i: (0, 0))
    return pl.pallas_call(
        body,
        grid=(s // tr,),
        in_specs=[row, vec, row],
        out_specs=[row, row, vec, pl.BlockSpec((1, LANES), lambda i: (0, 0))],
        out_shape=[
            jax.ShapeDtypeStruct((s, d), F32),
            jax.ShapeDtypeStruct((s, d), BF16),
            jax.ShapeDtypeStruct((1, d), F32),
            jax.ShapeDtypeStruct((1, LANES), F32),
        ],
        compiler_params=_cparams(("arbitrary",)),
        name=name,
    )(x, g, target)


def _sigmoid(x):
    return 1.0 / (1.0 + jnp.exp(-x))


def _gate_fwd(o, proj, gate_blk, name):
    s, di = o.shape
    tr = _pick(s, (256, 128))

    def body(o_ref, gate_ref, g_ref):
        gt = gate_ref[...]
        g_ref[...] = (o_ref[...] * (gt * _sigmoid(gt))).astype(BF16)

    row = pl.BlockSpec((tr, di), lambda i: (i, 0))
    return pl.pallas_call(
        body,
        grid=(s // tr,),
        in_specs=[row, pl.BlockSpec((tr, di), lambda i: (i, gate_blk))],
        out_specs=row,
        out_shape=jax.ShapeDtypeStruct((s, di), BF16),
        compiler_params=_cparams(("parallel",)),
        name=name,
    )(o, proj)


def _gate_bwd(dg, o, proj, gate_blk, name):
    s, di = o.shape
    tr = _pick(s, (256, 128))

    def body(dg_ref, o_ref, gate_ref, do_ref, dgate_ref):
        gt = gate_ref[...]
        sg = _sigmoid(gt)
        dgv = dg_ref[...]
        do_ref[...] = dgv * (gt * sg)
        dgate_ref[...] = (dgv * o_ref[...] * (sg * (1.0 + gt * (1.0 - sg)))).astype(BF16)

    row = pl.BlockSpec((tr, di), lambda i: (i, 0))
    return pl.pallas_call(
        body,
        grid=(s // tr,),
        in_specs=[row, row, pl.BlockSpec((tr, di), lambda i: (i, gate_blk))],
        out_specs=[row, row],
        out_shape=[jax.ShapeDtypeStruct((s, di), F32), jax.ShapeDtypeStruct((s, di), BF16)],
        compiler_params=_cparams(("parallel",)),
        name=name,
    )(dg, o, proj)


def _iotas(tq, tk):
    return lax.broadcasted_iota(jnp.int32, (tq, tk), 0), lax.broadcasted_iota(jnp.int32, (tq, tk), 1)


def _softplus(s):
    return jnp.maximum(s, 0.0) + jnp.log(1.0 + jnp.exp(-jnp.abs(s)))


def _split2(v):
    hi = v.astype(BF16)
    lo = (v - hi.astype(F32)).astype(BF16)
    return hi, lo


def _cat2(v):
    return jnp.concatenate(_split2(v), axis=1)


def _tri2(keep):
    m = keep.astype(BF16)
    return jnp.concatenate([m, m], axis=0)


def _split3(v):
    a = v.astype(BF16)
    r1 = v - a.astype(F32)
    b = r1.astype(BF16)
    c = (r1 - b.astype(F32)).astype(BF16)
    return a, b, c


SB_TQ = 512
SB_TK = 128
SB_UNROLL = 4


def _sb_fwd(proj, n_heads, name):
    s = proj.shape[0]
    d = HEAD_DIM
    t = min(SB_TQ, s)
    tk = min(SB_TK, s)
    r = t // tk
    un = SB_UNROLL if r % SB_UNROLL == 0 else 1
    scale = d**-0.5

    def body(q_ref, k_ref, v_ref, o_ref, lt_ref, kb_ref, vb_ref):
        i = pl.program_id(1)

        @pl.when(i == 0)
        def _():
            kb_ref[...] = k_ref[...].astype(BF16)
            vb_ref[...] = v_ref[...].astype(BF16)

        q = (q_ref[...] * scale).astype(BF16)
        rows, cols = _iotas(t, tk)
        trows, tcols = _iotas(tk, tk)
        tri = (trows > tcols).astype(BF16)

        def block(kb, cl, acc, diag):
            k0 = pl.multiple_of(kb * tk, tk)
            sc = _dot(q, kb_ref[pl.ds(k0, tk), :], "nt")
            sp = _softplus(sc)
            ls = -sp
            if diag is not None:
                strict = cols + diag * tk < rows
                ls = jnp.where(strict, ls, 0.0)
            hi, lo = _split2(ls)
            later = _dot(hi, tri, "nn") + _dot(lo, tri, "nn")
            w = jnp.exp((sc - sp) + later + cl)
            if diag is not None:
                w = jnp.where(strict, w, 0.0)
            acc = acc + _dot(w.astype(BF16), vb_ref[pl.ds(k0, tk), :], "nn")
            return cl + jnp.sum(ls, axis=1, keepdims=True), acc

        cl, acc = jnp.zeros((t, 1), F32), jnp.zeros((t, d), F32)
        for dd in reversed(range(r)):
            cl, acc = block(i * r + dd, cl, acc, dd)

        def loop(j, carry):
            for u in range(un):
                carry = block(i * r - 1 - (un * j + u), carry[0], carry[1], None)
            return carry

        cl, acc = lax.fori_loop(0, (i * r) // un, loop, (cl, acc))
        o_ref[...] = acc
        lt_ref[...] = cl

    h = n_heads
    return pl.pallas_call(
        body,
        grid=(h, s // t),
        in_specs=[
            pl.BlockSpec((t, d), lambda hh, i: (i, hh)),
            pl.BlockSpec((s, d), lambda hh, i: (0, h + hh)),
            pl.BlockSpec((s, d), lambda hh, i: (0, 2 * h + hh)),
        ],
        out_specs=[pl.BlockSpec((t, d), lambda hh, i: (i, hh)), pl.BlockSpec((None, t, 1), lambda hh, i: (hh, i, 0))],
        out_shape=[jax.ShapeDtypeStruct((s, h * d), F32), jax.ShapeDtypeStruct((h, s, 1), F32)],
        scratch_shapes=[pltpu.VMEM((s, d), BF16), pltpu.VMEM((s, d), BF16)],
        compiler_params=_cparams(("arbitrary", "arbitrary")),
        name=name,
    )(proj, proj, proj)


def _sb_bwd(proj, ltot, do, n_heads, name):
    s = proj.shape[0]
    d = HEAD_DIM
    t = min(SB_TQ, s)
    tk = min(SB_TK, s)
    r = t // tk
    un = SB_UNROLL if r % SB_UNROLL == 0 else 1
    nq = s // t
    scale = d**-0.5

    def body(q_ref, k_ref, v_ref, lt_ref, do_ref, dq_ref, dk_ref, dv_ref, kb_ref, vb_ref):
        i = pl.program_id(1)

        @pl.when(i == 0)
        def _():
            kb_ref[...] = k_ref[...].astype(BF16)
            vb_ref[...] = v_ref[...].astype(BF16)
            dk_ref[...] = jnp.zeros_like(dk_ref)
            dv_ref[...] = jnp.zeros_like(dv_ref)

        q = (q_ref[...] * scale).astype(BF16)
        dob = do_ref[...].astype(BF16)
        ltv = lt_ref[...]
        rows, cols = _iotas(t, tk)
        trows, tcols = _iotas(tk, tk)
        upto = _tri2(trows <= tcols)
        before = _tri2(trows < tcols)

        def block(kb, cp, cc, dq, diag):
            k0 = pl.multiple_of(kb * tk, tk)
            kk = kb_ref[pl.ds(k0, tk), :]
            sc = _dot(q, kk, "nt")
            sp = _softplus(sc)
            ls = -sp
            if diag is not None:
                strict = cols + diag * tk < rows
                ls = jnp.where(strict, ls, 0.0)
            prefix = _dot(_cat2(ls), upto, "nn") + cp
            lsig = sc - sp
            w = jnp.exp(lsig + (ltv - prefix))
            if diag is not None:
                w = jnp.where(strict, w, 0.0)
            da = _dot(dob, vb_ref[pl.ds(k0, tk), :], "nt") * w
            csum = _dot(_cat2(da), before, "nn") + cc
            beta = jnp.exp(lsig)
            dz = da * (1.0 - beta) - beta * csum
            if diag is not None:
                dz = jnp.where(strict, dz, 0.0)
            dzb = dz.astype(BF16)
            dq = dq + _dot(dzb, kk, "nn")
            dk_ref[pl.ds(k0, tk), :] += _dot(dzb, q, "tn")
            dv_ref[pl.ds(k0, tk), :] += _dot(w.astype(BF16), dob, "tn")
            return cp + jnp.sum(ls, axis=1, keepdims=True), cc + jnp.sum(da, axis=1, keepdims=True), dq

        def loop(j, carry):
            for u in range(un):
                carry = block(un * j + u, carry[0], carry[1], carry[2], None)
            return carry

        z1 = jnp.zeros((t, 1), F32)
        cp, cc, dq = lax.fori_loop(0, (i * r) // un, loop, (z1, z1, jnp.zeros((t, d), F32)))
        for dd in range(r):
            cp, cc, dq = block(i * r + dd, cp, cc, dq, dd)
        dq_ref[...] = dq * scale

    h = n_heads
    qblk = pl.BlockSpec((t, d), lambda hh, i: (i, hh))
    full = pl.BlockSpec((s, d), lambda hh, i: (0, hh))
    shp = jax.ShapeDtypeStruct((s, h * d), F32)
    return pl.pallas_call(
        body,
        grid=(h, nq),
        in_specs=[
            qblk,
            pl.BlockSpec((s, d), lambda hh, i: (0, h + hh)),
            pl.BlockSpec((s, d), lambda hh, i: (0, 2 * h + hh)),
            pl.BlockSpec((None, t, 1), lambda hh, i: (hh, i, 0)),
            qblk,
        ],
        out_specs=[qblk, full, full],
        out_shape=[shp, shp, shp],
        scratch_shapes=[pltpu.VMEM((s, d), BF16), pltpu.VMEM((s, d), BF16)],
        compiler_params=_cparams(("arbitrary", "arbitrary")),
        name=name,
    )(proj, proj, proj, ltot, do)


SM_TQ = 512
SM_TK = 256
SM_UNROLL = 2


def _allowed(mode, rows, cols, q0, k0):
    r = rows + q0
    c = cols + k0
    if mode == "causal":
        return c <= r
    return (c >> CHUNK_SHIFT) <= (r >> CHUNK_SHIFT)


def _sm_fwd(q_arr, k_arr, v_arr, n_heads, dqk, q_blk, k_blk, v_blk, mode, name, crow=None, ccol=None):
    s = q_arr.shape[0]
    dv = HEAD_DIM
    t = min(SM_TQ, s)
    tk = min(SM_TK, s)
    r = t // tk
    un = SM_UNROLL if r % SM_UNROLL == 0 else 1
    bias = crow is not None
    scale = (HEAD_DIM if mode == "causal" else HEAD_DIM + MLA_ROPE) ** -0.5

    def body(*refs):
        if bias:
            q_ref, k_ref, v_ref, cr_ref, cc_ref, o_ref, lse_ref, kb_ref, vb_ref = refs
        else:
            q_ref, k_ref, v_ref, o_ref, lse_ref, kb_ref, vb_ref = refs
        i = pl.program_id(1)

        @pl.when(i == 0)
        def _():
            kb_ref[...] = k_ref[...].astype(BF16)
            vb_ref[...] = v_ref[...].astype(BF16)

        q = (q_ref[...] * scale).astype(BF16)
        q0 = i * t
        rows, cols = _iotas(t, tk)
        crv = cr_ref[...] if bias else None

        def block(kb, m, l, acc, masked):
            k0 = pl.multiple_of(kb * tk, tk)
            sc = _dot(q, kb_ref[pl.ds(k0, tk), :], "nt")
            if bias:
                sc = sc + crv - cc_ref[pl.ds(kb, 1), :]
            if masked:
                sc = jnp.where(_allowed(mode, rows, cols, q0, k0), sc, NEG)
            m_new = jnp.maximum(m, jnp.max(sc, axis=1, keepdims=True))
            alpha = jnp.exp(m - m_new)
            p = jnp.exp(sc - m_new)
            l = alpha * l + jnp.sum(p, axis=1, keepdims=True)
            acc = alpha * acc + _dot(p.astype(BF16), vb_ref[pl.ds(k0, tk), :], "nn")
            return m_new, l, acc

        def loop(j, carry):
            for u in range(un):
                carry = block(un * j + u, carry[0], carry[1], carry[2], False)
            return carry

        init = (jnp.full((t, 1), NEG, F32), jnp.zeros((t, 1), F32), jnp.zeros((t, dv), F32))
        m, l, acc = lax.fori_loop(0, (i * r) // un, loop, init)
        for dd in range(r):
            m, l, acc = block(i * r + dd, m, l, acc, True)
        o_ref[...] = acc / l
        lse_ref[...] = m + jnp.log(l)

    h = n_heads
    in_specs = [
        pl.BlockSpec((t, dqk), lambda hh, i: (i, q_blk(hh))),
        pl.BlockSpec((s, dqk), lambda hh, i: (0, k_blk(hh))),
        pl.BlockSpec((s, dv), lambda hh, i: (0, v_blk(hh))),
    ]
    args = [q_arr, k_arr, v_arr]
    if bias:
        in_specs += [pl.BlockSpec((None, t, 1), lambda hh, i: (hh, i, 0)), pl.BlockSpec((None, s // tk, tk), lambda hh, i: (hh, 0, 0))]
        args += [crow, ccol]
    return pl.pallas_call(
        body,
        grid=(h, s // t),
        in_specs=in_specs,
        out_specs=[pl.BlockSpec((t, dv), lambda hh, i: (i, hh)), pl.BlockSpec((None, t, 1), lambda hh, i: (hh, i, 0))],
        out_shape=[jax.ShapeDtypeStruct((s, h * dv), F32), jax.ShapeDtypeStruct((h, s, 1), F32)],
        scratch_shapes=[pltpu.VMEM((s, dqk), BF16), pltpu.VMEM((s, dv), BF16)],
        compiler_params=_cparams(("arbitrary", "arbitrary")),
        name=name,
    )(*args)


def _sm_bwd(q_arr, k_arr, v_arr, o, do, lse, n_heads, dqk, q_blk, k_blk, v_blk, mode, name, crow=None, ccol=None):
    s = q_arr.shape[0]
    dv = HEAD_DIM
    t = min(SM_TQ, s)
    tk = min(SM_TK, s)
    r = t // tk
    un = SM_UNROLL if r % SM_UNROLL == 0 else 1
    nq = s // t
    bias = crow is not None
    scale = (HEAD_DIM if mode == "causal" else HEAD_DIM + MLA_ROPE) ** -0.5

    def body(*refs):
        if bias:
            q_ref, k_ref, v_ref, o_ref, do_ref, lse_ref, cr_ref, cc_ref, dq_ref, dk_ref, dv_ref, dcc_ref, dcr_ref, kb_ref, vb_ref = refs
        else:
            q_ref, k_ref, v_ref, o_ref, do_ref, lse_ref, dq_ref, dk_ref, dv_ref, kb_ref, vb_ref = refs
        i = pl.program_id(1)

        @pl.when(i == 0)
        def _():
            kb_ref[...] = k_ref[...].astype(BF16)
            vb_ref[...] = v_ref[...].astype(BF16)
            dk_ref[...] = jnp.zeros_like(dk_ref)
            dv_ref[...] = jnp.zeros_like(dv_ref)
            if bias:
                dcc_ref[...] = jnp.zeros_like(dcc_ref)

        q = (q_ref[...] * scale).astype(BF16)
        dov = do_ref[...]
        dob = dov.astype(BF16)
        dsum = jnp.sum(dov * o_ref[...], axis=1, keepdims=True)
        lsev = lse_ref[...]
        q0 = i * t
        rows, cols = _iotas(t, tk)
        crv = cr_ref[...] if bias else None

        def block(kb, dq, dr, masked):
            k0 = pl.multiple_of(kb * tk, tk)
            kk = kb_ref[pl.ds(k0, tk), :]
            sc = _dot(q, kk, "nt")
            if bias:
                sc = sc + crv - cc_ref[pl.ds(kb, 1), :]
            if masked:
                sc = jnp.where(_allowed(mode, rows, cols, q0, k0), sc, NEG)
            p = jnp.exp(sc - lsev)
            dz = p * (_dot(dob, vb_ref[pl.ds(k0, tk), :], "nt") - dsum)
            dzb = dz.astype(BF16)
            dk_ref[pl.ds(k0, tk), :] += _dot(dzb, q, "tn")
            dv_ref[pl.ds(k0, tk), :] += _dot(p.astype(BF16), dob, "tn")
            if bias:
                dcc_ref[pl.ds(kb, 1), :] -= jnp.sum(dz, axis=0, keepdims=True)
                dr = dr + jnp.sum(dz, axis=1, keepdims=True)
            return dq + _dot(dzb, kk, "nn"), dr

        def loop(j, carry):
            for u in range(un):
                carry = block(un * j + u, carry[0], carry[1], False)
            return carry

        dq, dr = lax.fori_loop(0, (i * r) // un, loop, (jnp.zeros((t, dqk), F32), jnp.zeros((t, 1), F32)))
        for dd in range(r):
            dq, dr = block(i * r + dd, dq, dr, True)
        dq_ref[...] = dq * scale
        if bias:
            dcr_ref[...] = dr

    h = n_heads
    qblk = pl.BlockSpec((t, dqk), lambda hh, i: (i, q_blk(hh)))
    oblk = pl.BlockSpec((t, dv), lambda hh, i: (i, hh))
    vec = pl.BlockSpec((None, t, 1), lambda hh, i: (hh, i, 0))
    in_specs = [
        qblk,
        pl.BlockSpec((s, dqk), lambda hh, i: (0, k_blk(hh))),
        pl.BlockSpec((s, dv), lambda hh, i: (0, v_blk(hh))),
        oblk,
        oblk,
        vec,
    ]
    args = [q_arr, k_arr, v_arr, o, do, lse]
    out_specs = [
        pl.BlockSpec((t, dqk), lambda hh, i: (i, hh)),
        pl.BlockSpec((s, dqk), lambda hh, i: (0, hh)),
        pl.BlockSpec((s, dv), lambda hh, i: (0, hh)),
    ]
    out_shape = [
        jax.ShapeDtypeStruct((s, h * dqk), F32),
        jax.ShapeDtypeStruct((s, h * dqk), F32),
        jax.ShapeDtypeStruct((s, h * dv), F32),
    ]
    if bias:
        ccs = pl.BlockSpec((None, s // tk, tk), lambda hh, i: (hh, 0, 0))
        in_specs += [vec, ccs]
        args += [crow, ccol]
        out_specs += [ccs, vec]
        out_shape += [jax.ShapeDtypeStruct((h, s // tk, tk), F32), jax.ShapeDtypeStruct((h, s, 1), F32)]
    return pl.pallas_call(
        body,
        grid=(h, nq),
        in_specs=in_specs,
        out_specs=out_specs,
        out_shape=out_shape,
        scratch_shapes=[pltpu.VMEM((s, dqk), BF16), pltpu.VMEM((s, dv), BF16)],
        compiler_params=_cparams(("arbitrary", "arbitrary")),
        name=name,
    )(*args)


def _rope_tables(pos):
    half = MLA_ROPE // 2
    inv_freq = ROPE_BASE ** (-jnp.arange(0, MLA_ROPE, 2, dtype=F32) / MLA_ROPE)
    ang = pos.astype(F32)[:, None] * inv_freq
    cos, sin = jnp.cos(ang), jnp.sin(ang)
    z = lambda n: jnp.zeros((pos.shape[0], n), F32)
    tc = jnp.concatenate([cos, cos, z(LANES - 2 * half)], axis=1)
    ta = jnp.concatenate([-sin, z(LANES - half)], axis=1)
    tb = jnp.concatenate([z(half), sin, z(LANES - 2 * half)], axis=1)
    return tc, ta, tb


def _rot(v, tc, ta, tb, sign):
    half = MLA_ROPE // 2
    return v * tc + sign * (pltpu.roll(v, LANES - half, 1) * ta + pltpu.roll(v, half, 1) * tb)


def _mla_assemble(qpre, kv, proj1, kr_blk, tabs, n_heads, name):
    s = qpre.shape[0]
    tr = _pick(s, (512, 256, 128))

    def body(qp_ref, kn_ref, kr_ref, tc_ref, ta_ref, tb_ref, qc_ref, kc_ref):
        tc, ta, tb = tc_ref[...], ta_ref[...], tb_ref[...]
        qc_ref[:, :LANES] = qp_ref[:, :LANES]
        qc_ref[:, LANES:] = _rot(qp_ref[:, LANES:], tc, ta, tb, 1.0)
        kc_ref[:, :LANES] = kn_ref[...]
        kc_ref[:, LANES:] = _rot(kr_ref[...], tc, ta, tb, 1.0)

    tab = pl.BlockSpec((tr, LANES), lambda i, hh: (i, 0))
    wide = pl.BlockSpec((tr, MLA_QK), lambda i, hh: (i, hh))
    shp = jax.ShapeDtypeStruct((s, n_heads * MLA_QK), F32)
    return pl.pallas_call(
        body,
        grid=(s // tr, n_heads),
        in_specs=[
            wide,
            pl.BlockSpec((tr, LANES), lambda i, hh: (i, 2 * hh)),
            pl.BlockSpec((tr, LANES), lambda i, hh: (i, kr_blk)),
            tab,
            tab,
            tab,
        ],
        out_specs=[wide, wide],
        out_shape=[shp, shp],
        compiler_params=_cparams(("parallel", "parallel")),
        name=name,
    )(qpre, kv, proj1, *tabs)


def _mla_disassemble(dqcat, dkcat, dv, tabs, n_heads, name):
    s = dqcat.shape[0]
    tr = _pick(s, (512, 256, 128))

    def body(dq_ref, dk_ref, dv_ref, tc_ref, ta_ref, tb_ref, dqp_ref, dkv_ref, dkr_ref):
        hh = pl.program_id(1)
        tc, ta, tb = tc_ref[...], ta_ref[...], tb_ref[...]
        dqp_ref[:, :LANES] = dq_ref[:, :LANES].astype(BF16)
        dqp_ref[:, LANES:] = _rot(dq_ref[:, LANES:], tc, ta, tb, -1.0).astype(BF16)
        dkv_ref[:, :LANES] = dk_ref[:, :LANES].astype(BF16)
        dkv_ref[:, LANES:] = dv_ref[...].astype(BF16)
        part = dk_ref[:, LANES:]

        @pl.when(hh == 0)
        def _():
            dkr_ref[...] = part

        @pl.when(hh > 0)
        def _():
            dkr_ref[...] += part

        @pl.when(hh == n_heads - 1)
        def _():
            dkr_ref[...] = _rot(dkr_ref[...], tc, ta, tb, -1.0)

    tab = pl.BlockSpec((tr, LANES), lambda i, hh: (i, 0))
    wide = pl.BlockSpec((tr, MLA_QK), lambda i, hh: (i, hh))
    shp = jax.ShapeDtypeStruct((s, n_heads * MLA_QK), BF16)
    return pl.pallas_call(
        body,
        grid=(s // tr, n_heads),
        in_specs=[wide, wide, pl.BlockSpec((tr, LANES), lambda i, hh: (i, hh)), tab, tab, tab],
        out_specs=[wide, wide, tab],
        out_shape=[shp, shp, jax.ShapeDtypeStruct((s, LANES), F32)],
        compiler_params=_cparams(("parallel", "arbitrary")),
        name=name,
    )(dqcat, dkcat, dv, *tabs)


def _forget_scan(proj2, f_blk, bias, name):
    s = proj2.shape[0]
    n = LANES

    def body(f_ref, b_ref, c_ref):
        rows, cols = _iotas(n, n)
        tri = (cols <= rows).astype(BF16)

        def step(j, carry):
            r0 = pl.multiple_of(j * n, n)
            f = f_ref[pl.ds(r0, n), :] + b_ref[...]
            lf = jnp.minimum(f, 0.0) - jnp.log1p(jnp.exp(-jnp.abs(f)))
            a, b, c = _split3(lf)
            cs = _dot(tri, a, "nn") + _dot(tri, b, "nn") + _dot(tri, c, "nn") + carry
            c_ref[pl.ds(r0, n), :] = cs
            return cs[n - 1 : n, :]

        lax.fori_loop(0, s // n, step, jnp.zeros((1, n), F32))

    return pl.pallas_call(
        body,
        grid=(1,),
        in_specs=[pl.BlockSpec((s, n), lambda i: (0, f_blk)), pl.BlockSpec((1, n), lambda i: (0, 0))],
        out_specs=pl.BlockSpec((s, n), lambda i: (0, 0)),
        out_shape=jax.ShapeDtypeStruct((s, n), F32),
        compiler_params=_cparams(("arbitrary",)),
        name=name,
    )(proj2, bias)


def _forget_scan_bwd(dc_col, dc_row, proj2, f_blk, bias, n_heads, name):
    s = proj2.shape[0]
    n = LANES
    nb = s // n

    def body(dcc_ref, dcr_ref, f_ref, b_ref, df_ref, db_ref):
        rows, cols = _iotas(n, n)
        tri = (cols >= rows).astype(BF16)
        live = cols < n_heads

        def step(j, carry):
            acc, dbv = carry
            r0 = pl.multiple_of((nb - 1 - j) * n, n)
            a, b, c = _split3(dcc_ref[pl.ds(r0, n), :] + dcr_ref[pl.ds(r0, n), :])
            dl = _dot(tri, a, "nn") + _dot(tri, b, "nn") + _dot(tri, c, "nn") + acc
            f = f_ref[pl.ds(r0, n), :] + b_ref[...]
            df = jnp.where(live, dl / (1.0 + jnp.exp(f)), 0.0)
            df_ref[pl.ds(r0, n), :] = df.astype(BF16)
            return dl[0:1, :], dbv + jnp.sum(df, axis=0, keepdims=True)

        z = jnp.zeros((1, n), F32)
        _, dbv = lax.fori_loop(0, nb, step, (z, z))
        db_ref[...] = dbv

    return pl.pallas_call(
        body,
        grid=(1,),
        in_specs=[
            pl.BlockSpec((s, n), lambda i: (0, 0)),
            pl.BlockSpec((s, n), lambda i: (0, 0)),
            pl.BlockSpec((s, n), lambda i: (0, f_blk)),
            pl.BlockSpec((1, n), lambda i: (0, 0)),
        ],
        out_specs=[pl.BlockSpec((s, n), lambda i: (0, 0)), pl.BlockSpec((1, n), lambda i: (0, 0))],
        out_shape=[jax.ShapeDtypeStruct((s, n), BF16), jax.ShapeDtypeStruct((1, n), F32)],
        compiler_params=_cparams(("arbitrary",)),
        name=name,
    )(dc_col, dc_row, proj2, bias)


def _adamw(w, g, m, v, name):
    r, c = w.shape
    tr = _pick(r, (128, 64, 32, 16, 8))
    c1 = 1.0 - ADAM_B1**ADAM_STEP
    c2 = 1.0 - ADAM_B2**ADAM_STEP

    def body(w_ref, g_ref, m_ref, v_ref, d_ref, mo_ref, vo_ref):
        gv = g_ref[...]
        mn = ADAM_B1 * m_ref[...] + (1.0 - ADAM_B1) * gv
        vn = ADAM_B2 * v_ref[...] + (1.0 - ADAM_B2) * (gv * gv)
        mo_ref[...] = mn
        vo_ref[...] = vn
        d_ref[...] = -ADAM_LR * ((mn / c1) / (jnp.sqrt(vn / c2) + ADAM_EPS) + ADAM_WD * w_ref[...])

    blk = pl.BlockSpec((tr, c), lambda i: (i, 0))
    shp = jax.ShapeDtypeStruct((r, c), F32)
    return pl.pallas_call(
        body,
        grid=(r // tr,),
        in_specs=[blk, blk, blk, blk],
        out_specs=[blk, blk, blk],
        out_shape=[shp, shp, shp],
        compiler_params=_cparams(("parallel",)),
        name=name,
    )(w, g, m, v)


def _mesh_pos():
    return lax.axis_index("x"), lax.axis_index("y"), lax.axis_index("c")


def _other_chips(x, y):
    return [(1 - x, y), (x, 1 - y), (1 - x, 1 - y)]


ANY = pl.BlockSpec(memory_space=pl.ANY)


def _gather_weights(wp):
    rp, wd = wp.shape
    half = rp // 2

    def body(w_ref, out_ref, send_sems, recv_sems):
        x, y, c = _mesh_pos()
        me = 2 * x + y
        chips = _other_chips(x, y)

        def region(chip, hc):
            return out_ref.at[chip, pl.ds(hc * half, half), :]

        def copy(k, src, dst, to):
            return pltpu.make_async_remote_copy(
                src_ref=src, dst_ref=dst, send_sem=send_sems.at[k], recv_sem=recv_sems.at[k], device_id=to, device_id_type=MESH
            )

        first = [copy(j, w_ref.at[pl.ds(c * half, half), :], region(me, c), (cx, cy, c)) for j, (cx, cy) in enumerate(chips)]
        for cp in first:
            cp.start()
        passed = [copy(3 + j, region(2 * cx + cy, c), region(2 * cx + cy, c), (x, y, 1 - c)) for j, (cx, cy) in enumerate(chips)]
        for j, (cx, cy) in enumerate(chips):
            copy(j, region(2 * cx + cy, c), region(2 * cx + cy, c), (x, y, c)).wait_recv()
            passed[j].start()
        for j, (cx, cy) in enumerate(chips):
            copy(3 + j, region(2 * cx + cy, 1 - c), region(2 * cx + cy, 1 - c), (x, y, c)).wait_recv()
        for cp in first + passed:
            cp.wait_send()

    return pl.pallas_call(
        body,
        in_specs=[ANY],
        out_specs=ANY,
        out_shape=jax.ShapeDtypeStruct((N_CHIPS, rp, wd), wp.dtype),
        scratch_shapes=[pltpu.SemaphoreType.DMA((6,)), pltpu.SemaphoreType.DMA((6,))],
        name="gather_weights",
    )(wp)


def _place_own(wall, wp, pos):
    rp, wd = wp.shape

    def body(x_ref, y_ref, c_ref, wall_ref, w_ref, o_ref):
        o_ref[0] = w_ref[...]

    grid_spec = pltpu.PrefetchScalarGridSpec(
        num_scalar_prefetch=3,
        grid=(rp // PACK_TR,),
        in_specs=[ANY, pl.BlockSpec((PACK_TR, wd), lambda i, xr, yr, cr: (i, 0))],
        out_specs=pl.BlockSpec((1, PACK_TR, wd), lambda i, xr, yr, cr: (2 * xr[0] + yr[0], i, 0)),
    )
    return pl.pallas_call(
        body,
        grid_spec=grid_spec,
        out_shape=jax.ShapeDtypeStruct(wall.shape, wall.dtype),
        input_output_aliases={3: 0},
        compiler_params=_cparams(("parallel",)),
        name="place_own_shard",
    )(*pos, wall, wp)


def _pair_exchange(g):
    _, rp, wd = g.shape
    half = rp // 2

    def body(g_ref, out_ref, send_sem, recv_sem):
        x, y, c = _mesh_pos()
        cp = pltpu.make_async_remote_copy(
            src_ref=g_ref.at[:, pl.ds((1 - c) * half, half), :],
            dst_ref=out_ref,
            send_sem=send_sem,
            recv_sem=recv_sem,
            device_id=(x, y, 1 - c),
            device_id_type=MESH,
        )
        cp.start()
        cp.wait()

    return pl.pallas_call(
        body,
        in_specs=[ANY],
        out_specs=ANY,
        out_shape=jax.ShapeDtypeStruct((N_CHIPS, half, wd), g.dtype),
        scratch_shapes=[pltpu.SemaphoreType.DMA, pltpu.SemaphoreType.DMA],
        name="rs_pair_exchange",
    )(g)


def _pair_add(g, recv, pos):
    _, rp, wd = g.shape
    half = rp // 2
    nb = half // PACK_TR

    def body(x_ref, y_ref, c_ref, g_ref, r_ref, o_ref):
        o_ref[...] = (g_ref[...] + r_ref[...]).astype(BF16)

    blk = (1, PACK_TR, wd)
    grid_spec = pltpu.PrefetchScalarGridSpec(
        num_scalar_prefetch=3,
        grid=(N_CHIPS, nb),
        in_specs=[
            pl.BlockSpec(blk, lambda j, i, xr, yr, cr: (j, cr[0] * nb + i, 0)),
            pl.BlockSpec(blk, lambda j, i, xr, yr, cr: (j, i, 0)),
        ],
        out_specs=pl.BlockSpec(blk, lambda j, i, xr, yr, cr: (j, i, 0)),
    )
    return pl.pallas_call(
        body,
        grid_spec=grid_spec,
        out_shape=jax.ShapeDtypeStruct((N_CHIPS, half, wd), BF16),
        compiler_params=_cparams(("parallel", "parallel")),
        name="rs_pair_add",
    )(*pos, g, recv)


def _chip_exchange(sp):
    _, rh, wd = sp.shape

    def body(s_ref, out_ref, send_sems, recv_sems):
        x, y, c = _mesh_pos()
        me = 2 * x + y
        chips = _other_chips(x, y)
        sends = []
        for j, (cx, cy) in enumerate(chips):
            cp = pltpu.make_async_remote_copy(
                src_ref=s_ref.at[2 * cx + cy],
                dst_ref=out_ref.at[me],
                send_sem=send_sems.at[j],
                recv_sem=recv_sems.at[j],
                device_id=(cx, cy, c),
                device_id_type=MESH,
            )
            cp.start()
            sends.append(cp)
        for j, (cx, cy) in enumerate(chips):
            pltpu.make_async_remote_copy(
                src_ref=s_ref.at[me],
                dst_ref=out_ref.at[2 * cx + cy],
                send_sem=send_sems.at[j],
                recv_sem=recv_sems.at[j],
                device_id=(x, y, c),
                device_id_type=MESH,
            ).wait_recv()
        for cp in sends:
            cp.wait_send()

    return pl.pallas_call(
        body,
        in_specs=[ANY],
        out_specs=ANY,
        out_shape=jax.ShapeDtypeStruct(sp.shape, sp.dtype),
        scratch_shapes=[pltpu.SemaphoreType.DMA((3,)), pltpu.SemaphoreType.DMA((3,))],
        name="rs_chip_exchange",
    )(sp)


def _sum_slots(own, slots, pos):
    _, rh, wd = slots.shape
    nb = rh // PACK_TR

    def body(x_ref, y_ref, c_ref, own_ref, a_ref, b_ref, d_ref, o_ref):
        f = lambda r: r[0].astype(F32)
        o_ref[...] = ((f(own_ref) + f(a_ref)) + f(b_ref)) + f(d_ref)

    blk = (1, PACK_TR, wd)

    def other(k):
        return pl.BlockSpec(blk, lambda i, xr, yr, cr: (k + (k >= 2 * xr[0] + yr[0]).astype(jnp.int32), i, 0))

    grid_spec = pltpu.PrefetchScalarGridSpec(
        num_scalar_prefetch=3,
        grid=(nb,),
        in_specs=[pl.BlockSpec(blk, lambda i, xr, yr, cr: (2 * xr[0] + yr[0], i, 0)), other(0), other(1), other(2)],
        out_specs=pl.BlockSpec((PACK_TR, wd), lambda i, xr, yr, cr: (cr[0] * nb + i, 0)),
    )
    return pl.pallas_call(
        body,
        grid_spec=grid_spec,
        out_shape=jax.ShapeDtypeStruct((2 * rh, wd), F32),
        compiler_params=_cparams(("parallel",)),
        name="rs_sum_slots",
    )(*pos, own, slots, slots, slots)


def _pair_gather(t):
    rh = t.shape[0] // 2

    def body(t_ref, out_ref, send_sem, recv_sem):
        x, y, c = _mesh_pos()
        cp = pltpu.make_async_remote_copy(
            src_ref=t_ref.at[pl.ds(c * rh, rh), :],
            dst_ref=out_ref.at[pl.ds(c * rh, rh), :],
            send_sem=send_sem,
            recv_sem=recv_sem,
            device_id=(x, y, 1 - c),
            device_id_type=MESH,
        )
        cp.start()
        cp.wait_send()
        pltpu.make_async_remote_copy(
            src_ref=t_ref.at[pl.ds((1 - c) * rh, rh), :],
            dst_ref=out_ref.at[pl.ds((1 - c) * rh, rh), :],
            send_sem=send_sem,
            recv_sem=recv_sem,
            device_id=(x, y, c),
            device_id_type=MESH,
        ).wait_recv()

    return pl.pallas_call(
        body,
        in_specs=[ANY],
        out_specs=ANY,
        out_shape=jax.ShapeDtypeStruct(t.shape, t.dtype),
        input_output_aliases={0: 0},
        scratch_shapes=[pltpu.SemaphoreType.DMA, pltpu.SemaphoreType.DMA],
        name="rs_pair_gather",
    )(t)


def _allreduce_small(v):
    shape = v.shape
    n_dev = 8

    def body(v_ref, o_ref, slots, send_sems, recv_sems):
        x, y, c = _mesh_pos()
        me = 4 * x + 2 * y + c
        slots[me] = v_ref[...]
        sends = []
        for k in range(1, n_dev):
            fx, fy, fc = (k >> 2) & 1, (k >> 1) & 1, k & 1
            to = (x ^ fx, y ^ fy, c ^ fc)
            cp = pltpu.make_async_remote_copy(
                src_ref=v_ref,
                dst_ref=slots.at[me],
                send_sem=send_sems.at[k - 1],
                recv_sem=recv_sems.at[k - 1],
                device_id=to,
                device_id_type=MESH,
            )
            cp.start()
            sends.append(cp)
        for k in range(1, n_dev):
            fx, fy, fc = (k >> 2) & 1, (k >> 1) & 1, k & 1
            frm = 4 * (x ^ fx) + 2 * (y ^ fy) + (c ^ fc)
            pltpu.make_async_remote_copy(
                src_ref=v_ref,
                dst_ref=slots.at[frm],
                send_sem=send_sems.at[k - 1],
                recv_sem=recv_sems.at[k - 1],
                device_id=(x, y, c),
                device_id_type=MESH,
            ).wait_recv()
        for cp in sends:
            cp.wait_send()
        acc = slots[0]
        for k in range(1, n_dev):
            acc = acc + slots[k]
        o_ref[...] = acc

    vm = pl.BlockSpec(memory_space=pltpu.VMEM)
    return pl.pallas_call(
        body,
        in_specs=[vm],
        out_specs=vm,
        out_shape=jax.ShapeDtypeStruct(shape, F32),
        scratch_shapes=[pltpu.VMEM((n_dev,) + shape, F32), pltpu.SemaphoreType.DMA((n_dev - 1,)), pltpu.SemaphoreType.DMA((n_dev - 1,))],
        name="allreduce_small",
    )(v)


def _pack_layout(shard_shapes):
    offs, rows = [], []
    off = 0
    for r, c in shard_shapes:
        assert (r * c) % PACK_W == 0
        n = r * c // PACK_W
        offs.append(off)
        rows.append(n)
        off += -(-n // 16) * 16
    rp = -(-off // (2 * PACK_TR)) * (2 * PACK_TR)
    return offs, rows, rp


def _pack_rows(parts, offs, rows, rp, lead):
    ends = list(offs[1:]) + [rp]
    nolead = ((0, 0),) * len(lead)
    out = [jnp.pad(p, nolead + ((0, e - o - n), (0, 0))) for p, o, n, e in zip(parts, offs, rows, ends)]
    return jnp.concatenate(out, axis=len(lead))


def kernel(x, positions, ln0, w_in0, w_out0, ln1, w_in1, q_norm1, w_qb1, kv_norm1, w_kvb1, w_out1, ln2, w_in2, b_f2, w_out2, ln3, w_in3, w_out3, final_norm, loss_target, m_ln0, m_w_in0, m_w_out0, m_ln1, m_w_in1, m_q_norm1, m_w_qb1, m_kv_norm1, m_w_kvb1, m_w_out1, m_ln2, m_w_in2, m_b_f2, m_w_out2, m_ln3, m_w_in3, m_w_out3, m_final_norm, v_ln0, v_w_in0, v_w_out0, v_ln1, v_w_in1, v_q_norm1, v_w_qb1, v_kv_norm1, v_w_kvb1, v_w_out1, v_ln2, v_w_in2, v_b_f2, v_w_out2, v_ln3, v_w_in3, v_w_out3, v_final_norm):
    xs = x[0]
    s, d = xs.shape
    di = 4 * w_out0.shape[0]
    nh = di // HEAD_DIM
    idx = tuple(lax.axis_index(a).astype(jnp.int32).reshape(1) for a in ("x", "y", "c"))

    big = [w_in0, w_out0, w_in1, w_qb1, w_kvb1, w_out1, w_in2, w_out2, w_in3, w_out3]
    col_sharded = [True, False, True, True, True, False, True, False, True, False]
    shard_shapes = [w.shape for w in big]
    offs, rows, rp = _pack_layout(shard_shapes)
    wp = _pack_rows([w.astype(BF16).reshape(n, PACK_W) for w, n in zip(big, rows)], offs, rows, rp, ())
    wall = _place_own(_gather_weights(wp), wp, idx)
    full = []
    for (r, c), o, n, cs in zip(shard_shapes, offs, rows, col_sharded):
        slab = wall[:, o : o + n, :].reshape(N_CHIPS, r, c)
        full.append(slab.transpose(1, 0, 2).reshape(r, N_CHIPS * c) if cs else slab.reshape(N_CHIPS * r, c))
    f_in0, f_out0, f_in1, f_qb1, f_kvb1, f_out1, f_in2, f_out2, f_in3, f_out3 = full

    i_kr = MLA_Q_RANK + MLA_KV_RANK + MLA_ROPE
    w1p = jnp.concatenate([f_in1[:, i_kr:], f_in1[:, :i_kr], jnp.zeros((d, LANES - MLA_ROPE), BF16)], axis=1)
    qlat_blk = di // MLA_Q_RANK
    kvlat_blk = (di + MLA_Q_RANK) // MLA_KV_RANK
    kr_blk = (di + MLA_Q_RANK + MLA_KV_RANK) // LANES
    qk_w = HEAD_DIM + MLA_ROPE
    wqbp = jnp.pad(f_qb1.reshape(MLA_Q_RANK, nh, qk_w), ((0, 0), (0, 0), (0, MLA_QK - qk_w))).reshape(MLA_Q_RANK, nh * MLA_QK)
    n2 = f_in2.shape[1]
    w2p = jnp.pad(f_in2, ((0, 0), (0, 4 * di + LANES - n2)))
    b2p = jnp.pad(b_f2, (0, LANES - nh)).reshape(1, LANES)

    row = lambda v: v.reshape(1, -1)
    tabs = _rope_tables(positions[0])

    def sb_layer_fwd(xin, ln, w_in, w_out, tag):
        h = _rmsnorm_fwd(xin, row(ln), f"norm_fwd_{tag}")
        proj = _matmul(h, w_in, "nn", f"proj_in_{tag}")
        o, lt = _sb_fwd(proj, nh, f"sb_fwd_{tag}")
        g = _gate_fwd(o, proj, 3, f"gate_fwd_{tag}")
        xout = _matmul(g, w_out, "nn", f"proj_out_{tag}", res=xin)
        return xout, (xin, h, proj, o, lt, g)

    def sb_layer_bwd(dxn, dxnb, saved, ln, w_in, w_out, tag):
        xin, h, proj, o, lt, g = saved
        dgf = _matmul(dxnb, w_out, "nt", f"dgate_in_{tag}")
        dw_out = _matmul(g, dxnb, "tn", f"dw_out_{tag}")
        do, dgate = _gate_bwd(dgf, o, proj, 3, f"gate_bwd_{tag}")
        dq, dk, dv = _sb_bwd(proj, lt, do, nh, f"sb_bwd_{tag}")
        dproj = jnp.concatenate([dq.astype(BF16), dk.astype(BF16), dv.astype(BF16), dgate], axis=1)
        dh = _matmul(dproj, w_in, "nt", f"dh_{tag}")
        dw_in = _matmul(h, dproj, "tn", f"dw_in_{tag}")
        dx, dxb, dln = _rmsnorm_bwd(xin, row(ln), dh, f"norm_bwd_{tag}", dres=dxn)
        return dx, dxb, dln, dw_in, dw_out

    x1, sv0 = sb_layer_fwd(xs, ln0, f_in0, f_out0, "l0")

    h1 = _rmsnorm_fwd(x1, row(ln1), "norm_fwd_l1")
    proj1 = _matmul(h1, w1p, "nn", "proj_in_l1")
    qn = _rmsnorm_fwd(proj1, row(q_norm1), "qnorm_fwd_l1", col_block=qlat_blk)
    kvn = _rmsnorm_fwd(proj1, row(kv_norm1), "kvnorm_fwd_l1", col_block=kvlat_blk)
    qpre = _matmul(qn, wqbp, "nn", "q_up_l1")
    kv1 = _matmul(kvn, f_kvb1, "nn", "kv_up_l1")
    qcat, kcat = _mla_assemble(qpre, kv1, proj1, kr_blk, tabs, nh, "mla_assemble_l1")
    mla_blk = (lambda hh: hh, lambda hh: hh, lambda hh: 2 * hh + 1)
    o1, lse1 = _sm_fwd(qcat, kcat, kv1, nh, MLA_QK, *mla_blk, "chunk", "mla_fwd_l1")
    g1 = _gate_fwd(o1, proj1, 0, "gate_fwd_l1")
    x2 = _matmul(g1, f_out1, "nn", "proj_out_l1", res=x1)

    h2 = _rmsnorm_fwd(x2, row(ln2), "norm_fwd_l2")
    proj2 = _matmul(h2, w2p, "nn", "proj_in_l2")
    f_blk = 4 * di // LANES
    cum = _forget_scan(proj2, f_blk, b2p, "forget_scan_l2")
    cum_h = cum[:, :nh].T
    t_sm = min(SM_TK, s)
    crow = cum_h.reshape(nh, s, 1)
    ccol = cum_h.reshape(nh, s // t_sm, t_sm)
    fg_blk = (lambda hh: hh, lambda hh: nh + hh, lambda hh: 2 * nh + hh)
    o2, lse2 = _sm_fwd(proj2, proj2, proj2, nh, HEAD_DIM, *fg_blk, "causal", "forget_fwd_l2", crow=crow, ccol=ccol)
    g2 = _gate_fwd(o2, proj2, 3, "gate_fwd_l2")
    x3 = _matmul(g2, f_out2, "nn", "proj_out_l2", res=x2)

    x4, sv3 = sb_layer_fwd(x3, ln3, f_in3, f_out3, "l3")

    dx, dxb, d_final, loss_part = _loss_head(x4, row(final_norm), loss_target[0], "loss_head")
    dx, dxb, d_ln3, dw_in3, dw_out3 = sb_layer_bwd(dx, dxb, sv3, ln3, f_in3, f_out3, "l3")

    dgf2 = _matmul(dxb, f_out2, "nt", "dgate_in_l2")
    dw_out2 = _matmul(g2, dxb, "tn", "dw_out_l2")
    do2, dgate2 = _gate_bwd(dgf2, o2, proj2, 3, "gate_bwd_l2")
    dq2, dk2, dv2, dcc2, dcr2 = _sm_bwd(proj2, proj2, proj2, o2, do2, lse2, nh, HEAD_DIM, *fg_blk, "causal", "forget_bwd_l2", crow=crow, ccol=ccol)
    lanes_of = lambda a: jnp.pad(a.reshape(nh, s).T, ((0, 0), (0, LANES - nh)))
    df2, d_bf = _forget_scan_bwd(lanes_of(dcc2), lanes_of(dcr2), proj2, f_blk, b2p, nh, "forget_scan_bwd_l2")
    dproj2 = jnp.concatenate([dq2.astype(BF16), dk2.astype(BF16), dv2.astype(BF16), dgate2, df2], axis=1)
    dh2 = _matmul(dproj2, w2p, "nt", "dh_l2")
    dw_in2 = _matmul(h2, dproj2, "tn", "dw_in_l2")[:, :n2]
    dx, dxb, d_ln2 = _rmsnorm_bwd(x2, row(ln2), dh2, "norm_bwd_l2", dres=dx)

    dgf1 = _matmul(dxb, f_out1, "nt", "dgate_in_l1")
    dw_out1 = _matmul(g1, dxb, "tn", "dw_out_l1")
    do1, dgate1 = _gate_bwd(dgf1, o1, proj1, 0, "gate_bwd_l1")
    dqc, dkc, dv1 = _sm_bwd(qcat, kcat, kv1, o1, do1, lse1, nh, MLA_QK, *mla_blk, "chunk", "mla_bwd_l1")
    dqpre, dkv1, dkr = _mla_disassemble(dqc, dkc, dv1, tabs, nh, "mla_disassemble_l1")
    dqn = _matmul(dqpre, wqbp, "nt", "dqn_l1")
    dw_qbp = _matmul(qn, dqpre, "tn", "dw_qb_l1")
    dw_qb1 = dw_qbp.reshape(MLA_Q_RANK, nh, MLA_QK)[:, :, :qk_w].reshape(MLA_Q_RANK, nh * qk_w)
    dkvn = _matmul(dkv1, f_kvb1, "nt", "dkvn_l1")
    dw_kvb1 = _matmul(kvn, dkv1, "tn", "dw_kvb_l1")
    _, dqlat_b, d_qnorm = _rmsnorm_bwd(proj1, row(q_norm1), dqn, "qnorm_bwd_l1", col_block=qlat_blk)
    _, dkvlat_b, d_kvnorm = _rmsnorm_bwd(proj1, row(kv_norm1), dkvn, "kvnorm_bwd_l1", col_block=kvlat_blk)
    dproj1 = jnp.concatenate([dgate1, dqlat_b, dkvlat_b, dkr.astype(BF16)], axis=1)
    dh1 = _matmul(dproj1, w1p, "nt", "dh_l1")
    dw1p = _matmul(h1, dproj1, "tn", "dw_in_l1")
    dw_in1 = jnp.concatenate([dw1p[:, di : di + i_kr], dw1p[:, :di]], axis=1)
    dx, dxb, d_ln1 = _rmsnorm_bwd(x1, row(ln1), dh1, "norm_bwd_l1", dres=dx)

    dx, dxb, d_ln0, dw_in0, dw_out0 = sb_layer_bwd(dx, dxb, sv0, ln0, f_in0, f_out0, "l0")
    grad_x = dx.reshape(x.shape)

    dws = [dw_in0, dw_out0, dw_in1, dw_qb1, dw_kvb1, dw_out1, dw_in2, dw_out2, dw_in3, dw_out3]
    parts = []
    for g, (r, c), n, cs in zip(dws, shard_shapes, rows, col_sharded):
        g4 = g.reshape(r, N_CHIPS, c).transpose(1, 0, 2) if cs else g.reshape(N_CHIPS, r, c)
        parts.append(g4.reshape(N_CHIPS, n, PACK_W))
    gp = _pack_rows(parts, offs, rows, rp, (N_CHIPS,))
    sib = _pair_exchange(gp)
    pair = _pair_add(gp, sib, idx)
    slots = _chip_exchange(pair)
    gred = _pair_gather(_sum_slots(pair, slots, idx))
    big_grads = [gred[o : o + n, :].reshape(r, c) for (r, c), o, n in zip(shard_shapes, offs, rows)]

    small = [ln0, ln1, q_norm1, kv_norm1, ln2, b_f2, ln3, final_norm]
    small_g = [d_ln0[0], d_ln1[0], d_qnorm[0], d_kvnorm[0], d_ln2[0], d_bf[0, :nh], d_ln3[0], d_final[0]]
    n_small = SMALL_SHAPE[0] * SMALL_SHAPE[1]
    used = sum(v.shape[0] for v in small) + 1
    assert used <= n_small

    def pack_small(vs, last):
        return jnp.concatenate(list(vs) + [last, jnp.zeros((n_small - used,), F32)]).reshape(SMALL_SHAPE)

    sm_sum = _allreduce_small(pack_small(small_g, loss_part[0, :1]))
    flat = sm_sum.reshape(-1)
    loss = flat[used - 1]

    big_m = [m_w_in0, m_w_out0, m_w_in1, m_w_qb1, m_w_kvb1, m_w_out1, m_w_in2, m_w_out2, m_w_in3, m_w_out3]
    big_v = [v_w_in0, v_w_out0, v_w_in1, v_w_qb1, v_w_kvb1, v_w_out1, v_w_in2, v_w_out2, v_w_in3, v_w_out3]
    big_names = ["w_in0", "w_out0", "w_in1", "w_qb1", "w_kvb1", "w_out1", "w_in2", "w_out2", "w_in3", "w_out3"]
    big_upd = [_adamw(w, g, m, v, f"adamw_{nm}") for w, g, m, v, nm in zip(big, big_grads, big_m, big_v, big_names)]

    small_m = [m_ln0, m_ln1, m_q_norm1, m_kv_norm1, m_ln2, m_b_f2, m_ln3, m_final_norm]
    small_v = [v_ln0, v_ln1, v_q_norm1, v_kv_norm1, v_ln2, v_b_f2, v_ln3, v_final_norm]
    one = jnp.ones((1,), F32)
    sd, smn, svn = _adamw(pack_small(small, one), sm_sum, pack_small(small_m, one), pack_small(small_v, one), "adamw_small")

    def unpack_small(p):
        out, at = [], 0
        fl = p.reshape(-1)
        for v in small:
            out.append(fl[at : at + v.shape[0]])
            at += v.shape[0]
        return out

    sg_l, sd_l, sm_l, sv_l = unpack_small(sm_sum), unpack_small(sd), unpack_small(smn), unpack_small(svn)

    order = ["ln0", "w_in0", "w_out0", "ln1", "w_in1", "q_norm1", "w_qb1", "kv_norm1", "w_kvb1", "w_out1", "ln2", "w_in2", "b_f2", "w_out2", "ln3", "w_in3", "w_out3", "final_norm"]
    small_names = ["ln0", "ln1", "q_norm1", "kv_norm1", "ln2", "b_f2", "ln3", "final_norm"]
    grads, deltas, new_m, new_v = {}, {}, {}, {}
    for nm, g, (dl, mn, vn) in zip(big_names, big_grads, big_upd):
        grads[nm], deltas[nm], new_m[nm], new_v[nm] = g, dl, mn, vn
    for nm, g, dl, mn, vn in zip(small_names, sg_l, sd_l, sm_l, sv_l):
        grads[nm], deltas[nm], new_m[nm], new_v[nm] = g, dl, mn, vn
    return (loss, grad_x, *[grads[n] for n in order], *[deltas[n] for n in order], *[new_m[n] for n in order], *[new_v[n] for n in order])
```

```python
import functools

import jax
import jax.numpy as jnp
from jax import lax
from jax.experimental import pallas as pl
from jax.experimental.pallas import tpu as pltpu

F32 = jnp.float32
BF16 = jnp.bfloat16
EPS = 1e-6
NEG = -1e30
HEAD_DIM = 128
CHUNK_SHIFT = 6
MLA_Q_RANK = 256
MLA_KV_RANK = 128
MLA_ROPE = 64
MLA_QK = 256
ROPE_BASE = 10000.0
ADAM_LR = 0.001
ADAM_B1 = 0.9
ADAM_B2 = 0.999
ADAM_EPS = 1e-08
ADAM_WD = 0.01
ADAM_STEP = 10
VMEM_LIMIT_BYTES = 56 * 2**20
LANES = 128
PACK_W = 1024
PACK_TR = 256
SMALL_SHAPE = (8, 768)
MESH = pl.DeviceIdType.MESH
N_CHIPS = 4


def _pick(n, cands):
    for c in cands:
        if n % c == 0:
            return c
    return n


def _cparams(sem):
    return pltpu.CompilerParams(dimension_semantics=sem, vmem_limit_bytes=VMEM_LIMIT_BYTES)


def _dot(a, b, dims):
    dn = {"nn": (((1,), (0,)), ((), ())), "nt": (((1,), (1,)), ((), ())), "tn": (((0,), (0,)), ((), ()))}[dims]
    return lax.dot_general(a, b, dn, preferred_element_type=F32)


def _matmul(a, b, dims, name, res=None):
    if dims == "nn":
        (m, k), (k2, n) = a.shape, b.shape
    elif dims == "nt":
        (m, k), (n, k2) = a.shape, b.shape
    else:
        (k, m), (k2, n) = a.shape, b.shape
    assert k == k2, (a.shape, b.shape, dims)
    tm = _pick(m, (1024, 512, 256, 128))
    tn = _pick(n, (1024, 640, 512, 384, 256, 128))
    tk = _pick(k, (1024, 640, 512, 256, 128))
    nk = k // tk

    def body(*refs):
        if res is None:
            a_ref, b_ref, o_ref = refs
            r_ref = None
        else:
            a_ref, b_ref, r_ref, o_ref = refs
        kk = pl.program_id(2)
        p = _dot(a_ref[...].astype(BF16), b_ref[...].astype(BF16), dims)

        @pl.when(kk == 0)
        def _():
            o_ref[...] = p if r_ref is None else p + r_ref[...]

        @pl.when(kk > 0)
        def _():
            o_ref[...] += p

    a_spec = pl.BlockSpec((tk, tm), lambda i, j, kk: (kk, i)) if dims == "tn" else pl.BlockSpec((tm, tk), lambda i, j, kk: (i, kk))
    b_spec = pl.BlockSpec((tn, tk), lambda i, j, kk: (j, kk)) if dims == "nt" else pl.BlockSpec((tk, tn), lambda i, j, kk: (kk, j))
    o_spec = pl.BlockSpec((tm, tn), lambda i, j, kk: (i, j))
    in_specs = [a_spec, b_spec] + ([] if res is None else [o_spec])
    args = (a, b) + (() if res is None else (res,))
    return pl.pallas_call(
        body,
        grid=(m // tm, n // tn, nk),
        in_specs=in_specs,
        out_specs=o_spec,
        out_shape=jax.ShapeDtypeStruct((m, n), F32),
        compiler_params=_cparams(("parallel", "parallel", "arbitrary")),
        name=name,
    )(*args)


def _proj_w4(h, w4, name):
    (s, d), (ns, d2, c) = h.shape, w4.shape
    assert d == d2
    tm = _pick(s, (1024, 512, 256, 128))
    tn = _pick(c, (1024, 512, 256, 128))
    nbs = c // tn

    def body(a_ref, b_ref, o_ref):
        o_ref[...] = _dot(a_ref[...].astype(BF16), b_ref[...].astype(BF16), "nn")

    return pl.pallas_call(
        body,
        grid=(s // tm, ns * nbs),
        in_specs=[pl.BlockSpec((tm, d), lambda i, j: (i, 0)), pl.BlockSpec((None, d, tn), lambda i, j: (j // nbs, 0, j % nbs))],
        out_specs=pl.BlockSpec((tm, tn), lambda i, j: (i, j)),
        out_shape=jax.ShapeDtypeStruct((s, ns * c), F32),
        compiler_params=_cparams(("parallel", "parallel")),
        name=name,
    )(h, w4)


def _dh_w4(parts, w4, name):
    ns, d, c = w4.shape
    s = parts[0].shape[0]
    assert len(parts) == ns and all(p.shape == (s, c) for p in parts)
    tm = _pick(s, (1024, 512, 256, 128))
    tk = _pick(c, (1024, 512, 256, 128))
    nkp = c // tk

    def body(*refs):
        a_refs, b_ref, o_ref = refs[:ns], refs[ns], refs[ns + 1]
        kk = pl.program_id(1)
        for p in range(ns):

            @pl.when(kk // nkp == p)
            def _(p=p):
                pv = _dot(a_refs[p][...].astype(BF16), b_ref[...].astype(BF16), "nt")

                @pl.when(kk == 0)
                def _():
                    o_ref[...] = pv

                @pl.when(kk > 0)
                def _():
                    o_ref[...] += pv

    def a_spec(p):
        return pl.BlockSpec((tm, tk), lambda i, kk: (i, jnp.clip(kk - p * nkp, 0, nkp - 1)))

    return pl.pallas_call(
        body,
        grid=(s // tm, ns * nkp),
        in_specs=[a_spec(p) for p in range(ns)] + [pl.BlockSpec((None, d, tk), lambda i, kk: (kk // nkp, 0, kk % nkp))],
        out_specs=pl.BlockSpec((tm, d), lambda i, kk: (i, 0)),
        out_shape=jax.ShapeDtypeStruct((s, d), F32),
        compiler_params=_cparams(("parallel", "arbitrary")),
        name=name,
    )(*parts, w4)


def _dw_w4(h, parts, name):
    s, d = h.shape
    ns = len(parts)
    c = parts[0].shape[1]
    tn = _pick(c, (1024, 512, 256, 128))
    tk = _pick(s, (1024, 512, 256, 128))
    nbp = c // tn

    def body(*refs):
        a_ref, b_refs, o_ref = refs[0], refs[1 : 1 + ns], refs[1 + ns]
        j, kk = pl.program_id(0), pl.program_id(1)
        for p in range(ns):

            @pl.when(j // nbp == p)
            def _(p=p):
                pv = _dot(a_ref[...].astype(BF16), b_refs[p][...].astype(BF16), "tn")

                @pl.when(kk == 0)
                def _():
                    o_ref[...] = pv

                @pl.when(kk > 0)
                def _():
                    o_ref[...] += pv

    def b_spec(p):
        return pl.BlockSpec((tk, tn), lambda j, kk: (kk, jnp.clip(j - p * nbp, 0, nbp - 1)))

    return pl.pallas_call(
        body,
        grid=(ns * nbp, s // tk),
        in_specs=[pl.BlockSpec((tk, d), lambda j, kk: (kk, 0))] + [b_spec(p) for p in range(ns)],
        out_specs=pl.BlockSpec((None, d, tn), lambda j, kk: (j // nbp, 0, j % nbp)),
        out_shape=jax.ShapeDtypeStruct((ns, d, c), F32),
        compiler_params=_cparams(("parallel", "arbitrary")),
        name=name,
    )(h, *parts)


def _rmsnorm_fwd(x, g, name, col_block=0):
    s = x.shape[0]
    w = g.shape[1]
    tr = _pick(s, (512, 256, 128))

    def body(x_ref, g_ref, h_ref):
        xv = x_ref[...]
        r = lax.rsqrt(jnp.mean(xv * xv, axis=-1, keepdims=True) + EPS)
        h_ref[...] = ((xv * r) * g_ref[...]).astype(BF16)

    return pl.pallas_call(
        body,
        grid=(s // tr,),
        in_specs=[pl.BlockSpec((tr, w), lambda i: (i, col_block)), pl.BlockSpec((1, w), lambda i: (0, 0))],
        out_specs=pl.BlockSpec((tr, w), lambda i: (i, 0)),
        out_shape=jax.ShapeDtypeStruct((s, w), BF16),
        compiler_params=_cparams(("parallel",)),
        name=name,
    )(x, g)


def _rmsnorm_bwd(x, g, dh, name, col_block=0, dres=None):
    s = x.shape[0]
    w = g.shape[1]
    tr = _pick(s, (512, 256, 128))

    def body(*refs):
        if dres is None:
            x_ref, g_ref, dh_ref, dx_ref, dxb_ref, dg_ref = refs
        else:
            x_ref, g_ref, dh_ref, dr_ref, dx_ref, dxb_ref, dg_ref = refs
        i = pl.program_id(0)
        xv = x_ref[...]
        r = lax.rsqrt(jnp.mean(xv * xv, axis=-1, keepdims=True) + EPS)
        xh = xv * r
        dhv = dh_ref[...]
        dyg = dhv * g_ref[...]
        dx = r * (dyg - xh * jnp.mean(dyg * xh, axis=-1, keepdims=True))
        if dres is not None:
            dx = dx + dr_ref[...]
        dx_ref[...] = dx
        dxb_ref[...] = dx.astype(BF16)
        part = jnp.sum(dhv * xh, axis=0, keepdims=True)

        @pl.when(i == 0)
        def _():
            dg_ref[...] = part

        @pl.when(i > 0)
        def _():
            dg_ref[...] += part

    row = pl.BlockSpec((tr, w), lambda i: (i, 0))
    in_specs = [pl.BlockSpec((tr, w), lambda i: (i, col_block)), pl.BlockSpec((1, w), lambda i: (0, 0)), row]
    args = [x, g, dh]
    if dres is not None:
        in_specs.append(row)
        args.append(dres)
    return pl.pallas_call(
        body,
        grid=(s // tr,),
        in_specs=in_specs,
        out_specs=[row, row, pl.BlockSpec((1, w), lambda i: (0, 0))],
        out_shape=[jax.ShapeDtypeStruct((s, w), F32), jax.ShapeDtypeStruct((s, w), BF16), jax.ShapeDtypeStruct((1, w), F32)],
        compiler_params=_cparams(("arbitrary",)),
        name=name,
    )(*args)


def _loss_head(x, g, target, name):
    s, d = x.shape
    tr = _pick(s, (512, 256, 128))

    def body(x_ref, g_ref, t_ref, dx_ref, dxb_ref, dg_ref, loss_ref):
        i = pl.program_id(0)
        xv = x_ref[...]
        gv = g_ref[...]
        r = lax.rsqrt(jnp.mean(xv * xv, axis=-1, keepdims=True) + EPS)
        xh = xv * r
        err = xh * gv - t_ref[...]
        lpart = 0.5 * jnp.sum(jnp.mean(err * err, axis=-1, keepdims=True), axis=0, keepdims=True)
        dy = err / d
        dyg = dy * gv
        dx = r * (dyg - xh * jnp.mean(dyg * xh, axis=-1, keepdims=True))
        dx_ref[...] = dx
        dxb_ref[...] = dx.astype(BF16)
        part = jnp.sum(dy * xh, axis=0, keepdims=True)
        lrow = jnp.broadcast_to(lpart, (1, LANES))

        @pl.when(i == 0)
        def _():
            dg_ref[...] = part
            loss_ref[...] = lrow

        @pl.when(i > 0)
        def _():
            dg_ref[...] += part
            loss_ref[...] += lrow

    row = pl.BlockSpec((tr, d), lambda i: (i, 0))
    vec = pl.BlockSpec((1, d), lambda i: (0, 0))
    return pl.pallas_call(
        body,
        grid=(s // tr,),
        in_specs=[row, vec, row],
        out_specs=[row, row, vec, pl.BlockSpec((1, LANES), lambda i: (0, 0))],
        out_shape=[
            jax.ShapeDtypeStruct((s, d), F32),
            jax.ShapeDtypeStruct((s, d), BF16),
            jax.ShapeDtypeStruct((1, d), F32),
            jax.ShapeDtypeStruct((1, LANES), F32),
        ],
        compiler_params=_cparams(("arbitrary",)),
        name=name,
    )(x, g, target)


def _sigmoid(x):
    return 1.0 / (1.0 + jnp.exp(-x))


def _gate_fwd(o, proj, gate_blk, name):
    s, di = o.shape
    tr = _pick(s, (256, 128))

    def body(o_ref, gate_ref, g_ref):
        gt = gate_ref[...]
        g_ref[...] = (o_ref[...] * (gt * _sigmoid(gt))).astype(BF16)

    row = pl.BlockSpec((tr, di), lambda i: (i, 0))
    return pl.pallas_call(
        body,
        grid=(s // tr,),
        in_specs=[row, pl.BlockSpec((tr, di), lambda i: (i, gate_blk))],
        out_specs=row,
        out_shape=jax.ShapeDtypeStruct((s, di), BF16),
        compiler_params=_cparams(("parallel",)),
        name=name,
    )(o, proj)


def _gate_bwd(dg, o, proj, gate_blk, name):
    s, di = o.shape
    tr = _pick(s, (256, 128))

    def body(dg_ref, o_ref, gate_ref, do_ref, dgate_ref):
        gt = gate_ref[...]
        sg = _sigmoid(gt)
        dgv = dg_ref[...]
        do_ref[...] = dgv * (gt * sg)
        dgate_ref[...] = (dgv * o_ref[...] * (sg * (1.0 + gt * (1.0 - sg)))).astype(BF16)

    row = pl.BlockSpec((tr, di), lambda i: (i, 0))
    return pl.pallas_call(
        body,
        grid=(s // tr,),
        in_specs=[row, row, pl.BlockSpec((tr, di), lambda i: (i, gate_blk))],
        out_specs=[row, row],
        out_shape=[jax.ShapeDtypeStruct((s, di), F32), jax.ShapeDtypeStruct((s, di), BF16)],
        compiler_params=_cparams(("parallel",)),
        name=name,
    )(dg, o, proj)


def _iotas(tq, tk):
    return lax.broadcasted_iota(jnp.int32, (tq, tk), 0), lax.broadcasted_iota(jnp.int32, (tq, tk), 1)


def _softplus(s):
    return jnp.maximum(s, 0.0) + jnp.log(1.0 + jnp.exp(-jnp.abs(s)))


def _split2(v):
    hi = v.astype(BF16)
    lo = (v - hi.astype(F32)).astype(BF16)
    return hi, lo


def _cat2(v):
    return jnp.concatenate(_split2(v), axis=1)


def _tri2(keep):
    m = keep.astype(BF16)
    return jnp.concatenate([m, m], axis=0)


def _split3(v):
    a = v.astype(BF16)
    r1 = v - a.astype(F32)
    b = r1.astype(BF16)
    c = (r1 - b.astype(F32)).astype(BF16)
    return a, b, c


SB_TQ = 512
SB_TK = 128
SB_UNROLL = 4


def _sb_fwd(proj, n_heads, name):
    s = proj.shape[0]
    d = HEAD_DIM
    t = min(SB_TQ, s)
    tk = min(SB_TK, s)
    r = t // tk
    un = SB_UNROLL if r % SB_UNROLL == 0 else 1
    scale = d**-0.5

    def body(q_ref, k_ref, v_ref, o_ref, lt_ref, kb_ref, vb_ref):
        i = pl.program_id(1)

        @pl.when(i == 0)
        def _():
            kb_ref[...] = k_ref[...].astype(BF16)
            vb_ref[...] = v_ref[...].astype(BF16)

        q = (q_ref[...] * scale).astype(BF16)
        rows, cols = _iotas(t, tk)
        trows, tcols = _iotas(tk, tk)
        tri = (trows > tcols).astype(BF16)

        def block(kb, cl, acc, diag):
            k0 = pl.multiple_of(kb * tk, tk)
            sc = _dot(q, kb_ref[pl.ds(k0, tk), :], "nt")
            sp = _softplus(sc)
            ls = -sp
            if diag is not None:
                strict = cols + diag * tk < rows
                ls = jnp.where(strict, ls, 0.0)
            hi, lo = _split2(ls)
            later = _dot(hi, tri, "nn") + _dot(lo, tri, "nn")
            w = jnp.exp((sc - sp) + later + cl)
            if diag is not None:
                w = jnp.where(strict, w, 0.0)
            acc = acc + _dot(w.astype(BF16), vb_ref[pl.ds(k0, tk), :], "nn")
            return cl + jnp.sum(ls, axis=1, keepdims=True), acc

        cl, acc = jnp.zeros((t, 1), F32), jnp.zeros((t, d), F32)
        for dd in reversed(range(r)):
            cl, acc = block(i * r + dd, cl, acc, dd)

        def loop(j, carry):
            for u in range(un):
                carry = block(i * r - 1 - (un * j + u), carry[0], carry[1], None)
            return carry

        cl, acc = lax.fori_loop(0, (i * r) // un, loop, (cl, acc))
        o_ref[...] = acc
        lt_ref[...] = cl

    h = n_heads
    return pl.pallas_call(
        body,
        grid=(h, s // t),
        in_specs=[
            pl.BlockSpec((t, d), lambda hh, i: (i, hh)),
            pl.BlockSpec((s, d), lambda hh, i: (0, h + hh)),
            pl.BlockSpec((s, d), lambda hh, i: (0, 2 * h + hh)),
        ],
        out_specs=[pl.BlockSpec((t, d), lambda hh, i: (i, hh)), pl.BlockSpec((None, t, 1), lambda hh, i: (hh, i, 0))],
        out_shape=[jax.ShapeDtypeStruct((s, h * d), F32), jax.ShapeDtypeStruct((h, s, 1), F32)],
        scratch_shapes=[pltpu.VMEM((s, d), BF16), pltpu.VMEM((s, d), BF16)],
        compiler_params=_cparams(("arbitrary", "arbitrary")),
        name=name,
    )(proj, proj, proj)


def _sb_bwd(proj, ltot, do, n_heads, name):
    s = proj.shape[0]
    d = HEAD_DIM
    t = min(SB_TQ, s)
    tk = min(SB_TK, s)
    r = t // tk
    un = SB_UNROLL if r % SB_UNROLL == 0 else 1
    nq = s // t
    scale = d**-0.5

    def body(q_ref, k_ref, v_ref, lt_ref, do_ref, dq_ref, dko_ref, dvo_ref, kb_ref, vb_ref, dk_ref, dv_ref):
        i = pl.program_id(1)

        @pl.when(i == 0)
        def _():
            kb_ref[...] = k_ref[...].astype(BF16)
            vb_ref[...] = v_ref[...].astype(BF16)
            dk_ref[...] = jnp.zeros_like(dk_ref)
            dv_ref[...] = jnp.zeros_like(dv_ref)

        q = (q_ref[...] * scale).astype(BF16)
        dob = do_ref[...].astype(BF16)
        ltv = lt_ref[...]
        rows, cols = _iotas(t, tk)
        trows, tcols = _iotas(tk, tk)
        upto = _tri2(trows <= tcols)
        before = _tri2(trows < tcols)

        def block(kb, cp, cc, dq, diag):
            k0 = pl.multiple_of(kb * tk, tk)
            kk = kb_ref[pl.ds(k0, tk), :]
            sc = _dot(q, kk, "nt")
            sp = _softplus(sc)
            ls = -sp
            if diag is not None:
                strict = cols + diag * tk < rows
                ls = jnp.where(strict, ls, 0.0)
            prefix = _dot(_cat2(ls), upto, "nn") + cp
            lsig = sc - sp
            w = jnp.exp(lsig + (ltv - prefix))
            if diag is not None:
                w = jnp.where(strict, w, 0.0)
            da = _dot(dob, vb_ref[pl.ds(k0, tk), :], "nt") * w
            csum = _dot(_cat2(da), before, "nn") + cc
            beta = jnp.exp(lsig)
            dz = da * (1.0 - beta) - beta * csum
            if diag is not None:
                dz = jnp.where(strict, dz, 0.0)
            dzb = dz.astype(BF16)
            dq = dq + _dot(dzb, kk, "nn")
            dk_ref[pl.ds(k0, tk), :] += _dot(dzb, q, "tn")
            dv_ref[pl.ds(k0, tk), :] += _dot(w.astype(BF16), dob, "tn")
            return cp + jnp.sum(ls, axis=1, keepdims=True), cc + jnp.sum(da, axis=1, keepdims=True), dq

        def loop(j, carry):
            for u in range(un):
                carry = block(un * j + u, carry[0], carry[1], carry[2], None)
            return carry

        z1 = jnp.zeros((t, 1), F32)
        cp, cc, dq = lax.fori_loop(0, (i * r) // un, loop, (z1, z1, jnp.zeros((t, d), F32)))
        for dd in range(r):
            cp, cc, dq = block(i * r + dd, cp, cc, dq, dd)
        dq_ref[...] = (dq * scale).astype(BF16)

        @pl.when(i == nq - 1)
        def _():
            dko_ref[...] = dk_ref[...].astype(BF16)
            dvo_ref[...] = dv_ref[...].astype(BF16)

    h = n_heads
    qblk = pl.BlockSpec((t, d), lambda hh, i: (i, hh))
    full = pl.BlockSpec((s, d), lambda hh, i: (0, hh))
    shp = jax.ShapeDtypeStruct((s, h * d), BF16)
    return pl.pallas_call(
        body,
        grid=(h, nq),
        in_specs=[
            qblk,
            pl.BlockSpec((s, d), lambda hh, i: (0, h + hh)),
            pl.BlockSpec((s, d), lambda hh, i: (0, 2 * h + hh)),
            pl.BlockSpec((None, t, 1), lambda hh, i: (hh, i, 0)),
            qblk,
        ],
        out_specs=[qblk, full, full],
        out_shape=[shp, shp, shp],
        scratch_shapes=[pltpu.VMEM((s, d), BF16), pltpu.VMEM((s, d), BF16), pltpu.VMEM((s, d), F32), pltpu.VMEM((s, d), F32)],
        compiler_params=_cparams(("arbitrary", "arbitrary")),
        name=name,
    )(proj, proj, proj, ltot, do)


SM_TQ = 512
SM_TK = 256
SM_UNROLL = 2


def _allowed(mode, rows, cols, q0, k0):
    r = rows + q0
    c = cols + k0
    if mode == "causal":
        return c <= r
    return (c >> CHUNK_SHIFT) <= (r >> CHUNK_SHIFT)


def _sm_fwd(q_arr, k_arr, v_arr, n_heads, dqk, q_blk, k_blk, v_blk, mode, name, crow=None, ccol=None):
    s = q_arr.shape[0]
    dv = HEAD_DIM
    t = min(SM_TQ, s)
    tk = min(SM_TK, s)
    r = t // tk
    un = SM_UNROLL if r % SM_UNROLL == 0 else 1
    bias = crow is not None
    scale = (HEAD_DIM if mode == "causal" else HEAD_DIM + MLA_ROPE) ** -0.5

    def body(*refs):
        if bias:
            q_ref, k_ref, v_ref, cr_ref, cc_ref, o_ref, lse_ref, kb_ref, vb_ref = refs
        else:
            q_ref, k_ref, v_ref, o_ref, lse_ref, kb_ref, vb_ref = refs
        i = pl.program_id(1)

        @pl.when(i == 0)
        def _():
            kb_ref[...] = k_ref[...].astype(BF16)
            vb_ref[...] = v_ref[...].astype(BF16)

        q = (q_ref[...] * scale).astype(BF16)
        q0 = i * t
        rows, cols = _iotas(t, tk)
        crv = cr_ref[...] if bias else None

        def block(kb, m, l, acc, masked):
            k0 = pl.multiple_of(kb * tk, tk)
            sc = _dot(q, kb_ref[pl.ds(k0, tk), :], "nt")
            if bias:
                sc = sc + crv - cc_ref[pl.ds(kb, 1), :]
            if masked:
                sc = jnp.where(_allowed(mode, rows, cols, q0, k0), sc, NEG)
            m_new = jnp.maximum(m, jnp.max(sc, axis=1, keepdims=True))
            alpha = jnp.exp(m - m_new)
            p = jnp.exp(sc - m_new)
            l = alpha * l + jnp.sum(p, axis=1, keepdims=True)
            acc = alpha * acc + _dot(p.astype(BF16), vb_ref[pl.ds(k0, tk), :], "nn")
            return m_new, l, acc

        def loop(j, carry):
            for u in range(un):
                carry = block(un * j + u, carry[0], carry[1], carry[2], False)
            return carry

        init = (jnp.full((t, 1), NEG, F32), jnp.zeros((t, 1), F32), jnp.zeros((t, dv), F32))
        m, l, acc = lax.fori_loop(0, (i * r) // un, loop, init)
        for dd in range(r):
            m, l, acc = block(i * r + dd, m, l, acc, True)
        o_ref[...] = acc / l
        lse_ref[...] = m + jnp.log(l)

    h = n_heads
    in_specs = [
        pl.BlockSpec((t, dqk), lambda hh, i: (i, q_blk(hh))),
        pl.BlockSpec((s, dqk), lambda hh, i: (0, k_blk(hh))),
        pl.BlockSpec((s, dv), lambda hh, i: (0, v_blk(hh))),
    ]
    args = [q_arr, k_arr, v_arr]
    if bias:
        in_specs += [pl.BlockSpec((None, t, 1), lambda hh, i: (hh, i, 0)), pl.BlockSpec((None, s // tk, tk), lambda hh, i: (hh, 0, 0))]
        args += [crow, ccol]
    return pl.pallas_call(
        body,
        grid=(h, s // t),
        in_specs=in_specs,
        out_specs=[pl.BlockSpec((t, dv), lambda hh, i: (i, hh)), pl.BlockSpec((None, t, 1), lambda hh, i: (hh, i, 0))],
        out_shape=[jax.ShapeDtypeStruct((s, h * dv), F32), jax.ShapeDtypeStruct((h, s, 1), F32)],
        scratch_shapes=[pltpu.VMEM((s, dqk), BF16), pltpu.VMEM((s, dv), BF16)],
        compiler_params=_cparams(("arbitrary", "arbitrary")),
        name=name,
    )(*args)


def _sm_bwd(q_arr, k_arr, v_arr, o, do, lse, n_heads, dqk, q_blk, k_blk, v_blk, mode, name, crow=None, ccol=None):
    s = q_arr.shape[0]
    dv = HEAD_DIM
    t = min(SM_TQ, s)
    tk = min(SM_TK, s)
    r = t // tk
    un = SM_UNROLL if r % SM_UNROLL == 0 else 1
    nq = s // t
    bias = crow is not None
    scale = (HEAD_DIM if mode == "causal" else HEAD_DIM + MLA_ROPE) ** -0.5

    def body(*refs):
        if bias:
            q_ref, k_ref, v_ref, o_ref, do_ref, lse_ref, cr_ref, cc_ref, dq_ref, dk_ref, dv_ref, dcc_ref, dcr_ref, kb_ref, vb_ref = refs
        else:
            q_ref, k_ref, v_ref, o_ref, do_ref, lse_ref, dq_ref, dk_ref, dv_ref, kb_ref, vb_ref = refs
        i = pl.program_id(1)

        @pl.when(i == 0)
        def _():
            kb_ref[...] = k_ref[...].astype(BF16)
            vb_ref[...] = v_ref[...].astype(BF16)
            dk_ref[...] = jnp.zeros_like(dk_ref)
            dv_ref[...] = jnp.zeros_like(dv_ref)
            if bias:
                dcc_ref[...] = jnp.zeros_like(dcc_ref)

        q = (q_ref[...] * scale).astype(BF16)
        dov = do_ref[...]
        dob = dov.astype(BF16)
        dsum = jnp.sum(dov * o_ref[...], axis=1, keepdims=True)
        lsev = lse_ref[...]
        q0 = i * t
        rows, cols = _iotas(t, tk)
        crv = cr_ref[...] if bias else None

        def block(kb, dq, dr, masked):
            k0 = pl.multiple_of(kb * tk, tk)
            kk = kb_ref[pl.ds(k0, tk), :]
            sc = _dot(q, kk, "nt")
            if bias:
                sc = sc + crv - cc_ref[pl.ds(kb, 1), :]
            if masked:
                sc = jnp.where(_allowed(mode, rows, cols, q0, k0), sc, NEG)
            p = jnp.exp(sc - lsev)
            dz = p * (_dot(dob, vb_ref[pl.ds(k0, tk), :], "nt") - dsum)
            dzb = dz.astype(BF16)
            dk_ref[pl.ds(k0, tk), :] += _dot(dzb, q, "tn")
            dv_ref[pl.ds(k0, tk), :] += _dot(p.astype(BF16), dob, "tn")
            if bias:
                dcc_ref[pl.ds(kb, 1), :] -= jnp.sum(dz, axis=0, keepdims=True)
                dr = dr + jnp.sum(dz, axis=1, keepdims=True)
            return dq + _dot(dzb, kk, "nn"), dr

        def loop(j, carry):
            for u in range(un):
                carry = block(un * j + u, carry[0], carry[1], False)
            return carry

        dq, dr = lax.fori_loop(0, (i * r) // un, loop, (jnp.zeros((t, dqk), F32), jnp.zeros((t, 1), F32)))
        for dd in range(r):
            dq, dr = block(i * r + dd, dq, dr, True)
        dq_ref[...] = dq * scale
        if bias:
            dcr_ref[...] = dr

    h = n_heads
    qblk = pl.BlockSpec((t, dqk), lambda hh, i: (i, q_blk(hh)))
    oblk = pl.BlockSpec((t, dv), lambda hh, i: (i, hh))
    vec = pl.BlockSpec((None, t, 1), lambda hh, i: (hh, i, 0))
    in_specs = [
        qblk,
        pl.BlockSpec((s, dqk), lambda hh, i: (0, k_blk(hh))),
        pl.BlockSpec((s, dv), lambda hh, i: (0, v_blk(hh))),
        oblk,
        oblk,
        vec,
    ]
    args = [q_arr, k_arr, v_arr, o, do, lse]
    out_specs = [
        pl.BlockSpec((t, dqk), lambda hh, i: (i, hh)),
        pl.BlockSpec((s, dqk), lambda hh, i: (0, hh)),
        pl.BlockSpec((s, dv), lambda hh, i: (0, hh)),
    ]
    out_shape = [
        jax.ShapeDtypeStruct((s, h * dqk), F32),
        jax.ShapeDtypeStruct((s, h * dqk), F32),
        jax.ShapeDtypeStruct((s, h * dv), F32),
    ]
    if bias:
        ccs = pl.BlockSpec((None, s // tk, tk), lambda hh, i: (hh, 0, 0))
        in_specs += [vec, ccs]
        args += [crow, ccol]
        out_specs += [ccs, vec]
        out_shape += [jax.ShapeDtypeStruct((h, s // tk, tk), F32), jax.ShapeDtypeStruct((h, s, 1), F32)]
    return pl.pallas_call(
        body,
        grid=(h, nq),
        in_specs=in_specs,
        out_specs=out_specs,
        out_shape=out_shape,
        scratch_shapes=[pltpu.VMEM((s, dqk), BF16), pltpu.VMEM((s, dv), BF16)],
        compiler_params=_cparams(("arbitrary", "arbitrary")),
        name=name,
    )(*args)


def _rope_tables(pos):
    half = MLA_ROPE // 2
    inv_freq = ROPE_BASE ** (-jnp.arange(0, MLA_ROPE, 2, dtype=F32) / MLA_ROPE)
    ang = pos.astype(F32)[:, None] * inv_freq
    cos, sin = jnp.cos(ang), jnp.sin(ang)
    z = lambda n: jnp.zeros((pos.shape[0], n), F32)
    tc = jnp.concatenate([cos, cos, z(LANES - 2 * half)], axis=1)
    ta = jnp.concatenate([-sin, z(LANES - half)], axis=1)
    tb = jnp.concatenate([z(half), sin, z(LANES - 2 * half)], axis=1)
    return tc, ta, tb


def _rot(v, tc, ta, tb, sign):
    half = MLA_ROPE // 2
    return v * tc + sign * (pltpu.roll(v, LANES - half, 1) * ta + pltpu.roll(v, half, 1) * tb)


def _mla_assemble(qpre, kv, proj1, kr_blk, tabs, n_heads, name):
    s = qpre.shape[0]
    tr = _pick(s, (512, 256, 128))

    def body(qp_ref, kn_ref, kr_ref, tc_ref, ta_ref, tb_ref, qc_ref, kc_ref):
        tc, ta, tb = tc_ref[...], ta_ref[...], tb_ref[...]
        qc_ref[:, :LANES] = qp_ref[:, :LANES]
        qc_ref[:, LANES:] = _rot(qp_ref[:, LANES:], tc, ta, tb, 1.0)
        kc_ref[:, :LANES] = kn_ref[...]
        kc_ref[:, LANES:] = _rot(kr_ref[...], tc, ta, tb, 1.0)

    tab = pl.BlockSpec((tr, LANES), lambda i, hh: (i, 0))
    wide = pl.BlockSpec((tr, MLA_QK), lambda i, hh: (i, hh))
    shp = jax.ShapeDtypeStruct((s, n_heads * MLA_QK), F32)
    return pl.pallas_call(
        body,
        grid=(s // tr, n_heads),
        in_specs=[
            wide,
            pl.BlockSpec((tr, LANES), lambda i, hh: (i, 2 * hh)),
            pl.BlockSpec((tr, LANES), lambda i, hh: (i, kr_blk)),
            tab,
            tab,
            tab,
        ],
        out_specs=[wide, wide],
        out_shape=[shp, shp],
        compiler_params=_cparams(("parallel", "parallel")),
        name=name,
    )(qpre, kv, proj1, *tabs)


def _mla_disassemble(dqcat, dkcat, dv, tabs, n_heads, name):
    s = dqcat.shape[0]
    tr = _pick(s, (512, 256, 128))

    def body(dq_ref, dk_ref, dv_ref, tc_ref, ta_ref, tb_ref, dqp_ref, dkv_ref, dkr_ref):
        hh = pl.program_id(1)
        tc, ta, tb = tc_ref[...], ta_ref[...], tb_ref[...]
        dqp_ref[:, :LANES] = dq_ref[:, :LANES].astype(BF16)
        dqp_ref[:, LANES:] = _rot(dq_ref[:, LANES:], tc, ta, tb, -1.0).astype(BF16)
        dkv_ref[:, :LANES] = dk_ref[:, :LANES].astype(BF16)
        dkv_ref[:, LANES:] = dv_ref[...].astype(BF16)
        part = dk_ref[:, LANES:]

        @pl.when(hh == 0)
        def _():
            dkr_ref[...] = part

        @pl.when(hh > 0)
        def _():
            dkr_ref[...] += part

        @pl.when(hh == n_heads - 1)
        def _():
            dkr_ref[...] = _rot(dkr_ref[...], tc, ta, tb, -1.0)

    tab = pl.BlockSpec((tr, LANES), lambda i, hh: (i, 0))
    wide = pl.BlockSpec((tr, MLA_QK), lambda i, hh: (i, hh))
    shp = jax.ShapeDtypeStruct((s, n_heads * MLA_QK), BF16)
    return pl.pallas_call(
        body,
        grid=(s // tr, n_heads),
        in_specs=[wide, wide, pl.BlockSpec((tr, LANES), lambda i, hh: (i, hh)), tab, tab, tab],
        out_specs=[wide, wide, tab],
        out_shape=[shp, shp, jax.ShapeDtypeStruct((s, LANES), F32)],
        compiler_params=_cparams(("parallel", "arbitrary")),
        name=name,
    )(dqcat, dkcat, dv, *tabs)


def _forget_scan(proj2, f_blk, bias, name):
    s = proj2.shape[0]
    n = LANES

    def body(f_ref, b_ref, c_ref):
        rows, cols = _iotas(n, n)
        tri = (cols <= rows).astype(BF16)

        def step(j, carry):
            r0 = pl.multiple_of(j * n, n)
            f = f_ref[pl.ds(r0, n), :] + b_ref[...]
            lf = jnp.minimum(f, 0.0) - jnp.log1p(jnp.exp(-jnp.abs(f)))
            a, b, c = _split3(lf)
            cs = _dot(tri, a, "nn") + _dot(tri, b, "nn") + _dot(tri, c, "nn") + carry
            c_ref[pl.ds(r0, n), :] = cs
            return cs[n - 1 : n, :]

        lax.fori_loop(0, s // n, step, jnp.zeros((1, n), F32))

    return pl.pallas_call(
        body,
        grid=(1,),
        in_specs=[pl.BlockSpec((s, n), lambda i: (0, f_blk)), pl.BlockSpec((1, n), lambda i: (0, 0))],
        out_specs=pl.BlockSpec((s, n), lambda i: (0, 0)),
        out_shape=jax.ShapeDtypeStruct((s, n), F32),
        compiler_params=_cparams(("arbitrary",)),
        name=name,
    )(proj2, bias)


def _forget_scan_bwd(dc_col, dc_row, proj2, f_blk, bias, n_heads, name):
    s = proj2.shape[0]
    n = LANES
    nb = s // n

    def body(dcc_ref, dcr_ref, f_ref, b_ref, df_ref, db_ref):
        rows, cols = _iotas(n, n)
        tri = (cols >= rows).astype(BF16)
        live = cols < n_heads

        def step(j, carry):
            acc, dbv = carry
            r0 = pl.multiple_of((nb - 1 - j) * n, n)
            a, b, c = _split3(dcc_ref[pl.ds(r0, n), :] + dcr_ref[pl.ds(r0, n), :])
            dl = _dot(tri, a, "nn") + _dot(tri, b, "nn") + _dot(tri, c, "nn") + acc
            f = f_ref[pl.ds(r0, n), :] + b_ref[...]
            df = jnp.where(live, dl / (1.0 + jnp.exp(f)), 0.0)
            df_ref[pl.ds(r0, n), :] = df.astype(BF16)
            return dl[0:1, :], dbv + jnp.sum(df, axis=0, keepdims=True)

        z = jnp.zeros((1, n), F32)
        _, dbv = lax.fori_loop(0, nb, step, (z, z))
        db_ref[...] = dbv

    return pl.pallas_call(
        body,
        grid=(1,),
        in_specs=[
            pl.BlockSpec((s, n), lambda i: (0, 0)),
            pl.BlockSpec((s, n), lambda i: (0, 0)),
            pl.BlockSpec((s, n), lambda i: (0, f_blk)),
            pl.BlockSpec((1, n), lambda i: (0, 0)),
        ],
        out_specs=[pl.BlockSpec((s, n), lambda i: (0, 0)), pl.BlockSpec((1, n), lambda i: (0, 0))],
        out_shape=[jax.ShapeDtypeStruct((s, n), BF16), jax.ShapeDtypeStruct((1, n), F32)],
        compiler_params=_cparams(("arbitrary",)),
        name=name,
    )(dc_col, dc_row, proj2, bias)


def _adamw(w, g, m, v, name):
    r, c = w.shape
    tr = _pick(r, (128, 64, 32, 16, 8))
    c1 = 1.0 - ADAM_B1**ADAM_STEP
    c2 = 1.0 - ADAM_B2**ADAM_STEP

    def body(w_ref, g_ref, m_ref, v_ref, d_ref, mo_ref, vo_ref):
        gv = g_ref[...]
        mn = ADAM_B1 * m_ref[...] + (1.0 - ADAM_B1) * gv
        vn = ADAM_B2 * v_ref[...] + (1.0 - ADAM_B2) * (gv * gv)
        mo_ref[...] = mn
        vo_ref[...] = vn
        d_ref[...] = -ADAM_LR * ((mn / c1) / (jnp.sqrt(vn / c2) + ADAM_EPS) + ADAM_WD * w_ref[...])

    blk = pl.BlockSpec((tr, c), lambda i: (i, 0))
    shp = jax.ShapeDtypeStruct((r, c), F32)
    return pl.pallas_call(
        body,
        grid=(r // tr,),
        in_specs=[blk, blk, blk, blk],
        out_specs=[blk, blk, blk],
        out_shape=[shp, shp, shp],
        compiler_params=_cparams(("parallel",)),
        name=name,
    )(w, g, m, v)


def _mesh_pos():
    return lax.axis_index("x"), lax.axis_index("y"), lax.axis_index("c")


def _other_chips(x, y):
    return [(1 - x, y), (x, 1 - y), (1 - x, 1 - y)]


ANY = pl.BlockSpec(memory_space=pl.ANY)


def _gather_weights(wp):
    rp, wd = wp.shape
    half = rp // 2

    def body(w_ref, out_ref, send_sems, recv_sems):
        x, y, c = _mesh_pos()
        me = 2 * x + y
        chips = _other_chips(x, y)

        def region(chip, hc):
            return out_ref.at[chip, pl.ds(hc * half, half), :]

        def copy(k, src, dst, to):
            return pltpu.make_async_remote_copy(
                src_ref=src, dst_ref=dst, send_sem=send_sems.at[k], recv_sem=recv_sems.at[k], device_id=to, device_id_type=MESH
            )

        first = [copy(j, w_ref.at[pl.ds(c * half, half), :], region(me, c), (cx, cy, c)) for j, (cx, cy) in enumerate(chips)]
        for cp in first:
            cp.start()
        passed = [copy(3 + j, region(2 * cx + cy, c), region(2 * cx + cy, c), (x, y, 1 - c)) for j, (cx, cy) in enumerate(chips)]
        for j, (cx, cy) in enumerate(chips):
            copy(j, region(2 * cx + cy, c), region(2 * cx + cy, c), (x, y, c)).wait_recv()
            passed[j].start()
        for j, (cx, cy) in enumerate(chips):
            copy(3 + j, region(2 * cx + cy, 1 - c), region(2 * cx + cy, 1 - c), (x, y, c)).wait_recv()
        for cp in first + passed:
            cp.wait_send()

    return pl.pallas_call(
        body,
        in_specs=[ANY],
        out_specs=ANY,
        out_shape=jax.ShapeDtypeStruct((N_CHIPS, rp, wd), wp.dtype),
        scratch_shapes=[pltpu.SemaphoreType.DMA((6,)), pltpu.SemaphoreType.DMA((6,))],
        name="gather_weights",
    )(wp)


def _place_own(wall, wp, pos):
    rp, wd = wp.shape

    def body(x_ref, y_ref, c_ref, wall_ref, w_ref, o_ref):
        o_ref[0] = w_ref[...]

    grid_spec = pltpu.PrefetchScalarGridSpec(
        num_scalar_prefetch=3,
        grid=(rp // PACK_TR,),
        in_specs=[ANY, pl.BlockSpec((PACK_TR, wd), lambda i, xr, yr, cr: (i, 0))],
        out_specs=pl.BlockSpec((1, PACK_TR, wd), lambda i, xr, yr, cr: (2 * xr[0] + yr[0], i, 0)),
    )
    return pl.pallas_call(
        body,
        grid_spec=grid_spec,
        out_shape=jax.ShapeDtypeStruct(wall.shape, wall.dtype),
        input_output_aliases={3: 0},
        compiler_params=_cparams(("parallel",)),
        name="place_own_shard",
    )(*pos, wall, wp)


def _pair_exchange(g):
    _, rp, wd = g.shape
    half = rp // 2

    def body(g_ref, out_ref, send_sem, recv_sem):
        x, y, c = _mesh_pos()
        cp = pltpu.make_async_remote_copy(
            src_ref=g_ref.at[:, pl.ds((1 - c) * half, half), :],
            dst_ref=out_ref,
            send_sem=send_sem,
            recv_sem=recv_sem,
            device_id=(x, y, 1 - c),
            device_id_type=MESH,
        )
        cp.start()
        cp.wait()

    return pl.pallas_call(
        body,
        in_specs=[ANY],
        out_specs=ANY,
        out_shape=jax.ShapeDtypeStruct((N_CHIPS, half, wd), g.dtype),
        scratch_shapes=[pltpu.SemaphoreType.DMA, pltpu.SemaphoreType.DMA],
        name="rs_pair_exchange",
    )(g)


def _pair_add(g, recv, pos):
    _, rp, wd = g.shape
    half = rp // 2
    nb = half // PACK_TR

    def body(x_ref, y_ref, c_ref, g_ref, r_ref, o_ref):
        o_ref[...] = (g_ref[...] + r_ref[...]).astype(BF16)

    blk = (1, PACK_TR, wd)
    grid_spec = pltpu.PrefetchScalarGridSpec(
        num_scalar_prefetch=3,
        grid=(N_CHIPS, nb),
        in_specs=[
            pl.BlockSpec(blk, lambda j, i, xr, yr, cr: (j, cr[0] * nb + i, 0)),
            pl.BlockSpec(blk, lambda j, i, xr, yr, cr: (j, i, 0)),
        ],
        out_specs=pl.BlockSpec(blk, lambda j, i, xr, yr, cr: (j, i, 0)),
    )
    return pl.pallas_call(
        body,
        grid_spec=grid_spec,
        out_shape=jax.ShapeDtypeStruct((N_CHIPS, half, wd), BF16),
        compiler_params=_cparams(("parallel", "parallel")),
        name="rs_pair_add",
    )(*pos, g, recv)


def _chip_exchange(sp):
    _, rh, wd = sp.shape

    def body(s_ref, out_ref, send_sems, recv_sems):
        x, y, c = _mesh_pos()
        me = 2 * x + y
        chips = _other_chips(x, y)
        sends = []
        for j, (cx, cy) in enumerate(chips):
            cp = pltpu.make_async_remote_copy(
                src_ref=s_ref.at[2 * cx + cy],
                dst_ref=out_ref.at[me],
                send_sem=send_sems.at[j],
                recv_sem=recv_sems.at[j],
                device_id=(cx, cy, c),
                device_id_type=MESH,
            )
            cp.start()
            sends.append(cp)
        for j, (cx, cy) in enumerate(chips):
            pltpu.make_async_remote_copy(
                src_ref=s_ref.at[me],
                dst_ref=out_ref.at[2 * cx + cy],
                send_sem=send_sems.at[j],
                recv_sem=recv_sems.at[j],
                device_id=(x, y, c),
                device_id_type=MESH,
            ).wait_recv()
        for cp in sends:
            cp.wait_send()

    return pl.pallas_call(
        body,
        in_specs=[ANY],
        out_specs=ANY,
        out_shape=jax.ShapeDtypeStruct(sp.shape, sp.dtype),
        scratch_shapes=[pltpu.SemaphoreType.DMA((3,)), pltpu.SemaphoreType.DMA((3,))],
        name="rs_chip_exchange",
    )(sp)


def _sum_slots(own, slots, pos):
    _, rh, wd = slots.shape
    nb = rh // PACK_TR

    def body(x_ref, y_ref, c_ref, own_ref, a_ref, b_ref, d_ref, o_ref):
        f = lambda r: r[0].astype(F32)
        o_ref[...] = ((f(own_ref) + f(a_ref)) + f(b_ref)) + f(d_ref)

    blk = (1, PACK_TR, wd)

    def other(k):
        return pl.BlockSpec(blk, lambda i, xr, yr, cr: (k + (k >= 2 * xr[0] + yr[0]).astype(jnp.int32), i, 0))

    grid_spec = pltpu.PrefetchScalarGridSpec(
        num_scalar_prefetch=3,
        grid=(nb,),
        in_specs=[pl.BlockSpec(blk, lambda i, xr, yr, cr: (2 * xr[0] + yr[0], i, 0)), other(0), other(1), other(2)],
        out_specs=pl.BlockSpec((PACK_TR, wd), lambda i, xr, yr, cr: (cr[0] * nb + i, 0)),
    )
    return pl.pallas_call(
        body,
        grid_spec=grid_spec,
        out_shape=jax.ShapeDtypeStruct((2 * rh, wd), F32),
        compiler_params=_cparams(("parallel",)),
        name="rs_sum_slots",
    )(*pos, own, slots, slots, slots)


def _pair_gather(t):
    rh = t.shape[0] // 2

    def body(t_ref, out_ref, send_sem, recv_sem):
        x, y, c = _mesh_pos()
        cp = pltpu.make_async_remote_copy(
            src_ref=t_ref.at[pl.ds(c * rh, rh), :],
            dst_ref=out_ref.at[pl.ds(c * rh, rh), :],
            send_sem=send_sem,
            recv_sem=recv_sem,
            device_id=(x, y, 1 - c),
            device_id_type=MESH,
        )
        cp.start()
        cp.wait_send()
        pltpu.make_async_remote_copy(
            src_ref=t_ref.at[pl.ds((1 - c) * rh, rh), :],
            dst_ref=out_ref.at[pl.ds((1 - c) * rh, rh), :],
            send_sem=send_sem,
            recv_sem=recv_sem,
            device_id=(x, y, c),
            device_id_type=MESH,
        ).wait_recv()

    return pl.pallas_call(
        body,
        in_specs=[ANY],
        out_specs=ANY,
        out_shape=jax.ShapeDtypeStruct(t.shape, t.dtype),
        input_output_aliases={0: 0},
        scratch_shapes=[pltpu.SemaphoreType.DMA, pltpu.SemaphoreType.DMA],
        name="rs_pair_gather",
    )(t)


def _allreduce_small(v):
    shape = v.shape
    n_dev = 8

    def body(v_ref, o_ref, slots, send_sems, recv_sems):
        x, y, c = _mesh_pos()
        me = 4 * x + 2 * y + c
        slots[me] = v_ref[...]
        sends = []
        for k in range(1, n_dev):
            fx, fy, fc = (k >> 2) & 1, (k >> 1) & 1, k & 1
            to = (x ^ fx, y ^ fy, c ^ fc)
            cp = pltpu.make_async_remote_copy(
                src_ref=v_ref,
                dst_ref=slots.at[me],
                send_sem=send_sems.at[k - 1],
                recv_sem=recv_sems.at[k - 1],
                device_id=to,
                device_id_type=MESH,
            )
            cp.start()
            sends.append(cp)
        for k in range(1, n_dev):
            fx, fy, fc = (k >> 2) & 1, (k >> 1) & 1, k & 1
            frm = 4 * (x ^ fx) + 2 * (y ^ fy) + (c ^ fc)
            pltpu.make_async_remote_copy(
                src_ref=v_ref,
                dst_ref=slots.at[frm],
                send_sem=send_sems.at[k - 1],
                recv_sem=recv_sems.at[k - 1],
                device_id=(x, y, c),
                device_id_type=MESH,
            ).wait_recv()
        for cp in sends:
            cp.wait_send()
        acc = slots[0]
        for k in range(1, n_dev):
            acc = acc + slots[k]
        o_ref[...] = acc

    vm = pl.BlockSpec(memory_space=pltpu.VMEM)
    return pl.pallas_call(
        body,
        in_specs=[vm],
        out_specs=vm,
        out_shape=jax.ShapeDtypeStruct(shape, F32),
        scratch_shapes=[pltpu.VMEM((n_dev,) + shape, F32), pltpu.SemaphoreType.DMA((n_dev - 1,)), pltpu.SemaphoreType.DMA((n_dev - 1,))],
        name="allreduce_small",
    )(v)


def _pack_layout(shard_shapes):
    offs, rows = [], []
    off = 0
    for r, c in shard_shapes:
        assert (r * c) % PACK_W == 0
        n = r * c // PACK_W
        offs.append(off)
        rows.append(n)
        off += -(-n // 16) * 16
    rp = -(-off // (2 * PACK_TR)) * (2 * PACK_TR)
    return offs, rows, rp


def _pack_rows(parts, offs, rows, rp, lead):
    ends = list(offs[1:]) + [rp]
    nolead = ((0, 0),) * len(lead)
    out = [jnp.pad(p, nolead + ((0, e - o - n), (0, 0))) for p, o, n, e in zip(parts, offs, rows, ends)]
    return jnp.concatenate(out, axis=len(lead))


def kernel(x, positions, ln0, w_in0, w_out0, ln1, w_in1, q_norm1, w_qb1, kv_norm1, w_kvb1, w_out1, ln2, w_in2, b_f2, w_out2, ln3, w_in3, w_out3, final_norm, loss_target, m_ln0, m_w_in0, m_w_out0, m_ln1, m_w_in1, m_q_norm1, m_w_qb1, m_kv_norm1, m_w_kvb1, m_w_out1, m_ln2, m_w_in2, m_b_f2, m_w_out2, m_ln3, m_w_in3, m_w_out3, m_final_norm, v_ln0, v_w_in0, v_w_out0, v_ln1, v_w_in1, v_q_norm1, v_w_qb1, v_kv_norm1, v_w_kvb1, v_w_out1, v_ln2, v_w_in2, v_b_f2, v_w_out2, v_ln3, v_w_in3, v_w_out3, v_final_norm):
    xs = x[0]
    s, d = xs.shape
    di = 4 * w_out0.shape[0]
    nh = di // HEAD_DIM
    idx = tuple(lax.axis_index(a).astype(jnp.int32).reshape(1) for a in ("x", "y", "c"))

    big = [w_in0, w_out0, w_in1, w_qb1, w_kvb1, w_out1, w_in2, w_out2, w_in3, w_out3]
    col_sharded = [True, False, True, True, True, False, True, False, True, False]
    shard_shapes = [w.shape for w in big]
    offs, rows, rp = _pack_layout(shard_shapes)
    wp = _pack_rows([w.astype(BF16).reshape(n, PACK_W) for w, n in zip(big, rows)], offs, rows, rp, ())
    wall = _place_own(_gather_weights(wp), wp, idx)
    as_gathered = [True, False, False, False, False, False, False, False, True, False]
    full = []
    for (r, c), o, n, cs, g4 in zip(shard_shapes, offs, rows, col_sharded, as_gathered):
        slab = wall[:, o : o + n, :].reshape(N_CHIPS, r, c)
        if g4:
            full.append(slab)
        else:
            full.append(slab.transpose(1, 0, 2).reshape(r, N_CHIPS * c) if cs else slab.reshape(N_CHIPS * r, c))
    f_in0, f_out0, f_in1, f_qb1, f_kvb1, f_out1, f_in2, f_out2, f_in3, f_out3 = full

    i_kr = MLA_Q_RANK + MLA_KV_RANK + MLA_ROPE
    w1p = jnp.concatenate([f_in1[:, i_kr:], f_in1[:, :i_kr], jnp.zeros((d, LANES - MLA_ROPE), BF16)], axis=1)
    qlat_blk = di // MLA_Q_RANK
    kvlat_blk = (di + MLA_Q_RANK) // MLA_KV_RANK
    kr_blk = (di + MLA_Q_RANK + MLA_KV_RANK) // LANES
    qk_w = HEAD_DIM + MLA_ROPE
    wqbp = jnp.pad(f_qb1.reshape(MLA_Q_RANK, nh, qk_w), ((0, 0), (0, 0), (0, MLA_QK - qk_w))).reshape(MLA_Q_RANK, nh * MLA_QK)
    n2 = f_in2.shape[1]
    w2p = jnp.pad(f_in2, ((0, 0), (0, 4 * di + LANES - n2)))
    b2p = jnp.pad(b_f2, (0, LANES - nh)).reshape(1, LANES)

    row = lambda v: v.reshape(1, -1)
    tabs = _rope_tables(positions[0])

    def sb_layer_fwd(xin, ln, w_in, w_out, tag):
        h = _rmsnorm_fwd(xin, row(ln), f"norm_fwd_{tag}")
        proj = _proj_w4(h, w_in, f"proj_in_{tag}")
        o, lt = _sb_fwd(proj, nh, f"sb_fwd_{tag}")
        g = _gate_fwd(o, proj, 3, f"gate_fwd_{tag}")
        xout = _matmul(g, w_out, "nn", f"proj_out_{tag}", res=xin)
        return xout, (xin, h, proj, o, lt, g)

    def sb_layer_bwd(dxn, dxnb, saved, ln, w_in, w_out, tag):
        xin, h, proj, o, lt, g = saved
        dgf = _matmul(dxnb, w_out, "nt", f"dgate_in_{tag}")
        dw_out = _matmul(g, dxnb, "tn", f"dw_out_{tag}")
        do, dgate = _gate_bwd(dgf, o, proj, 3, f"gate_bwd_{tag}")
        dq, dk, dv = _sb_bwd(proj, lt, do, nh, f"sb_bwd_{tag}")
        dproj = [dq, dk, dv, dgate]
        dh = _dh_w4(dproj, w_in, f"dh_{tag}")
        dw_in = _dw_w4(h, dproj, f"dw_in_{tag}")
        dx, dxb, dln = _rmsnorm_bwd(xin, row(ln), dh, f"norm_bwd_{tag}", dres=dxn)
        return dx, dxb, dln, dw_in, dw_out

    x1, sv0 = sb_layer_fwd(xs, ln0, f_in0, f_out0, "l0")

    h1 = _rmsnorm_fwd(x1, row(ln1), "norm_fwd_l1")
    proj1 = _matmul(h1, w1p, "nn", "proj_in_l1")
    qn = _rmsnorm_fwd(proj1, row(q_norm1), "qnorm_fwd_l1", col_block=qlat_blk)
    kvn = _rmsnorm_fwd(proj1, row(kv_norm1), "kvnorm_fwd_l1", col_block=kvlat_blk)
    qpre = _matmul(qn, wqbp, "nn", "q_up_l1")
    kv1 = _matmul(kvn, f_kvb1, "nn", "kv_up_l1")
    qcat, kcat = _mla_assemble(qpre, kv1, proj1, kr_blk, tabs, nh, "mla_assemble_l1")
    mla_blk = (lambda hh: hh, lambda hh: hh, lambda hh: 2 * hh + 1)
    o1, lse1 = _sm_fwd(qcat, kcat, kv1, nh, MLA_QK, *mla_blk, "chunk", "mla_fwd_l1")
    g1 = _gate_fwd(o1, proj1, 0, "gate_fwd_l1")
    x2 = _matmul(g1, f_out1, "nn", "proj_out_l1", res=x1)

    h2 = _rmsnorm_fwd(x2, row(ln2), "norm_fwd_l2")
    proj2 = _matmul(h2, w2p, "nn", "proj_in_l2")
    f_blk = 4 * di // LANES
    cum = _forget_scan(proj2, f_blk, b2p, "forget_scan_l2")
    cum_h = cum[:, :nh].T
    t_sm = min(SM_TK, s)
    crow = cum_h.reshape(nh, s, 1)
    ccol = cum_h.reshape(nh, s // t_sm, t_sm)
    fg_blk = (lambda hh: hh, lambda hh: nh + hh, lambda hh: 2 * nh + hh)
    o2, lse2 = _sm_fwd(proj2, proj2, proj2, nh, HEAD_DIM, *fg_blk, "causal", "forget_fwd_l2", crow=crow, ccol=ccol)
    g2 = _gate_fwd(o2, proj2, 3, "gate_fwd_l2")
    x3 = _matmul(g2, f_out2, "nn", "proj_out_l2", res=x2)

    x4, sv3 = sb_layer_fwd(x3, ln3, f_in3, f_out3, "l3")

    dx, dxb, d_final, loss_part = _loss_head(x4, row(final_norm), loss_target[0], "loss_head")
    dx, dxb, d_ln3, dw_in3, dw_out3 = sb_layer_bwd(dx, dxb, sv3, ln3, f_in3, f_out3, "l3")

    dgf2 = _matmul(dxb, f_out2, "nt", "dgate_in_l2")
    dw_out2 = _matmul(g2, dxb, "tn", "dw_out_l2")
    do2, dgate2 = _gate_bwd(dgf2, o2, proj2, 3, "gate_bwd_l2")
    dq2, dk2, dv2, dcc2, dcr2 = _sm_bwd(proj2, proj2, proj2, o2, do2, lse2, nh, HEAD_DIM, *fg_blk, "causal", "forget_bwd_l2", crow=crow, ccol=ccol)
    lanes_of = lambda a: jnp.pad(a.reshape(nh, s).T, ((0, 0), (0, LANES - nh)))
    df2, d_bf = _forget_scan_bwd(lanes_of(dcc2), lanes_of(dcr2), proj2, f_blk, b2p, nh, "forget_scan_bwd_l2")
    dproj2 = jnp.concatenate([dq2.astype(BF16), dk2.astype(BF16), dv2.astype(BF16), dgate2, df2], axis=1)
    dh2 = _matmul(dproj2, w2p, "nt", "dh_l2")
    dw_in2 = _matmul(h2, dproj2, "tn", "dw_in_l2")[:, :n2]
    dx, dxb, d_ln2 = _rmsnorm_bwd(x2, row(ln2), dh2, "norm_bwd_l2", dres=dx)

    dgf1 = _matmul(dxb, f_out1, "nt", "dgate_in_l1")
    dw_out1 = _matmul(g1, dxb, "tn", "dw_out_l1")
    do1, dgate1 = _gate_bwd(dgf1, o1, proj1, 0, "gate_bwd_l1")
    dqc, dkc, dv1 = _sm_bwd(qcat, kcat, kv1, o1, do1, lse1, nh, MLA_QK, *mla_blk, "chunk", "mla_bwd_l1")
    dqpre, dkv1, dkr = _mla_disassemble(dqc, dkc, dv1, tabs, nh, "mla_disassemble_l1")
    dqn = _matmul(dqpre, wqbp, "nt", "dqn_l1")
    dw_qbp = _matmul(qn, dqpre, "tn", "dw_qb_l1")
    dw_qb1 = dw_qbp.reshape(MLA_Q_RANK, nh, MLA_QK)[:, :, :qk_w].reshape(MLA_Q_RANK, nh * qk_w)
    dkvn = _matmul(dkv1, f_kvb1, "nt", "dkvn_l1")
    dw_kvb1 = _matmul(kvn, dkv1, "tn", "dw_kvb_l1")
    _, dqlat_b, d_qnorm = _rmsnorm_bwd(proj1, row(q_norm1), dqn, "qnorm_bwd_l1", col_block=qlat_blk)
    _, dkvlat_b, d_kvnorm = _rmsnorm_bwd(proj1, row(kv_norm1), dkvn, "kvnorm_bwd_l1", col_block=kvlat_blk)
    dproj1 = jnp.concatenate([dgate1, dqlat_b, dkvlat_b, dkr.astype(BF16)], axis=1)
    dh1 = _matmul(dproj1, w1p, "nt", "dh_l1")
    dw1p = _matmul(h1, dproj1, "tn", "dw_in_l1")
    dw_in1 = jnp.concatenate([dw1p[:, di : di + i_kr], dw1p[:, :di]], axis=1)
    dx, dxb, d_ln1 = _rmsnorm_bwd(x1, row(ln1), dh1, "norm_bwd_l1", dres=dx)

    dx, dxb, d_ln0, dw_in0, dw_out0 = sb_layer_bwd(dx, dxb, sv0, ln0, f_in0, f_out0, "l0")
    grad_x = dx.reshape(x.shape)

    dws = [dw_in0, dw_out0, dw_in1, dw_qb1, dw_kvb1, dw_out1, dw_in2, dw_out2, dw_in3, dw_out3]
    parts = []
    for g, (r, c), n, cs, is4 in zip(dws, shard_shapes, rows, col_sharded, as_gathered):
        g4 = g if is4 else (g.reshape(r, N_CHIPS, c).transpose(1, 0, 2) if cs else g.reshape(N_CHIPS, r, c))
        parts.append(g4.reshape(N_CHIPS, n, PACK_W))
    gp = _pack_rows(parts, offs, rows, rp, (N_CHIPS,))
    sib = _pair_exchange(gp)
    pair = _pair_add(gp, sib, idx)
    slots = _chip_exchange(pair)
    gred = _pair_gather(_sum_slots(pair, slots, idx))
    big_grads = [gred[o : o + n, :].reshape(r, c) for (r, c), o, n in zip(shard_shapes, offs, rows)]

    small = [ln0, ln1, q_norm1, kv_norm1, ln2, b_f2, ln3, final_norm]
    small_g = [d_ln0[0], d_ln1[0], d_qnorm[0], d_kvnorm[0], d_ln2[0], d_bf[0, :nh], d_ln3[0], d_final[0]]
    n_small = SMALL_SHAPE[0] * SMALL_SHAPE[1]
    used = sum(v.shape[0] for v in small) + 1
    assert used <= n_small

    def pack_small(vs, last):
        return jnp.concatenate(list(vs) + [last, jnp.zeros((n_small - used,), F32)]).reshape(SMALL_SHAPE)

    sm_sum = _allreduce_small(pack_small(small_g, loss_part[0, :1]))
    flat = sm_sum.reshape(-1)
    loss = flat[used - 1]

    big_m = [m_w_in0, m_w_out0, m_w_in1, m_w_qb1, m_w_kvb1, m_w_out1, m_w_in2, m_w_out2, m_w_in3, m_w_out3]
    big_v = [v_w_in0, v_w_out0, v_w_in1, v_w_qb1, v_w_kvb1, v_w_out1, v_w_in2, v_w_out2, v_w_in3, v_w_out3]
    big_names = ["w_in0", "w_out0", "w_in1", "w_qb1", "w_kvb1", "w_out1", "w_in2", "w_out2", "w_in3", "w_out3"]
    big_upd = [_adamw(w, g, m, v, f"adamw_{nm}") for w, g, m, v, nm in zip(big, big_grads, big_m, big_v, big_names)]

    small_m = [m_ln0, m_ln1, m_q_norm1, m_kv_norm1, m_ln2, m_b_f2, m_ln3, m_final_norm]
    small_v = [v_ln0, v_ln1, v_q_norm1, v_kv_norm1, v_ln2, v_b_f2, v_ln3, v_final_norm]
    one = jnp.ones((1,), F32)
    sd, smn, svn = _adamw(pack_small(small, one), sm_sum, pack_small(small_m, one), pack_small(small_v, one), "adamw_small")

    def unpack_small(p):
        out, at = [], 0
        fl = p.reshape(-1)
        for v in small:
            out.append(fl[at : at + v.shape[0]])
            at += v.shape[0]
        return out

    sg_l, sd_l, sm_l, sv_l = unpack_small(sm_sum), unpack_small(sd), unpack_small(smn), unpack_small(svn)

    order = ["ln0", "w_in0", "w_out0", "ln1", "w_in1", "q_norm1", "w_qb1", "kv_norm1", "w_kvb1", "w_out1", "ln2", "w_in2", "b_f2", "w_out2", "ln3", "w_in3", "w_out3", "final_norm"]
    small_names = ["ln0", "ln1", "q_norm1", "kv_norm1", "ln2", "b_f2", "ln3", "final_norm"]
    grads, deltas, new_m, new_v = {}, {}, {}, {}
    for nm, g, (dl, mn, vn) in zip(big_names, big_grads, big_upd):
        grads[nm], deltas[nm], new_m[nm], new_v[nm] = g, dl, mn, vn
    for nm, g, dl, mn, vn in zip(small_names, sg_l, sd_l, sm_l, sv_l):
        grads[nm], deltas[nm], new_m[nm], new_v[nm] = g, dl, mn, vn
    return (loss, grad_x, *[grads[n] for n in order], *[deltas[n] for n in order], *[new_m[n] for n in order], *[new_v[n] for n in order])
```

```python
import functools

import jax
import jax.numpy as jnp
from jax import lax
from jax.experimental import pallas as pl
from jax.experimental.pallas import tpu as pltpu

F32 = jnp.float32
BF16 = jnp.bfloat16
EPS = 1e-6
NEG = -1e30
HEAD_DIM = 128
CHUNK_SHIFT = 6
MLA_Q_RANK = 256
MLA_KV_RANK = 128
MLA_ROPE = 64
MLA_QK = 256
ROPE_BASE = 10000.0
ADAM_LR = 0.001
ADAM_B1 = 0.9
ADAM_B2 = 0.999
ADAM_EPS = 1e-08
ADAM_WD = 0.01
ADAM_STEP = 10
VMEM_LIMIT_BYTES = 56 * 2**20
LANES = 128
PACK_W = 1024
PACK_TR = 128
SMALL_SHAPE = (8, 768)
MESH = pl.DeviceIdType.MESH
N_CHIPS = 4


def _pick(n, cands):
    for c in cands:
        if n % c == 0:
            return c
    return n


def _cparams(sem):
    return pltpu.CompilerParams(dimension_semantics=sem, vmem_limit_bytes=VMEM_LIMIT_BYTES)


def _dot(a, b, dims):
    dn = {"nn": (((1,), (0,)), ((), ())), "nt": (((1,), (1,)), ((), ())), "tn": (((0,), (0,)), ((), ()))}[dims]
    return lax.dot_general(a, b, dn, preferred_element_type=F32)


def _matmul(a, b, dims, name, res=None):
    if dims == "nn":
        (m, k), (k2, n) = a.shape, b.shape
    elif dims == "nt":
        (m, k), (n, k2) = a.shape, b.shape
    else:
        (k, m), (k2, n) = a.shape, b.shape
    assert k == k2, (a.shape, b.shape, dims)
    tm = _pick(m, (1024, 512, 256, 128))
    tn = _pick(n, (1024, 640, 512, 384, 256, 128))
    tk = _pick(k, (1024, 640, 512, 256, 128))
    nk = k // tk

    def body(*refs):
        if res is None:
            a_ref, b_ref, o_ref = refs
            r_ref = None
        else:
            a_ref, b_ref, r_ref, o_ref = refs
        kk = pl.program_id(2)
        p = _dot(a_ref[...].astype(BF16), b_ref[...].astype(BF16), dims)

        @pl.when(kk == 0)
        def _():
            o_ref[...] = p if r_ref is None else p + r_ref[...]

        @pl.when(kk > 0)
        def _():
            o_ref[...] += p

    a_spec = pl.BlockSpec((tk, tm), lambda i, j, kk: (kk, i)) if dims == "tn" else pl.BlockSpec((tm, tk), lambda i, j, kk: (i, kk))
    b_spec = pl.BlockSpec((tn, tk), lambda i, j, kk: (j, kk)) if dims == "nt" else pl.BlockSpec((tk, tn), lambda i, j, kk: (kk, j))
    o_spec = pl.BlockSpec((tm, tn), lambda i, j, kk: (i, j))
    in_specs = [a_spec, b_spec] + ([] if res is None else [o_spec])
    args = (a, b) + (() if res is None else (res,))
    return pl.pallas_call(
        body,
        grid=(m // tm, n // tn, nk),
        in_specs=in_specs,
        out_specs=o_spec,
        out_shape=jax.ShapeDtypeStruct((m, n), F32),
        compiler_params=_cparams(("parallel", "parallel", "arbitrary")),
        name=name,
    )(*args)


def _proj_w4(h, w4, name):
    (s, d), (ns, d2, c) = h.shape, w4.shape
    assert d == d2
    tm = _pick(s, (1024, 512, 256, 128))
    tn = _pick(c, (1024, 512, 256, 128))
    nbs = c // tn

    def body(a_ref, b_ref, o_ref):
        o_ref[...] = _dot(a_ref[...].astype(BF16), b_ref[...].astype(BF16), "nn")

    return pl.pallas_call(
        body,
        grid=(s // tm, ns * nbs),
        in_specs=[pl.BlockSpec((tm, d), lambda i, j: (i, 0)), pl.BlockSpec((None, d, tn), lambda i, j: (j // nbs, 0, j % nbs))],
        out_specs=pl.BlockSpec((tm, tn), lambda i, j: (i, j)),
        out_shape=jax.ShapeDtypeStruct((s, ns * c), F32),
        compiler_params=_cparams(("parallel", "parallel")),
        name=name,
    )(h, w4)


def _dh_w4(parts, w4, name):
    ns, d, c = w4.shape
    s = parts[0].shape[0]
    assert len(parts) == ns and all(p.shape == (s, c) for p in parts)
    tm = _pick(s, (1024, 512, 256, 128))
    tk = _pick(c, (1024, 512, 256, 128))
    nkp = c // tk

    def body(*refs):
        a_refs, b_ref, o_ref = refs[:ns], refs[ns], refs[ns + 1]
        kk = pl.program_id(1)
        for p in range(ns):

            @pl.when(kk // nkp == p)
            def _(p=p):
                pv = _dot(a_refs[p][...].astype(BF16), b_ref[...].astype(BF16), "nt")

                @pl.when(kk == 0)
                def _():
                    o_ref[...] = pv

                @pl.when(kk > 0)
                def _():
                    o_ref[...] += pv

    def a_spec(p):
        return pl.BlockSpec((tm, tk), lambda i, kk: (i, jnp.clip(kk - p * nkp, 0, nkp - 1)))

    return pl.pallas_call(
        body,
        grid=(s // tm, ns * nkp),
        in_specs=[a_spec(p) for p in range(ns)] + [pl.BlockSpec((None, d, tk), lambda i, kk: (kk // nkp, 0, kk % nkp))],
        out_specs=pl.BlockSpec((tm, d), lambda i, kk: (i, 0)),
        out_shape=jax.ShapeDtypeStruct((s, d), F32),
        compiler_params=_cparams(("parallel", "arbitrary")),
        name=name,
    )(*parts, w4)


def _dw_w4(h, parts, name):
    s, d = h.shape
    ns = len(parts)
    c = parts[0].shape[1]
    tn = _pick(c, (1024, 512, 256, 128))
    tk = _pick(s, (1024, 512, 256, 128))
    nbp = c // tn

    def body(*refs):
        a_ref, b_refs, o_ref = refs[0], refs[1 : 1 + ns], refs[1 + ns]
        j, kk = pl.program_id(0), pl.program_id(1)
        for p in range(ns):

            @pl.when(j // nbp == p)
            def _(p=p):
                pv = _dot(a_ref[...].astype(BF16), b_refs[p][...].astype(BF16), "tn")

                @pl.when(kk == 0)
                def _():
                    o_ref[...] = pv

                @pl.when(kk > 0)
                def _():
                    o_ref[...] += pv

    def b_spec(p):
        return pl.BlockSpec((tk, tn), lambda j, kk: (kk, jnp.clip(j - p * nbp, 0, nbp - 1)))

    return pl.pallas_call(
        body,
        grid=(ns * nbp, s // tk),
        in_specs=[pl.BlockSpec((tk, d), lambda j, kk: (kk, 0))] + [b_spec(p) for p in range(ns)],
        out_specs=pl.BlockSpec((None, d, tn), lambda j, kk: (j // nbp, 0, j % nbp)),
        out_shape=jax.ShapeDtypeStruct((ns, d, c), F32),
        compiler_params=_cparams(("parallel", "arbitrary")),
        name=name,
    )(h, *parts)


def _rmsnorm_fwd(x, g, name, col_block=0):
    s = x.shape[0]
    w = g.shape[1]
    tr = _pick(s, (512, 256, 128))

    def body(x_ref, g_ref, h_ref):
        xv = x_ref[...]
        r = lax.rsqrt(jnp.mean(xv * xv, axis=-1, keepdims=True) + EPS)
        h_ref[...] = ((xv * r) * g_ref[...]).astype(BF16)

    return pl.pallas_call(
        body,
        grid=(s // tr,),
        in_specs=[pl.BlockSpec((tr, w), lambda i: (i, col_block)), pl.BlockSpec((1, w), lambda i: (0, 0))],
        out_specs=pl.BlockSpec((tr, w), lambda i: (i, 0)),
        out_shape=jax.ShapeDtypeStruct((s, w), BF16),
        compiler_params=_cparams(("parallel",)),
        name=name,
    )(x, g)


def _rmsnorm_bwd(x, g, dh, name, col_block=0, dres=None):
    s = x.shape[0]
    w = g.shape[1]
    tr = _pick(s, (512, 256, 128))

    def body(*refs):
        if dres is None:
            x_ref, g_ref, dh_ref, dx_ref, dxb_ref, dg_ref = refs
        else:
            x_ref, g_ref, dh_ref, dr_ref, dx_ref, dxb_ref, dg_ref = refs
        i = pl.program_id(0)
        xv = x_ref[...]
        r = lax.rsqrt(jnp.mean(xv * xv, axis=-1, keepdims=True) + EPS)
        xh = xv * r
        dhv = dh_ref[...]
        dyg = dhv * g_ref[...]
        dx = r * (dyg - xh * jnp.mean(dyg * xh, axis=-1, keepdims=True))
        if dres is not None:
            dx = dx + dr_ref[...]
        dx_ref[...] = dx
        dxb_ref[...] = dx.astype(BF16)
        part = jnp.sum(dhv * xh, axis=0, keepdims=True)

        @pl.when(i == 0)
        def _():
            dg_ref[...] = part

        @pl.when(i > 0)
        def _():
            dg_ref[...] += part

    row = pl.BlockSpec((tr, w), lambda i: (i, 0))
    in_specs = [pl.BlockSpec((tr, w), lambda i: (i, col_block)), pl.BlockSpec((1, w), lambda i: (0, 0)), row]
    args = [x, g, dh]
    if dres is not None:
        in_specs.append(row)
        args.append(dres)
    return pl.pallas_call(
        body,
        grid=(s // tr,),
        in_specs=in_specs,
        out_specs=[row, row, pl.BlockSpec((1, w), lambda i: (0, 0))],
        out_shape=[jax.ShapeDtypeStruct((s, w), F32), jax.ShapeDtypeStruct((s, w), BF16), jax.ShapeDtypeStruct((1, w), F32)],
        compiler_params=_cparams(("arbitrary",)),
        name=name,
    )(*args)


def _loss_head(x, g, target, name):
    s, d = x.shape
    tr = _pick(s, (512, 256, 128))

    def body(x_ref, g_ref, t_ref, dx_ref, dxb_ref, dg_ref, loss_ref):
        i = pl.program_id(0)
        xv = x_ref[...]
        gv = g_ref[...]
        r = lax.rsqrt(jnp.mean(xv * xv, axis=-1, keepdims=True) + EPS)
        xh = xv * r
        err = xh * gv - t_ref[...]
        lpart = 0.5 * jnp.sum(jnp.mean(err * err, axis=-1, keepdims=True), axis=0, keepdims=True)
        dy = err / d
        dyg = dy * gv
        dx = r * (dyg - xh * jnp.mean(dyg * xh, axis=-1, keepdims=True))
        dx_ref[...] = dx
        dxb_ref[...] = dx.astype(BF16)
        part = jnp.sum(dy * xh, axis=0, keepdims=True)
        lrow = jnp.broadcast_to(lpart, (1, LANES))

        @pl.when(i == 0)
        def _():
            dg_ref[...] = part
            loss_ref[...] = lrow

        @pl.when(i > 0)
        def _():
            dg_ref[...] += part
            loss_ref[...] += lrow

    row = pl.BlockSpec((tr, d), lambda i: (i, 0))
    vec = pl.BlockSpec((1, d), lambda i: (0, 0))
    return pl.pallas_call(
        body,
        grid=(s // tr,),
        in_specs=[row, vec, row],
        out_specs=[row, row, vec, pl.BlockSpec((1, LANES), lambda i: (0, 0))],
        out_shape=[
            jax.ShapeDtypeStruct((s, d), F32),
            jax.ShapeDtypeStruct((s, d), BF16),
            jax.ShapeDtypeStruct((1, d), F32),
            jax.ShapeDtypeStruct((1, LANES), F32),
        ],
        compiler_params=_cparams(("arbitrary",)),
        name=name,
    )(x, g, target)


def _sigmoid(x):
    return 1.0 / (1.0 + jnp.exp(-x))


def _gate_fwd(o, proj, gate_blk, name):
    s, di = o.shape
    tr = _pick(s, (256, 128))

    def body(o_ref, gate_ref, g_ref):
        gt = gate_ref[...]
        g_ref[...] = (o_ref[...] * (gt * _sigmoid(gt))).astype(BF16)

    row = pl.BlockSpec((tr, di), lambda i: (i, 0))
    return pl.pallas_call(
        body,
        grid=(s // tr,),
        in_specs=[row, pl.BlockSpec((tr, di), lambda i: (i, gate_blk))],
        out_specs=row,
        out_shape=jax.ShapeDtypeStruct((s, di), BF16),
        compiler_params=_cparams(("parallel",)),
        name=name,
    )(o, proj)


def _gate_bwd(dg, o, proj, gate_blk, name):
    s, di = o.shape
    tr = _pick(s, (256, 128))

    def body(dg_ref, o_ref, gate_ref, do_ref, dgate_ref):
        gt = gate_ref[...]
        sg = _sigmoid(gt)
        dgv = dg_ref[...]
        do_ref[...] = dgv * (gt * sg)
        dgate_ref[...] = (dgv * o_ref[...] * (sg * (1.0 + gt * (1.0 - sg)))).astype(BF16)

    row = pl.BlockSpec((tr, di), lambda i: (i, 0))
    return pl.pallas_call(
        body,
        grid=(s // tr,),
        in_specs=[row, row, pl.BlockSpec((tr, di), lambda i: (i, gate_blk))],
        out_specs=[row, row],
        out_shape=[jax.ShapeDtypeStruct((s, di), F32), jax.ShapeDtypeStruct((s, di), BF16)],
        compiler_params=_cparams(("parallel",)),
        name=name,
    )(dg, o, proj)


def _iotas(tq, tk):
    return lax.broadcasted_iota(jnp.int32, (tq, tk), 0), lax.broadcasted_iota(jnp.int32, (tq, tk), 1)


def _softplus(s):
    return jnp.maximum(s, 0.0) + jnp.log(1.0 + jnp.exp(-jnp.abs(s)))


def _split2(v):
    hi = v.astype(BF16)
    lo = (v - hi.astype(F32)).astype(BF16)
    return hi, lo


def _cat2(v):
    return jnp.concatenate(_split2(v), axis=1)


def _tri2(keep):
    m = keep.astype(BF16)
    return jnp.concatenate([m, m], axis=0)


def _split3(v):
    a = v.astype(BF16)
    r1 = v - a.astype(F32)
    b = r1.astype(BF16)
    c = (r1 - b.astype(F32)).astype(BF16)
    return a, b, c


SB_TQ = 512
SB_TK = 128
SB_UNROLL = 4


def _sb_fwd(proj, n_heads, name, gather=None):
    s = proj.shape[0]
    d = HEAD_DIM
    t = min(SB_TQ, s)
    tk = min(SB_TK, s)
    r = t // tk
    un = SB_UNROLL if r % SB_UNROLL == 0 else 1
    nq = s // t
    scale = d**-0.5

    def body(*refs):
        if gather is None:
            q_ref, k_ref, v_ref, o_ref, lt_ref, kb_ref, vb_ref = refs
        else:
            q_ref, k_ref, v_ref, w_ref, o_ref, lt_ref, wall_ref, kb_ref, vb_ref, send_sems, recv_sems = refs
        i = pl.program_id(1)

        if gather is not None:
            hh = pl.program_id(0)
            send, forward, finish = _gather_steps(w_ref, wall_ref, send_sems, recv_sems)
            pl.when((hh == 0) & (i == 0))(send)
            pl.when((hh == n_heads // 2) & (i == 0))(forward)

        @pl.when(i == 0)
        def _():
            kb_ref[...] = k_ref[...].astype(BF16)
            vb_ref[...] = v_ref[...].astype(BF16)

        q = (q_ref[...] * scale).astype(BF16)
        rows, cols = _iotas(t, tk)
        trows, tcols = _iotas(tk, tk)
        tri = (trows > tcols).astype(BF16)

        def block(kb, cl, acc, diag):
            k0 = pl.multiple_of(kb * tk, tk)
            sc = _dot(q, kb_ref[pl.ds(k0, tk), :], "nt")
            sp = _softplus(sc)
            ls = -sp
            if diag is not None:
                strict = cols + diag * tk < rows
                ls = jnp.where(strict, ls, 0.0)
            hi, lo = _split2(ls)
            later = _dot(hi, tri, "nn") + _dot(lo, tri, "nn")
            w = jnp.exp((sc - sp) + later + cl)
            if diag is not None:
                w = jnp.where(strict, w, 0.0)
            acc = acc + _dot(w.astype(BF16), vb_ref[pl.ds(k0, tk), :], "nn")
            return cl + jnp.sum(ls, axis=1, keepdims=True), acc

        cl, acc = jnp.zeros((t, 1), F32), jnp.zeros((t, d), F32)
        for dd in reversed(range(r)):
            cl, acc = block(i * r + dd, cl, acc, dd)

        def loop(j, carry):
            for u in range(un):
                carry = block(i * r - 1 - (un * j + u), carry[0], carry[1], None)
            return carry

        cl, acc = lax.fori_loop(0, (i * r) // un, loop, (cl, acc))
        o_ref[...] = acc
        lt_ref[...] = cl

        if gather is not None:
            pl.when((hh == n_heads - 1) & (i == nq - 1))(finish)

    h = n_heads
    in_specs = [
        pl.BlockSpec((t, d), lambda hh, i: (i, hh)),
        pl.BlockSpec((s, d), lambda hh, i: (0, h + hh)),
        pl.BlockSpec((s, d), lambda hh, i: (0, 2 * h + hh)),
    ]
    out_specs = [pl.BlockSpec((t, d), lambda hh, i: (i, hh)), pl.BlockSpec((None, t, 1), lambda hh, i: (hh, i, 0))]
    out_shape = [jax.ShapeDtypeStruct((s, h * d), F32), jax.ShapeDtypeStruct((h, s, 1), F32)]
    scratch = [pltpu.VMEM((s, d), BF16), pltpu.VMEM((s, d), BF16)]
    args = [proj, proj, proj]
    if gather is not None:
        in_specs.append(ANY)
        out_specs.append(ANY)
        out_shape.append(jax.ShapeDtypeStruct((N_CHIPS,) + gather.shape, gather.dtype))
        scratch += GATHER_SEMS
        args.append(gather)
    return pl.pallas_call(
        body,
        grid=(h, nq),
        in_specs=in_specs,
        out_specs=out_specs,
        out_shape=out_shape,
        scratch_shapes=scratch,
        compiler_params=_cparams(("arbitrary", "arbitrary")),
        name=name,
    )(*args)


def _sb_bwd(proj, ltot, do, n_heads, name):
    s = proj.shape[0]
    d = HEAD_DIM
    t = min(SB_TQ, s)
    tk = min(SB_TK, s)
    r = t // tk
    un = SB_UNROLL if r % SB_UNROLL == 0 else 1
    nq = s // t
    scale = d**-0.5

    def body(q_ref, k_ref, v_ref, lt_ref, do_ref, dq_ref, dko_ref, dvo_ref, kb_ref, vb_ref, dk_ref, dv_ref):
        i = pl.program_id(1)

        @pl.when(i == 0)
        def _():
            kb_ref[...] = k_ref[...].astype(BF16)
            vb_ref[...] = v_ref[...].astype(BF16)
            dk_ref[...] = jnp.zeros_like(dk_ref)
            dv_ref[...] = jnp.zeros_like(dv_ref)

        q = (q_ref[...] * scale).astype(BF16)
        dob = do_ref[...].astype(BF16)
        ltv = lt_ref[...]
        rows, cols = _iotas(t, tk)
        trows, tcols = _iotas(tk, tk)
        upto = _tri2(trows <= tcols)
        before = _tri2(trows < tcols)

        def block(kb, cp, cc, dq, diag):
            k0 = pl.multiple_of(kb * tk, tk)
            kk = kb_ref[pl.ds(k0, tk), :]
            sc = _dot(q, kk, "nt")
            sp = _softplus(sc)
            ls = -sp
            if diag is not None:
                strict = cols + diag * tk < rows
                ls = jnp.where(strict, ls, 0.0)
            prefix = _dot(_cat2(ls), upto, "nn") + cp
            lsig = sc - sp
            w = jnp.exp(lsig + (ltv - prefix))
            if diag is not None:
                w = jnp.where(strict, w, 0.0)
            da = _dot(dob, vb_ref[pl.ds(k0, tk), :], "nt") * w
            csum = _dot(_cat2(da), before, "nn") + cc
            beta = jnp.exp(lsig)
            dz = da * (1.0 - beta) - beta * csum
            if diag is not None:
                dz = jnp.where(strict, dz, 0.0)
            dzb = dz.astype(BF16)
            dq = dq + _dot(dzb, kk, "nn")
            dk_ref[pl.ds(k0, tk), :] += _dot(dzb, q, "tn")
            dv_ref[pl.ds(k0, tk), :] += _dot(w.astype(BF16), dob, "tn")
            return cp + jnp.sum(ls, axis=1, keepdims=True), cc + jnp.sum(da, axis=1, keepdims=True), dq

        def loop(j, carry):
            for u in range(un):
                carry = block(un * j + u, carry[0], carry[1], carry[2], None)
            return carry

        z1 = jnp.zeros((t, 1), F32)
        cp, cc, dq = lax.fori_loop(0, (i * r) // un, loop, (z1, z1, jnp.zeros((t, d), F32)))
        for dd in range(r):
            cp, cc, dq = block(i * r + dd, cp, cc, dq, dd)
        dq_ref[...] = (dq * scale).astype(BF16)

        @pl.when(i == nq - 1)
        def _():
            dko_ref[...] = dk_ref[...].astype(BF16)
            dvo_ref[...] = dv_ref[...].astype(BF16)

    h = n_heads
    qblk = pl.BlockSpec((t, d), lambda hh, i: (i, hh))
    full = pl.BlockSpec((s, d), lambda hh, i: (0, hh))
    shp = jax.ShapeDtypeStruct((s, h * d), BF16)
    return pl.pallas_call(
        body,
        grid=(h, nq),
        in_specs=[
            qblk,
            pl.BlockSpec((s, d), lambda hh, i: (0, h + hh)),
            pl.BlockSpec((s, d), lambda hh, i: (0, 2 * h + hh)),
            pl.BlockSpec((None, t, 1), lambda hh, i: (hh, i, 0)),
            qblk,
        ],
        out_specs=[qblk, full, full],
        out_shape=[shp, shp, shp],
        scratch_shapes=[pltpu.VMEM((s, d), BF16), pltpu.VMEM((s, d), BF16), pltpu.VMEM((s, d), F32), pltpu.VMEM((s, d), F32)],
        compiler_params=_cparams(("arbitrary", "arbitrary")),
        name=name,
    )(proj, proj, proj, ltot, do)


SM_TQ = 512
SM_TK = 256
SM_UNROLL = 2


def _allowed(mode, rows, cols, q0, k0):
    r = rows + q0
    c = cols + k0
    if mode == "causal":
        return c <= r
    return (c >> CHUNK_SHIFT) <= (r >> CHUNK_SHIFT)


def _sm_fwd(q_arr, k_arr, v_arr, n_heads, dqk, q_blk, k_blk, v_blk, mode, name, crow=None, ccol=None):
    s = q_arr.shape[0]
    dv = HEAD_DIM
    t = min(SM_TQ, s)
    tk = min(SM_TK, s)
    r = t // tk
    un = SM_UNROLL if r % SM_UNROLL == 0 else 1
    bias = crow is not None
    scale = (HEAD_DIM if mode == "causal" else HEAD_DIM + MLA_ROPE) ** -0.5

    def body(*refs):
        if bias:
            q_ref, k_ref, v_ref, cr_ref, cc_ref, o_ref, lse_ref, kb_ref, vb_ref = refs
        else:
            q_ref, k_ref, v_ref, o_ref, lse_ref, kb_ref, vb_ref = refs
        i = pl.program_id(1)

        @pl.when(i == 0)
        def _():
            kb_ref[...] = k_ref[...].astype(BF16)
            vb_ref[...] = v_ref[...].astype(BF16)

        q = (q_ref[...] * scale).astype(BF16)
        q0 = i * t
        rows, cols = _iotas(t, tk)
        crv = cr_ref[...] if bias else None

        def block(kb, m, l, acc, masked):
            k0 = pl.multiple_of(kb * tk, tk)
            sc = _dot(q, kb_ref[pl.ds(k0, tk), :], "nt")
            if bias:
                sc = sc + crv - cc_ref[pl.ds(kb, 1), :]
            if masked:
                sc = jnp.where(_allowed(mode, rows, cols, q0, k0), sc, NEG)
            m_new = jnp.maximum(m, jnp.max(sc, axis=1, keepdims=True))
            alpha = jnp.exp(m - m_new)
            p = jnp.exp(sc - m_new)
            l = alpha * l + jnp.sum(p, axis=1, keepdims=True)
            acc = alpha * acc + _dot(p.astype(BF16), vb_ref[pl.ds(k0, tk), :], "nn")
            return m_new, l, acc

        def loop(j, carry):
            for u in range(un):
                carry = block(un * j + u, carry[0], carry[1], carry[2], False)
            return carry

        init = (jnp.full((t, 1), NEG, F32), jnp.zeros((t, 1), F32), jnp.zeros((t, dv), F32))
        m, l, acc = lax.fori_loop(0, (i * r) // un, loop, init)
        for dd in range(r):
            m, l, acc = block(i * r + dd, m, l, acc, True)
        o_ref[...] = acc / l
        lse_ref[...] = m + jnp.log(l)

    h = n_heads
    in_specs = [
        pl.BlockSpec((t, dqk), lambda hh, i: (i, q_blk(hh))),
        pl.BlockSpec((s, dqk), lambda hh, i: (0, k_blk(hh))),
        pl.BlockSpec((s, dv), lambda hh, i: (0, v_blk(hh))),
    ]
    args = [q_arr, k_arr, v_arr]
    if bias:
        in_specs += [pl.BlockSpec((None, t, 1), lambda hh, i: (hh, i, 0)), pl.BlockSpec((None, s // tk, tk), lambda hh, i: (hh, 0, 0))]
        args += [crow, ccol]
    return pl.pallas_call(
        body,
        grid=(h, s // t),
        in_specs=in_specs,
        out_specs=[pl.BlockSpec((t, dv), lambda hh, i: (i, hh)), pl.BlockSpec((None, t, 1), lambda hh, i: (hh, i, 0))],
        out_shape=[jax.ShapeDtypeStruct((s, h * dv), F32), jax.ShapeDtypeStruct((h, s, 1), F32)],
        scratch_shapes=[pltpu.VMEM((s, dqk), BF16), pltpu.VMEM((s, dv), BF16)],
        compiler_params=_cparams(("arbitrary", "arbitrary")),
        name=name,
    )(*args)


def _sm_bwd(q_arr, k_arr, v_arr, o, do, lse, n_heads, dqk, q_blk, k_blk, v_blk, mode, name, crow=None, ccol=None, exchange=None):
    s = q_arr.shape[0]
    dv = HEAD_DIM
    t = min(SM_TQ, s)
    tk = min(SM_TK, s)
    r = t // tk
    un = SM_UNROLL if r % SM_UNROLL == 0 else 1
    nq = s // t
    bias = crow is not None
    assert not (bias and exchange is not None)
    scale = (HEAD_DIM if mode == "causal" else HEAD_DIM + MLA_ROPE) ** -0.5

    def body(*refs):
        if bias:
            q_ref, k_ref, v_ref, o_ref, do_ref, lse_ref, cr_ref, cc_ref, dq_ref, dk_ref, dv_ref, dcc_ref, dcr_ref, kb_ref, vb_ref = refs
        elif exchange is not None:
            q_ref, k_ref, v_ref, o_ref, do_ref, lse_ref, sp_ref, dq_ref, dk_ref, dv_ref, slots_ref, kb_ref, vb_ref, send_sems, recv_sems = refs
        else:
            q_ref, k_ref, v_ref, o_ref, do_ref, lse_ref, dq_ref, dk_ref, dv_ref, kb_ref, vb_ref = refs
        i = pl.program_id(1)

        if exchange is not None:
            hh = pl.program_id(0)
            send, finish = _exchange_steps(sp_ref, slots_ref, send_sems, recv_sems)
            pl.when((hh == 0) & (i == 0))(send)

        @pl.when(i == 0)
        def _():
            kb_ref[...] = k_ref[...].astype(BF16)
            vb_ref[...] = v_ref[...].astype(BF16)
            dk_ref[...] = jnp.zeros_like(dk_ref)
            dv_ref[...] = jnp.zeros_like(dv_ref)
            if bias:
                dcc_ref[...] = jnp.zeros_like(dcc_ref)

        q = (q_ref[...] * scale).astype(BF16)
        dov = do_ref[...]
        dob = dov.astype(BF16)
        dsum = jnp.sum(dov * o_ref[...], axis=1, keepdims=True)
        lsev = lse_ref[...]
        q0 = i * t
        rows, cols = _iotas(t, tk)
        crv = cr_ref[...] if bias else None

        def block(kb, dq, dr, masked):
            k0 = pl.multiple_of(kb * tk, tk)
            kk = kb_ref[pl.ds(k0, tk), :]
            sc = _dot(q, kk, "nt")
            if bias:
                sc = sc + crv - cc_ref[pl.ds(kb, 1), :]
            if masked:
                sc = jnp.where(_allowed(mode, rows, cols, q0, k0), sc, NEG)
            p = jnp.exp(sc - lsev)
            dz = p * (_dot(dob, vb_ref[pl.ds(k0, tk), :], "nt") - dsum)
            dzb = dz.astype(BF16)
            dk_ref[pl.ds(k0, tk), :] += _dot(dzb, q, "tn")
            dv_ref[pl.ds(k0, tk), :] += _dot(p.astype(BF16), dob, "tn")
            if bias:
                dcc_ref[pl.ds(kb, 1), :] -= jnp.sum(dz, axis=0, keepdims=True)
                dr = dr + jnp.sum(dz, axis=1, keepdims=True)
            return dq + _dot(dzb, kk, "nn"), dr

        def loop(j, carry):
            for u in range(un):
                carry = block(un * j + u, carry[0], carry[1], False)
            return carry

        dq, dr = lax.fori_loop(0, (i * r) // un, loop, (jnp.zeros((t, dqk), F32), jnp.zeros((t, 1), F32)))
        for dd in range(r):
            dq, dr = block(i * r + dd, dq, dr, True)
        dq_ref[...] = dq * scale
        if bias:
            dcr_ref[...] = dr
        if exchange is not None:
            pl.when((hh == n_heads - 1) & (i == nq - 1))(finish)

    h = n_heads
    qblk = pl.BlockSpec((t, dqk), lambda hh, i: (i, q_blk(hh)))
    oblk = pl.BlockSpec((t, dv), lambda hh, i: (i, hh))
    vec = pl.BlockSpec((None, t, 1), lambda hh, i: (hh, i, 0))
    in_specs = [
        qblk,
        pl.BlockSpec((s, dqk), lambda hh, i: (0, k_blk(hh))),
        pl.BlockSpec((s, dv), lambda hh, i: (0, v_blk(hh))),
        oblk,
        oblk,
        vec,
    ]
    args = [q_arr, k_arr, v_arr, o, do, lse]
    out_specs = [
        pl.BlockSpec((t, dqk), lambda hh, i: (i, hh)),
        pl.BlockSpec((s, dqk), lambda hh, i: (0, hh)),
        pl.BlockSpec((s, dv), lambda hh, i: (0, hh)),
    ]
    out_shape = [
        jax.ShapeDtypeStruct((s, h * dqk), F32),
        jax.ShapeDtypeStruct((s, h * dqk), F32),
        jax.ShapeDtypeStruct((s, h * dv), F32),
    ]
    if bias:
        ccs = pl.BlockSpec((None, s // tk, tk), lambda hh, i: (hh, 0, 0))
        in_specs += [vec, ccs]
        args += [crow, ccol]
        out_specs += [ccs, vec]
        out_shape += [jax.ShapeDtypeStruct((h, s // tk, tk), F32), jax.ShapeDtypeStruct((h, s, 1), F32)]
    scratch = [pltpu.VMEM((s, dqk), BF16), pltpu.VMEM((s, dv), BF16)]
    if exchange is not None:
        in_specs.append(ANY)
        args.append(exchange)
        out_specs.append(ANY)
        out_shape.append(jax.ShapeDtypeStruct(exchange.shape, exchange.dtype))
        scratch += EXCHANGE_SEMS
    return pl.pallas_call(
        body,
        grid=(h, nq),
        in_specs=in_specs,
        out_specs=out_specs,
        out_shape=out_shape,
        scratch_shapes=scratch,
        compiler_params=_cparams(("arbitrary", "arbitrary")),
        name=name,
    )(*args)


def _rope_tables(pos):
    half = MLA_ROPE // 2
    inv_freq = ROPE_BASE ** (-jnp.arange(0, MLA_ROPE, 2, dtype=F32) / MLA_ROPE)
    ang = pos.astype(F32)[:, None] * inv_freq
    cos, sin = jnp.cos(ang), jnp.sin(ang)
    z = lambda n: jnp.zeros((pos.shape[0], n), F32)
    tc = jnp.concatenate([cos, cos, z(LANES - 2 * half)], axis=1)
    ta = jnp.concatenate([-sin, z(LANES - half)], axis=1)
    tb = jnp.concatenate([z(half), sin, z(LANES - 2 * half)], axis=1)
    return tc, ta, tb


def _rot(v, tc, ta, tb, sign):
    half = MLA_ROPE // 2
    return v * tc + sign * (pltpu.roll(v, LANES - half, 1) * ta + pltpu.roll(v, half, 1) * tb)


def _mla_assemble(qpre, kv, proj1, kr_blk, tabs, n_heads, name):
    s = qpre.shape[0]
    tr = _pick(s, (512, 256, 128))

    def body(qp_ref, kn_ref, kr_ref, tc_ref, ta_ref, tb_ref, qc_ref, kc_ref):
        tc, ta, tb = tc_ref[...], ta_ref[...], tb_ref[...]
        qc_ref[:, :LANES] = qp_ref[:, :LANES]
        qc_ref[:, LANES:] = _rot(qp_ref[:, LANES:], tc, ta, tb, 1.0)
        kc_ref[:, :LANES] = kn_ref[...]
        kc_ref[:, LANES:] = _rot(kr_ref[...], tc, ta, tb, 1.0)

    tab = pl.BlockSpec((tr, LANES), lambda i, hh: (i, 0))
    wide = pl.BlockSpec((tr, MLA_QK), lambda i, hh: (i, hh))
    shp = jax.ShapeDtypeStruct((s, n_heads * MLA_QK), F32)
    return pl.pallas_call(
        body,
        grid=(s // tr, n_heads),
        in_specs=[
            wide,
            pl.BlockSpec((tr, LANES), lambda i, hh: (i, 2 * hh)),
            pl.BlockSpec((tr, LANES), lambda i, hh: (i, kr_blk)),
            tab,
            tab,
            tab,
        ],
        out_specs=[wide, wide],
        out_shape=[shp, shp],
        compiler_params=_cparams(("parallel", "parallel")),
        name=name,
    )(qpre, kv, proj1, *tabs)


def _mla_disassemble(dqcat, dkcat, dv, tabs, n_heads, name):
    s = dqcat.shape[0]
    tr = _pick(s, (512, 256, 128))

    def body(dq_ref, dk_ref, dv_ref, tc_ref, ta_ref, tb_ref, dqp_ref, dkv_ref, dkr_ref):
        hh = pl.program_id(1)
        tc, ta, tb = tc_ref[...], ta_ref[...], tb_ref[...]
        dqp_ref[:, :LANES] = dq_ref[:, :LANES].astype(BF16)
        dqp_ref[:, LANES:] = _rot(dq_ref[:, LANES:], tc, ta, tb, -1.0).astype(BF16)
        dkv_ref[:, :LANES] = dk_ref[:, :LANES].astype(BF16)
        dkv_ref[:, LANES:] = dv_ref[...].astype(BF16)
        part = dk_ref[:, LANES:]

        @pl.when(hh == 0)
        def _():
            dkr_ref[...] = part

        @pl.when(hh > 0)
        def _():
            dkr_ref[...] += part

        @pl.when(hh == n_heads - 1)
        def _():
            dkr_ref[...] = _rot(dkr_ref[...], tc, ta, tb, -1.0)

    tab = pl.BlockSpec((tr, LANES), lambda i, hh: (i, 0))
    wide = pl.BlockSpec((tr, MLA_QK), lambda i, hh: (i, hh))
    shp = jax.ShapeDtypeStruct((s, n_heads * MLA_QK), BF16)
    return pl.pallas_call(
        body,
        grid=(s // tr, n_heads),
        in_specs=[wide, wide, pl.BlockSpec((tr, LANES), lambda i, hh: (i, hh)), tab, tab, tab],
        out_specs=[wide, wide, tab],
        out_shape=[shp, shp, jax.ShapeDtypeStruct((s, LANES), F32)],
        compiler_params=_cparams(("parallel", "arbitrary")),
        name=name,
    )(dqcat, dkcat, dv, *tabs)


def _forget_scan(proj2, f_blk, bias, name):
    s = proj2.shape[0]
    n = LANES

    def body(f_ref, b_ref, c_ref):
        rows, cols = _iotas(n, n)
        tri = (cols <= rows).astype(BF16)

        def step(j, carry):
            r0 = pl.multiple_of(j * n, n)
            f = f_ref[pl.ds(r0, n), :] + b_ref[...]
            lf = jnp.minimum(f, 0.0) - jnp.log1p(jnp.exp(-jnp.abs(f)))
            a, b, c = _split3(lf)
            cs = _dot(tri, a, "nn") + _dot(tri, b, "nn") + _dot(tri, c, "nn") + carry
            c_ref[pl.ds(r0, n), :] = cs
            return cs[n - 1 : n, :]

        lax.fori_loop(0, s // n, step, jnp.zeros((1, n), F32))

    return pl.pallas_call(
        body,
        grid=(1,),
        in_specs=[pl.BlockSpec((s, n), lambda i: (0, f_blk)), pl.BlockSpec((1, n), lambda i: (0, 0))],
        out_specs=pl.BlockSpec((s, n), lambda i: (0, 0)),
        out_shape=jax.ShapeDtypeStruct((s, n), F32),
        compiler_params=_cparams(("arbitrary",)),
        name=name,
    )(proj2, bias)


def _forget_scan_bwd(dc_col, dc_row, proj2, f_blk, bias, n_heads, name):
    s = proj2.shape[0]
    n = LANES
    nb = s // n

    def body(dcc_ref, dcr_ref, f_ref, b_ref, df_ref, db_ref):
        rows, cols = _iotas(n, n)
        tri = (cols >= rows).astype(BF16)
        live = cols < n_heads

        def step(j, carry):
            acc, dbv = carry
            r0 = pl.multiple_of((nb - 1 - j) * n, n)
            a, b, c = _split3(dcc_ref[pl.ds(r0, n), :] + dcr_ref[pl.ds(r0, n), :])
            dl = _dot(tri, a, "nn") + _dot(tri, b, "nn") + _dot(tri, c, "nn") + acc
            f = f_ref[pl.ds(r0, n), :] + b_ref[...]
            df = jnp.where(live, dl / (1.0 + jnp.exp(f)), 0.0)
            df_ref[pl.ds(r0, n), :] = df.astype(BF16)
            return dl[0:1, :], dbv + jnp.sum(df, axis=0, keepdims=True)

        z = jnp.zeros((1, n), F32)
        _, dbv = lax.fori_loop(0, nb, step, (z, z))
        db_ref[...] = dbv

    return pl.pallas_call(
        body,
        grid=(1,),
        in_specs=[
            pl.BlockSpec((s, n), lambda i: (0, 0)),
            pl.BlockSpec((s, n), lambda i: (0, 0)),
            pl.BlockSpec((s, n), lambda i: (0, f_blk)),
            pl.BlockSpec((1, n), lambda i: (0, 0)),
        ],
        out_specs=[pl.BlockSpec((s, n), lambda i: (0, 0)), pl.BlockSpec((1, n), lambda i: (0, 0))],
        out_shape=[jax.ShapeDtypeStruct((s, n), BF16), jax.ShapeDtypeStruct((1, n), F32)],
        compiler_params=_cparams(("arbitrary",)),
        name=name,
    )(dc_col, dc_row, proj2, bias)


def _adamw(w, g, m, v, name):
    r, c = w.shape
    tr = _pick(r, (128, 64, 32, 16, 8))
    c1 = 1.0 - ADAM_B1**ADAM_STEP
    c2 = 1.0 - ADAM_B2**ADAM_STEP

    def body(w_ref, g_ref, m_ref, v_ref, d_ref, mo_ref, vo_ref):
        gv = g_ref[...]
        mn = ADAM_B1 * m_ref[...] + (1.0 - ADAM_B1) * gv
        vn = ADAM_B2 * v_ref[...] + (1.0 - ADAM_B2) * (gv * gv)
        mo_ref[...] = mn
        vo_ref[...] = vn
        d_ref[...] = -ADAM_LR * ((mn / c1) / (jnp.sqrt(vn / c2) + ADAM_EPS) + ADAM_WD * w_ref[...])

    blk = pl.BlockSpec((tr, c), lambda i: (i, 0))
    shp = jax.ShapeDtypeStruct((r, c), F32)
    return pl.pallas_call(
        body,
        grid=(r // tr,),
        in_specs=[blk, blk, blk, blk],
        out_specs=[blk, blk, blk],
        out_shape=[shp, shp, shp],
        compiler_params=_cparams(("parallel",)),
        name=name,
    )(w, g, m, v)


def _mesh_pos():
    return lax.axis_index("x"), lax.axis_index("y"), lax.axis_index("c")


def _other_chips(x, y):
    return [(1 - x, y), (x, 1 - y), (1 - x, 1 - y)]


ANY = pl.BlockSpec(memory_space=pl.ANY)


GATHER_SEMS = [pltpu.SemaphoreType.DMA((6,)), pltpu.SemaphoreType.DMA((6,))]
EXCHANGE_SEMS = [pltpu.SemaphoreType.DMA((3,)), pltpu.SemaphoreType.DMA((3,))]


def _gather_steps(w_ref, out_ref, send_sems, recv_sems):
    half = w_ref.shape[0] // 2
    x, y, c = _mesh_pos()
    me = 2 * x + y
    chips = _other_chips(x, y)

    def region(chip, hc):
        return out_ref.at[chip, pl.ds(hc * half, half), :]

    def copy(k, src, dst, to):
        return pltpu.make_async_remote_copy(
            src_ref=src, dst_ref=dst, send_sem=send_sems.at[k], recv_sem=recv_sems.at[k], device_id=to, device_id_type=MESH
        )

    first = [copy(j, w_ref.at[pl.ds(c * half, half), :], region(me, c), (cx, cy, c)) for j, (cx, cy) in enumerate(chips)]
    passed = [copy(3 + j, region(2 * cx + cy, c), region(2 * cx + cy, c), (x, y, 1 - c)) for j, (cx, cy) in enumerate(chips)]

    def send():
        for cp in first:
            cp.start()

    def forward():
        for j, (cx, cy) in enumerate(chips):
            copy(j, region(2 * cx + cy, c), region(2 * cx + cy, c), (x, y, c)).wait_recv()
            passed[j].start()

    def finish():
        for j, (cx, cy) in enumerate(chips):
            copy(3 + j, region(2 * cx + cy, 1 - c), region(2 * cx + cy, 1 - c), (x, y, c)).wait_recv()
        for cp in first + passed:
            cp.wait_send()

    return send, forward, finish


def _gather_weights(wp, tag):
    rp, wd = wp.shape

    def body(w_ref, out_ref, send_sems, recv_sems):
        for step in _gather_steps(w_ref, out_ref, send_sems, recv_sems):
            step()

    return pl.pallas_call(
        body,
        in_specs=[ANY],
        out_specs=ANY,
        out_shape=jax.ShapeDtypeStruct((N_CHIPS, rp, wd), wp.dtype),
        scratch_shapes=GATHER_SEMS,
        name=f"gather_weights_{tag}",
    )(wp)


def _place_own(wall, wp, pos, tag):
    rp, wd = wp.shape

    def body(x_ref, y_ref, c_ref, wall_ref, w_ref, o_ref):
        o_ref[0] = w_ref[...]

    grid_spec = pltpu.PrefetchScalarGridSpec(
        num_scalar_prefetch=3,
        grid=(rp // PACK_TR,),
        in_specs=[ANY, pl.BlockSpec((PACK_TR, wd), lambda i, xr, yr, cr: (i, 0))],
        out_specs=pl.BlockSpec((1, PACK_TR, wd), lambda i, xr, yr, cr: (2 * xr[0] + yr[0], i, 0)),
    )
    return pl.pallas_call(
        body,
        grid_spec=grid_spec,
        out_shape=jax.ShapeDtypeStruct(wall.shape, wall.dtype),
        input_output_aliases={3: 0},
        compiler_params=_cparams(("parallel",)),
        name=f"place_own_shard_{tag}",
    )(*pos, wall, wp)


def _pair_exchange(g, tag):
    _, rp, wd = g.shape
    half = rp // 2

    def body(g_ref, out_ref, send_sem, recv_sem):
        x, y, c = _mesh_pos()
        cp = pltpu.make_async_remote_copy(
            src_ref=g_ref.at[:, pl.ds((1 - c) * half, half), :],
            dst_ref=out_ref,
            send_sem=send_sem,
            recv_sem=recv_sem,
            device_id=(x, y, 1 - c),
            device_id_type=MESH,
        )
        cp.start()
        cp.wait()

    return pl.pallas_call(
        body,
        in_specs=[ANY],
        out_specs=ANY,
        out_shape=jax.ShapeDtypeStruct((N_CHIPS, half, wd), g.dtype),
        scratch_shapes=[pltpu.SemaphoreType.DMA, pltpu.SemaphoreType.DMA],
        name=f"rs_pair_exchange_{tag}",
    )(g)


def _pair_add(g, recv, pos, tag):
    _, rp, wd = g.shape
    half = rp // 2
    nb = half // PACK_TR

    def body(x_ref, y_ref, c_ref, g_ref, r_ref, o_ref):
        o_ref[...] = (g_ref[...] + r_ref[...]).astype(BF16)

    blk = (1, PACK_TR, wd)
    grid_spec = pltpu.PrefetchScalarGridSpec(
        num_scalar_prefetch=3,
        grid=(N_CHIPS, nb),
        in_specs=[
            pl.BlockSpec(blk, lambda j, i, xr, yr, cr: (j, cr[0] * nb + i, 0)),
            pl.BlockSpec(blk, lambda j, i, xr, yr, cr: (j, i, 0)),
        ],
        out_specs=pl.BlockSpec(blk, lambda j, i, xr, yr, cr: (j, i, 0)),
    )
    return pl.pallas_call(
        body,
        grid_spec=grid_spec,
        out_shape=jax.ShapeDtypeStruct((N_CHIPS, half, wd), BF16),
        compiler_params=_cparams(("parallel", "parallel")),
        name=f"rs_pair_add_{tag}",
    )(*pos, g, recv)


def _exchange_steps(s_ref, out_ref, send_sems, recv_sems):
    x, y, c = _mesh_pos()
    me = 2 * x + y
    chips = _other_chips(x, y)

    def copy(j, src, dst, to):
        return pltpu.make_async_remote_copy(
            src_ref=src, dst_ref=dst, send_sem=send_sems.at[j], recv_sem=recv_sems.at[j], device_id=to, device_id_type=MESH
        )

    sends = [copy(j, s_ref.at[2 * cx + cy], out_ref.at[me], (cx, cy, c)) for j, (cx, cy) in enumerate(chips)]

    def send():
        for cp in sends:
            cp.start()

    def finish():
        for j, (cx, cy) in enumerate(chips):
            copy(j, s_ref.at[me], out_ref.at[2 * cx + cy], (x, y, c)).wait_recv()
        for cp in sends:
            cp.wait_send()

    return send, finish


def _chip_exchange(sp, tag):
    def body(s_ref, out_ref, send_sems, recv_sems):
        for step in _exchange_steps(s_ref, out_ref, send_sems, recv_sems):
            step()

    return pl.pallas_call(
        body,
        in_specs=[ANY],
        out_specs=ANY,
        out_shape=jax.ShapeDtypeStruct(sp.shape, sp.dtype),
        scratch_shapes=EXCHANGE_SEMS,
        name=f"rs_chip_exchange_{tag}",
    )(sp)


def _sum_slots(own, slots, pos, tag):
    _, rh, wd = slots.shape
    nb = rh // PACK_TR

    def body(x_ref, y_ref, c_ref, own_ref, a_ref, b_ref, d_ref, o_ref):
        f = lambda r: r[0].astype(F32)
        o_ref[...] = ((f(own_ref) + f(a_ref)) + f(b_ref)) + f(d_ref)

    blk = (1, PACK_TR, wd)

    def other(k):
        return pl.BlockSpec(blk, lambda i, xr, yr, cr: (k + (k >= 2 * xr[0] + yr[0]).astype(jnp.int32), i, 0))

    grid_spec = pltpu.PrefetchScalarGridSpec(
        num_scalar_prefetch=3,
        grid=(nb,),
        in_specs=[pl.BlockSpec(blk, lambda i, xr, yr, cr: (2 * xr[0] + yr[0], i, 0)), other(0), other(1), other(2)],
        out_specs=pl.BlockSpec((PACK_TR, wd), lambda i, xr, yr, cr: (cr[0] * nb + i, 0)),
    )
    return pl.pallas_call(
        body,
        grid_spec=grid_spec,
        out_shape=jax.ShapeDtypeStruct((2 * rh, wd), F32),
        compiler_params=_cparams(("parallel",)),
        name=f"rs_sum_slots_{tag}",
    )(*pos, own, slots, slots, slots)


def _pair_gather(t, tag):
    rh = t.shape[0] // 2

    def body(t_ref, out_ref, send_sem, recv_sem):
        x, y, c = _mesh_pos()
        cp = pltpu.make_async_remote_copy(
            src_ref=t_ref.at[pl.ds(c * rh, rh), :],
            dst_ref=out_ref.at[pl.ds(c * rh, rh), :],
            send_sem=send_sem,
            recv_sem=recv_sem,
            device_id=(x, y, 1 - c),
            device_id_type=MESH,
        )
        cp.start()
        cp.wait_send()
        pltpu.make_async_remote_copy(
            src_ref=t_ref.at[pl.ds((1 - c) * rh, rh), :],
            dst_ref=out_ref.at[pl.ds((1 - c) * rh, rh), :],
            send_sem=send_sem,
            recv_sem=recv_sem,
            device_id=(x, y, c),
            device_id_type=MESH,
        ).wait_recv()

    return pl.pallas_call(
        body,
        in_specs=[ANY],
        out_specs=ANY,
        out_shape=jax.ShapeDtypeStruct(t.shape, t.dtype),
        input_output_aliases={0: 0},
        scratch_shapes=[pltpu.SemaphoreType.DMA, pltpu.SemaphoreType.DMA],
        name=f"rs_pair_gather_{tag}",
    )(t)


def _allreduce_small(v):
    shape = v.shape
    n_dev = 8

    def body(v_ref, o_ref, slots, send_sems, recv_sems):
        x, y, c = _mesh_pos()
        me = 4 * x + 2 * y + c
        slots[me] = v_ref[...]
        sends = []
        for k in range(1, n_dev):
            fx, fy, fc = (k >> 2) & 1, (k >> 1) & 1, k & 1
            to = (x ^ fx, y ^ fy, c ^ fc)
            cp = pltpu.make_async_remote_copy(
                src_ref=v_ref,
                dst_ref=slots.at[me],
                send_sem=send_sems.at[k - 1],
                recv_sem=recv_sems.at[k - 1],
                device_id=to,
                device_id_type=MESH,
            )
            cp.start()
            sends.append(cp)
        for k in range(1, n_dev):
            fx, fy, fc = (k >> 2) & 1, (k >> 1) & 1, k & 1
            frm = 4 * (x ^ fx) + 2 * (y ^ fy) + (c ^ fc)
            pltpu.make_async_remote_copy(
                src_ref=v_ref,
                dst_ref=slots.at[frm],
                send_sem=send_sems.at[k - 1],
                recv_sem=recv_sems.at[k - 1],
                device_id=(x, y, c),
                device_id_type=MESH,
            ).wait_recv()
        for cp in sends:
            cp.wait_send()
        acc = slots[0]
        for k in range(1, n_dev):
            acc = acc + slots[k]
        o_ref[...] = acc

    vm = pl.BlockSpec(memory_space=pltpu.VMEM)
    return pl.pallas_call(
        body,
        in_specs=[vm],
        out_specs=vm,
        out_shape=jax.ShapeDtypeStruct(shape, F32),
        scratch_shapes=[pltpu.VMEM((n_dev,) + shape, F32), pltpu.SemaphoreType.DMA((n_dev - 1,)), pltpu.SemaphoreType.DMA((n_dev - 1,))],
        name="allreduce_small",
    )(v)


def _pack_layout(shard_shapes):
    offs, rows = [], []
    off = 0
    for r, c in shard_shapes:
        assert (r * c) % PACK_W == 0
        n = r * c // PACK_W
        offs.append(off)
        rows.append(n)
        off += -(-n // 16) * 16
    rp = -(-off // (2 * PACK_TR)) * (2 * PACK_TR)
    return offs, rows, rp


def _pack_rows(parts, offs, rows, rp, lead):
    ends = list(offs[1:]) + [rp]
    nolead = ((0, 0),) * len(lead)
    out = [jnp.pad(p, nolead + ((0, e - o - n), (0, 0))) for p, o, n, e in zip(parts, offs, rows, ends)]
    return jnp.concatenate(out, axis=len(lead))


def kernel(x, positions, ln0, w_in0, w_out0, ln1, w_in1, q_norm1, w_qb1, kv_norm1, w_kvb1, w_out1, ln2, w_in2, b_f2, w_out2, ln3, w_in3, w_out3, final_norm, loss_target, m_ln0, m_w_in0, m_w_out0, m_ln1, m_w_in1, m_q_norm1, m_w_qb1, m_kv_norm1, m_w_kvb1, m_w_out1, m_ln2, m_w_in2, m_b_f2, m_w_out2, m_ln3, m_w_in3, m_w_out3, m_final_norm, v_ln0, v_w_in0, v_w_out0, v_ln1, v_w_in1, v_q_norm1, v_w_qb1, v_kv_norm1, v_w_kvb1, v_w_out1, v_ln2, v_w_in2, v_b_f2, v_w_out2, v_ln3, v_w_in3, v_w_out3, v_final_norm):
    xs = x[0]
    s, d = xs.shape
    di = 4 * w_out0.shape[0]
    nh = di // HEAD_DIM
    idx = tuple(lax.axis_index(a).astype(jnp.int32).reshape(1) for a in ("x", "y", "c"))

    big = [w_in0, w_out0, w_in1, w_qb1, w_kvb1, w_out1, w_in2, w_out2, w_in3, w_out3]
    col_sharded = [True, False, True, True, True, False, True, False, True, False]
    shard_shapes = [w.shape for w in big]
    as_gathered = [True, False, False, False, False, False, False, False, True, False]
    n_first = 2

    def pack_weights(lo, hi):
        offs, rows, rp = _pack_layout(shard_shapes[lo:hi])
        return _pack_rows([w.astype(BF16).reshape(n, PACK_W) for w, n in zip(big[lo:hi], rows)], offs, rows, rp, ()), offs, rows

    def unpack_weights(wall, lo, hi, offs, rows):
        out = []
        for (r, c), o, n, cs, g4 in zip(shard_shapes[lo:hi], offs, rows, col_sharded[lo:hi], as_gathered[lo:hi]):
            slab = wall[:, o : o + n, :].reshape(N_CHIPS, r, c)
            if g4:
                out.append(slab)
            else:
                out.append(slab.transpose(1, 0, 2).reshape(r, N_CHIPS * c) if cs else slab.reshape(N_CHIPS * r, c))
        return out

    wp_a, offs_a, rows_a = pack_weights(0, n_first)
    wp_b, offs_b, rows_b = pack_weights(n_first, len(big))
    wall_a = _place_own(_gather_weights(wp_a, "l0"), wp_a, idx, "l0")
    f_in0, f_out0 = unpack_weights(wall_a, 0, n_first, offs_a, rows_a)

    row = lambda v: v.reshape(1, -1)

    def sb_layer_fwd(xin, ln, w_in, w_out, tag, gather=None):
        h = _rmsnorm_fwd(xin, row(ln), f"norm_fwd_{tag}")
        proj = _proj_w4(h, w_in, f"proj_in_{tag}")
        o, lt, *gathered = _sb_fwd(proj, nh, f"sb_fwd_{tag}", gather=gather)
        g = _gate_fwd(o, proj, 3, f"gate_fwd_{tag}")
        xout = _matmul(g, w_out, "nn", f"proj_out_{tag}", res=xin)
        return xout, (xin, h, proj, o, lt, g), gathered

    x1, sv0, (wall_b_raw,) = sb_layer_fwd(xs, ln0, f_in0, f_out0, "l0", gather=wp_b)
    wall_b = _place_own(wall_b_raw, wp_b, idx, "rest")
    f_in1, f_qb1, f_kvb1, f_out1, f_in2, f_out2, f_in3, f_out3 = unpack_weights(wall_b, n_first, len(big), offs_b, rows_b)

    i_kr = MLA_Q_RANK + MLA_KV_RANK + MLA_ROPE
    w1p = jnp.concatenate([f_in1[:, i_kr:], f_in1[:, :i_kr], jnp.zeros((d, LANES - MLA_ROPE), BF16)], axis=1)
    qlat_blk = di // MLA_Q_RANK
    kvlat_blk = (di + MLA_Q_RANK) // MLA_KV_RANK
    kr_blk = (di + MLA_Q_RANK + MLA_KV_RANK) // LANES
    qk_w = HEAD_DIM + MLA_ROPE
    wqbp = jnp.pad(f_qb1.reshape(MLA_Q_RANK, nh, qk_w), ((0, 0), (0, 0), (0, MLA_QK - qk_w))).reshape(MLA_Q_RANK, nh * MLA_QK)
    n2 = f_in2.shape[1]
    w2p = jnp.pad(f_in2, ((0, 0), (0, 4 * di + LANES - n2)))
    b2p = jnp.pad(b_f2, (0, LANES - nh)).reshape(1, LANES)

    tabs = _rope_tables(positions[0])

    def sb_layer_bwd(dxn, dxnb, saved, ln, w_in, w_out, tag):
        xin, h, proj, o, lt, g = saved
        dgf = _matmul(dxnb, w_out, "nt", f"dgate_in_{tag}")
        dw_out = _matmul(g, dxnb, "tn", f"dw_out_{tag}")
        do, dgate = _gate_bwd(dgf, o, proj, 3, f"gate_bwd_{tag}")
        dq, dk, dv = _sb_bwd(proj, lt, do, nh, f"sb_bwd_{tag}")
        dproj = [dq, dk, dv, dgate]
        dh = _dh_w4(dproj, w_in, f"dh_{tag}")
        dw_in = _dw_w4(h, dproj, f"dw_in_{tag}")
        dx, dxb, dln = _rmsnorm_bwd(xin, row(ln), dh, f"norm_bwd_{tag}", dres=dxn)
        return dx, dxb, dln, dw_in, dw_out

    h1 = _rmsnorm_fwd(x1, row(ln1), "norm_fwd_l1")
    proj1 = _matmul(h1, w1p, "nn", "proj_in_l1")
    qn = _rmsnorm_fwd(proj1, row(q_norm1), "qnorm_fwd_l1", col_block=qlat_blk)
    kvn = _rmsnorm_fwd(proj1, row(kv_norm1), "kvnorm_fwd_l1", col_block=kvlat_blk)
    qpre = _matmul(qn, wqbp, "nn", "q_up_l1")
    kv1 = _matmul(kvn, f_kvb1, "nn", "kv_up_l1")
    qcat, kcat = _mla_assemble(qpre, kv1, proj1, kr_blk, tabs, nh, "mla_assemble_l1")
    mla_blk = (lambda hh: hh, lambda hh: hh, lambda hh: 2 * hh + 1)
    o1, lse1 = _sm_fwd(qcat, kcat, kv1, nh, MLA_QK, *mla_blk, "chunk", "mla_fwd_l1")
    g1 = _gate_fwd(o1, proj1, 0, "gate_fwd_l1")
    x2 = _matmul(g1, f_out1, "nn", "proj_out_l1", res=x1)

    h2 = _rmsnorm_fwd(x2, row(ln2), "norm_fwd_l2")
    proj2 = _matmul(h2, w2p, "nn", "proj_in_l2")
    f_blk = 4 * di // LANES
    cum = _forget_scan(proj2, f_blk, b2p, "forget_scan_l2")
    cum_h = cum[:, :nh].T
    t_sm = min(SM_TK, s)
    crow = cum_h.reshape(nh, s, 1)
    ccol = cum_h.reshape(nh, s // t_sm, t_sm)
    fg_blk = (lambda hh: hh, lambda hh: nh + hh, lambda hh: 2 * nh + hh)
    o2, lse2 = _sm_fwd(proj2, proj2, proj2, nh, HEAD_DIM, *fg_blk, "causal", "forget_fwd_l2", crow=crow, ccol=ccol)
    g2 = _gate_fwd(o2, proj2, 3, "gate_fwd_l2")
    x3 = _matmul(g2, f_out2, "nn", "proj_out_l2", res=x2)

    x4, sv3, _ = sb_layer_fwd(x3, ln3, f_in3, f_out3, "l3")

    dx, dxb, d_final, loss_part = _loss_head(x4, row(final_norm), loss_target[0], "loss_head")
    dx, dxb, d_ln3, dw_in3, dw_out3 = sb_layer_bwd(dx, dxb, sv3, ln3, f_in3, f_out3, "l3")

    dgf2 = _matmul(dxb, f_out2, "nt", "dgate_in_l2")
    dw_out2 = _matmul(g2, dxb, "tn", "dw_out_l2")
    do2, dgate2 = _gate_bwd(dgf2, o2, proj2, 3, "gate_bwd_l2")
    dq2, dk2, dv2, dcc2, dcr2 = _sm_bwd(proj2, proj2, proj2, o2, do2, lse2, nh, HEAD_DIM, *fg_blk, "causal", "forget_bwd_l2", crow=crow, ccol=ccol)
    lanes_of = lambda a: jnp.pad(a.reshape(nh, s).T, ((0, 0), (0, LANES - nh)))
    df2, d_bf = _forget_scan_bwd(lanes_of(dcc2), lanes_of(dcr2), proj2, f_blk, b2p, nh, "forget_scan_bwd_l2")
    dproj2 = jnp.concatenate([dq2.astype(BF16), dk2.astype(BF16), dv2.astype(BF16), dgate2, df2], axis=1)
    dh2 = _matmul(dproj2, w2p, "nt", "dh_l2")
    dw_in2 = _matmul(h2, dproj2, "tn", "dw_in_l2")[:, :n2]
    dx, dxb, d_ln2 = _rmsnorm_bwd(x2, row(ln2), dh2, "norm_bwd_l2", dres=dx)

    n_late = 6

    def reduce_start(lo, hi, dws, tag):
        offs, rows, rp = _pack_layout(shard_shapes[lo:hi])
        parts = []
        for g, (r, c), n, cs, is4 in zip(dws, shard_shapes[lo:hi], rows, col_sharded[lo:hi], as_gathered[lo:hi]):
            g4 = g if is4 else (g.reshape(r, N_CHIPS, c).transpose(1, 0, 2) if cs else g.reshape(N_CHIPS, r, c))
            parts.append(g4.reshape(N_CHIPS, n, PACK_W))
        gp = _pack_rows(parts, offs, rows, rp, (N_CHIPS,))
        return _pair_add(gp, _pair_exchange(gp, tag), idx, tag), offs, rows

    def reduce_finish(pair, slots, lo, hi, offs, rows, tag):
        gred = _pair_gather(_sum_slots(pair, slots, idx, tag), tag)
        return [gred[o : o + n, :].reshape(r, c) for (r, c), o, n in zip(shard_shapes[lo:hi], offs, rows)]

    pair_b, offs_gb, rows_gb = reduce_start(n_late, len(big), [dw_in2, dw_out2, dw_in3, dw_out3], "l23")

    dgf1 = _matmul(dxb, f_out1, "nt", "dgate_in_l1")
    dw_out1 = _matmul(g1, dxb, "tn", "dw_out_l1")
    do1, dgate1 = _gate_bwd(dgf1, o1, proj1, 0, "gate_bwd_l1")
    dqc, dkc, dv1, slots_b = _sm_bwd(qcat, kcat, kv1, o1, do1, lse1, nh, MLA_QK, *mla_blk, "chunk", "mla_bwd_l1", exchange=pair_b)
    dqpre, dkv1, dkr = _mla_disassemble(dqc, dkc, dv1, tabs, nh, "mla_disassemble_l1")
    dqn = _matmul(dqpre, wqbp, "nt", "dqn_l1")
    dw_qbp = _matmul(qn, dqpre, "tn", "dw_qb_l1")
    dw_qb1 = dw_qbp.reshape(MLA_Q_RANK, nh, MLA_QK)[:, :, :qk_w].reshape(MLA_Q_RANK, nh * qk_w)
    dkvn = _matmul(dkv1, f_kvb1, "nt", "dkvn_l1")
    dw_kvb1 = _matmul(kvn, dkv1, "tn", "dw_kvb_l1")
    _, dqlat_b, d_qnorm = _rmsnorm_bwd(proj1, row(q_norm1), dqn, "qnorm_bwd_l1", col_block=qlat_blk)
    _, dkvlat_b, d_kvnorm = _rmsnorm_bwd(proj1, row(kv_norm1), dkvn, "kvnorm_bwd_l1", col_block=kvlat_blk)
    dproj1 = jnp.concatenate([dgate1, dqlat_b, dkvlat_b, dkr.astype(BF16)], axis=1)
    dh1 = _matmul(dproj1, w1p, "nt", "dh_l1")
    dw1p = _matmul(h1, dproj1, "tn", "dw_in_l1")
    dw_in1 = jnp.concatenate([dw1p[:, di : di + i_kr], dw1p[:, :di]], axis=1)
    dx, dxb, d_ln1 = _rmsnorm_bwd(x1, row(ln1), dh1, "norm_bwd_l1", dres=dx)

    dx, dxb, d_ln0, dw_in0, dw_out0 = sb_layer_bwd(dx, dxb, sv0, ln0, f_in0, f_out0, "l0")
    grad_x = dx.reshape(x.shape)

    pair_a, offs_ga, rows_ga = reduce_start(0, n_late, [dw_in0, dw_out0, dw_in1, dw_qb1, dw_kvb1, dw_out1], "l01")
    slots_a = _chip_exchange(pair_a, "l01")
    big_grads = reduce_finish(pair_a, slots_a, 0, n_late, offs_ga, rows_ga, "l01")
    big_grads += reduce_finish(pair_b, slots_b, n_late, len(big), offs_gb, rows_gb, "l23")

    small = [ln0, ln1, q_norm1, kv_norm1, ln2, b_f2, ln3, final_norm]
    small_g = [d_ln0[0], d_ln1[0], d_qnorm[0], d_kvnorm[0], d_ln2[0], d_bf[0, :nh], d_ln3[0], d_final[0]]
    n_small = SMALL_SHAPE[0] * SMALL_SHAPE[1]
    used = sum(v.shape[0] for v in small) + 1
    assert used <= n_small

    def pack_small(vs, last):
        return jnp.concatenate(list(vs) + [last, jnp.zeros((n_small - used,), F32)]).reshape(SMALL_SHAPE)

    sm_sum = _allreduce_small(pack_small(small_g, loss_part[0, :1]))
    flat = sm_sum.reshape(-1)
    loss = flat[used - 1]

    big_m = [m_w_in0, m_w_out0, m_w_in1, m_w_qb1, m_w_kvb1, m_w_out1, m_w_in2, m_w_out2, m_w_in3, m_w_out3]
    big_v = [v_w_in0, v_w_out0, v_w_in1, v_w_qb1, v_w_kvb1, v_w_out1, v_w_in2, v_w_out2, v_w_in3, v_w_out3]
    big_names = ["w_in0", "w_out0", "w_in1", "w_qb1", "w_kvb1", "w_out1", "w_in2", "w_out2", "w_in3", "w_out3"]
    big_upd = [_adamw(w, g, m, v, f"adamw_{nm}") for w, g, m, v, nm in zip(big, big_grads, big_m, big_v, big_names)]

    small_m = [m_ln0, m_ln1, m_q_norm1, m_kv_norm1, m_ln2, m_b_f2, m_ln3, m_final_norm]
    small_v = [v_ln0, v_ln1, v_q_norm1, v_kv_norm1, v_ln2, v_b_f2, v_ln3, v_final_norm]
    one = jnp.ones((1,), F32)
    sd, smn, svn = _adamw(pack_small(small, one), sm_sum, pack_small(small_m, one), pack_small(small_v, one), "adamw_small")

    def unpack_small(p):
        out, at = [], 0
        fl = p.reshape(-1)
        for v in small:
            out.append(fl[at : at + v.shape[0]])
            at += v.shape[0]
        return out

    sg_l, sd_l, sm_l, sv_l = unpack_small(sm_sum), unpack_small(sd), unpack_small(smn), unpack_small(svn)

    order = ["ln0", "w_in0", "w_out0", "ln1", "w_in1", "q_norm1", "w_qb1", "kv_norm1", "w_kvb1", "w_out1", "ln2", "w_in2", "b_f2", "w_out2", "ln3", "w_in3", "w_out3", "final_norm"]
    small_names = ["ln0", "ln1", "q_norm1", "kv_norm1", "ln2", "b_f2", "ln3", "final_norm"]
    grads, deltas, new_m, new_v = {}, {}, {}, {}
    for nm, g, (dl, mn, vn) in zip(big_names, big_grads, big_upd):
        grads[nm], deltas[nm], new_m[nm], new_v[nm] = g, dl, mn, vn
    for nm, g, dl, mn, vn in zip(small_names, sg_l, sd_l, sm_l, sv_l):
        grads[nm], deltas[nm], new_m[nm], new_v[nm] = g, dl, mn, vn
    return (loss, grad_x, *[grads[n] for n in order], *[deltas[n] for n in order], *[new_m[n] for n in order], *[new_v[n] for n in order])
```

```python
import functools

import jax
import jax.numpy as jnp
from jax import lax
from jax.experimental import pallas as pl
from jax.experimental.pallas import tpu as pltpu

F32 = jnp.float32
BF16 = jnp.bfloat16
EPS = 1e-6
NEG = -1e30
HEAD_DIM = 128
CHUNK_SHIFT = 6
MLA_Q_RANK = 256
MLA_KV_RANK = 128
MLA_ROPE = 64
MLA_QK = 256
ROPE_BASE = 10000.0
ADAM_LR = 0.001
ADAM_B1 = 0.9
ADAM_B2 = 0.999
ADAM_EPS = 1e-08
ADAM_WD = 0.01
ADAM_STEP = 10
VMEM_LIMIT_BYTES = 56 * 2**20
LANES = 128
PACK_W = 1024
PACK_TR = 128
SMALL_SHAPE = (8, 768)
MESH = pl.DeviceIdType.MESH
N_CHIPS = 4


def _pick(n, cands):
    for c in cands:
        if n % c == 0:
            return c
    return n


def _cparams(sem):
    return pltpu.CompilerParams(dimension_semantics=sem, vmem_limit_bytes=VMEM_LIMIT_BYTES)


def _dot(a, b, dims):
    dn = {"nn": (((1,), (0,)), ((), ())), "nt": (((1,), (1,)), ((), ())), "tn": (((0,), (0,)), ((), ()))}[dims]
    return lax.dot_general(a, b, dn, preferred_element_type=F32)


def _matmul(a, b, dims, name, res=None):
    if dims == "nn":
        (m, k), (k2, n) = a.shape, b.shape
    elif dims == "nt":
        (m, k), (n, k2) = a.shape, b.shape
    else:
        (k, m), (k2, n) = a.shape, b.shape
    assert k == k2, (a.shape, b.shape, dims)
    tm = _pick(m, (1024, 512, 256, 128))
    tn = _pick(n, (1024, 640, 512, 384, 256, 128))
    tk = _pick(k, (1024, 640, 512, 256, 128))
    nk = k // tk

    def body(*refs):
        if res is None:
            a_ref, b_ref, o_ref = refs
            r_ref = None
        else:
            a_ref, b_ref, r_ref, o_ref = refs
        kk = pl.program_id(2)
        p = _dot(a_ref[...].astype(BF16), b_ref[...].astype(BF16), dims)

        @pl.when(kk == 0)
        def _():
            o_ref[...] = p if r_ref is None else p + r_ref[...]

        @pl.when(kk > 0)
        def _():
            o_ref[...] += p

    a_spec = pl.BlockSpec((tk, tm), lambda i, j, kk: (kk, i)) if dims == "tn" else pl.BlockSpec((tm, tk), lambda i, j, kk: (i, kk))
    b_spec = pl.BlockSpec((tn, tk), lambda i, j, kk: (j, kk)) if dims == "nt" else pl.BlockSpec((tk, tn), lambda i, j, kk: (kk, j))
    o_spec = pl.BlockSpec((tm, tn), lambda i, j, kk: (i, j))
    in_specs = [a_spec, b_spec] + ([] if res is None else [o_spec])
    args = (a, b) + (() if res is None else (res,))
    return pl.pallas_call(
        body,
        grid=(m // tm, n // tn, nk),
        in_specs=in_specs,
        out_specs=o_spec,
        out_shape=jax.ShapeDtypeStruct((m, n), F32),
        compiler_params=_cparams(("parallel", "parallel", "arbitrary")),
        name=name,
    )(*args)


def _proj_w4(h, w4, name):
    (s, d), (ns, d2, c) = h.shape, w4.shape
    assert d == d2
    tm = _pick(s, (1024, 512, 256, 128))
    tn = _pick(c, (1024, 512, 256, 128))
    nbs = c // tn

    def body(a_ref, b_ref, o_ref):
        o_ref[...] = _dot(a_ref[...].astype(BF16), b_ref[...].astype(BF16), "nn")

    return pl.pallas_call(
        body,
        grid=(s // tm, ns * nbs),
        in_specs=[pl.BlockSpec((tm, d), lambda i, j: (i, 0)), pl.BlockSpec((None, d, tn), lambda i, j: (j // nbs, 0, j % nbs))],
        out_specs=pl.BlockSpec((tm, tn), lambda i, j: (i, j)),
        out_shape=jax.ShapeDtypeStruct((s, ns * c), F32),
        compiler_params=_cparams(("parallel", "parallel")),
        name=name,
    )(h, w4)


def _dh_w4(parts, w4, name):
    ns, d, c = w4.shape
    s = parts[0].shape[0]
    assert len(parts) == ns and all(p.shape == (s, c) for p in parts)
    tm = _pick(s, (1024, 512, 256, 128))
    tk = _pick(c, (1024, 512, 256, 128))
    nkp = c // tk

    def body(*refs):
        a_refs, b_ref, o_ref = refs[:ns], refs[ns], refs[ns + 1]
        kk = pl.program_id(1)
        for p in range(ns):

            @pl.when(kk // nkp == p)
            def _(p=p):
                pv = _dot(a_refs[p][...].astype(BF16), b_ref[...].astype(BF16), "nt")

                @pl.when(kk == 0)
                def _():
                    o_ref[...] = pv

                @pl.when(kk > 0)
                def _():
                    o_ref[...] += pv

    def a_spec(p):
        return pl.BlockSpec((tm, tk), lambda i, kk: (i, jnp.clip(kk - p * nkp, 0, nkp - 1)))

    return pl.pallas_call(
        body,
        grid=(s // tm, ns * nkp),
        in_specs=[a_spec(p) for p in range(ns)] + [pl.BlockSpec((None, d, tk), lambda i, kk: (kk // nkp, 0, kk % nkp))],
        out_specs=pl.BlockSpec((tm, d), lambda i, kk: (i, 0)),
        out_shape=jax.ShapeDtypeStruct((s, d), F32),
        compiler_params=_cparams(("parallel", "arbitrary")),
        name=name,
    )(*parts, w4)


def _dw_w4(h, parts, name):
    s, d = h.shape
    ns = len(parts)
    c = parts[0].shape[1]
    tn = _pick(c, (1024, 512, 256, 128))
    tk = _pick(s, (1024, 512, 256, 128))
    nbp = c // tn

    def body(*refs):
        a_ref, b_refs, o_ref = refs[0], refs[1 : 1 + ns], refs[1 + ns]
        j, kk = pl.program_id(0), pl.program_id(1)
        for p in range(ns):

            @pl.when(j // nbp == p)
            def _(p=p):
                pv = _dot(a_ref[...].astype(BF16), b_refs[p][...].astype(BF16), "tn")

                @pl.when(kk == 0)
                def _():
                    o_ref[...] = pv

                @pl.when(kk > 0)
                def _():
                    o_ref[...] += pv

    def b_spec(p):
        return pl.BlockSpec((tk, tn), lambda j, kk: (kk, jnp.clip(j - p * nbp, 0, nbp - 1)))

    return pl.pallas_call(
        body,
        grid=(ns * nbp, s // tk),
        in_specs=[pl.BlockSpec((tk, d), lambda j, kk: (kk, 0))] + [b_spec(p) for p in range(ns)],
        out_specs=pl.BlockSpec((None, d, tn), lambda j, kk: (j // nbp, 0, j % nbp)),
        out_shape=jax.ShapeDtypeStruct((ns, d, c), F32),
        compiler_params=_cparams(("parallel", "arbitrary")),
        name=name,
    )(h, *parts)


def _rmsnorm_fwd(x, g, name, col_block=0):
    s = x.shape[0]
    w = g.shape[1]
    tr = _pick(s, (512, 256, 128))

    def body(x_ref, g_ref, h_ref):
        xv = x_ref[...]
        r = lax.rsqrt(jnp.mean(xv * xv, axis=-1, keepdims=True) + EPS)
        h_ref[...] = ((xv * r) * g_ref[...]).astype(BF16)

    return pl.pallas_call(
        body,
        grid=(s // tr,),
        in_specs=[pl.BlockSpec((tr, w), lambda i: (i, col_block)), pl.BlockSpec((1, w), lambda i: (0, 0))],
        out_specs=pl.BlockSpec((tr, w), lambda i: (i, 0)),
        out_shape=jax.ShapeDtypeStruct((s, w), BF16),
        compiler_params=_cparams(("parallel",)),
        name=name,
    )(x, g)


def _rmsnorm_bwd(x, g, dh, name, col_block=0, dres=None):
    s = x.shape[0]
    w = g.shape[1]
    tr = _pick(s, (512, 256, 128))

    def body(*refs):
        if dres is None:
            x_ref, g_ref, dh_ref, dx_ref, dxb_ref, dg_ref = refs
        else:
            x_ref, g_ref, dh_ref, dr_ref, dx_ref, dxb_ref, dg_ref = refs
        i = pl.program_id(0)
        xv = x_ref[...]
        r = lax.rsqrt(jnp.mean(xv * xv, axis=-1, keepdims=True) + EPS)
        xh = xv * r
        dhv = dh_ref[...]
        dyg = dhv * g_ref[...]
        dx = r * (dyg - xh * jnp.mean(dyg * xh, axis=-1, keepdims=True))
        if dres is not None:
            dx = dx + dr_ref[...]
        dx_ref[...] = dx
        dxb_ref[...] = dx.astype(BF16)
        part = jnp.sum(dhv * xh, axis=0, keepdims=True)

        @pl.when(i == 0)
        def _():
            dg_ref[...] = part

        @pl.when(i > 0)
        def _():
            dg_ref[...] += part

    row = pl.BlockSpec((tr, w), lambda i: (i, 0))
    in_specs = [pl.BlockSpec((tr, w), lambda i: (i, col_block)), pl.BlockSpec((1, w), lambda i: (0, 0)), row]
    args = [x, g, dh]
    if dres is not None:
        in_specs.append(row)
        args.append(dres)
    return pl.pallas_call(
        body,
        grid=(s // tr,),
        in_specs=in_specs,
        out_specs=[row, row, pl.BlockSpec((1, w), lambda i: (0, 0))],
        out_shape=[jax.ShapeDtypeStruct((s, w), F32), jax.ShapeDtypeStruct((s, w), BF16), jax.ShapeDtypeStruct((1, w), F32)],
        compiler_params=_cparams(("arbitrary",)),
        name=name,
    )(*args)


def _loss_head(x, g, target, name):
    s, d = x.shape
    tr = _pick(s, (512, 256, 128))

    def body(x_ref, g_ref, t_ref, dx_ref, dxb_ref, dg_ref, loss_ref):
        i = pl.program_id(0)
        xv = x_ref[...]
        gv = g_ref[...]
        r = lax.rsqrt(jnp.mean(xv * xv, axis=-1, keepdims=True) + EPS)
        xh = xv * r
        err = xh * gv - t_ref[...]
        lpart = 0.5 * jnp.sum(jnp.mean(err * err, axis=-1, keepdims=True), axis=0, keepdims=True)
        dy = err / d
        dyg = dy * gv
        dx = r * (dyg - xh * jnp.mean(dyg * xh, axis=-1, keepdims=True))
        dx_ref[...] = dx
        dxb_ref[...] = dx.astype(BF16)
        part = jnp.sum(dy * xh, axis=0, keepdims=True)
        lrow = jnp.broadcast_to(lpart, (1, LANES))

        @pl.when(i == 0)
        def _():
            dg_ref[...] = part
            loss_ref[...] = lrow

        @pl.when(i > 0)
        def _():
            dg_ref[...] += part
            loss_ref[...] += lrow

    row = pl.BlockSpec((tr, d), lambda i: (i, 0))
    vec = pl.BlockSpec((1, d), lambda i: (0, 0))
    return pl.pallas_call(
        body,
        grid=(s // tr,),
        in_specs=[row, vec, row],
        out_specs=[row, row, vec, pl.BlockSpec((1, LANES), lambda i: (0, 0))],
        out_shape=[
            jax.ShapeDtypeStruct((s, d), F32),
            jax.ShapeDtypeStruct((s, d), BF16),
            jax.ShapeDtypeStruct((1, d), F32),
            jax.ShapeDtypeStruct((1, LANES), F32),
        ],
        compiler_params=_cparams(("arbitrary",)),
        name=name,
    )(x, g, target)


def _sigmoid(x):
    return 1.0 / (1.0 + jnp.exp(-x))


def _gate_out(gate_ref, o):
    gt = gate_ref[...]
    return (o * (gt * _sigmoid(gt))).astype(BF16)


def _gate_grads(gate_ref, o_ref, dg_ref, dgate_ref):
    gt = gate_ref[...]
    sg = _sigmoid(gt)
    dgv = dg_ref[...]
    dgate_ref[...] = (dgv * o_ref[...] * (sg * (1.0 + gt * (1.0 - sg)))).astype(BF16)
    return dgv * (gt * sg)


def _iotas(tq, tk):
    return lax.broadcasted_iota(jnp.int32, (tq, tk), 0), lax.broadcasted_iota(jnp.int32, (tq, tk), 1)


def _softplus(s):
    return jnp.maximum(s, 0.0) + jnp.log(1.0 + jnp.exp(-jnp.abs(s)))


def _split2(v):
    hi = v.astype(BF16)
    lo = (v - hi.astype(F32)).astype(BF16)
    return hi, lo


def _cat2(v):
    return jnp.concatenate(_split2(v), axis=1)


def _tri2(keep):
    m = keep.astype(BF16)
    return jnp.concatenate([m, m], axis=0)


def _split3(v):
    a = v.astype(BF16)
    r1 = v - a.astype(F32)
    b = r1.astype(BF16)
    c = (r1 - b.astype(F32)).astype(BF16)
    return a, b, c


SB_TQ = 512
SB_TK = 128
SB_UNROLL = 4


def _sb_fwd(proj, n_heads, name, gather=None):
    s = proj.shape[0]
    d = HEAD_DIM
    t = min(SB_TQ, s)
    tk = min(SB_TK, s)
    r = t // tk
    un = SB_UNROLL if r % SB_UNROLL == 0 else 1
    nq = s // t
    scale = d**-0.5

    def body(*refs):
        if gather is None:
            q_ref, k_ref, v_ref, gate_ref, o_ref, lt_ref, g_ref, kb_ref, vb_ref = refs
        else:
            q_ref, k_ref, v_ref, gate_ref, w_ref, o_ref, lt_ref, g_ref, wall_ref, kb_ref, vb_ref, send_sems, recv_sems = refs
        i = pl.program_id(1)

        if gather is not None:
            hh = pl.program_id(0)
            send, forward, finish = _gather_steps(w_ref, wall_ref, send_sems, recv_sems)
            pl.when((hh == 0) & (i == 0))(send)
            pl.when((hh == n_heads // 2) & (i == 0))(forward)

        @pl.when(i == 0)
        def _():
            kb_ref[...] = k_ref[...].astype(BF16)
            vb_ref[...] = v_ref[...].astype(BF16)

        q = (q_ref[...] * scale).astype(BF16)
        rows, cols = _iotas(t, tk)
        trows, tcols = _iotas(tk, tk)
        tri = (trows > tcols).astype(BF16)

        def block(kb, cl, acc, diag):
            k0 = pl.multiple_of(kb * tk, tk)
            sc = _dot(q, kb_ref[pl.ds(k0, tk), :], "nt")
            sp = _softplus(sc)
            ls = -sp
            if diag is not None:
                strict = cols + diag * tk < rows
                ls = jnp.where(strict, ls, 0.0)
            hi, lo = _split2(ls)
            later = _dot(hi, tri, "nn") + _dot(lo, tri, "nn")
            w = jnp.exp((sc - sp) + later + cl)
            if diag is not None:
                w = jnp.where(strict, w, 0.0)
            acc = acc + _dot(w.astype(BF16), vb_ref[pl.ds(k0, tk), :], "nn")
            return cl + jnp.sum(ls, axis=1, keepdims=True), acc

        cl, acc = jnp.zeros((t, 1), F32), jnp.zeros((t, d), F32)
        for dd in reversed(range(r)):
            cl, acc = block(i * r + dd, cl, acc, dd)

        def loop(j, carry):
            for u in range(un):
                carry = block(i * r - 1 - (un * j + u), carry[0], carry[1], None)
            return carry

        cl, acc = lax.fori_loop(0, (i * r) // un, loop, (cl, acc))
        o_ref[...] = acc
        lt_ref[...] = cl
        g_ref[...] = _gate_out(gate_ref, acc)

        if gather is not None:
            pl.when((hh == n_heads - 1) & (i == nq - 1))(finish)

    h = n_heads
    qblk = pl.BlockSpec((t, d), lambda hh, i: (i, hh))
    in_specs = [
        qblk,
        pl.BlockSpec((s, d), lambda hh, i: (0, h + hh)),
        pl.BlockSpec((s, d), lambda hh, i: (0, 2 * h + hh)),
        pl.BlockSpec((t, d), lambda hh, i: (i, 3 * h + hh)),
    ]
    out_specs = [qblk, pl.BlockSpec((None, t, 1), lambda hh, i: (hh, i, 0)), qblk]
    out_shape = [jax.ShapeDtypeStruct((s, h * d), F32), jax.ShapeDtypeStruct((h, s, 1), F32), jax.ShapeDtypeStruct((s, h * d), BF16)]
    scratch = [pltpu.VMEM((s, d), BF16), pltpu.VMEM((s, d), BF16)]
    args = [proj, proj, proj, proj]
    if gather is not None:
        in_specs.append(ANY)
        out_specs.append(ANY)
        out_shape.append(jax.ShapeDtypeStruct((N_CHIPS,) + gather.shape, gather.dtype))
        scratch += GATHER_SEMS
        args.append(gather)
    return pl.pallas_call(
        body,
        grid=(h, nq),
        in_specs=in_specs,
        out_specs=out_specs,
        out_shape=out_shape,
        scratch_shapes=scratch,
        compiler_params=_cparams(("arbitrary", "arbitrary")),
        name=name,
    )(*args)


def _sb_bwd(proj, ltot, o, dg, n_heads, name):
    s = proj.shape[0]
    d = HEAD_DIM
    t = min(SB_TQ, s)
    tk = min(SB_TK, s)
    r = t // tk
    un = SB_UNROLL if r % SB_UNROLL == 0 else 1
    nq = s // t
    scale = d**-0.5

    def body(q_ref, k_ref, v_ref, gate_ref, lt_ref, o_ref, dg_ref, dq_ref, dko_ref, dvo_ref, dgate_ref, kb_ref, vb_ref, dk_ref, dv_ref):
        i = pl.program_id(1)

        @pl.when(i == 0)
        def _():
            kb_ref[...] = k_ref[...].astype(BF16)
            vb_ref[...] = v_ref[...].astype(BF16)
            dk_ref[...] = jnp.zeros_like(dk_ref)
            dv_ref[...] = jnp.zeros_like(dv_ref)

        dob = _gate_grads(gate_ref, o_ref, dg_ref, dgate_ref).astype(BF16)
        q = (q_ref[...] * scale).astype(BF16)
        ltv = lt_ref[...]
        rows, cols = _iotas(t, tk)
        trows, tcols = _iotas(tk, tk)
        upto = _tri2(trows <= tcols)
        before = _tri2(trows < tcols)

        def block(kb, cp, cc, dq, diag):
            k0 = pl.multiple_of(kb * tk, tk)
            kk = kb_ref[pl.ds(k0, tk), :]
            sc = _dot(q, kk, "nt")
            sp = _softplus(sc)
            ls = -sp
            if diag is not None:
                strict = cols + diag * tk < rows
                ls = jnp.where(strict, ls, 0.0)
            prefix = _dot(_cat2(ls), upto, "nn") + cp
            lsig = sc - sp
            w = jnp.exp(lsig + (ltv - prefix))
            if diag is not None:
                w = jnp.where(strict, w, 0.0)
            da = _dot(dob, vb_ref[pl.ds(k0, tk), :], "nt") * w
            csum = _dot(_cat2(da), before, "nn") + cc
            beta = jnp.exp(lsig)
            dz = da * (1.0 - beta) - beta * csum
            if diag is not None:
                dz = jnp.where(strict, dz, 0.0)
            dzb = dz.astype(BF16)
            dq = dq + _dot(dzb, kk, "nn")
            dk_ref[pl.ds(k0, tk), :] += _dot(dzb, q, "tn")
            dv_ref[pl.ds(k0, tk), :] += _dot(w.astype(BF16), dob, "tn")
            return cp + jnp.sum(ls, axis=1, keepdims=True), cc + jnp.sum(da, axis=1, keepdims=True), dq

        def loop(j, carry):
            for u in range(un):
                carry = block(un * j + u, carry[0], carry[1], carry[2], None)
            return carry

        z1 = jnp.zeros((t, 1), F32)
        cp, cc, dq = lax.fori_loop(0, (i * r) // un, loop, (z1, z1, jnp.zeros((t, d), F32)))
        for dd in range(r):
            cp, cc, dq = block(i * r + dd, cp, cc, dq, dd)
        dq_ref[...] = (dq * scale).astype(BF16)

        @pl.when(i == nq - 1)
        def _():
            dko_ref[...] = dk_ref[...].astype(BF16)
            dvo_ref[...] = dv_ref[...].astype(BF16)

    h = n_heads
    qblk = pl.BlockSpec((t, d), lambda hh, i: (i, hh))
    full = pl.BlockSpec((s, d), lambda hh, i: (0, hh))
    shp = jax.ShapeDtypeStruct((s, h * d), BF16)
    return pl.pallas_call(
        body,
        grid=(h, nq),
        in_specs=[
            qblk,
            pl.BlockSpec((s, d), lambda hh, i: (0, h + hh)),
            pl.BlockSpec((s, d), lambda hh, i: (0, 2 * h + hh)),
            pl.BlockSpec((t, d), lambda hh, i: (i, 3 * h + hh)),
            pl.BlockSpec((None, t, 1), lambda hh, i: (hh, i, 0)),
            qblk,
            qblk,
        ],
        out_specs=[qblk, full, full, qblk],
        out_shape=[shp, shp, shp, shp],
        scratch_shapes=[pltpu.VMEM((s, d), BF16), pltpu.VMEM((s, d), BF16), pltpu.VMEM((s, d), F32), pltpu.VMEM((s, d), F32)],
        compiler_params=_cparams(("arbitrary", "arbitrary")),
        name=name,
    )(proj, proj, proj, proj, ltot, o, dg)


SM_TQ = 512
SM_TK = 256
SM_UNROLL = 2


def _allowed(mode, rows, cols, q0, k0):
    r = rows + q0
    c = cols + k0
    if mode == "causal":
        return c <= r
    return (c >> CHUNK_SHIFT) <= (r >> CHUNK_SHIFT)


def _sm_fwd(q_arr, k_arr, v_arr, gate_arr, n_heads, dqk, q_blk, k_blk, v_blk, gate_blk, mode, name, crow=None, ccol=None):
    s = q_arr.shape[0]
    dv = HEAD_DIM
    t = min(SM_TQ, s)
    tk = min(SM_TK, s)
    r = t // tk
    un = SM_UNROLL if r % SM_UNROLL == 0 else 1
    bias = crow is not None
    scale = (HEAD_DIM if mode == "causal" else HEAD_DIM + MLA_ROPE) ** -0.5

    def body(*refs):
        if bias:
            q_ref, k_ref, v_ref, gate_ref, cr_ref, cc_ref, o_ref, lse_ref, g_ref, kb_ref, vb_ref = refs
        else:
            q_ref, k_ref, v_ref, gate_ref, o_ref, lse_ref, g_ref, kb_ref, vb_ref = refs
        i = pl.program_id(1)

        @pl.when(i == 0)
        def _():
            kb_ref[...] = k_ref[...].astype(BF16)
            vb_ref[...] = v_ref[...].astype(BF16)

        q = (q_ref[...] * scale).astype(BF16)
        q0 = i * t
        rows, cols = _iotas(t, tk)
        crv = cr_ref[...] if bias else None

        def block(kb, m, l, acc, masked):
            k0 = pl.multiple_of(kb * tk, tk)
            sc = _dot(q, kb_ref[pl.ds(k0, tk), :], "nt")
            if bias:
                sc = sc + crv - cc_ref[pl.ds(kb, 1), :]
            if masked:
                sc = jnp.where(_allowed(mode, rows, cols, q0, k0), sc, NEG)
            m_new = jnp.maximum(m, jnp.max(sc, axis=1, keepdims=True))
            alpha = jnp.exp(m - m_new)
            p = jnp.exp(sc - m_new)
            l = alpha * l + jnp.sum(p, axis=1, keepdims=True)
            acc = alpha * acc + _dot(p.astype(BF16), vb_ref[pl.ds(k0, tk), :], "nn")
            return m_new, l, acc

        def loop(j, carry):
            for u in range(un):
                carry = block(un * j + u, carry[0], carry[1], carry[2], False)
            return carry

        init = (jnp.full((t, 1), NEG, F32), jnp.zeros((t, 1), F32), jnp.zeros((t, dv), F32))
        m, l, acc = lax.fori_loop(0, (i * r) // un, loop, init)
        for dd in range(r):
            m, l, acc = block(i * r + dd, m, l, acc, True)
        ov = acc / l
        o_ref[...] = ov
        lse_ref[...] = m + jnp.log(l)
        g_ref[...] = _gate_out(gate_ref, ov)

    h = n_heads
    oblk = pl.BlockSpec((t, dv), lambda hh, i: (i, hh))
    in_specs = [
        pl.BlockSpec((t, dqk), lambda hh, i: (i, q_blk(hh))),
        pl.BlockSpec((s, dqk), lambda hh, i: (0, k_blk(hh))),
        pl.BlockSpec((s, dv), lambda hh, i: (0, v_blk(hh))),
        pl.BlockSpec((t, dv), lambda hh, i: (i, gate_blk(hh))),
    ]
    args = [q_arr, k_arr, v_arr, gate_arr]
    if bias:
        in_specs += [pl.BlockSpec((None, t, 1), lambda hh, i: (hh, i, 0)), pl.BlockSpec((None, s // tk, tk), lambda hh, i: (hh, 0, 0))]
        args += [crow, ccol]
    return pl.pallas_call(
        body,
        grid=(h, s // t),
        in_specs=in_specs,
        out_specs=[oblk, pl.BlockSpec((None, t, 1), lambda hh, i: (hh, i, 0)), oblk],
        out_shape=[jax.ShapeDtypeStruct((s, h * dv), F32), jax.ShapeDtypeStruct((h, s, 1), F32), jax.ShapeDtypeStruct((s, h * dv), BF16)],
        scratch_shapes=[pltpu.VMEM((s, dqk), BF16), pltpu.VMEM((s, dv), BF16)],
        compiler_params=_cparams(("arbitrary", "arbitrary")),
        name=name,
    )(*args)


def _sm_bwd(q_arr, k_arr, v_arr, gate_arr, o, dg, lse, n_heads, dqk, q_blk, k_blk, v_blk, gate_blk, mode, name, crow=None, ccol=None, exchange=None):
    s = q_arr.shape[0]
    dv = HEAD_DIM
    t = min(SM_TQ, s)
    tk = min(SM_TK, s)
    r = t // tk
    un = SM_UNROLL if r % SM_UNROLL == 0 else 1
    nq = s // t
    bias = crow is not None
    assert not (bias and exchange is not None)
    scale = (HEAD_DIM if mode == "causal" else HEAD_DIM + MLA_ROPE) ** -0.5

    def body(*refs):
        if bias:
            q_ref, k_ref, v_ref, gate_ref, o_ref, dg_ref, lse_ref, cr_ref, cc_ref, dq_ref, dk_ref, dv_ref, dgate_ref, dcc_ref, dcr_ref, kb_ref, vb_ref = refs
        elif exchange is not None:
            q_ref, k_ref, v_ref, gate_ref, o_ref, dg_ref, lse_ref, sp_ref, dq_ref, dk_ref, dv_ref, dgate_ref, slots_ref, kb_ref, vb_ref, send_sems, recv_sems = refs
        else:
            q_ref, k_ref, v_ref, gate_ref, o_ref, dg_ref, lse_ref, dq_ref, dk_ref, dv_ref, dgate_ref, kb_ref, vb_ref = refs
        i = pl.program_id(1)

        if exchange is not None:
            hh = pl.program_id(0)
            send, finish = _exchange_steps(sp_ref, slots_ref, send_sems, recv_sems)
            pl.when((hh == 0) & (i == 0))(send)

        @pl.when(i == 0)
        def _():
            kb_ref[...] = k_ref[...].astype(BF16)
            vb_ref[...] = v_ref[...].astype(BF16)
            dk_ref[...] = jnp.zeros_like(dk_ref)
            dv_ref[...] = jnp.zeros_like(dv_ref)
            if bias:
                dcc_ref[...] = jnp.zeros_like(dcc_ref)

        q = (q_ref[...] * scale).astype(BF16)
        dov = _gate_grads(gate_ref, o_ref, dg_ref, dgate_ref)
        dob = dov.astype(BF16)
        dsum = jnp.sum(dov * o_ref[...], axis=1, keepdims=True)
        lsev = lse_ref[...]
        q0 = i * t
        rows, cols = _iotas(t, tk)
        crv = cr_ref[...] if bias else None

        def block(kb, dq, dr, masked):
            k0 = pl.multiple_of(kb * tk, tk)
            kk = kb_ref[pl.ds(k0, tk), :]
            sc = _dot(q, kk, "nt")
            if bias:
                sc = sc + crv - cc_ref[pl.ds(kb, 1), :]
            if masked:
                sc = jnp.where(_allowed(mode, rows, cols, q0, k0), sc, NEG)
            p = jnp.exp(sc - lsev)
            dz = p * (_dot(dob, vb_ref[pl.ds(k0, tk), :], "nt") - dsum)
            dzb = dz.astype(BF16)
            dk_ref[pl.ds(k0, tk), :] += _dot(dzb, q, "tn")
            dv_ref[pl.ds(k0, tk), :] += _dot(p.astype(BF16), dob, "tn")
            if bias:
                dcc_ref[pl.ds(kb, 1), :] -= jnp.sum(dz, axis=0, keepdims=True)
                dr = dr + jnp.sum(dz, axis=1, keepdims=True)
            return dq + _dot(dzb, kk, "nn"), dr

        def loop(j, carry):
            for u in range(un):
                carry = block(un * j + u, carry[0], carry[1], False)
            return carry

        dq, dr = lax.fori_loop(0, (i * r) // un, loop, (jnp.zeros((t, dqk), F32), jnp.zeros((t, 1), F32)))
        for dd in range(r):
            dq, dr = block(i * r + dd, dq, dr, True)
        dq_ref[...] = dq * scale
        if bias:
            dcr_ref[...] = dr
        if exchange is not None:
            pl.when((hh == n_heads - 1) & (i == nq - 1))(finish)

    h = n_heads
    qblk = pl.BlockSpec((t, dqk), lambda hh, i: (i, q_blk(hh)))
    oblk = pl.BlockSpec((t, dv), lambda hh, i: (i, hh))
    vec = pl.BlockSpec((None, t, 1), lambda hh, i: (hh, i, 0))
    in_specs = [
        qblk,
        pl.BlockSpec((s, dqk), lambda hh, i: (0, k_blk(hh))),
        pl.BlockSpec((s, dv), lambda hh, i: (0, v_blk(hh))),
        pl.BlockSpec((t, dv), lambda hh, i: (i, gate_blk(hh))),
        oblk,
        oblk,
        vec,
    ]
    args = [q_arr, k_arr, v_arr, gate_arr, o, dg, lse]
    out_specs = [
        pl.BlockSpec((t, dqk), lambda hh, i: (i, hh)),
        pl.BlockSpec((s, dqk), lambda hh, i: (0, hh)),
        pl.BlockSpec((s, dv), lambda hh, i: (0, hh)),
        oblk,
    ]
    out_shape = [
        jax.ShapeDtypeStruct((s, h * dqk), F32),
        jax.ShapeDtypeStruct((s, h * dqk), F32),
        jax.ShapeDtypeStruct((s, h * dv), F32),
        jax.ShapeDtypeStruct((s, h * dv), BF16),
    ]
    if bias:
        ccs = pl.BlockSpec((None, s // tk, tk), lambda hh, i: (hh, 0, 0))
        in_specs += [vec, ccs]
        args += [crow, ccol]
        out_specs += [ccs, vec]
        out_shape += [jax.ShapeDtypeStruct((h, s // tk, tk), F32), jax.ShapeDtypeStruct((h, s, 1), F32)]
    scratch = [pltpu.VMEM((s, dqk), BF16), pltpu.VMEM((s, dv), BF16)]
    if exchange is not None:
        in_specs.append(ANY)
        args.append(exchange)
        out_specs.append(ANY)
        out_shape.append(jax.ShapeDtypeStruct(exchange.shape, exchange.dtype))
        scratch += EXCHANGE_SEMS
    return pl.pallas_call(
        body,
        grid=(h, nq),
        in_specs=in_specs,
        out_specs=out_specs,
        out_shape=out_shape,
        scratch_shapes=scratch,
        compiler_params=_cparams(("arbitrary", "arbitrary")),
        name=name,
    )(*args)


def _rope_tables(pos):
    half = MLA_ROPE // 2
    inv_freq = ROPE_BASE ** (-jnp.arange(0, MLA_ROPE, 2, dtype=F32) / MLA_ROPE)
    ang = pos.astype(F32)[:, None] * inv_freq
    cos, sin = jnp.cos(ang), jnp.sin(ang)
    z = lambda n: jnp.zeros((pos.shape[0], n), F32)
    tc = jnp.concatenate([cos, cos, z(LANES - 2 * half)], axis=1)
    ta = jnp.concatenate([-sin, z(LANES - half)], axis=1)
    tb = jnp.concatenate([z(half), sin, z(LANES - 2 * half)], axis=1)
    return tc, ta, tb


def _rot(v, tc, ta, tb, sign):
    half = MLA_ROPE // 2
    return v * tc + sign * (pltpu.roll(v, LANES - half, 1) * ta + pltpu.roll(v, half, 1) * tb)


def _mla_assemble(qpre, kv, proj1, kr_blk, tabs, n_heads, name):
    s = qpre.shape[0]
    tr = LANES
    wd = n_heads * MLA_QK

    def body(qp_ref, kv_ref, kr_ref, tc_ref, ta_ref, tb_ref, qc_ref, kc_ref):
        tc, ta, tb = tc_ref[...], ta_ref[...], tb_ref[...]
        kr = _rot(kr_ref[...], tc, ta, tb, 1.0)
        for hh in range(n_heads):
            lo, mid, hi = hh * MLA_QK, hh * MLA_QK + LANES, (hh + 1) * MLA_QK
            qc_ref[:, lo:mid] = qp_ref[:, lo:mid]
            qc_ref[:, mid:hi] = _rot(qp_ref[:, mid:hi], tc, ta, tb, 1.0)
            kc_ref[:, lo:mid] = kv_ref[:, lo:mid]
            kc_ref[:, mid:hi] = kr

    tab = pl.BlockSpec((tr, LANES), lambda i: (i, 0))
    wide = pl.BlockSpec((tr, wd), lambda i: (i, 0))
    shp = jax.ShapeDtypeStruct((s, wd), F32)
    return pl.pallas_call(
        body,
        grid=(s // tr,),
        in_specs=[wide, wide, pl.BlockSpec((tr, LANES), lambda i: (i, kr_blk)), tab, tab, tab],
        out_specs=[wide, wide],
        out_shape=[shp, shp],
        compiler_params=_cparams(("parallel",)),
        name=name,
    )(qpre, kv, proj1, *tabs)


def _mla_disassemble(dqcat, dkcat, dv, tabs, n_heads, name):
    s = dqcat.shape[0]
    tr = LANES
    wd = n_heads * MLA_QK

    def body(dq_ref, dk_ref, dv_ref, tc_ref, ta_ref, tb_ref, dqp_ref, dkv_ref, dkr_ref):
        tc, ta, tb = tc_ref[...], ta_ref[...], tb_ref[...]
        acc = jnp.zeros((tr, LANES), F32)
        for hh in range(n_heads):
            lo, mid, hi = hh * MLA_QK, hh * MLA_QK + LANES, (hh + 1) * MLA_QK
            dqp_ref[:, lo:mid] = dq_ref[:, lo:mid].astype(BF16)
            dqp_ref[:, mid:hi] = _rot(dq_ref[:, mid:hi], tc, ta, tb, -1.0).astype(BF16)
            dkv_ref[:, lo:mid] = dk_ref[:, lo:mid].astype(BF16)
            dkv_ref[:, mid:hi] = dv_ref[:, hh * LANES : (hh + 1) * LANES].astype(BF16)
            acc = acc + dk_ref[:, mid:hi]
        dkr_ref[...] = _rot(acc, tc, ta, tb, -1.0)

    tab = pl.BlockSpec((tr, LANES), lambda i: (i, 0))
    wide = pl.BlockSpec((tr, wd), lambda i: (i, 0))
    shp = jax.ShapeDtypeStruct((s, wd), BF16)
    return pl.pallas_call(
        body,
        grid=(s // tr,),
        in_specs=[wide, wide, pl.BlockSpec((tr, n_heads * LANES), lambda i: (i, 0)), tab, tab, tab],
        out_specs=[wide, wide, tab],
        out_shape=[shp, shp, jax.ShapeDtypeStruct((s, LANES), F32)],
        compiler_params=_cparams(("parallel",)),
        name=name,
    )(dqcat, dkcat, dv, *tabs)


def _forget_scan(proj2, f_blk, bias, name):
    s = proj2.shape[0]
    n = LANES

    def body(f_ref, b_ref, c_ref):
        rows, cols = _iotas(n, n)
        tri = (cols <= rows).astype(BF16)

        def step(j, carry):
            r0 = pl.multiple_of(j * n, n)
            f = f_ref[pl.ds(r0, n), :] + b_ref[...]
            lf = jnp.minimum(f, 0.0) - jnp.log1p(jnp.exp(-jnp.abs(f)))
            a, b, c = _split3(lf)
            cs = _dot(tri, a, "nn") + _dot(tri, b, "nn") + _dot(tri, c, "nn") + carry
            c_ref[pl.ds(r0, n), :] = cs
            return cs[n - 1 : n, :]

        lax.fori_loop(0, s // n, step, jnp.zeros((1, n), F32))

    return pl.pallas_call(
        body,
        grid=(1,),
        in_specs=[pl.BlockSpec((s, n), lambda i: (0, f_blk)), pl.BlockSpec((1, n), lambda i: (0, 0))],
        out_specs=pl.BlockSpec((s, n), lambda i: (0, 0)),
        out_shape=jax.ShapeDtypeStruct((s, n), F32),
        compiler_params=_cparams(("arbitrary",)),
        name=name,
    )(proj2, bias)


def _forget_scan_bwd(dc_col, dc_row, proj2, f_blk, bias, n_heads, name):
    s = proj2.shape[0]
    n = LANES
    nb = s // n

    def body(dcc_ref, dcr_ref, f_ref, b_ref, df_ref, db_ref):
        rows, cols = _iotas(n, n)
        tri = (cols >= rows).astype(BF16)
        live = cols < n_heads

        def step(j, carry):
            acc, dbv = carry
            r0 = pl.multiple_of((nb - 1 - j) * n, n)
            a, b, c = _split3(dcc_ref[pl.ds(r0, n), :] + dcr_ref[pl.ds(r0, n), :])
            dl = _dot(tri, a, "nn") + _dot(tri, b, "nn") + _dot(tri, c, "nn") + acc
            f = f_ref[pl.ds(r0, n), :] + b_ref[...]
            df = jnp.where(live, dl / (1.0 + jnp.exp(f)), 0.0)
            df_ref[pl.ds(r0, n), :] = df.astype(BF16)
            return dl[0:1, :], dbv + jnp.sum(df, axis=0, keepdims=True)

        z = jnp.zeros((1, n), F32)
        _, dbv = lax.fori_loop(0, nb, step, (z, z))
        db_ref[...] = dbv

    return pl.pallas_call(
        body,
        grid=(1,),
        in_specs=[
            pl.BlockSpec((s, n), lambda i: (0, 0)),
            pl.BlockSpec((s, n), lambda i: (0, 0)),
            pl.BlockSpec((s, n), lambda i: (0, f_blk)),
            pl.BlockSpec((1, n), lambda i: (0, 0)),
        ],
        out_specs=[pl.BlockSpec((s, n), lambda i: (0, 0)), pl.BlockSpec((1, n), lambda i: (0, 0))],
        out_shape=[jax.ShapeDtypeStruct((s, n), BF16), jax.ShapeDtypeStruct((1, n), F32)],
        compiler_params=_cparams(("arbitrary",)),
        name=name,
    )(dc_col, dc_row, proj2, bias)


def _adamw(w, g, m, v, name):
    r, c = w.shape
    tr = _pick(r, (128, 64, 32, 16, 8))
    c1 = 1.0 - ADAM_B1**ADAM_STEP
    c2 = 1.0 - ADAM_B2**ADAM_STEP

    def body(w_ref, g_ref, m_ref, v_ref, d_ref, mo_ref, vo_ref):
        gv = g_ref[...]
        mn = ADAM_B1 * m_ref[...] + (1.0 - ADAM_B1) * gv
        vn = ADAM_B2 * v_ref[...] + (1.0 - ADAM_B2) * (gv * gv)
        mo_ref[...] = mn
        vo_ref[...] = vn
        d_ref[...] = -ADAM_LR * ((mn / c1) / (jnp.sqrt(vn / c2) + ADAM_EPS) + ADAM_WD * w_ref[...])

    blk = pl.BlockSpec((tr, c), lambda i: (i, 0))
    shp = jax.ShapeDtypeStruct((r, c), F32)
    return pl.pallas_call(
        body,
        grid=(r // tr,),
        in_specs=[blk, blk, blk, blk],
        out_specs=[blk, blk, blk],
        out_shape=[shp, shp, shp],
        compiler_params=_cparams(("parallel",)),
        name=name,
    )(w, g, m, v)


def _mesh_pos():
    return lax.axis_index("x"), lax.axis_index("y"), lax.axis_index("c")


def _other_chips(x, y):
    return [(1 - x, y), (x, 1 - y), (1 - x, 1 - y)]


ANY = pl.BlockSpec(memory_space=pl.ANY)


GATHER_SEMS = [pltpu.SemaphoreType.DMA((6,)), pltpu.SemaphoreType.DMA((6,))]
EXCHANGE_SEMS = [pltpu.SemaphoreType.DMA((3,)), pltpu.SemaphoreType.DMA((3,))]


def _gather_steps(w_ref, out_ref, send_sems, recv_sems):
    half = w_ref.shape[0] // 2
    x, y, c = _mesh_pos()
    me = 2 * x + y
    chips = _other_chips(x, y)

    def region(chip, hc):
        return out_ref.at[chip, pl.ds(hc * half, half), :]

    def copy(k, src, dst, to):
        return pltpu.make_async_remote_copy(
            src_ref=src, dst_ref=dst, send_sem=send_sems.at[k], recv_sem=recv_sems.at[k], device_id=to, device_id_type=MESH
        )

    first = [copy(j, w_ref.at[pl.ds(c * half, half), :], region(me, c), (cx, cy, c)) for j, (cx, cy) in enumerate(chips)]
    passed = [copy(3 + j, region(2 * cx + cy, c), region(2 * cx + cy, c), (x, y, 1 - c)) for j, (cx, cy) in enumerate(chips)]

    def send():
        for cp in first:
            cp.start()

    def forward():
        for j, (cx, cy) in enumerate(chips):
            copy(j, region(2 * cx + cy, c), region(2 * cx + cy, c), (x, y, c)).wait_recv()
            passed[j].start()

    def finish():
        for j, (cx, cy) in enumerate(chips):
            copy(3 + j, region(2 * cx + cy, 1 - c), region(2 * cx + cy, 1 - c), (x, y, c)).wait_recv()
        for cp in first + passed:
            cp.wait_send()

    return send, forward, finish


def _gather_weights(wp, tag):
    rp, wd = wp.shape

    def body(w_ref, out_ref, send_sems, recv_sems):
        for step in _gather_steps(w_ref, out_ref, send_sems, recv_sems):
            step()

    return pl.pallas_call(
        body,
        in_specs=[ANY],
        out_specs=ANY,
        out_shape=jax.ShapeDtypeStruct((N_CHIPS, rp, wd), wp.dtype),
        scratch_shapes=GATHER_SEMS,
        name=f"gather_weights_{tag}",
    )(wp)


def _place_own(wall, wp, pos, tag):
    rp, wd = wp.shape

    def body(x_ref, y_ref, c_ref, wall_ref, w_ref, o_ref):
        o_ref[0] = w_ref[...]

    grid_spec = pltpu.PrefetchScalarGridSpec(
        num_scalar_prefetch=3,
        grid=(rp // PACK_TR,),
        in_specs=[ANY, pl.BlockSpec((PACK_TR, wd), lambda i, xr, yr, cr: (i, 0))],
        out_specs=pl.BlockSpec((1, PACK_TR, wd), lambda i, xr, yr, cr: (2 * xr[0] + yr[0], i, 0)),
    )
    return pl.pallas_call(
        body,
        grid_spec=grid_spec,
        out_shape=jax.ShapeDtypeStruct(wall.shape, wall.dtype),
        input_output_aliases={3: 0},
        compiler_params=_cparams(("parallel",)),
        name=f"place_own_shard_{tag}",
    )(*pos, wall, wp)


def _pair_exchange(g, tag):
    _, rp, wd = g.shape
    half = rp // 2

    def body(g_ref, out_ref, send_sem, recv_sem):
        x, y, c = _mesh_pos()
        cp = pltpu.make_async_remote_copy(
            src_ref=g_ref.at[:, pl.ds((1 - c) * half, half), :],
            dst_ref=out_ref,
            send_sem=send_sem,
            recv_sem=recv_sem,
            device_id=(x, y, 1 - c),
            device_id_type=MESH,
        )
        cp.start()
        cp.wait()

    return pl.pallas_call(
        body,
        in_specs=[ANY],
        out_specs=ANY,
        out_shape=jax.ShapeDtypeStruct((N_CHIPS, half, wd), g.dtype),
        scratch_shapes=[pltpu.SemaphoreType.DMA, pltpu.SemaphoreType.DMA],
        name=f"rs_pair_exchange_{tag}",
    )(g)


def _pair_add(g, recv, pos, tag):
    _, rp, wd = g.shape
    half = rp // 2
    nb = half // PACK_TR

    def body(x_ref, y_ref, c_ref, g_ref, r_ref, o_ref):
        o_ref[...] = (g_ref[...] + r_ref[...]).astype(BF16)

    blk = (1, PACK_TR, wd)
    grid_spec = pltpu.PrefetchScalarGridSpec(
        num_scalar_prefetch=3,
        grid=(N_CHIPS, nb),
        in_specs=[
            pl.BlockSpec(blk, lambda j, i, xr, yr, cr: (j, cr[0] * nb + i, 0)),
            pl.BlockSpec(blk, lambda j, i, xr, yr, cr: (j, i, 0)),
        ],
        out_specs=pl.BlockSpec(blk, lambda j, i, xr, yr, cr: (j, i, 0)),
    )
    return pl.pallas_call(
        body,
        grid_spec=grid_spec,
        out_shape=jax.ShapeDtypeStruct((N_CHIPS, half, wd), BF16),
        compiler_params=_cparams(("parallel", "parallel")),
        name=f"rs_pair_add_{tag}",
    )(*pos, g, recv)


def _exchange_steps(s_ref, out_ref, send_sems, recv_sems):
    x, y, c = _mesh_pos()
    me = 2 * x + y
    chips = _other_chips(x, y)

    def copy(j, src, dst, to):
        return pltpu.make_async_remote_copy(
            src_ref=src, dst_ref=dst, send_sem=send_sems.at[j], recv_sem=recv_sems.at[j], device_id=to, device_id_type=MESH
        )

    sends = [copy(j, s_ref.at[2 * cx + cy], out_ref.at[me], (cx, cy, c)) for j, (cx, cy) in enumerate(chips)]

    def send():
        for cp in sends:
            cp.start()

    def finish():
        for j, (cx, cy) in enumerate(chips):
            copy(j, s_ref.at[me], out_ref.at[2 * cx + cy], (x, y, c)).wait_recv()
        for cp in sends:
            cp.wait_send()

    return send, finish


def _chip_exchange(sp, tag):
    def body(s_ref, out_ref, send_sems, recv_sems):
        for step in _exchange_steps(s_ref, out_ref, send_sems, recv_sems):
            step()

    return pl.pallas_call(
        body,
        in_specs=[ANY],
        out_specs=ANY,
        out_shape=jax.ShapeDtypeStruct(sp.shape, sp.dtype),
        scratch_shapes=EXCHANGE_SEMS,
        name=f"rs_chip_exchange_{tag}",
    )(sp)


def _sum_slots(own, slots, pos, tag):
    _, rh, wd = slots.shape
    nb = rh // PACK_TR

    def body(x_ref, y_ref, c_ref, own_ref, a_ref, b_ref, d_ref, o_ref):
        f = lambda r: r[0].astype(F32)
        o_ref[...] = ((f(own_ref) + f(a_ref)) + f(b_ref)) + f(d_ref)

    blk = (1, PACK_TR, wd)

    def other(k):
        return pl.BlockSpec(blk, lambda i, xr, yr, cr: (k + (k >= 2 * xr[0] + yr[0]).astype(jnp.int32), i, 0))

    grid_spec = pltpu.PrefetchScalarGridSpec(
        num_scalar_prefetch=3,
        grid=(nb,),
        in_specs=[pl.BlockSpec(blk, lambda i, xr, yr, cr: (2 * xr[0] + yr[0], i, 0)), other(0), other(1), other(2)],
        out_specs=pl.BlockSpec((PACK_TR, wd), lambda i, xr, yr, cr: (cr[0] * nb + i, 0)),
    )
    return pl.pallas_call(
        body,
        grid_spec=grid_spec,
        out_shape=jax.ShapeDtypeStruct((2 * rh, wd), F32),
        compiler_params=_cparams(("parallel",)),
        name=f"rs_sum_slots_{tag}",
    )(*pos, own, slots, slots, slots)


def _pair_gather(t, tag):
    rh = t.shape[0] // 2

    def body(t_ref, out_ref, send_sem, recv_sem):
        x, y, c = _mesh_pos()
        cp = pltpu.make_async_remote_copy(
            src_ref=t_ref.at[pl.ds(c * rh, rh), :],
            dst_ref=out_ref.at[pl.ds(c * rh, rh), :],
            send_sem=send_sem,
            recv_sem=recv_sem,
            device_id=(x, y, 1 - c),
            device_id_type=MESH,
        )
        cp.start()
        cp.wait_send()
        pltpu.make_async_remote_copy(
            src_ref=t_ref.at[pl.ds((1 - c) * rh, rh), :],
            dst_ref=out_ref.at[pl.ds((1 - c) * rh, rh), :],
            send_sem=send_sem,
            recv_sem=recv_sem,
            device_id=(x, y, c),
            device_id_type=MESH,
        ).wait_recv()

    return pl.pallas_call(
        body,
        in_specs=[ANY],
        out_specs=ANY,
        out_shape=jax.ShapeDtypeStruct(t.shape, t.dtype),
        input_output_aliases={0: 0},
        scratch_shapes=[pltpu.SemaphoreType.DMA, pltpu.SemaphoreType.DMA],
        name=f"rs_pair_gather_{tag}",
    )(t)


def _allreduce_small(v):
    shape = v.shape
    n_dev = 8

    def body(v_ref, o_ref, slots, send_sems, recv_sems):
        x, y, c = _mesh_pos()
        me = 4 * x + 2 * y + c
        slots[me] = v_ref[...]
        sends = []
        for k in range(1, n_dev):
            fx, fy, fc = (k >> 2) & 1, (k >> 1) & 1, k & 1
            to = (x ^ fx, y ^ fy, c ^ fc)
            cp = pltpu.make_async_remote_copy(
                src_ref=v_ref,
                dst_ref=slots.at[me],
                send_sem=send_sems.at[k - 1],
                recv_sem=recv_sems.at[k - 1],
                device_id=to,
                device_id_type=MESH,
            )
            cp.start()
            sends.append(cp)
        for k in range(1, n_dev):
            fx, fy, fc = (k >> 2) & 1, (k >> 1) & 1, k & 1
            frm = 4 * (x ^ fx) + 2 * (y ^ fy) + (c ^ fc)
            pltpu.make_async_remote_copy(
                src_ref=v_ref,
                dst_ref=slots.at[frm],
                send_sem=send_sems.at[k - 1],
                recv_sem=recv_sems.at[k - 1],
                device_id=(x, y, c),
                device_id_type=MESH,
            ).wait_recv()
        for cp in sends:
            cp.wait_send()
        acc = slots[0]
        for k in range(1, n_dev):
            acc = acc + slots[k]
        o_ref[...] = acc

    vm = pl.BlockSpec(memory_space=pltpu.VMEM)
    return pl.pallas_call(
        body,
        in_specs=[vm],
        out_specs=vm,
        out_shape=jax.ShapeDtypeStruct(shape, F32),
        scratch_shapes=[pltpu.VMEM((n_dev,) + shape, F32), pltpu.SemaphoreType.DMA((n_dev - 1,)), pltpu.SemaphoreType.DMA((n_dev - 1,))],
        name="allreduce_small",
    )(v)


def _pack_layout(shard_shapes):
    offs, rows = [], []
    off = 0
    for r, c in shard_shapes:
        assert (r * c) % PACK_W == 0
        n = r * c // PACK_W
        offs.append(off)
        rows.append(n)
        off += -(-n // 16) * 16
    rp = -(-off // (2 * PACK_TR)) * (2 * PACK_TR)
    return offs, rows, rp


def _pack_rows(parts, offs, rows, rp, lead):
    ends = list(offs[1:]) + [rp]
    nolead = ((0, 0),) * len(lead)
    out = [jnp.pad(p, nolead + ((0, e - o - n), (0, 0))) for p, o, n, e in zip(parts, offs, rows, ends)]
    return jnp.concatenate(out, axis=len(lead))


def kernel(x, positions, ln0, w_in0, w_out0, ln1, w_in1, q_norm1, w_qb1, kv_norm1, w_kvb1, w_out1, ln2, w_in2, b_f2, w_out2, ln3, w_in3, w_out3, final_norm, loss_target, m_ln0, m_w_in0, m_w_out0, m_ln1, m_w_in1, m_q_norm1, m_w_qb1, m_kv_norm1, m_w_kvb1, m_w_out1, m_ln2, m_w_in2, m_b_f2, m_w_out2, m_ln3, m_w_in3, m_w_out3, m_final_norm, v_ln0, v_w_in0, v_w_out0, v_ln1, v_w_in1, v_q_norm1, v_w_qb1, v_kv_norm1, v_w_kvb1, v_w_out1, v_ln2, v_w_in2, v_b_f2, v_w_out2, v_ln3, v_w_in3, v_w_out3, v_final_norm):
    xs = x[0]
    s, d = xs.shape
    di = 4 * w_out0.shape[0]
    nh = di // HEAD_DIM
    idx = tuple(lax.axis_index(a).astype(jnp.int32).reshape(1) for a in ("x", "y", "c"))

    big = [w_in0, w_out0, w_in1, w_qb1, w_kvb1, w_out1, w_in2, w_out2, w_in3, w_out3]
    col_sharded = [True, False, True, True, True, False, True, False, True, False]
    shard_shapes = [w.shape for w in big]
    as_gathered = [True, False, False, False, False, False, False, False, True, False]
    n_first = 2

    def pack_weights(lo, hi):
        offs, rows, rp = _pack_layout(shard_shapes[lo:hi])
        return _pack_rows([w.astype(BF16).reshape(n, PACK_W) for w, n in zip(big[lo:hi], rows)], offs, rows, rp, ()), offs, rows

    def unpack_weights(wall, lo, hi, offs, rows):
        out = []
        for (r, c), o, n, cs, g4 in zip(shard_shapes[lo:hi], offs, rows, col_sharded[lo:hi], as_gathered[lo:hi]):
            slab = wall[:, o : o + n, :].reshape(N_CHIPS, r, c)
            if g4:
                out.append(slab)
            else:
                out.append(slab.transpose(1, 0, 2).reshape(r, N_CHIPS * c) if cs else slab.reshape(N_CHIPS * r, c))
        return out

    wp_a, offs_a, rows_a = pack_weights(0, n_first)
    wp_b, offs_b, rows_b = pack_weights(n_first, len(big))
    wall_a = _place_own(_gather_weights(wp_a, "l0"), wp_a, idx, "l0")
    f_in0, f_out0 = unpack_weights(wall_a, 0, n_first, offs_a, rows_a)

    row = lambda v: v.reshape(1, -1)

    def sb_layer_fwd(xin, ln, w_in, w_out, tag, gather=None):
        h = _rmsnorm_fwd(xin, row(ln), f"norm_fwd_{tag}")
        proj = _proj_w4(h, w_in, f"proj_in_{tag}")
        o, lt, g, *gathered = _sb_fwd(proj, nh, f"sb_fwd_{tag}", gather=gather)
        xout = _matmul(g, w_out, "nn", f"proj_out_{tag}", res=xin)
        return xout, (xin, h, proj, o, lt, g), gathered

    x1, sv0, (wall_b_raw,) = sb_layer_fwd(xs, ln0, f_in0, f_out0, "l0", gather=wp_b)
    wall_b = _place_own(wall_b_raw, wp_b, idx, "rest")
    f_in1, f_qb1, f_kvb1, f_out1, f_in2, f_out2, f_in3, f_out3 = unpack_weights(wall_b, n_first, len(big), offs_b, rows_b)

    i_kr = MLA_Q_RANK + MLA_KV_RANK + MLA_ROPE
    w1p = jnp.concatenate([f_in1[:, i_kr:], f_in1[:, :i_kr], jnp.zeros((d, LANES - MLA_ROPE), BF16)], axis=1)
    qlat_blk = di // MLA_Q_RANK
    kvlat_blk = (di + MLA_Q_RANK) // MLA_KV_RANK
    kr_blk = (di + MLA_Q_RANK + MLA_KV_RANK) // LANES
    qk_w = HEAD_DIM + MLA_ROPE
    wqbp = jnp.pad(f_qb1.reshape(MLA_Q_RANK, nh, qk_w), ((0, 0), (0, 0), (0, MLA_QK - qk_w))).reshape(MLA_Q_RANK, nh * MLA_QK)
    n2 = f_in2.shape[1]
    w2p = jnp.pad(f_in2, ((0, 0), (0, 4 * di + LANES - n2)))
    b2p = jnp.pad(b_f2, (0, LANES - nh)).reshape(1, LANES)

    tabs = _rope_tables(positions[0])

    def sb_layer_bwd(dxn, dxnb, saved, ln, w_in, w_out, tag):
        xin, h, proj, o, lt, g = saved
        dgf = _matmul(dxnb, w_out, "nt", f"dgate_in_{tag}")
        dw_out = _matmul(g, dxnb, "tn", f"dw_out_{tag}")
        dq, dk, dv, dgate = _sb_bwd(proj, lt, o, dgf, nh, f"sb_bwd_{tag}")
        dproj = [dq, dk, dv, dgate]
        dh = _dh_w4(dproj, w_in, f"dh_{tag}")
        dw_in = _dw_w4(h, dproj, f"dw_in_{tag}")
        dx, dxb, dln = _rmsnorm_bwd(xin, row(ln), dh, f"norm_bwd_{tag}", dres=dxn)
        return dx, dxb, dln, dw_in, dw_out

    h1 = _rmsnorm_fwd(x1, row(ln1), "norm_fwd_l1")
    proj1 = _matmul(h1, w1p, "nn", "proj_in_l1")
    qn = _rmsnorm_fwd(proj1, row(q_norm1), "qnorm_fwd_l1", col_block=qlat_blk)
    kvn = _rmsnorm_fwd(proj1, row(kv_norm1), "kvnorm_fwd_l1", col_block=kvlat_blk)
    qpre = _matmul(qn, wqbp, "nn", "q_up_l1")
    kv1 = _matmul(kvn, f_kvb1, "nn", "kv_up_l1")
    qcat, kcat = _mla_assemble(qpre, kv1, proj1, kr_blk, tabs, nh, "mla_assemble_l1")
    mla_blk = (lambda hh: hh, lambda hh: hh, lambda hh: 2 * hh + 1)
    gate1_blk = lambda hh: hh
    o1, lse1, g1 = _sm_fwd(qcat, kcat, kv1, proj1, nh, MLA_QK, *mla_blk, gate1_blk, "chunk", "mla_fwd_l1")
    x2 = _matmul(g1, f_out1, "nn", "proj_out_l1", res=x1)

    h2 = _rmsnorm_fwd(x2, row(ln2), "norm_fwd_l2")
    proj2 = _matmul(h2, w2p, "nn", "proj_in_l2")
    f_blk = 4 * di // LANES
    cum = _forget_scan(proj2, f_blk, b2p, "forget_scan_l2")
    cum_h = cum[:, :nh].T
    t_sm = min(SM_TK, s)
    crow = cum_h.reshape(nh, s, 1)
    ccol = cum_h.reshape(nh, s // t_sm, t_sm)
    fg_blk = (lambda hh: hh, lambda hh: nh + hh, lambda hh: 2 * nh + hh)
    gate2_blk = lambda hh: 3 * nh + hh
    o2, lse2, g2 = _sm_fwd(proj2, proj2, proj2, proj2, nh, HEAD_DIM, *fg_blk, gate2_blk, "causal", "forget_fwd_l2", crow=crow, ccol=ccol)
    x3 = _matmul(g2, f_out2, "nn", "proj_out_l2", res=x2)

    x4, sv3, _ = sb_layer_fwd(x3, ln3, f_in3, f_out3, "l3")

    dx, dxb, d_final, loss_part = _loss_head(x4, row(final_norm), loss_target[0], "loss_head")
    dx, dxb, d_ln3, dw_in3, dw_out3 = sb_layer_bwd(dx, dxb, sv3, ln3, f_in3, f_out3, "l3")

    dgf2 = _matmul(dxb, f_out2, "nt", "dgate_in_l2")
    dw_out2 = _matmul(g2, dxb, "tn", "dw_out_l2")
    dq2, dk2, dv2, dgate2, dcc2, dcr2 = _sm_bwd(proj2, proj2, proj2, proj2, o2, dgf2, lse2, nh, HEAD_DIM, *fg_blk, gate2_blk, "causal", "forget_bwd_l2", crow=crow, ccol=ccol)
    lanes_of = lambda a: jnp.pad(a.reshape(nh, s).T, ((0, 0), (0, LANES - nh)))
    df2, d_bf = _forget_scan_bwd(lanes_of(dcc2), lanes_of(dcr2), proj2, f_blk, b2p, nh, "forget_scan_bwd_l2")
    dproj2 = jnp.concatenate([dq2.astype(BF16), dk2.astype(BF16), dv2.astype(BF16), dgate2, df2], axis=1)
    dh2 = _matmul(dproj2, w2p, "nt", "dh_l2")
    dw_in2 = _matmul(h2, dproj2, "tn", "dw_in_l2")[:, :n2]
    dx, dxb, d_ln2 = _rmsnorm_bwd(x2, row(ln2), dh2, "norm_bwd_l2", dres=dx)

    n_late = 6

    def reduce_start(lo, hi, dws, tag):
        offs, rows, rp = _pack_layout(shard_shapes[lo:hi])
        parts = []
        for g, (r, c), n, cs, is4 in zip(dws, shard_shapes[lo:hi], rows, col_sharded[lo:hi], as_gathered[lo:hi]):
            g4 = g if is4 else (g.reshape(r, N_CHIPS, c).transpose(1, 0, 2) if cs else g.reshape(N_CHIPS, r, c))
            parts.append(g4.reshape(N_CHIPS, n, PACK_W))
        gp = _pack_rows(parts, offs, rows, rp, (N_CHIPS,))
        return _pair_add(gp, _pair_exchange(gp, tag), idx, tag), offs, rows

    def reduce_finish(pair, slots, lo, hi, offs, rows, tag):
        gred = _pair_gather(_sum_slots(pair, slots, idx, tag), tag)
        return [gred[o : o + n, :].reshape(r, c) for (r, c), o, n in zip(shard_shapes[lo:hi], offs, rows)]

    pair_b, offs_gb, rows_gb = reduce_start(n_late, len(big), [dw_in2, dw_out2, dw_in3, dw_out3], "l23")

    dgf1 = _matmul(dxb, f_out1, "nt", "dgate_in_l1")
    dw_out1 = _matmul(g1, dxb, "tn", "dw_out_l1")
    dqc, dkc, dv1, dgate1, slots_b = _sm_bwd(qcat, kcat, kv1, proj1, o1, dgf1, lse1, nh, MLA_QK, *mla_blk, gate1_blk, "chunk", "mla_bwd_l1", exchange=pair_b)
    dqpre, dkv1, dkr = _mla_disassemble(dqc, dkc, dv1, tabs, nh, "mla_disassemble_l1")
    dqn = _matmul(dqpre, wqbp, "nt", "dqn_l1")
    dw_qbp = _matmul(qn, dqpre, "tn", "dw_qb_l1")
    dw_qb1 = dw_qbp.reshape(MLA_Q_RANK, nh, MLA_QK)[:, :, :qk_w].reshape(MLA_Q_RANK, nh * qk_w)
    dkvn = _matmul(dkv1, f_kvb1, "nt", "dkvn_l1")
    dw_kvb1 = _matmul(kvn, dkv1, "tn", "dw_kvb_l1")
    _, dqlat_b, d_qnorm = _rmsnorm_bwd(proj1, row(q_norm1), dqn, "qnorm_bwd_l1", col_block=qlat_blk)
    _, dkvlat_b, d_kvnorm = _rmsnorm_bwd(proj1, row(kv_norm1), dkvn, "kvnorm_bwd_l1", col_block=kvlat_blk)
    dproj1 = jnp.concatenate([dgate1, dqlat_b, dkvlat_b, dkr.astype(BF16)], axis=1)
    dh1 = _matmul(dproj1, w1p, "nt", "dh_l1")
    dw1p = _matmul(h1, dproj1, "tn", "dw_in_l1")
    dw_in1 = jnp.concatenate([dw1p[:, di : di + i_kr], dw1p[:, :di]], axis=1)
    dx, dxb, d_ln1 = _rmsnorm_bwd(x1, row(ln1), dh1, "norm_bwd_l1", dres=dx)

    dx, dxb, d_ln0, dw_in0, dw_out0 = sb_layer_bwd(dx, dxb, sv0, ln0, f_in0, f_out0, "l0")
    grad_x = dx.reshape(x.shape)

    pair_a, offs_ga, rows_ga = reduce_start(0, n_late, [dw_in0, dw_out0, dw_in1, dw_qb1, dw_kvb1, dw_out1], "l01")
    slots_a = _chip_exchange(pair_a, "l01")
    big_grads = reduce_finish(pair_a, slots_a, 0, n_late, offs_ga, rows_ga, "l01")
    big_grads += reduce_finish(pair_b, slots_b, n_late, len(big), offs_gb, rows_gb, "l23")

    small = [ln0, ln1, q_norm1, kv_norm1, ln2, b_f2, ln3, final_norm]
    small_g = [d_ln0[0], d_ln1[0], d_qnorm[0], d_kvnorm[0], d_ln2[0], d_bf[0, :nh], d_ln3[0], d_final[0]]
    n_small = SMALL_SHAPE[0] * SMALL_SHAPE[1]
    used = sum(v.shape[0] for v in small) + 1
    assert used <= n_small

    def pack_small(vs, last):
        return jnp.concatenate(list(vs) + [last, jnp.zeros((n_small - used,), F32)]).reshape(SMALL_SHAPE)

    sm_sum = _allreduce_small(pack_small(small_g, loss_part[0, :1]))
    flat = sm_sum.reshape(-1)
    loss = flat[used - 1]

    big_m = [m_w_in0, m_w_out0, m_w_in1, m_w_qb1, m_w_kvb1, m_w_out1, m_w_in2, m_w_out2, m_w_in3, m_w_out3]
    big_v = [v_w_in0, v_w_out0, v_w_in1, v_w_qb1, v_w_kvb1, v_w_out1, v_w_in2, v_w_out2, v_w_in3, v_w_out3]
    big_names = ["w_in0", "w_out0", "w_in1", "w_qb1", "w_kvb1", "w_out1", "w_in2", "w_out2", "w_in3", "w_out3"]
    big_upd = [_adamw(w, g, m, v, f"adamw_{nm}") for w, g, m, v, nm in zip(big, big_grads, big_m, big_v, big_names)]

    small_m = [m_ln0, m_ln1, m_q_norm1, m_kv_norm1, m_ln2, m_b_f2, m_ln3, m_final_norm]
    small_v = [v_ln0, v_ln1, v_q_norm1, v_kv_norm1, v_ln2, v_b_f2, v_ln3, v_final_norm]
    one = jnp.ones((1,), F32)
    sd, smn, svn = _adamw(pack_small(small, one), sm_sum, pack_small(small_m, one), pack_small(small_v, one), "adamw_small")

    def unpack_small(p):
        out, at = [], 0
        fl = p.reshape(-1)
        for v in small:
            out.append(fl[at : at + v.shape[0]])
            at += v.shape[0]
        return out

    sg_l, sd_l, sm_l, sv_l = unpack_small(sm_sum), unpack_small(sd), unpack_small(smn), unpack_small(svn)

    order = ["ln0", "w_in0", "w_out0", "ln1", "w_in1", "q_norm1", "w_qb1", "kv_norm1", "w_kvb1", "w_out1", "ln2", "w_in2", "b_f2", "w_out2", "ln3", "w_in3", "w_out3", "final_norm"]
    small_names = ["ln0", "ln1", "q_norm1", "kv_norm1", "ln2", "b_f2", "ln3", "final_norm"]
    grads, deltas, new_m, new_v = {}, {}, {}, {}
    for nm, g, (dl, mn, vn) in zip(big_names, big_grads, big_upd):
        grads[nm], deltas[nm], new_m[nm], new_v[nm] = g, dl, mn, vn
    for nm, g, dl, mn, vn in zip(small_names, sg_l, sd_l, sm_l, sv_l):
        grads[nm], deltas[nm], new_m[nm], new_v[nm] = g, dl, mn, vn
    return (loss, grad_x, *[grads[n] for n in order], *[deltas[n] for n in order], *[new_m[n] for n in order], *[new_v[n] for n in order])
```

```python
import functools

import jax
import jax.numpy as jnp
from jax import lax
from jax.experimental import pallas as pl
from jax.experimental.pallas import tpu as pltpu

F32 = jnp.float32
BF16 = jnp.bfloat16
EPS = 1e-6
NEG = -1e30
HEAD_DIM = 128
CHUNK_SHIFT = 6
MLA_Q_RANK = 256
MLA_KV_RANK = 128
MLA_ROPE = 64
MLA_QK = 256
ROPE_BASE = 10000.0
ADAM_LR = 0.001
ADAM_B1 = 0.9
ADAM_B2 = 0.999
ADAM_EPS = 1e-08
ADAM_WD = 0.01
ADAM_STEP = 10
VMEM_LIMIT_BYTES = 56 * 2**20
LANES = 128
PACK_W = 1024
PACK_TR = 128
SMALL_SHAPE = (8, 768)
MESH = pl.DeviceIdType.MESH
N_CHIPS = 4


def _pick(n, cands):
    for c in cands:
        if n % c == 0:
            return c
    return n


def _cparams(sem):
    return pltpu.CompilerParams(dimension_semantics=sem, vmem_limit_bytes=VMEM_LIMIT_BYTES)


def _dot(a, b, dims):
    dn = {"nn": (((1,), (0,)), ((), ())), "nt": (((1,), (1,)), ((), ())), "tn": (((0,), (0,)), ((), ()))}[dims]
    return lax.dot_general(a, b, dn, preferred_element_type=F32)


def _matmul(a, b, dims, name, res=None):
    if dims == "nn":
        (m, k), (k2, n) = a.shape, b.shape
    elif dims == "nt":
        (m, k), (n, k2) = a.shape, b.shape
    else:
        (k, m), (k2, n) = a.shape, b.shape
    assert k == k2, (a.shape, b.shape, dims)
    tm = _pick(m, (1024, 512, 256, 128))
    tn = _pick(n, (1024, 640, 512, 384, 256, 128))
    tk = _pick(k, (1024, 640, 512, 256, 128))
    nk = k // tk

    def body(*refs):
        if res is None:
            a_ref, b_ref, o_ref = refs
            r_ref = None
        else:
            a_ref, b_ref, r_ref, o_ref = refs
        kk = pl.program_id(2)
        p = _dot(a_ref[...].astype(BF16), b_ref[...].astype(BF16), dims)

        @pl.when(kk == 0)
        def _():
            o_ref[...] = p if r_ref is None else p + r_ref[...]

        @pl.when(kk > 0)
        def _():
            o_ref[...] += p

    a_spec = pl.BlockSpec((tk, tm), lambda i, j, kk: (kk, i)) if dims == "tn" else pl.BlockSpec((tm, tk), lambda i, j, kk: (i, kk))
    b_spec = pl.BlockSpec((tn, tk), lambda i, j, kk: (j, kk)) if dims == "nt" else pl.BlockSpec((tk, tn), lambda i, j, kk: (kk, j))
    o_spec = pl.BlockSpec((tm, tn), lambda i, j, kk: (i, j))
    in_specs = [a_spec, b_spec] + ([] if res is None else [o_spec])
    args = (a, b) + (() if res is None else (res,))
    return pl.pallas_call(
        body,
        grid=(m // tm, n // tn, nk),
        in_specs=in_specs,
        out_specs=o_spec,
        out_shape=jax.ShapeDtypeStruct((m, n), F32),
        compiler_params=_cparams(("parallel", "parallel", "arbitrary")),
        name=name,
    )(*args)


def _proj_w4(h, w4, name):
    (s, d), (ns, d2, c) = h.shape, w4.shape
    assert d == d2
    tm = _pick(s, (1024, 512, 256, 128))
    tn = _pick(c, (1024, 512, 256, 128))
    nbs = c // tn

    def body(a_ref, b_ref, o_ref):
        o_ref[...] = _dot(a_ref[...].astype(BF16), b_ref[...].astype(BF16), "nn")

    return pl.pallas_call(
        body,
        grid=(s // tm, ns * nbs),
        in_specs=[pl.BlockSpec((tm, d), lambda i, j: (i, 0)), pl.BlockSpec((None, d, tn), lambda i, j: (j // nbs, 0, j % nbs))],
        out_specs=pl.BlockSpec((tm, tn), lambda i, j: (i, j)),
        out_shape=jax.ShapeDtypeStruct((s, ns * c), F32),
        compiler_params=_cparams(("parallel", "parallel")),
        name=name,
    )(h, w4)


def _dh_w4(parts, w4, name):
    ns, d, c = w4.shape
    s = parts[0].shape[0]
    assert len(parts) == ns and all(p.shape == (s, c) for p in parts)
    tm = _pick(s, (1024, 512, 256, 128))
    tk = _pick(c, (1024, 512, 256, 128))
    nkp = c // tk

    def body(*refs):
        a_refs, b_ref, o_ref = refs[:ns], refs[ns], refs[ns + 1]
        kk = pl.program_id(1)
        for p in range(ns):

            @pl.when(kk // nkp == p)
            def _(p=p):
                pv = _dot(a_refs[p][...].astype(BF16), b_ref[...].astype(BF16), "nt")

                @pl.when(kk == 0)
                def _():
                    o_ref[...] = pv

                @pl.when(kk > 0)
                def _():
                    o_ref[...] += pv

    def a_spec(p):
        return pl.BlockSpec((tm, tk), lambda i, kk: (i, jnp.clip(kk - p * nkp, 0, nkp - 1)))

    return pl.pallas_call(
        body,
        grid=(s // tm, ns * nkp),
        in_specs=[a_spec(p) for p in range(ns)] + [pl.BlockSpec((None, d, tk), lambda i, kk: (kk // nkp, 0, kk % nkp))],
        out_specs=pl.BlockSpec((tm, d), lambda i, kk: (i, 0)),
        out_shape=jax.ShapeDtypeStruct((s, d), F32),
        compiler_params=_cparams(("parallel", "arbitrary")),
        name=name,
    )(*parts, w4)


def _dw_w4(h, parts, name):
    s, d = h.shape
    ns = len(parts)
    c = parts[0].shape[1]
    tn = _pick(c, (1024, 512, 256, 128))
    tk = _pick(s, (1024, 512, 256, 128))
    nbp = c // tn

    def body(*refs):
        a_ref, b_refs, o_ref = refs[0], refs[1 : 1 + ns], refs[1 + ns]
        j, kk = pl.program_id(0), pl.program_id(1)
        for p in range(ns):

            @pl.when(j // nbp == p)
            def _(p=p):
                pv = _dot(a_ref[...].astype(BF16), b_refs[p][...].astype(BF16), "tn")

                @pl.when(kk == 0)
                def _():
                    o_ref[...] = pv

                @pl.when(kk > 0)
                def _():
                    o_ref[...] += pv

    def b_spec(p):
        return pl.BlockSpec((tk, tn), lambda j, kk: (kk, jnp.clip(j - p * nbp, 0, nbp - 1)))

    return pl.pallas_call(
        body,
        grid=(ns * nbp, s // tk),
        in_specs=[pl.BlockSpec((tk, d), lambda j, kk: (kk, 0))] + [b_spec(p) for p in range(ns)],
        out_specs=pl.BlockSpec((None, d, tn), lambda j, kk: (j // nbp, 0, j % nbp)),
        out_shape=jax.ShapeDtypeStruct((ns, d, c), F32),
        compiler_params=_cparams(("parallel", "arbitrary")),
        name=name,
    )(h, *parts)


def _rmsnorm_fwd(x, g, name, col_block=0):
    s = x.shape[0]
    w = g.shape[1]
    tr = _pick(s, (512, 256, 128))

    def body(x_ref, g_ref, h_ref):
        xv = x_ref[...]
        r = lax.rsqrt(jnp.mean(xv * xv, axis=-1, keepdims=True) + EPS)
        h_ref[...] = ((xv * r) * g_ref[...]).astype(BF16)

    return pl.pallas_call(
        body,
        grid=(s // tr,),
        in_specs=[pl.BlockSpec((tr, w), lambda i: (i, col_block)), pl.BlockSpec((1, w), lambda i: (0, 0))],
        out_specs=pl.BlockSpec((tr, w), lambda i: (i, 0)),
        out_shape=jax.ShapeDtypeStruct((s, w), BF16),
        compiler_params=_cparams(("parallel",)),
        name=name,
    )(x, g)


def _rmsnorm_bwd(x, g, dh, name, col_block=0, dres=None):
    s = x.shape[0]
    w = g.shape[1]
    tr = _pick(s, (512, 256, 128))

    def body(*refs):
        if dres is None:
            x_ref, g_ref, dh_ref, dx_ref, dxb_ref, dg_ref = refs
        else:
            x_ref, g_ref, dh_ref, dr_ref, dx_ref, dxb_ref, dg_ref = refs
        i = pl.program_id(0)
        xv = x_ref[...]
        r = lax.rsqrt(jnp.mean(xv * xv, axis=-1, keepdims=True) + EPS)
        xh = xv * r
        dhv = dh_ref[...]
        dyg = dhv * g_ref[...]
        dx = r * (dyg - xh * jnp.mean(dyg * xh, axis=-1, keepdims=True))
        if dres is not None:
            dx = dx + dr_ref[...]
        dx_ref[...] = dx
        dxb_ref[...] = dx.astype(BF16)
        part = jnp.sum(dhv * xh, axis=0, keepdims=True)

        @pl.when(i == 0)
        def _():
            dg_ref[...] = part

        @pl.when(i > 0)
        def _():
            dg_ref[...] += part

    row = pl.BlockSpec((tr, w), lambda i: (i, 0))
    in_specs = [pl.BlockSpec((tr, w), lambda i: (i, col_block)), pl.BlockSpec((1, w), lambda i: (0, 0)), row]
    args = [x, g, dh]
    if dres is not None:
        in_specs.append(row)
        args.append(dres)
    return pl.pallas_call(
        body,
        grid=(s // tr,),
        in_specs=in_specs,
        out_specs=[row, row, pl.BlockSpec((1, w), lambda i: (0, 0))],
        out_shape=[jax.ShapeDtypeStruct((s, w), F32), jax.ShapeDtypeStruct((s, w), BF16), jax.ShapeDtypeStruct((1, w), F32)],
        compiler_params=_cparams(("arbitrary",)),
        name=name,
    )(*args)


def _loss_head(x, g, target, name):
    s, d = x.shape
    tr = _pick(s, (512, 256, 128))

    def body(x_ref, g_ref, t_ref, dx_ref, dxb_ref, dg_ref, loss_ref):
        i = pl.program_id(0)
        xv = x_ref[...]
        gv = g_ref[...]
        r = lax.rsqrt(jnp.mean(xv * xv, axis=-1, keepdims=True) + EPS)
        xh = xv * r
        err = xh * gv - t_ref[...]
        lpart = 0.5 * jnp.sum(jnp.mean(err * err, axis=-1, keepdims=True), axis=0, keepdims=True)
        dy = err / d
        dyg = dy * gv
        dx = r * (dyg - xh * jnp.mean(dyg * xh, axis=-1, keepdims=True))
        dx_ref[...] = dx
        dxb_ref[...] = dx.astype(BF16)
        part = jnp.sum(dy * xh, axis=0, keepdims=True)
        lrow = jnp.broadcast_to(lpart, (1, LANES))

        @pl.when(i == 0)
        def _():
            dg_ref[...] = part
            loss_ref[...] = lrow

        @pl.when(i > 0)
        def _():
            dg_ref[...] += part
            loss_ref[...] += lrow

    row = pl.BlockSpec((tr, d), lambda i: (i, 0))
    vec = pl.BlockSpec((1, d), lambda i: (0, 0))
    return pl.pallas_call(
        body,
        grid=(s // tr,),
        in_specs=[row, vec, row],
        out_specs=[row, row, vec, pl.BlockSpec((1, LANES), lambda i: (0, 0))],
        out_shape=[
            jax.ShapeDtypeStruct((s, d), F32),
            jax.ShapeDtypeStruct((s, d), BF16),
            jax.ShapeDtypeStruct((1, d), F32),
            jax.ShapeDtypeStruct((1, LANES), F32),
        ],
        compiler_params=_cparams(("arbitrary",)),
        name=name,
    )(x, g, target)


def _sigmoid(x):
    return 1.0 / (1.0 + jnp.exp(-x))


def _gate_out(gate_ref, o):
    gt = gate_ref[...]
    return (o * (gt * _sigmoid(gt))).astype(BF16)


def _gate_grads(gate_ref, o_ref, dg_ref, dgate_ref):
    gt = gate_ref[...]
    sg = _sigmoid(gt)
    dgv = dg_ref[...]
    dgate_ref[...] = (dgv * o_ref[...] * (sg * (1.0 + gt * (1.0 - sg)))).astype(BF16)
    return dgv * (gt * sg)


def _iotas(tq, tk):
    return lax.broadcasted_iota(jnp.int32, (tq, tk), 0), lax.broadcasted_iota(jnp.int32, (tq, tk), 1)


def _softplus(s):
    return jnp.maximum(s, 0.0) + jnp.log(1.0 + jnp.exp(-jnp.abs(s)))


def _split2(v):
    hi = v.astype(BF16)
    lo = (v - hi.astype(F32)).astype(BF16)
    return hi, lo


def _cat2(v):
    return jnp.concatenate(_split2(v), axis=1)


def _tri2(keep):
    m = keep.astype(BF16)
    return jnp.concatenate([m, m], axis=0)


def _split3(v):
    a = v.astype(BF16)
    r1 = v - a.astype(F32)
    b = r1.astype(BF16)
    c = (r1 - b.astype(F32)).astype(BF16)
    return a, b, c


SB_TQ = 1024
SB_TK = 128
SB_UNROLL = 8


def _sb_fwd(proj, n_heads, name, gather=None):
    s = proj.shape[0]
    d = HEAD_DIM
    t = min(SB_TQ, s)
    tk = min(SB_TK, s)
    r = t // tk
    un = SB_UNROLL if r % SB_UNROLL == 0 else 1
    nq = s // t
    scale = d**-0.5

    def body(*refs):
        if gather is None:
            q_ref, k_ref, v_ref, gate_ref, o_ref, lt_ref, g_ref, kb_ref, vb_ref = refs
        else:
            q_ref, k_ref, v_ref, gate_ref, w_ref, o_ref, lt_ref, g_ref, wall_ref, kb_ref, vb_ref, send_sems, recv_sems = refs
        i = pl.program_id(1)

        if gather is not None:
            hh = pl.program_id(0)
            send, forward, finish = _gather_steps(w_ref, wall_ref, send_sems, recv_sems)
            pl.when((hh == 0) & (i == 0))(send)
            pl.when((hh == n_heads // 2) & (i == 0))(forward)

        @pl.when(i == 0)
        def _():
            kb_ref[...] = k_ref[...].astype(BF16)
            vb_ref[...] = v_ref[...].astype(BF16)

        q = (q_ref[...] * scale).astype(BF16)
        rows, cols = _iotas(t, tk)
        trows, tcols = _iotas(tk, tk)
        tri = (trows > tcols).astype(BF16)

        def block(kb, cl, acc, diag):
            k0 = pl.multiple_of(kb * tk, tk)
            sc = _dot(q, kb_ref[pl.ds(k0, tk), :], "nt")
            sp = _softplus(sc)
            ls = -sp
            if diag is not None:
                strict = cols + diag * tk < rows
                ls = jnp.where(strict, ls, 0.0)
            hi, lo = _split2(ls)
            later = _dot(hi, tri, "nn") + _dot(lo, tri, "nn")
            w = jnp.exp((sc - sp) + later + cl)
            if diag is not None:
                w = jnp.where(strict, w, 0.0)
            acc = acc + _dot(w.astype(BF16), vb_ref[pl.ds(k0, tk), :], "nn")
            return cl + jnp.sum(ls, axis=1, keepdims=True), acc

        cl, acc = jnp.zeros((t, 1), F32), jnp.zeros((t, d), F32)
        for dd in reversed(range(r)):
            cl, acc = block(i * r + dd, cl, acc, dd)

        def loop(j, carry):
            for u in range(un):
                carry = block(i * r - 1 - (un * j + u), carry[0], carry[1], None)
            return carry

        cl, acc = lax.fori_loop(0, (i * r) // un, loop, (cl, acc))
        o_ref[...] = acc
        lt_ref[...] = cl
        g_ref[...] = _gate_out(gate_ref, acc)

        if gather is not None:
            pl.when((hh == n_heads - 1) & (i == nq - 1))(finish)

    h = n_heads
    qblk = pl.BlockSpec((t, d), lambda hh, i: (i, hh))
    in_specs = [
        qblk,
        pl.BlockSpec((s, d), lambda hh, i: (0, h + hh)),
        pl.BlockSpec((s, d), lambda hh, i: (0, 2 * h + hh)),
        pl.BlockSpec((t, d), lambda hh, i: (i, 3 * h + hh)),
    ]
    out_specs = [qblk, pl.BlockSpec((None, t, 1), lambda hh, i: (hh, i, 0)), qblk]
    out_shape = [jax.ShapeDtypeStruct((s, h * d), F32), jax.ShapeDtypeStruct((h, s, 1), F32), jax.ShapeDtypeStruct((s, h * d), BF16)]
    scratch = [pltpu.VMEM((s, d), BF16), pltpu.VMEM((s, d), BF16)]
    args = [proj, proj, proj, proj]
    if gather is not None:
        in_specs.append(ANY)
        out_specs.append(ANY)
        out_shape.append(jax.ShapeDtypeStruct((N_CHIPS,) + gather.shape, gather.dtype))
        scratch += GATHER_SEMS
        args.append(gather)
    return pl.pallas_call(
        body,
        grid=(h, nq),
        in_specs=in_specs,
        out_specs=out_specs,
        out_shape=out_shape,
        scratch_shapes=scratch,
        compiler_params=_cparams(("arbitrary", "arbitrary")),
        name=name,
    )(*args)


def _sb_bwd(proj, ltot, o, dg, n_heads, name):
    s = proj.shape[0]
    d = HEAD_DIM
    t = min(SB_TQ, s)
    tk = min(SB_TK, s)
    r = t // tk
    un = SB_UNROLL if r % SB_UNROLL == 0 else 1
    nq = s // t
    scale = d**-0.5

    def body(q_ref, k_ref, v_ref, gate_ref, lt_ref, o_ref, dg_ref, dq_ref, dko_ref, dvo_ref, dgate_ref, kb_ref, vb_ref, dk_ref, dv_ref):
        i = pl.program_id(1)

        @pl.when(i == 0)
        def _():
            kb_ref[...] = k_ref[...].astype(BF16)
            vb_ref[...] = v_ref[...].astype(BF16)
            dk_ref[...] = jnp.zeros_like(dk_ref)
            dv_ref[...] = jnp.zeros_like(dv_ref)

        dob = _gate_grads(gate_ref, o_ref, dg_ref, dgate_ref).astype(BF16)
        q = (q_ref[...] * scale).astype(BF16)
        ltv = lt_ref[...]
        rows, cols = _iotas(t, tk)
        trows, tcols = _iotas(tk, tk)
        upto = _tri2(trows <= tcols)
        before = _tri2(trows < tcols)

        def block(kb, cp, cc, dq, diag):
            k0 = pl.multiple_of(kb * tk, tk)
            kk = kb_ref[pl.ds(k0, tk), :]
            sc = _dot(q, kk, "nt")
            sp = _softplus(sc)
            ls = -sp
            if diag is not None:
                strict = cols + diag * tk < rows
                ls = jnp.where(strict, ls, 0.0)
            prefix = _dot(_cat2(ls), upto, "nn") + cp
            lsig = sc - sp
            w = jnp.exp(lsig + (ltv - prefix))
            if diag is not None:
                w = jnp.where(strict, w, 0.0)
            da = _dot(dob, vb_ref[pl.ds(k0, tk), :], "nt") * w
            csum = _dot(_cat2(da), before, "nn") + cc
            beta = jnp.exp(lsig)
            dz = da * (1.0 - beta) - beta * csum
            if diag is not None:
                dz = jnp.where(strict, dz, 0.0)
            dzb = dz.astype(BF16)
            dq = dq + _dot(dzb, kk, "nn")
            dk_ref[pl.ds(k0, tk), :] += _dot(dzb, q, "tn")
            dv_ref[pl.ds(k0, tk), :] += _dot(w.astype(BF16), dob, "tn")
            return cp + jnp.sum(ls, axis=1, keepdims=True), cc + jnp.sum(da, axis=1, keepdims=True), dq

        def loop(j, carry):
            for u in range(un):
                carry = block(un * j + u, carry[0], carry[1], carry[2], None)
            return carry

        z1 = jnp.zeros((t, 1), F32)
        cp, cc, dq = lax.fori_loop(0, (i * r) // un, loop, (z1, z1, jnp.zeros((t, d), F32)))
        for dd in range(r):
            cp, cc, dq = block(i * r + dd, cp, cc, dq, dd)
        dq_ref[...] = (dq * scale).astype(BF16)

        @pl.when(i == nq - 1)
        def _():
            dko_ref[...] = dk_ref[...].astype(BF16)
            dvo_ref[...] = dv_ref[...].astype(BF16)

    h = n_heads
    qblk = pl.BlockSpec((t, d), lambda hh, i: (i, hh))
    full = pl.BlockSpec((s, d), lambda hh, i: (0, hh))
    shp = jax.ShapeDtypeStruct((s, h * d), BF16)
    return pl.pallas_call(
        body,
        grid=(h, nq),
        in_specs=[
            qblk,
            pl.BlockSpec((s, d), lambda hh, i: (0, h + hh)),
            pl.BlockSpec((s, d), lambda hh, i: (0, 2 * h + hh)),
            pl.BlockSpec((t, d), lambda hh, i: (i, 3 * h + hh)),
            pl.BlockSpec((None, t, 1), lambda hh, i: (hh, i, 0)),
            qblk,
            qblk,
        ],
        out_specs=[qblk, full, full, qblk],
        out_shape=[shp, shp, shp, shp],
        scratch_shapes=[pltpu.VMEM((s, d), BF16), pltpu.VMEM((s, d), BF16), pltpu.VMEM((s, d), F32), pltpu.VMEM((s, d), F32)],
        compiler_params=_cparams(("arbitrary", "arbitrary")),
        name=name,
    )(proj, proj, proj, proj, ltot, o, dg)


SM_TQ = 1024
SM_TK = 256
SM_UNROLL = 4


def _allowed(mode, rows, cols, q0, k0):
    r = rows + q0
    c = cols + k0
    if mode == "causal":
        return c <= r
    return (c >> CHUNK_SHIFT) <= (r >> CHUNK_SHIFT)


def _sm_fwd(q_arr, k_arr, v_arr, gate_arr, n_heads, dqk, q_blk, k_blk, v_blk, gate_blk, mode, name, crow=None, ccol=None):
    s = q_arr.shape[0]
    dv = HEAD_DIM
    t = min(SM_TQ, s)
    tk = min(SM_TK, s)
    r = t // tk
    un = SM_UNROLL if r % SM_UNROLL == 0 else 1
    bias = crow is not None
    scale = (HEAD_DIM if mode == "causal" else HEAD_DIM + MLA_ROPE) ** -0.5

    def body(*refs):
        if bias:
            q_ref, k_ref, v_ref, gate_ref, cr_ref, cc_ref, o_ref, lse_ref, g_ref, kb_ref, vb_ref = refs
        else:
            q_ref, k_ref, v_ref, gate_ref, o_ref, lse_ref, g_ref, kb_ref, vb_ref = refs
        i = pl.program_id(1)

        @pl.when(i == 0)
        def _():
            kb_ref[...] = k_ref[...].astype(BF16)
            vb_ref[...] = v_ref[...].astype(BF16)

        q = (q_ref[...] * scale).astype(BF16)
        q0 = i * t
        rows, cols = _iotas(t, tk)
        crv = cr_ref[...] if bias else None

        def block(kb, m, l, acc, masked):
            k0 = pl.multiple_of(kb * tk, tk)
            sc = _dot(q, kb_ref[pl.ds(k0, tk), :], "nt")
            if bias:
                sc = sc + crv - cc_ref[pl.ds(kb, 1), :]
            if masked:
                sc = jnp.where(_allowed(mode, rows, cols, q0, k0), sc, NEG)
            m_new = jnp.maximum(m, jnp.max(sc, axis=1, keepdims=True))
            alpha = jnp.exp(m - m_new)
            p = jnp.exp(sc - m_new)
            l = alpha * l + jnp.sum(p, axis=1, keepdims=True)
            acc = alpha * acc + _dot(p.astype(BF16), vb_ref[pl.ds(k0, tk), :], "nn")
            return m_new, l, acc

        def loop(j, carry):
            for u in range(un):
                carry = block(un * j + u, carry[0], carry[1], carry[2], False)
            return carry

        init = (jnp.full((t, 1), NEG, F32), jnp.zeros((t, 1), F32), jnp.zeros((t, dv), F32))
        m, l, acc = lax.fori_loop(0, (i * r) // un, loop, init)
        for dd in range(r):
            m, l, acc = block(i * r + dd, m, l, acc, True)
        ov = acc / l
        o_ref[...] = ov
        lse_ref[...] = m + jnp.log(l)
        g_ref[...] = _gate_out(gate_ref, ov)

    h = n_heads
    oblk = pl.BlockSpec((t, dv), lambda hh, i: (i, hh))
    in_specs = [
        pl.BlockSpec((t, dqk), lambda hh, i: (i, q_blk(hh))),
        pl.BlockSpec((s, dqk), lambda hh, i: (0, k_blk(hh))),
        pl.BlockSpec((s, dv), lambda hh, i: (0, v_blk(hh))),
        pl.BlockSpec((t, dv), lambda hh, i: (i, gate_blk(hh))),
    ]
    args = [q_arr, k_arr, v_arr, gate_arr]
    if bias:
        in_specs += [pl.BlockSpec((None, t, 1), lambda hh, i: (hh, i, 0)), pl.BlockSpec((None, s // tk, tk), lambda hh, i: (hh, 0, 0))]
        args += [crow, ccol]
    return pl.pallas_call(
        body,
        grid=(h, s // t),
        in_specs=in_specs,
        out_specs=[oblk, pl.BlockSpec((None, t, 1), lambda hh, i: (hh, i, 0)), oblk],
        out_shape=[jax.ShapeDtypeStruct((s, h * dv), F32), jax.ShapeDtypeStruct((h, s, 1), F32), jax.ShapeDtypeStruct((s, h * dv), BF16)],
        scratch_shapes=[pltpu.VMEM((s, dqk), BF16), pltpu.VMEM((s, dv), BF16)],
        compiler_params=_cparams(("arbitrary", "arbitrary")),
        name=name,
    )(*args)


def _sm_bwd(q_arr, k_arr, v_arr, gate_arr, o, dg, lse, n_heads, dqk, q_blk, k_blk, v_blk, gate_blk, mode, name, crow=None, ccol=None, exchange=None):
    s = q_arr.shape[0]
    dv = HEAD_DIM
    t = min(SM_TQ, s)
    tk = min(SM_TK, s)
    r = t // tk
    un = SM_UNROLL if r % SM_UNROLL == 0 else 1
    nq = s // t
    bias = crow is not None
    assert not (bias and exchange is not None)
    scale = (HEAD_DIM if mode == "causal" else HEAD_DIM + MLA_ROPE) ** -0.5

    def body(*refs):
        if bias:
            q_ref, k_ref, v_ref, gate_ref, o_ref, dg_ref, lse_ref, cr_ref, cc_ref, dq_ref, dk_ref, dv_ref, dgate_ref, dcc_ref, dcr_ref, kb_ref, vb_ref = refs
        elif exchange is not None:
            q_ref, k_ref, v_ref, gate_ref, o_ref, dg_ref, lse_ref, sp_ref, dq_ref, dk_ref, dv_ref, dgate_ref, slots_ref, kb_ref, vb_ref, send_sems, recv_sems = refs
        else:
            q_ref, k_ref, v_ref, gate_ref, o_ref, dg_ref, lse_ref, dq_ref, dk_ref, dv_ref, dgate_ref, kb_ref, vb_ref = refs
        i = pl.program_id(1)

        if exchange is not None:
            hh = pl.program_id(0)
            send, finish = _exchange_steps(sp_ref, slots_ref, send_sems, recv_sems)
            pl.when((hh == 0) & (i == 0))(send)

        @pl.when(i == 0)
        def _():
            kb_ref[...] = k_ref[...].astype(BF16)
            vb_ref[...] = v_ref[...].astype(BF16)
            dk_ref[...] = jnp.zeros_like(dk_ref)
            dv_ref[...] = jnp.zeros_like(dv_ref)
            if bias:
                dcc_ref[...] = jnp.zeros_like(dcc_ref)

        q = (q_ref[...] * scale).astype(BF16)
        dov = _gate_grads(gate_ref, o_ref, dg_ref, dgate_ref)
        dob = dov.astype(BF16)
        dsum = jnp.sum(dov * o_ref[...], axis=1, keepdims=True)
        lsev = lse_ref[...]
        q0 = i * t
        rows, cols = _iotas(t, tk)
        crv = cr_ref[...] if bias else None

        def block(kb, dq, dr, masked):
            k0 = pl.multiple_of(kb * tk, tk)
            kk = kb_ref[pl.ds(k0, tk), :]
            sc = _dot(q, kk, "nt")
            if bias:
                sc = sc + crv - cc_ref[pl.ds(kb, 1), :]
            if masked:
                sc = jnp.where(_allowed(mode, rows, cols, q0, k0), sc, NEG)
            p = jnp.exp(sc - lsev)
            dz = p * (_dot(dob, vb_ref[pl.ds(k0, tk), :], "nt") - dsum)
            dzb = dz.astype(BF16)
            dk_ref[pl.ds(k0, tk), :] += _dot(dzb, q, "tn")
            dv_ref[pl.ds(k0, tk), :] += _dot(p.astype(BF16), dob, "tn")
            if bias:
                dcc_ref[pl.ds(kb, 1), :] -= jnp.sum(dz, axis=0, keepdims=True)
                dr = dr + jnp.sum(dz, axis=1, keepdims=True)
            return dq + _dot(dzb, kk, "nn"), dr

        def loop(j, carry):
            for u in range(un):
                carry = block(un * j + u, carry[0], carry[1], False)
            return carry

        dq, dr = lax.fori_loop(0, (i * r) // un, loop, (jnp.zeros((t, dqk), F32), jnp.zeros((t, 1), F32)))
        for dd in range(r):
            dq, dr = block(i * r + dd, dq, dr, True)
        dq_ref[...] = dq * scale
        if bias:
            dcr_ref[...] = dr
        if exchange is not None:
            pl.when((hh == n_heads - 1) & (i == nq - 1))(finish)

    h = n_heads
    qblk = pl.BlockSpec((t, dqk), lambda hh, i: (i, q_blk(hh)))
    oblk = pl.BlockSpec((t, dv), lambda hh, i: (i, hh))
    vec = pl.BlockSpec((None, t, 1), lambda hh, i: (hh, i, 0))
    in_specs = [
        qblk,
        pl.BlockSpec((s, dqk), lambda hh, i: (0, k_blk(hh))),
        pl.BlockSpec((s, dv), lambda hh, i: (0, v_blk(hh))),
        pl.BlockSpec((t, dv), lambda hh, i: (i, gate_blk(hh))),
        oblk,
        oblk,
        vec,
    ]
    args = [q_arr, k_arr, v_arr, gate_arr, o, dg, lse]
    out_specs = [
        pl.BlockSpec((t, dqk), lambda hh, i: (i, hh)),
        pl.BlockSpec((s, dqk), lambda hh, i: (0, hh)),
        pl.BlockSpec((s, dv), lambda hh, i: (0, hh)),
        oblk,
    ]
    out_shape = [
        jax.ShapeDtypeStruct((s, h * dqk), F32),
        jax.ShapeDtypeStruct((s, h * dqk), F32),
        jax.ShapeDtypeStruct((s, h * dv), F32),
        jax.ShapeDtypeStruct((s, h * dv), BF16),
    ]
    if bias:
        ccs = pl.BlockSpec((None, s // tk, tk), lambda hh, i: (hh, 0, 0))
        in_specs += [vec, ccs]
        args += [crow, ccol]
        out_specs += [ccs, vec]
        out_shape += [jax.ShapeDtypeStruct((h, s // tk, tk), F32), jax.ShapeDtypeStruct((h, s, 1), F32)]
    scratch = [pltpu.VMEM((s, dqk), BF16), pltpu.VMEM((s, dv), BF16)]
    if exchange is not None:
        in_specs.append(ANY)
        args.append(exchange)
        out_specs.append(ANY)
        out_shape.append(jax.ShapeDtypeStruct(exchange.shape, exchange.dtype))
        scratch += EXCHANGE_SEMS
    return pl.pallas_call(
        body,
        grid=(h, nq),
        in_specs=in_specs,
        out_specs=out_specs,
        out_shape=out_shape,
        scratch_shapes=scratch,
        compiler_params=_cparams(("arbitrary", "arbitrary")),
        name=name,
    )(*args)


def _rope_tables(pos):
    half = MLA_ROPE // 2
    inv_freq = ROPE_BASE ** (-jnp.arange(0, MLA_ROPE, 2, dtype=F32) / MLA_ROPE)
    ang = pos.astype(F32)[:, None] * inv_freq
    cos, sin = jnp.cos(ang), jnp.sin(ang)
    z = lambda n: jnp.zeros((pos.shape[0], n), F32)
    tc = jnp.concatenate([cos, cos, z(LANES - 2 * half)], axis=1)
    ta = jnp.concatenate([-sin, z(LANES - half)], axis=1)
    tb = jnp.concatenate([z(half), sin, z(LANES - 2 * half)], axis=1)
    return tc, ta, tb


def _rot(v, tc, ta, tb, sign):
    half = MLA_ROPE // 2
    return v * tc + sign * (pltpu.roll(v, LANES - half, 1) * ta + pltpu.roll(v, half, 1) * tb)


def _mla_assemble(qpre, kv, proj1, kr_blk, tabs, n_heads, name):
    s = qpre.shape[0]
    tr = LANES
    wd = n_heads * MLA_QK

    def body(qp_ref, kv_ref, kr_ref, tc_ref, ta_ref, tb_ref, qc_ref, kc_ref):
        tc, ta, tb = tc_ref[...], ta_ref[...], tb_ref[...]
        kr = _rot(kr_ref[...], tc, ta, tb, 1.0)
        for hh in range(n_heads):
            lo, mid, hi = hh * MLA_QK, hh * MLA_QK + LANES, (hh + 1) * MLA_QK
            qc_ref[:, lo:mid] = qp_ref[:, lo:mid]
            qc_ref[:, mid:hi] = _rot(qp_ref[:, mid:hi], tc, ta, tb, 1.0)
            kc_ref[:, lo:mid] = kv_ref[:, lo:mid]
            kc_ref[:, mid:hi] = kr

    tab = pl.BlockSpec((tr, LANES), lambda i: (i, 0))
    wide = pl.BlockSpec((tr, wd), lambda i: (i, 0))
    shp = jax.ShapeDtypeStruct((s, wd), F32)
    return pl.pallas_call(
        body,
        grid=(s // tr,),
        in_specs=[wide, wide, pl.BlockSpec((tr, LANES), lambda i: (i, kr_blk)), tab, tab, tab],
        out_specs=[wide, wide],
        out_shape=[shp, shp],
        compiler_params=_cparams(("parallel",)),
        name=name,
    )(qpre, kv, proj1, *tabs)


def _mla_disassemble(dqcat, dkcat, dv, tabs, n_heads, name):
    s = dqcat.shape[0]
    tr = LANES
    wd = n_heads * MLA_QK

    def body(dq_ref, dk_ref, dv_ref, tc_ref, ta_ref, tb_ref, dqp_ref, dkv_ref, dkr_ref):
        tc, ta, tb = tc_ref[...], ta_ref[...], tb_ref[...]
        acc = jnp.zeros((tr, LANES), F32)
        for hh in range(n_heads):
            lo, mid, hi = hh * MLA_QK, hh * MLA_QK + LANES, (hh + 1) * MLA_QK
            dqp_ref[:, lo:mid] = dq_ref[:, lo:mid].astype(BF16)
            dqp_ref[:, mid:hi] = _rot(dq_ref[:, mid:hi], tc, ta, tb, -1.0).astype(BF16)
            dkv_ref[:, lo:mid] = dk_ref[:, lo:mid].astype(BF16)
            dkv_ref[:, mid:hi] = dv_ref[:, hh * LANES : (hh + 1) * LANES].astype(BF16)
            acc = acc + dk_ref[:, mid:hi]
        dkr_ref[...] = _rot(acc, tc, ta, tb, -1.0)

    tab = pl.BlockSpec((tr, LANES), lambda i: (i, 0))
    wide = pl.BlockSpec((tr, wd), lambda i: (i, 0))
    shp = jax.ShapeDtypeStruct((s, wd), BF16)
    return pl.pallas_call(
        body,
        grid=(s // tr,),
        in_specs=[wide, wide, pl.BlockSpec((tr, n_heads * LANES), lambda i: (i, 0)), tab, tab, tab],
        out_specs=[wide, wide, tab],
        out_shape=[shp, shp, jax.ShapeDtypeStruct((s, LANES), F32)],
        compiler_params=_cparams(("parallel",)),
        name=name,
    )(dqcat, dkcat, dv, *tabs)


def _forget_scan(proj2, f_blk, bias, name):
    s = proj2.shape[0]
    n = LANES

    def body(f_ref, b_ref, c_ref):
        rows, cols = _iotas(n, n)
        tri = (cols <= rows).astype(BF16)

        def step(j, carry):
            r0 = pl.multiple_of(j * n, n)
            f = f_ref[pl.ds(r0, n), :] + b_ref[...]
            lf = jnp.minimum(f, 0.0) - jnp.log1p(jnp.exp(-jnp.abs(f)))
            a, b, c = _split3(lf)
            cs = _dot(tri, a, "nn") + _dot(tri, b, "nn") + _dot(tri, c, "nn") + carry
            c_ref[pl.ds(r0, n), :] = cs
            return cs[n - 1 : n, :]

        lax.fori_loop(0, s // n, step, jnp.zeros((1, n), F32))

    return pl.pallas_call(
        body,
        grid=(1,),
        in_specs=[pl.BlockSpec((s, n), lambda i: (0, f_blk)), pl.BlockSpec((1, n), lambda i: (0, 0))],
        out_specs=pl.BlockSpec((s, n), lambda i: (0, 0)),
        out_shape=jax.ShapeDtypeStruct((s, n), F32),
        compiler_params=_cparams(("arbitrary",)),
        name=name,
    )(proj2, bias)


def _forget_scan_bwd(dc_col, dc_row, proj2, f_blk, bias, n_heads, name):
    s = proj2.shape[0]
    n = LANES
    nb = s // n

    def body(dcc_ref, dcr_ref, f_ref, b_ref, df_ref, db_ref):
        rows, cols = _iotas(n, n)
        tri = (cols >= rows).astype(BF16)
        live = cols < n_heads

        def step(j, carry):
            acc, dbv = carry
            r0 = pl.multiple_of((nb - 1 - j) * n, n)
            a, b, c = _split3(dcc_ref[pl.ds(r0, n), :] + dcr_ref[pl.ds(r0, n), :])
            dl = _dot(tri, a, "nn") + _dot(tri, b, "nn") + _dot(tri, c, "nn") + acc
            f = f_ref[pl.ds(r0, n), :] + b_ref[...]
            df = jnp.where(live, dl / (1.0 + jnp.exp(f)), 0.0)
            df_ref[pl.ds(r0, n), :] = df.astype(BF16)
            return dl[0:1, :], dbv + jnp.sum(df, axis=0, keepdims=True)

        z = jnp.zeros((1, n), F32)
        _, dbv = lax.fori_loop(0, nb, step, (z, z))
        db_ref[...] = dbv

    return pl.pallas_call(
        body,
        grid=(1,),
        in_specs=[
            pl.BlockSpec((s, n), lambda i: (0, 0)),
            pl.BlockSpec((s, n), lambda i: (0, 0)),
            pl.BlockSpec((s, n), lambda i: (0, f_blk)),
            pl.BlockSpec((1, n), lambda i: (0, 0)),
        ],
        out_specs=[pl.BlockSpec((s, n), lambda i: (0, 0)), pl.BlockSpec((1, n), lambda i: (0, 0))],
        out_shape=[jax.ShapeDtypeStruct((s, n), BF16), jax.ShapeDtypeStruct((1, n), F32)],
        compiler_params=_cparams(("arbitrary",)),
        name=name,
    )(dc_col, dc_row, proj2, bias)


def _adamw(w, g, m, v, name):
    r, c = w.shape
    tr = _pick(r, (128, 64, 32, 16, 8))
    c1 = 1.0 - ADAM_B1**ADAM_STEP
    c2 = 1.0 - ADAM_B2**ADAM_STEP

    def body(w_ref, g_ref, m_ref, v_ref, d_ref, mo_ref, vo_ref):
        gv = g_ref[...]
        mn = ADAM_B1 * m_ref[...] + (1.0 - ADAM_B1) * gv
        vn = ADAM_B2 * v_ref[...] + (1.0 - ADAM_B2) * (gv * gv)
        mo_ref[...] = mn
        vo_ref[...] = vn
        d_ref[...] = -ADAM_LR * ((mn / c1) / (jnp.sqrt(vn / c2) + ADAM_EPS) + ADAM_WD * w_ref[...])

    blk = pl.BlockSpec((tr, c), lambda i: (i, 0))
    shp = jax.ShapeDtypeStruct((r, c), F32)
    return pl.pallas_call(
        body,
        grid=(r // tr,),
        in_specs=[blk, blk, blk, blk],
        out_specs=[blk, blk, blk],
        out_shape=[shp, shp, shp],
        compiler_params=_cparams(("parallel",)),
        name=name,
    )(w, g, m, v)


def _mesh_pos():
    return lax.axis_index("x"), lax.axis_index("y"), lax.axis_index("c")


def _other_chips(x, y):
    return [(1 - x, y), (x, 1 - y), (1 - x, 1 - y)]


ANY = pl.BlockSpec(memory_space=pl.ANY)


GATHER_SEMS = [pltpu.SemaphoreType.DMA((6,)), pltpu.SemaphoreType.DMA((6,))]
EXCHANGE_SEMS = [pltpu.SemaphoreType.DMA((3,)), pltpu.SemaphoreType.DMA((3,))]


def _gather_steps(w_ref, out_ref, send_sems, recv_sems):
    half = w_ref.shape[0] // 2
    x, y, c = _mesh_pos()
    me = 2 * x + y
    chips = _other_chips(x, y)

    def region(chip, hc):
        return out_ref.at[chip, pl.ds(hc * half, half), :]

    def copy(k, src, dst, to):
        return pltpu.make_async_remote_copy(
            src_ref=src, dst_ref=dst, send_sem=send_sems.at[k], recv_sem=recv_sems.at[k], device_id=to, device_id_type=MESH
        )

    first = [copy(j, w_ref.at[pl.ds(c * half, half), :], region(me, c), (cx, cy, c)) for j, (cx, cy) in enumerate(chips)]
    passed = [copy(3 + j, region(2 * cx + cy, c), region(2 * cx + cy, c), (x, y, 1 - c)) for j, (cx, cy) in enumerate(chips)]

    def send():
        for cp in first:
            cp.start()

    def forward():
        for j, (cx, cy) in enumerate(chips):
            copy(j, region(2 * cx + cy, c), region(2 * cx + cy, c), (x, y, c)).wait_recv()
            passed[j].start()

    def finish():
        for j, (cx, cy) in enumerate(chips):
            copy(3 + j, region(2 * cx + cy, 1 - c), region(2 * cx + cy, 1 - c), (x, y, c)).wait_recv()
        for cp in first + passed:
            cp.wait_send()

    return send, forward, finish


def _gather_weights(wp, tag):
    rp, wd = wp.shape

    def body(w_ref, out_ref, send_sems, recv_sems):
        for step in _gather_steps(w_ref, out_ref, send_sems, recv_sems):
            step()

    return pl.pallas_call(
        body,
        in_specs=[ANY],
        out_specs=ANY,
        out_shape=jax.ShapeDtypeStruct((N_CHIPS, rp, wd), wp.dtype),
        scratch_shapes=GATHER_SEMS,
        name=f"gather_weights_{tag}",
    )(wp)


def _place_own(wall, wp, pos, tag):
    rp, wd = wp.shape

    def body(x_ref, y_ref, c_ref, wall_ref, w_ref, o_ref):
        o_ref[0] = w_ref[...]

    grid_spec = pltpu.PrefetchScalarGridSpec(
        num_scalar_prefetch=3,
        grid=(rp // PACK_TR,),
        in_specs=[ANY, pl.BlockSpec((PACK_TR, wd), lambda i, xr, yr, cr: (i, 0))],
        out_specs=pl.BlockSpec((1, PACK_TR, wd), lambda i, xr, yr, cr: (2 * xr[0] + yr[0], i, 0)),
    )
    return pl.pallas_call(
        body,
        grid_spec=grid_spec,
        out_shape=jax.ShapeDtypeStruct(wall.shape, wall.dtype),
        input_output_aliases={3: 0},
        compiler_params=_cparams(("parallel",)),
        name=f"place_own_shard_{tag}",
    )(*pos, wall, wp)


def _pair_exchange(g, tag):
    _, rp, wd = g.shape
    half = rp // 2

    def body(g_ref, out_ref, send_sem, recv_sem):
        x, y, c = _mesh_pos()
        cp = pltpu.make_async_remote_copy(
            src_ref=g_ref.at[:, pl.ds((1 - c) * half, half), :],
            dst_ref=out_ref,
            send_sem=send_sem,
            recv_sem=recv_sem,
            device_id=(x, y, 1 - c),
            device_id_type=MESH,
        )
        cp.start()
        cp.wait()

    return pl.pallas_call(
        body,
        in_specs=[ANY],
        out_specs=ANY,
        out_shape=jax.ShapeDtypeStruct((N_CHIPS, half, wd), g.dtype),
        scratch_shapes=[pltpu.SemaphoreType.DMA, pltpu.SemaphoreType.DMA],
        name=f"rs_pair_exchange_{tag}",
    )(g)


def _pair_add(g, recv, pos, tag):
    _, rp, wd = g.shape
    half = rp // 2
    nb = half // PACK_TR

    def body(x_ref, y_ref, c_ref, g_ref, r_ref, o_ref):
        o_ref[...] = (g_ref[...] + r_ref[...]).astype(BF16)

    blk = (1, PACK_TR, wd)
    grid_spec = pltpu.PrefetchScalarGridSpec(
        num_scalar_prefetch=3,
        grid=(N_CHIPS, nb),
        in_specs=[
            pl.BlockSpec(blk, lambda j, i, xr, yr, cr: (j, cr[0] * nb + i, 0)),
            pl.BlockSpec(blk, lambda j, i, xr, yr, cr: (j, i, 0)),
        ],
        out_specs=pl.BlockSpec(blk, lambda j, i, xr, yr, cr: (j, i, 0)),
    )
    return pl.pallas_call(
        body,
        grid_spec=grid_spec,
        out_shape=jax.ShapeDtypeStruct((N_CHIPS, half, wd), BF16),
        compiler_params=_cparams(("parallel", "parallel")),
        name=f"rs_pair_add_{tag}",
    )(*pos, g, recv)


def _exchange_steps(s_ref, out_ref, send_sems, recv_sems):
    x, y, c = _mesh_pos()
    me = 2 * x + y
    chips = _other_chips(x, y)

    def copy(j, src, dst, to):
        return pltpu.make_async_remote_copy(
            src_ref=src, dst_ref=dst, send_sem=send_sems.at[j], recv_sem=recv_sems.at[j], device_id=to, device_id_type=MESH
        )

    sends = [copy(j, s_ref.at[2 * cx + cy], out_ref.at[me], (cx, cy, c)) for j, (cx, cy) in enumerate(chips)]

    def send():
        for cp in sends:
            cp.start()

    def finish():
        for j, (cx, cy) in enumerate(chips):
            copy(j, s_ref.at[me], out_ref.at[2 * cx + cy], (x, y, c)).wait_recv()
        for cp in sends:
            cp.wait_send()

    return send, finish


def _chip_exchange(sp, tag):
    def body(s_ref, out_ref, send_sems, recv_sems):
        for step in _exchange_steps(s_ref, out_ref, send_sems, recv_sems):
            step()

    return pl.pallas_call(
        body,
        in_specs=[ANY],
        out_specs=ANY,
        out_shape=jax.ShapeDtypeStruct(sp.shape, sp.dtype),
        scratch_shapes=EXCHANGE_SEMS,
        name=f"rs_chip_exchange_{tag}",
    )(sp)


def _sum_slots(own, slots, pos, tag):
    _, rh, wd = slots.shape
    nb = rh // PACK_TR

    def body(x_ref, y_ref, c_ref, own_ref, a_ref, b_ref, d_ref, o_ref):
        f = lambda r: r[0].astype(F32)
        o_ref[...] = ((f(own_ref) + f(a_ref)) + f(b_ref)) + f(d_ref)

    blk = (1, PACK_TR, wd)

    def other(k):
        return pl.BlockSpec(blk, lambda i, xr, yr, cr: (k + (k >= 2 * xr[0] + yr[0]).astype(jnp.int32), i, 0))

    grid_spec = pltpu.PrefetchScalarGridSpec(
        num_scalar_prefetch=3,
        grid=(nb,),
        in_specs=[pl.BlockSpec(blk, lambda i, xr, yr, cr: (2 * xr[0] + yr[0], i, 0)), other(0), other(1), other(2)],
        out_specs=pl.BlockSpec((PACK_TR, wd), lambda i, xr, yr, cr: (cr[0] * nb + i, 0)),
    )
    return pl.pallas_call(
        body,
        grid_spec=grid_spec,
        out_shape=jax.ShapeDtypeStruct((2 * rh, wd), F32),
        compiler_params=_cparams(("parallel",)),
        name=f"rs_sum_slots_{tag}",
    )(*pos, own, slots, slots, slots)


def _pair_gather(t, tag):
    rh = t.shape[0] // 2

    def body(t_ref, out_ref, send_sem, recv_sem):
        x, y, c = _mesh_pos()
        cp = pltpu.make_async_remote_copy(
            src_ref=t_ref.at[pl.ds(c * rh, rh), :],
            dst_ref=out_ref.at[pl.ds(c * rh, rh), :],
            send_sem=send_sem,
            recv_sem=recv_sem,
            device_id=(x, y, 1 - c),
            device_id_type=MESH,
        )
        cp.start()
        cp.wait_send()
        pltpu.make_async_remote_copy(
            src_ref=t_ref.at[pl.ds((1 - c) * rh, rh), :],
            dst_ref=out_ref.at[pl.ds((1 - c) * rh, rh), :],
            send_sem=send_sem,
            recv_sem=recv_sem,
            device_id=(x, y, c),
            device_id_type=MESH,
        ).wait_recv()

    return pl.pallas_call(
        body,
        in_specs=[ANY],
        out_specs=ANY,
        out_shape=jax.ShapeDtypeStruct(t.shape, t.dtype),
        input_output_aliases={0: 0},
        scratch_shapes=[pltpu.SemaphoreType.DMA, pltpu.SemaphoreType.DMA],
        name=f"rs_pair_gather_{tag}",
    )(t)


def _allreduce_small(v):
    shape = v.shape
    n_dev = 8

    def body(v_ref, o_ref, slots, send_sems, recv_sems):
        x, y, c = _mesh_pos()
        me = 4 * x + 2 * y + c
        slots[me] = v_ref[...]
        sends = []
        for k in range(1, n_dev):
            fx, fy, fc = (k >> 2) & 1, (k >> 1) & 1, k & 1
            to = (x ^ fx, y ^ fy, c ^ fc)
            cp = pltpu.make_async_remote_copy(
                src_ref=v_ref,
                dst_ref=slots.at[me],
                send_sem=send_sems.at[k - 1],
                recv_sem=recv_sems.at[k - 1],
                device_id=to,
                device_id_type=MESH,
            )
            cp.start()
            sends.append(cp)
        for k in range(1, n_dev):
            fx, fy, fc = (k >> 2) & 1, (k >> 1) & 1, k & 1
            frm = 4 * (x ^ fx) + 2 * (y ^ fy) + (c ^ fc)
            pltpu.make_async_remote_copy(
                src_ref=v_ref,
                dst_ref=slots.at[frm],
                send_sem=send_sems.at[k - 1],
                recv_sem=recv_sems.at[k - 1],
                device_id=(x, y, c),
                device_id_type=MESH,
            ).wait_recv()
        for cp in sends:
            cp.wait_send()
        acc = slots[0]
        for k in range(1, n_dev):
            acc = acc + slots[k]
        o_ref[...] = acc

    vm = pl.BlockSpec(memory_space=pltpu.VMEM)
    return pl.pallas_call(
        body,
        in_specs=[vm],
        out_specs=vm,
        out_shape=jax.ShapeDtypeStruct(shape, F32),
        scratch_shapes=[pltpu.VMEM((n_dev,) + shape, F32), pltpu.SemaphoreType.DMA((n_dev - 1,)), pltpu.SemaphoreType.DMA((n_dev - 1,))],
        name="allreduce_small",
    )(v)


def _pack_layout(shard_shapes):
    offs, rows = [], []
    off = 0
    for r, c in shard_shapes:
        assert (r * c) % PACK_W == 0
        n = r * c // PACK_W
        offs.append(off)
        rows.append(n)
        off += -(-n // 16) * 16
    rp = -(-off // (2 * PACK_TR)) * (2 * PACK_TR)
    return offs, rows, rp


def _pack_rows(parts, offs, rows, rp, lead):
    ends = list(offs[1:]) + [rp]
    nolead = ((0, 0),) * len(lead)
    out = [jnp.pad(p, nolead + ((0, e - o - n), (0, 0))) for p, o, n, e in zip(parts, offs, rows, ends)]
    return jnp.concatenate(out, axis=len(lead))


def kernel(x, positions, ln0, w_in0, w_out0, ln1, w_in1, q_norm1, w_qb1, kv_norm1, w_kvb1, w_out1, ln2, w_in2, b_f2, w_out2, ln3, w_in3, w_out3, final_norm, loss_target, m_ln0, m_w_in0, m_w_out0, m_ln1, m_w_in1, m_q_norm1, m_w_qb1, m_kv_norm1, m_w_kvb1, m_w_out1, m_ln2, m_w_in2, m_b_f2, m_w_out2, m_ln3, m_w_in3, m_w_out3, m_final_norm, v_ln0, v_w_in0, v_w_out0, v_ln1, v_w_in1, v_q_norm1, v_w_qb1, v_kv_norm1, v_w_kvb1, v_w_out1, v_ln2, v_w_in2, v_b_f2, v_w_out2, v_ln3, v_w_in3, v_w_out3, v_final_norm):
    xs = x[0]
    s, d = xs.shape
    di = 4 * w_out0.shape[0]
    nh = di // HEAD_DIM
    idx = tuple(lax.axis_index(a).astype(jnp.int32).reshape(1) for a in ("x", "y", "c"))

    big = [w_in0, w_out0, w_in1, w_qb1, w_kvb1, w_out1, w_in2, w_out2, w_in3, w_out3]
    col_sharded = [True, False, True, True, True, False, True, False, True, False]
    shard_shapes = [w.shape for w in big]
    as_gathered = [True, False, False, False, False, False, False, False, True, False]
    n_first = 2

    def pack_weights(lo, hi):
        offs, rows, rp = _pack_layout(shard_shapes[lo:hi])
        return _pack_rows([w.astype(BF16).reshape(n, PACK_W) for w, n in zip(big[lo:hi], rows)], offs, rows, rp, ()), offs, rows

    def unpack_weights(wall, lo, hi, offs, rows):
        out = []
        for (r, c), o, n, cs, g4 in zip(shard_shapes[lo:hi], offs, rows, col_sharded[lo:hi], as_gathered[lo:hi]):
            slab = wall[:, o : o + n, :].reshape(N_CHIPS, r, c)
            if g4:
                out.append(slab)
            else:
                out.append(slab.transpose(1, 0, 2).reshape(r, N_CHIPS * c) if cs else slab.reshape(N_CHIPS * r, c))
        return out

    wp_a, offs_a, rows_a = pack_weights(0, n_first)
    wp_b, offs_b, rows_b = pack_weights(n_first, len(big))
    wall_a = _place_own(_gather_weights(wp_a, "l0"), wp_a, idx, "l0")
    f_in0, f_out0 = unpack_weights(wall_a, 0, n_first, offs_a, rows_a)

    row = lambda v: v.reshape(1, -1)

    def sb_layer_fwd(xin, ln, w_in, w_out, tag, gather=None):
        h = _rmsnorm_fwd(xin, row(ln), f"norm_fwd_{tag}")
        proj = _proj_w4(h, w_in, f"proj_in_{tag}")
        o, lt, g, *gathered = _sb_fwd(proj, nh, f"sb_fwd_{tag}", gather=gather)
        xout = _matmul(g, w_out, "nn", f"proj_out_{tag}", res=xin)
        return xout, (xin, h, proj, o, lt, g), gathered

    x1, sv0, (wall_b_raw,) = sb_layer_fwd(xs, ln0, f_in0, f_out0, "l0", gather=wp_b)
    wall_b = _place_own(wall_b_raw, wp_b, idx, "rest")
    f_in1, f_qb1, f_kvb1, f_out1, f_in2, f_out2, f_in3, f_out3 = unpack_weights(wall_b, n_first, len(big), offs_b, rows_b)

    i_kr = MLA_Q_RANK + MLA_KV_RANK + MLA_ROPE
    w1p = jnp.concatenate([f_in1[:, i_kr:], f_in1[:, :i_kr], jnp.zeros((d, LANES - MLA_ROPE), BF16)], axis=1)
    qlat_blk = di // MLA_Q_RANK
    kvlat_blk = (di + MLA_Q_RANK) // MLA_KV_RANK
    kr_blk = (di + MLA_Q_RANK + MLA_KV_RANK) // LANES
    qk_w = HEAD_DIM + MLA_ROPE
    wqbp = jnp.pad(f_qb1.reshape(MLA_Q_RANK, nh, qk_w), ((0, 0), (0, 0), (0, MLA_QK - qk_w))).reshape(MLA_Q_RANK, nh * MLA_QK)
    n2 = f_in2.shape[1]
    w2p = jnp.pad(f_in2, ((0, 0), (0, 4 * di + LANES - n2)))
    b2p = jnp.pad(b_f2, (0, LANES - nh)).reshape(1, LANES)

    tabs = _rope_tables(positions[0])

    def sb_layer_bwd(dxn, dxnb, saved, ln, w_in, w_out, tag):
        xin, h, proj, o, lt, g = saved
        dgf = _matmul(dxnb, w_out, "nt", f"dgate_in_{tag}")
        dw_out = _matmul(g, dxnb, "tn", f"dw_out_{tag}")
        dq, dk, dv, dgate = _sb_bwd(proj, lt, o, dgf, nh, f"sb_bwd_{tag}")
        dproj = [dq, dk, dv, dgate]
        dh = _dh_w4(dproj, w_in, f"dh_{tag}")
        dw_in = _dw_w4(h, dproj, f"dw_in_{tag}")
        dx, dxb, dln = _rmsnorm_bwd(xin, row(ln), dh, f"norm_bwd_{tag}", dres=dxn)
        return dx, dxb, dln, dw_in, dw_out

    h1 = _rmsnorm_fwd(x1, row(ln1), "norm_fwd_l1")
    proj1 = _matmul(h1, w1p, "nn", "proj_in_l1")
    qn = _rmsnorm_fwd(proj1, row(q_norm1), "qnorm_fwd_l1", col_block=qlat_blk)
    kvn = _rmsnorm_fwd(proj1, row(kv_norm1), "kvnorm_fwd_l1", col_block=kvlat_blk)
    qpre = _matmul(qn, wqbp, "nn", "q_up_l1")
    kv1 = _matmul(kvn, f_kvb1, "nn", "kv_up_l1")
    qcat, kcat = _mla_assemble(qpre, kv1, proj1, kr_blk, tabs, nh, "mla_assemble_l1")
    mla_blk = (lambda hh: hh, lambda hh: hh, lambda hh: 2 * hh + 1)
    gate1_blk = lambda hh: hh
    o1, lse1, g1 = _sm_fwd(qcat, kcat, kv1, proj1, nh, MLA_QK, *mla_blk, gate1_blk, "chunk", "mla_fwd_l1")
    x2 = _matmul(g1, f_out1, "nn", "proj_out_l1", res=x1)

    h2 = _rmsnorm_fwd(x2, row(ln2), "norm_fwd_l2")
    proj2 = _matmul(h2, w2p, "nn", "proj_in_l2")
    f_blk = 4 * di // LANES
    cum = _forget_scan(proj2, f_blk, b2p, "forget_scan_l2")
    cum_h = cum[:, :nh].T
    t_sm = min(SM_TK, s)
    crow = cum_h.reshape(nh, s, 1)
    ccol = cum_h.reshape(nh, s // t_sm, t_sm)
    fg_blk = (lambda hh: hh, lambda hh: nh + hh, lambda hh: 2 * nh + hh)
    gate2_blk = lambda hh: 3 * nh + hh
    o2, lse2, g2 = _sm_fwd(proj2, proj2, proj2, proj2, nh, HEAD_DIM, *fg_blk, gate2_blk, "causal", "forget_fwd_l2", crow=crow, ccol=ccol)
    x3 = _matmul(g2, f_out2, "nn", "proj_out_l2", res=x2)

    x4, sv3, _ = sb_layer_fwd(x3, ln3, f_in3, f_out3, "l3")

    dx, dxb, d_final, loss_part = _loss_head(x4, row(final_norm), loss_target[0], "loss_head")
    dx, dxb, d_ln3, dw_in3, dw_out3 = sb_layer_bwd(dx, dxb, sv3, ln3, f_in3, f_out3, "l3")

    dgf2 = _matmul(dxb, f_out2, "nt", "dgate_in_l2")
    dw_out2 = _matmul(g2, dxb, "tn", "dw_out_l2")
    dq2, dk2, dv2, dgate2, dcc2, dcr2 = _sm_bwd(proj2, proj2, proj2, proj2, o2, dgf2, lse2, nh, HEAD_DIM, *fg_blk, gate2_blk, "causal", "forget_bwd_l2", crow=crow, ccol=ccol)
    lanes_of = lambda a: jnp.pad(a.reshape(nh, s).T, ((0, 0), (0, LANES - nh)))
    df2, d_bf = _forget_scan_bwd(lanes_of(dcc2), lanes_of(dcr2), proj2, f_blk, b2p, nh, "forget_scan_bwd_l2")
    dproj2 = jnp.concatenate([dq2.astype(BF16), dk2.astype(BF16), dv2.astype(BF16), dgate2, df2], axis=1)
    dh2 = _matmul(dproj2, w2p, "nt", "dh_l2")
    dw_in2 = _matmul(h2, dproj2, "tn", "dw_in_l2")[:, :n2]
    dx, dxb, d_ln2 = _rmsnorm_bwd(x2, row(ln2), dh2, "norm_bwd_l2", dres=dx)

    n_late = 6

    def reduce_start(lo, hi, dws, tag):
        offs, rows, rp = _pack_layout(shard_shapes[lo:hi])
        parts = []
        for g, (r, c), n, cs, is4 in zip(dws, shard_shapes[lo:hi], rows, col_sharded[lo:hi], as_gathered[lo:hi]):
            g4 = g if is4 else (g.reshape(r, N_CHIPS, c).transpose(1, 0, 2) if cs else g.reshape(N_CHIPS, r, c))
            parts.append(g4.reshape(N_CHIPS, n, PACK_W))
        gp = _pack_rows(parts, offs, rows, rp, (N_CHIPS,))
        return _pair_add(gp, _pair_exchange(gp, tag), idx, tag), offs, rows

    def reduce_finish(pair, slots, lo, hi, offs, rows, tag):
        gred = _pair_gather(_sum_slots(pair, slots, idx, tag), tag)
        return [gred[o : o + n, :].reshape(r, c) for (r, c), o, n in zip(shard_shapes[lo:hi], offs, rows)]

    pair_b, offs_gb, rows_gb = reduce_start(n_late, len(big), [dw_in2, dw_out2, dw_in3, dw_out3], "l23")

    dgf1 = _matmul(dxb, f_out1, "nt", "dgate_in_l1")
    dw_out1 = _matmul(g1, dxb, "tn", "dw_out_l1")
    dqc, dkc, dv1, dgate1, slots_b = _sm_bwd(qcat, kcat, kv1, proj1, o1, dgf1, lse1, nh, MLA_QK, *mla_blk, gate1_blk, "chunk", "mla_bwd_l1", exchange=pair_b)
    dqpre, dkv1, dkr = _mla_disassemble(dqc, dkc, dv1, tabs, nh, "mla_disassemble_l1")
    dqn = _matmul(dqpre, wqbp, "nt", "dqn_l1")
    dw_qbp = _matmul(qn, dqpre, "tn", "dw_qb_l1")
    dw_qb1 = dw_qbp.reshape(MLA_Q_RANK, nh, MLA_QK)[:, :, :qk_w].reshape(MLA_Q_RANK, nh * qk_w)
    dkvn = _matmul(dkv1, f_kvb1, "nt", "dkvn_l1")
    dw_kvb1 = _matmul(kvn, dkv1, "tn", "dw_kvb_l1")
    _, dqlat_b, d_qnorm = _rmsnorm_bwd(proj1, row(q_norm1), dqn, "qnorm_bwd_l1", col_block=qlat_blk)
    _, dkvlat_b, d_kvnorm = _rmsnorm_bwd(proj1, row(kv_norm1), dkvn, "kvnorm_bwd_l1", col_block=kvlat_blk)
    dproj1 = jnp.concatenate([dgate1, dqlat_b, dkvlat_b, dkr.astype(BF16)], axis=1)
    dh1 = _matmul(dproj1, w1p, "nt", "dh_l1")
    dw1p = _matmul(h1, dproj1, "tn", "dw_in_l1")
    dw_in1 = jnp.concatenate([dw1p[:, di : di + i_kr], dw1p[:, :di]], axis=1)
    dx, dxb, d_ln1 = _rmsnorm_bwd(x1, row(ln1), dh1, "norm_bwd_l1", dres=dx)

    dx, dxb, d_ln0, dw_in0, dw_out0 = sb_layer_bwd(dx, dxb, sv0, ln0, f_in0, f_out0, "l0")
    grad_x = dx.reshape(x.shape)

    pair_a, offs_ga, rows_ga = reduce_start(0, n_late, [dw_in0, dw_out0, dw_in1, dw_qb1, dw_kvb1, dw_out1], "l01")
    slots_a = _chip_exchange(pair_a, "l01")
    big_grads = reduce_finish(pair_a, slots_a, 0, n_late, offs_ga, rows_ga, "l01")
    big_grads += reduce_finish(pair_b, slots_b, n_late, len(big), offs_gb, rows_gb, "l23")

    small = [ln0, ln1, q_norm1, kv_norm1, ln2, b_f2, ln3, final_norm]
    small_g = [d_ln0[0], d_ln1[0], d_qnorm[0], d_kvnorm[0], d_ln2[0], d_bf[0, :nh], d_ln3[0], d_final[0]]
    n_small = SMALL_SHAPE[0] * SMALL_SHAPE[1]
    used = sum(v.shape[0] for v in small) + 1
    assert used <= n_small

    def pack_small(vs, last):
        return jnp.concatenate(list(vs) + [last, jnp.zeros((n_small - used,), F32)]).reshape(SMALL_SHAPE)

    sm_sum = _allreduce_small(pack_small(small_g, loss_part[0, :1]))
    flat = sm_sum.reshape(-1)
    loss = flat[used - 1]

    big_m = [m_w_in0, m_w_out0, m_w_in1, m_w_qb1, m_w_kvb1, m_w_out1, m_w_in2, m_w_out2, m_w_in3, m_w_out3]
    big_v = [v_w_in0, v_w_out0, v_w_in1, v_w_qb1, v_w_kvb1, v_w_out1, v_w_in2, v_w_out2, v_w_in3, v_w_out3]
    big_names = ["w_in0", "w_out0", "w_in1", "w_qb1", "w_kvb1", "w_out1", "w_in2", "w_out2", "w_in3", "w_out3"]
    big_upd = [_adamw(w, g, m, v, f"adamw_{nm}") for w, g, m, v, nm in zip(big, big_grads, big_m, big_v, big_names)]

    small_m = [m_ln0, m_ln1, m_q_norm1, m_kv_norm1, m_ln2, m_b_f2, m_ln3, m_final_norm]
    small_v = [v_ln0, v_ln1, v_q_norm1, v_kv_norm1, v_ln2, v_b_f2, v_ln3, v_final_norm]
    one = jnp.ones((1,), F32)
    sd, smn, svn = _adamw(pack_small(small, one), sm_sum, pack_small(small_m, one), pack_small(small_v, one), "adamw_small")

    def unpack_small(p):
        out, at = [], 0
        fl = p.reshape(-1)
        for v in small:
            out.append(fl[at : at + v.shape[0]])
            at += v.shape[0]
        return out

    sg_l, sd_l, sm_l, sv_l = unpack_small(sm_sum), unpack_small(sd), unpack_small(smn), unpack_small(svn)

    order = ["ln0", "w_in0", "w_out0", "ln1", "w_in1", "q_norm1", "w_qb1", "kv_norm1", "w_kvb1", "w_out1", "ln2", "w_in2", "b_f2", "w_out2", "ln3", "w_in3", "w_out3", "final_norm"]
    small_names = ["ln0", "ln1", "q_norm1", "kv_norm1", "ln2", "b_f2", "ln3", "final_norm"]
    grads, deltas, new_m, new_v = {}, {}, {}, {}
    for nm, g, (dl, mn, vn) in zip(big_names, big_grads, big_upd):
        grads[nm], deltas[nm], new_m[nm], new_v[nm] = g, dl, mn, vn
    for nm, g, dl, mn, vn in zip(small_names, sg_l, sd_l, sm_l, sv_l):
        grads[nm], deltas[nm], new_m[nm], new_v[nm] = g, dl, mn, vn
    return (loss, grad_x, *[grads[n] for n in order], *[deltas[n] for n in order], *[new_m[n] for n in order], *[new_v[n] for n in order])
```

```python
import functools

import jax
import jax.numpy as jnp
from jax import lax
from jax.experimental import pallas as pl
from jax.experimental.pallas import tpu as pltpu

F32 = jnp.float32
BF16 = jnp.bfloat16
EPS = 1e-6
NEG = -1e30
HEAD_DIM = 128
CHUNK_SHIFT = 6
MLA_Q_RANK = 256
MLA_KV_RANK = 128
MLA_ROPE = 64
MLA_QK = 256
ROPE_BASE = 10000.0
ADAM_LR = 0.001
ADAM_B1 = 0.9
ADAM_B2 = 0.999
ADAM_EPS = 1e-08
ADAM_WD = 0.01
ADAM_STEP = 10
VMEM_LIMIT_BYTES = 56 * 2**20
LANES = 128
PACK_W = 1024
PACK_TR = 128
SMALL_SHAPE = (8, 768)
MESH = pl.DeviceIdType.MESH
N_CHIPS = 4


def _pick(n, cands):
    for c in cands:
        if n % c == 0:
            return c
    return n


def _cparams(sem):
    return pltpu.CompilerParams(dimension_semantics=sem, vmem_limit_bytes=VMEM_LIMIT_BYTES)


def _dot(a, b, dims):
    dn = {"nn": (((1,), (0,)), ((), ())), "nt": (((1,), (1,)), ((), ())), "tn": (((0,), (0,)), ((), ()))}[dims]
    return lax.dot_general(a, b, dn, preferred_element_type=F32)


def _matmul(a, b, dims, name, res=None):
    if dims == "nn":
        (m, k), (k2, n) = a.shape, b.shape
    elif dims == "nt":
        (m, k), (n, k2) = a.shape, b.shape
    else:
        (k, m), (k2, n) = a.shape, b.shape
    assert k == k2, (a.shape, b.shape, dims)
    tm = _pick(m, (1024, 512, 256, 128))
    tn = _pick(n, (1024, 1664, 640, 512, 384, 256, 128))
    tk = _pick(k, (1024, 1664, 640, 512, 256, 128))
    nk = k // tk

    def body(*refs):
        if res is None:
            a_ref, b_ref, o_ref = refs
            r_ref = None
        else:
            a_ref, b_ref, r_ref, o_ref = refs
        kk = pl.program_id(2)
        p = _dot(a_ref[...].astype(BF16), b_ref[...].astype(BF16), dims)

        @pl.when(kk == 0)
        def _():
            o_ref[...] = p if r_ref is None else p + r_ref[...]

        @pl.when(kk > 0)
        def _():
            o_ref[...] += p

    a_spec = pl.BlockSpec((tk, tm), lambda i, j, kk: (kk, i)) if dims == "tn" else pl.BlockSpec((tm, tk), lambda i, j, kk: (i, kk))
    b_spec = pl.BlockSpec((tn, tk), lambda i, j, kk: (j, kk)) if dims == "nt" else pl.BlockSpec((tk, tn), lambda i, j, kk: (kk, j))
    o_spec = pl.BlockSpec((tm, tn), lambda i, j, kk: (i, j))
    in_specs = [a_spec, b_spec] + ([] if res is None else [o_spec])
    args = (a, b) + (() if res is None else (res,))
    return pl.pallas_call(
        body,
        grid=(m // tm, n // tn, nk),
        in_specs=in_specs,
        out_specs=o_spec,
        out_shape=jax.ShapeDtypeStruct((m, n), F32),
        compiler_params=_cparams(("parallel", "parallel", "arbitrary")),
        name=name,
    )(*args)


def _proj_w4(h, w4, name):
    (s, d), (ns, d2, c) = h.shape, w4.shape
    assert d == d2
    tm = _pick(s, (1024, 512, 256, 128))
    tn = _pick(c, (1024, 512, 256, 128))
    nbs = c // tn

    def body(a_ref, b_ref, o_ref):
        o_ref[...] = _dot(a_ref[...].astype(BF16), b_ref[...].astype(BF16), "nn")

    return pl.pallas_call(
        body,
        grid=(s // tm, ns * nbs),
        in_specs=[pl.BlockSpec((tm, d), lambda i, j: (i, 0)), pl.BlockSpec((None, d, tn), lambda i, j: (j // nbs, 0, j % nbs))],
        out_specs=pl.BlockSpec((tm, tn), lambda i, j: (i, j)),
        out_shape=jax.ShapeDtypeStruct((s, ns * c), F32),
        compiler_params=_cparams(("parallel", "parallel")),
        name=name,
    )(h, w4)


def _dh_w4(parts, w4, name):
    ns, d, c = w4.shape
    s = parts[0].shape[0]
    assert len(parts) == ns and all(p.shape == (s, c) for p in parts)
    tm = _pick(s, (1024, 512, 256, 128))
    tk = _pick(c, (1024, 512, 256, 128))
    nkp = c // tk

    def body(*refs):
        a_refs, b_ref, o_ref = refs[:ns], refs[ns], refs[ns + 1]
        kk = pl.program_id(1)
        for p in range(ns):

            @pl.when(kk // nkp == p)
            def _(p=p):
                pv = _dot(a_refs[p][...].astype(BF16), b_ref[...].astype(BF16), "nt")

                @pl.when(kk == 0)
                def _():
                    o_ref[...] = pv

                @pl.when(kk > 0)
                def _():
                    o_ref[...] += pv

    def a_spec(p):
        return pl.BlockSpec((tm, tk), lambda i, kk: (i, jnp.clip(kk - p * nkp, 0, nkp - 1)))

    return pl.pallas_call(
        body,
        grid=(s // tm, ns * nkp),
        in_specs=[a_spec(p) for p in range(ns)] + [pl.BlockSpec((None, d, tk), lambda i, kk: (kk // nkp, 0, kk % nkp))],
        out_specs=pl.BlockSpec((tm, d), lambda i, kk: (i, 0)),
        out_shape=jax.ShapeDtypeStruct((s, d), F32),
        compiler_params=_cparams(("parallel", "arbitrary")),
        name=name,
    )(*parts, w4)


def _dw_w4(h, parts, name):
    s, d = h.shape
    ns = len(parts)
    c = parts[0].shape[1]
    tn = _pick(c, (1024, 512, 256, 128))
    tk = _pick(s, (1024, 512, 256, 128))
    nbp = c // tn

    def body(*refs):
        a_ref, b_refs, o_ref = refs[0], refs[1 : 1 + ns], refs[1 + ns]
        j, kk = pl.program_id(0), pl.program_id(1)
        for p in range(ns):

            @pl.when(j // nbp == p)
            def _(p=p):
                pv = _dot(a_ref[...].astype(BF16), b_refs[p][...].astype(BF16), "tn")

                @pl.when(kk == 0)
                def _():
                    o_ref[...] = pv

                @pl.when(kk > 0)
                def _():
                    o_ref[...] += pv

    def b_spec(p):
        return pl.BlockSpec((tk, tn), lambda j, kk: (kk, jnp.clip(j - p * nbp, 0, nbp - 1)))

    return pl.pallas_call(
        body,
        grid=(ns * nbp, s // tk),
        in_specs=[pl.BlockSpec((tk, d), lambda j, kk: (kk, 0))] + [b_spec(p) for p in range(ns)],
        out_specs=pl.BlockSpec((None, d, tn), lambda j, kk: (j // nbp, 0, j % nbp)),
        out_shape=jax.ShapeDtypeStruct((ns, d, c), F32),
        compiler_params=_cparams(("parallel", "arbitrary")),
        name=name,
    )(h, *parts)


def _rmsnorm_fwd(x, g, name, col_block=0):
    s = x.shape[0]
    w = g.shape[1]
    tr = _pick(s, (512, 256, 128))

    def body(x_ref, g_ref, h_ref):
        xv = x_ref[...]
        r = lax.rsqrt(jnp.mean(xv * xv, axis=-1, keepdims=True) + EPS)
        h_ref[...] = ((xv * r) * g_ref[...]).astype(BF16)

    return pl.pallas_call(
        body,
        grid=(s // tr,),
        in_specs=[pl.BlockSpec((tr, w), lambda i: (i, col_block)), pl.BlockSpec((1, w), lambda i: (0, 0))],
        out_specs=pl.BlockSpec((tr, w), lambda i: (i, 0)),
        out_shape=jax.ShapeDtypeStruct((s, w), BF16),
        compiler_params=_cparams(("parallel",)),
        name=name,
    )(x, g)


def _rmsnorm_bwd(x, g, dh, name, col_block=0, dres=None):
    s = x.shape[0]
    w = g.shape[1]
    tr = _pick(s, (512, 256, 128))

    def body(*refs):
        if dres is None:
            x_ref, g_ref, dh_ref, dx_ref, dxb_ref, dg_ref = refs
        else:
            x_ref, g_ref, dh_ref, dr_ref, dx_ref, dxb_ref, dg_ref = refs
        i = pl.program_id(0)
        xv = x_ref[...]
        r = lax.rsqrt(jnp.mean(xv * xv, axis=-1, keepdims=True) + EPS)
        xh = xv * r
        dhv = dh_ref[...]
        dyg = dhv * g_ref[...]
        dx = r * (dyg - xh * jnp.mean(dyg * xh, axis=-1, keepdims=True))
        if dres is not None:
            dx = dx + dr_ref[...]
        dx_ref[...] = dx
        dxb_ref[...] = dx.astype(BF16)
        part = jnp.sum(dhv * xh, axis=0, keepdims=True)

        @pl.when(i == 0)
        def _():
            dg_ref[...] = part

        @pl.when(i > 0)
        def _():
            dg_ref[...] += part

    row = pl.BlockSpec((tr, w), lambda i: (i, 0))
    in_specs = [pl.BlockSpec((tr, w), lambda i: (i, col_block)), pl.BlockSpec((1, w), lambda i: (0, 0)), row]
    args = [x, g, dh]
    if dres is not None:
        in_specs.append(row)
        args.append(dres)
    return pl.pallas_call(
        body,
        grid=(s // tr,),
        in_specs=in_specs,
        out_specs=[row, row, pl.BlockSpec((1, w), lambda i: (0, 0))],
        out_shape=[jax.ShapeDtypeStruct((s, w), F32), jax.ShapeDtypeStruct((s, w), BF16), jax.ShapeDtypeStruct((1, w), F32)],
        compiler_params=_cparams(("arbitrary",)),
        name=name,
    )(*args)


def _loss_head(x, g, target, name):
    s, d = x.shape
    tr = _pick(s, (512, 256, 128))

    def body(x_ref, g_ref, t_ref, dx_ref, dxb_ref, dg_ref, loss_ref):
        i = pl.program_id(0)
        xv = x_ref[...]
        gv = g_ref[...]
        r = lax.rsqrt(jnp.mean(xv * xv, axis=-1, keepdims=True) + EPS)
        xh = xv * r
        err = xh * gv - t_ref[...]
        lpart = 0.5 * jnp.sum(jnp.mean(err * err, axis=-1, keepdims=True), axis=0, keepdims=True)
        dy = err / d
        dyg = dy * gv
        dx = r * (dyg - xh * jnp.mean(dyg * xh, axis=-1, keepdims=True))
        dx_ref[...] = dx
        dxb_ref[...] = dx.astype(BF16)
        part = jnp.sum(dy * xh, axis=0, keepdims=True)
        lrow = jnp.broadcast_to(lpart, (1, LANES))

        @pl.when(i == 0)
        def _():
            dg_ref[...] = part
            loss_ref[...] = lrow

        @pl.when(i > 0)
        def _():
            dg_ref[...] += part
            loss_ref[...] += lrow

    row = pl.BlockSpec((tr, d), lambda i: (i, 0))
    vec = pl.BlockSpec((1, d), lambda i: (0, 0))
    return pl.pallas_call(
        body,
        grid=(s // tr,),
        in_specs=[row, vec, row],
        out_specs=[row, row, vec, pl.BlockSpec((1, LANES), lambda i: (0, 0))],
        out_shape=[
            jax.ShapeDtypeStruct((s, d), F32),
            jax.ShapeDtypeStruct((s, d), BF16),
            jax.ShapeDtypeStruct((1, d), F32),
            jax.ShapeDtypeStruct((1, LANES), F32),
        ],
        compiler_params=_cparams(("arbitrary",)),
        name=name,
    )(x, g, target)


def _sigmoid(x):
    return 1.0 / (1.0 + jnp.exp(-x))


def _gate_out(gate_ref, o):
    gt = gate_ref[...]
    return (o * (gt * _sigmoid(gt))).astype(BF16)


def _gate_grads(gate_ref, o_ref, dg_ref, dgate_ref):
    gt = gate_ref[...]
    sg = _sigmoid(gt)
    dgv = dg_ref[...]
    dgate_ref[...] = (dgv * o_ref[...] * (sg * (1.0 + gt * (1.0 - sg)))).astype(BF16)
    return dgv * (gt * sg)


def _iotas(tq, tk):
    return lax.broadcasted_iota(jnp.int32, (tq, tk), 0), lax.broadcasted_iota(jnp.int32, (tq, tk), 1)


def _softplus(s):
    return jnp.maximum(s, 0.0) + jnp.log(1.0 + jnp.exp(-jnp.abs(s)))


def _split2(v):
    hi = v.astype(BF16)
    lo = (v - hi.astype(F32)).astype(BF16)
    return hi, lo


def _cat2(v):
    return jnp.concatenate(_split2(v), axis=1)


def _tri2(keep):
    m = keep.astype(BF16)
    return jnp.concatenate([m, m], axis=0)


def _split3(v):
    a = v.astype(BF16)
    r1 = v - a.astype(F32)
    b = r1.astype(BF16)
    c = (r1 - b.astype(F32)).astype(BF16)
    return a, b, c


SB_TQ = 1024
SB_TK = 128
SB_UNROLL = 8


def _sb_fwd(proj, n_heads, name, gather=None):
    s = proj.shape[0]
    d = HEAD_DIM
    t = min(SB_TQ, s)
    tk = min(SB_TK, s)
    r = t // tk
    un = SB_UNROLL if r % SB_UNROLL == 0 else 1
    nq = s // t
    scale = d**-0.5

    def body(*refs):
        if gather is None:
            q_ref, k_ref, v_ref, gate_ref, o_ref, lt_ref, g_ref, kb_ref, vb_ref = refs
        else:
            q_ref, k_ref, v_ref, gate_ref, w_ref, o_ref, lt_ref, g_ref, wall_ref, kb_ref, vb_ref, send_sems, recv_sems = refs
        i = pl.program_id(1)

        if gather is not None:
            hh = pl.program_id(0)
            send, forward, finish = _gather_steps(w_ref, wall_ref, send_sems, recv_sems)
            pl.when((hh == 0) & (i == 0))(send)
            pl.when((hh == n_heads // 2) & (i == 0))(forward)

        @pl.when(i == 0)
        def _():
            kb_ref[...] = k_ref[...].astype(BF16)
            vb_ref[...] = v_ref[...].astype(BF16)

        q = (q_ref[...] * scale).astype(BF16)
        rows, cols = _iotas(t, tk)
        trows, tcols = _iotas(tk, tk)
        tri = (trows > tcols).astype(BF16)

        def block(kb, cl, acc, diag):
            k0 = pl.multiple_of(kb * tk, tk)
            sc = _dot(q, kb_ref[pl.ds(k0, tk), :], "nt")
            sp = _softplus(sc)
            ls = -sp
            if diag is not None:
                strict = cols + diag * tk < rows
                ls = jnp.where(strict, ls, 0.0)
            hi, lo = _split2(ls)
            later = _dot(hi, tri, "nn") + _dot(lo, tri, "nn")
            w = jnp.exp((sc - sp) + later + cl)
            if diag is not None:
                w = jnp.where(strict, w, 0.0)
            acc = acc + _dot(w.astype(BF16), vb_ref[pl.ds(k0, tk), :], "nn")
            return cl + jnp.sum(ls, axis=1, keepdims=True), acc

        cl, acc = jnp.zeros((t, 1), F32), jnp.zeros((t, d), F32)
        for dd in reversed(range(r)):
            cl, acc = block(i * r + dd, cl, acc, dd)

        def loop(j, carry):
            for u in range(un):
                carry = block(i * r - 1 - (un * j + u), carry[0], carry[1], None)
            return carry

        cl, acc = lax.fori_loop(0, (i * r) // un, loop, (cl, acc))
        o_ref[...] = acc
        lt_ref[...] = cl
        g_ref[...] = _gate_out(gate_ref, acc)

        if gather is not None:
            pl.when((hh == n_heads - 1) & (i == nq - 1))(finish)

    h = n_heads
    qblk = pl.BlockSpec((t, d), lambda hh, i: (i, hh))
    in_specs = [
        qblk,
        pl.BlockSpec((s, d), lambda hh, i: (0, h + hh)),
        pl.BlockSpec((s, d), lambda hh, i: (0, 2 * h + hh)),
        pl.BlockSpec((t, d), lambda hh, i: (i, 3 * h + hh)),
    ]
    out_specs = [qblk, pl.BlockSpec((None, t, 1), lambda hh, i: (hh, i, 0)), qblk]
    out_shape = [jax.ShapeDtypeStruct((s, h * d), F32), jax.ShapeDtypeStruct((h, s, 1), F32), jax.ShapeDtypeStruct((s, h * d), BF16)]
    scratch = [pltpu.VMEM((s, d), BF16), pltpu.VMEM((s, d), BF16)]
    args = [proj, proj, proj, proj]
    if gather is not None:
        in_specs.append(ANY)
        out_specs.append(ANY)
        out_shape.append(jax.ShapeDtypeStruct((N_CHIPS,) + gather.shape, gather.dtype))
        scratch += GATHER_SEMS
        args.append(gather)
    return pl.pallas_call(
        body,
        grid=(h, nq),
        in_specs=in_specs,
        out_specs=out_specs,
        out_shape=out_shape,
        scratch_shapes=scratch,
        compiler_params=_cparams(("arbitrary", "arbitrary")),
        name=name,
    )(*args)


def _sb_bwd(proj, ltot, o, dg, n_heads, name):
    s = proj.shape[0]
    d = HEAD_DIM
    t = min(SB_TQ, s)
    tk = min(SB_TK, s)
    r = t // tk
    un = SB_UNROLL if r % SB_UNROLL == 0 else 1
    nq = s // t
    scale = d**-0.5

    def body(q_ref, k_ref, v_ref, gate_ref, lt_ref, o_ref, dg_ref, dq_ref, dko_ref, dvo_ref, dgate_ref, kb_ref, vb_ref, dk_ref, dv_ref):
        i = pl.program_id(1)

        @pl.when(i == 0)
        def _():
            kb_ref[...] = k_ref[...].astype(BF16)
            vb_ref[...] = v_ref[...].astype(BF16)
            dk_ref[...] = jnp.zeros_like(dk_ref)
            dv_ref[...] = jnp.zeros_like(dv_ref)

        dob = _gate_grads(gate_ref, o_ref, dg_ref, dgate_ref).astype(BF16)
        q = (q_ref[...] * scale).astype(BF16)
        ltv = lt_ref[...]
        rows, cols = _iotas(t, tk)
        trows, tcols = _iotas(tk, tk)
        upto = _tri2(trows <= tcols)
        before = _tri2(trows < tcols)

        def block(kb, cp, cc, dq, diag):
            k0 = pl.multiple_of(kb * tk, tk)
            kk = kb_ref[pl.ds(k0, tk), :]
            sc = _dot(q, kk, "nt")
            sp = _softplus(sc)
            ls = -sp
            if diag is not None:
                strict = cols + diag * tk < rows
                ls = jnp.where(strict, ls, 0.0)
            prefix = _dot(_cat2(ls), upto, "nn") + cp
            lsig = sc - sp
            w = jnp.exp(lsig + (ltv - prefix))
            if diag is not None:
                w = jnp.where(strict, w, 0.0)
            da = _dot(dob, vb_ref[pl.ds(k0, tk), :], "nt") * w
            csum = _dot(_cat2(da), before, "nn") + cc
            beta = jnp.exp(lsig)
            dz = da * (1.0 - beta) - beta * csum
            if diag is not None:
                dz = jnp.where(strict, dz, 0.0)
            dzb = dz.astype(BF16)
            dq = dq + _dot(dzb, kk, "nn")
            dk_ref[pl.ds(k0, tk), :] += _dot(dzb, q, "tn")
            dv_ref[pl.ds(k0, tk), :] += _dot(w.astype(BF16), dob, "tn")
            return cp + jnp.sum(ls, axis=1, keepdims=True), cc + jnp.sum(da, axis=1, keepdims=True), dq

        def loop(j, carry):
            for u in range(un):
                carry = block(un * j + u, carry[0], carry[1], carry[2], None)
            return carry

        z1 = jnp.zeros((t, 1), F32)
        cp, cc, dq = lax.fori_loop(0, (i * r) // un, loop, (z1, z1, jnp.zeros((t, d), F32)))
        for dd in range(r):
            cp, cc, dq = block(i * r + dd, cp, cc, dq, dd)
        dq_ref[...] = (dq * scale).astype(BF16)

        @pl.when(i == nq - 1)
        def _():
            dko_ref[...] = dk_ref[...].astype(BF16)
            dvo_ref[...] = dv_ref[...].astype(BF16)

    h = n_heads
    qblk = pl.BlockSpec((t, d), lambda hh, i: (i, hh))
    full = pl.BlockSpec((s, d), lambda hh, i: (0, hh))
    shp = jax.ShapeDtypeStruct((s, h * d), BF16)
    return pl.pallas_call(
        body,
        grid=(h, nq),
        in_specs=[
            qblk,
            pl.BlockSpec((s, d), lambda hh, i: (0, h + hh)),
            pl.BlockSpec((s, d), lambda hh, i: (0, 2 * h + hh)),
            pl.BlockSpec((t, d), lambda hh, i: (i, 3 * h + hh)),
            pl.BlockSpec((None, t, 1), lambda hh, i: (hh, i, 0)),
            qblk,
            qblk,
        ],
        out_specs=[qblk, full, full, qblk],
        out_shape=[shp, shp, shp, shp],
        scratch_shapes=[pltpu.VMEM((s, d), BF16), pltpu.VMEM((s, d), BF16), pltpu.VMEM((s, d), F32), pltpu.VMEM((s, d), F32)],
        compiler_params=_cparams(("arbitrary", "arbitrary")),
        name=name,
    )(proj, proj, proj, proj, ltot, o, dg)


SM_TQ = 1024
SM_TK = 512
SM_UNROLL = 2
SM_TK_WIDE = 256
SM_UNROLL_WIDE = 4


def _allowed(mode, rows, cols, q0, k0):
    r = rows + q0
    c = cols + k0
    if mode == "causal":
        return c <= r
    return (c >> CHUNK_SHIFT) <= (r >> CHUNK_SHIFT)


def _sm_fwd(q_arr, k_arr, v_arr, gate_arr, n_heads, dqk, q_blk, k_blk, v_blk, gate_blk, mode, name, crow=None, ccol=None):
    s = q_arr.shape[0]
    dv = HEAD_DIM
    t = min(SM_TQ, s)
    tk = min(SM_TK, s)
    r = t // tk
    un = SM_UNROLL if r % SM_UNROLL == 0 else 1
    bias = crow is not None
    scale = (HEAD_DIM if mode == "causal" else HEAD_DIM + MLA_ROPE) ** -0.5

    def body(*refs):
        if bias:
            q_ref, k_ref, v_ref, gate_ref, cr_ref, cc_ref, o_ref, lse_ref, g_ref, kb_ref, vb_ref = refs
        else:
            q_ref, k_ref, v_ref, gate_ref, o_ref, lse_ref, g_ref, kb_ref, vb_ref = refs
        i = pl.program_id(1)

        @pl.when(i == 0)
        def _():
            kb_ref[...] = k_ref[...].astype(BF16)
            vb_ref[...] = v_ref[...].astype(BF16)

        q = (q_ref[...] * scale).astype(BF16)
        q0 = i * t
        rows, cols = _iotas(t, tk)
        crv = cr_ref[...] if bias else None

        def block(kb, m, l, acc, masked):
            k0 = pl.multiple_of(kb * tk, tk)
            sc = _dot(q, kb_ref[pl.ds(k0, tk), :], "nt")
            if bias:
                sc = sc + crv - cc_ref[pl.ds(kb, 1), :]
            if masked:
                sc = jnp.where(_allowed(mode, rows, cols, q0, k0), sc, NEG)
            m_new = jnp.maximum(m, jnp.max(sc, axis=1, keepdims=True))
            alpha = jnp.exp(m - m_new)
            p = jnp.exp(sc - m_new)
            l = alpha * l + jnp.sum(p, axis=1, keepdims=True)
            acc = alpha * acc + _dot(p.astype(BF16), vb_ref[pl.ds(k0, tk), :], "nn")
            return m_new, l, acc

        def loop(j, carry):
            for u in range(un):
                carry = block(un * j + u, carry[0], carry[1], carry[2], False)
            return carry

        init = (jnp.full((t, 1), NEG, F32), jnp.zeros((t, 1), F32), jnp.zeros((t, dv), F32))
        m, l, acc = lax.fori_loop(0, (i * r) // un, loop, init)
        for dd in range(r):
            m, l, acc = block(i * r + dd, m, l, acc, True)
        ov = acc / l
        o_ref[...] = ov
        lse_ref[...] = m + jnp.log(l)
        g_ref[...] = _gate_out(gate_ref, ov)

    h = n_heads
    oblk = pl.BlockSpec((t, dv), lambda hh, i: (i, hh))
    in_specs = [
        pl.BlockSpec((t, dqk), lambda hh, i: (i, q_blk(hh))),
        pl.BlockSpec((s, dqk), lambda hh, i: (0, k_blk(hh))),
        pl.BlockSpec((s, dv), lambda hh, i: (0, v_blk(hh))),
        pl.BlockSpec((t, dv), lambda hh, i: (i, gate_blk(hh))),
    ]
    args = [q_arr, k_arr, v_arr, gate_arr]
    if bias:
        in_specs += [pl.BlockSpec((None, t, 1), lambda hh, i: (hh, i, 0)), pl.BlockSpec((None, s // tk, tk), lambda hh, i: (hh, 0, 0))]
        args += [crow, ccol]
    return pl.pallas_call(
        body,
        grid=(h, s // t),
        in_specs=in_specs,
        out_specs=[oblk, pl.BlockSpec((None, t, 1), lambda hh, i: (hh, i, 0)), oblk],
        out_shape=[jax.ShapeDtypeStruct((s, h * dv), F32), jax.ShapeDtypeStruct((h, s, 1), F32), jax.ShapeDtypeStruct((s, h * dv), BF16)],
        scratch_shapes=[pltpu.VMEM((s, dqk), BF16), pltpu.VMEM((s, dv), BF16)],
        compiler_params=_cparams(("arbitrary", "arbitrary")),
        name=name,
    )(*args)


def _sm_bwd(q_arr, k_arr, v_arr, gate_arr, o, dg, lse, n_heads, dqk, q_blk, k_blk, v_blk, gate_blk, mode, name, crow=None, ccol=None, exchange=None):
    s = q_arr.shape[0]
    dv = HEAD_DIM
    t = min(SM_TQ, s)
    tk_cfg, un_cfg = (SM_TK_WIDE, SM_UNROLL_WIDE) if dqk > HEAD_DIM else (SM_TK, SM_UNROLL)
    tk = min(tk_cfg, s)
    r = t // tk
    un = un_cfg if r % un_cfg == 0 else 1
    nq = s // t
    bias = crow is not None
    assert not (bias and exchange is not None)
    scale = (HEAD_DIM if mode == "causal" else HEAD_DIM + MLA_ROPE) ** -0.5

    def body(*refs):
        if bias:
            q_ref, k_ref, v_ref, gate_ref, o_ref, dg_ref, lse_ref, cr_ref, cc_ref, dq_ref, dk_ref, dv_ref, dgate_ref, dcc_ref, dcr_ref, kb_ref, vb_ref = refs
        elif exchange is not None:
            q_ref, k_ref, v_ref, gate_ref, o_ref, dg_ref, lse_ref, sp_ref, dq_ref, dk_ref, dv_ref, dgate_ref, slots_ref, kb_ref, vb_ref, send_sems, recv_sems = refs
        else:
            q_ref, k_ref, v_ref, gate_ref, o_ref, dg_ref, lse_ref, dq_ref, dk_ref, dv_ref, dgate_ref, kb_ref, vb_ref = refs
        i = pl.program_id(1)

        if exchange is not None:
            hh = pl.program_id(0)
            send, finish = _exchange_steps(sp_ref, slots_ref, send_sems, recv_sems)
            pl.when((hh == 0) & (i == 0))(send)

        @pl.when(i == 0)
        def _():
            kb_ref[...] = k_ref[...].astype(BF16)
            vb_ref[...] = v_ref[...].astype(BF16)
            dk_ref[...] = jnp.zeros_like(dk_ref)
            dv_ref[...] = jnp.zeros_like(dv_ref)
            if bias:
                dcc_ref[...] = jnp.zeros_like(dcc_ref)

        q = (q_ref[...] * scale).astype(BF16)
        dov = _gate_grads(gate_ref, o_ref, dg_ref, dgate_ref)
        dob = dov.astype(BF16)
        dsum = jnp.sum(dov * o_ref[...], axis=1, keepdims=True)
        lsev = lse_ref[...]
        q0 = i * t
        rows, cols = _iotas(t, tk)
        crv = cr_ref[...] if bias else None

        def block(kb, dq, dr, masked):
            k0 = pl.multiple_of(kb * tk, tk)
            kk = kb_ref[pl.ds(k0, tk), :]
            sc = _dot(q, kk, "nt")
            if bias:
                sc = sc + crv - cc_ref[pl.ds(kb, 1), :]
            if masked:
                sc = jnp.where(_allowed(mode, rows, cols, q0, k0), sc, NEG)
            p = jnp.exp(sc - lsev)
            dz = p * (_dot(dob, vb_ref[pl.ds(k0, tk), :], "nt") - dsum)
            dzb = dz.astype(BF16)
            dk_ref[pl.ds(k0, tk), :] += _dot(dzb, q, "tn")
            dv_ref[pl.ds(k0, tk), :] += _dot(p.astype(BF16), dob, "tn")
            if bias:
                dcc_ref[pl.ds(kb, 1), :] -= jnp.sum(dz, axis=0, keepdims=True)
                dr = dr + jnp.sum(dz, axis=1, keepdims=True)
            return dq + _dot(dzb, kk, "nn"), dr

        def loop(j, carry):
            for u in range(un):
                carry = block(un * j + u, carry[0], carry[1], False)
            return carry

        dq, dr = lax.fori_loop(0, (i * r) // un, loop, (jnp.zeros((t, dqk), F32), jnp.zeros((t, 1), F32)))
        for dd in range(r):
            dq, dr = block(i * r + dd, dq, dr, True)
        dq_ref[...] = dq * scale
        if bias:
            dcr_ref[...] = dr
        if exchange is not None:
            pl.when((hh == n_heads - 1) & (i == nq - 1))(finish)

    h = n_heads
    qblk = pl.BlockSpec((t, dqk), lambda hh, i: (i, q_blk(hh)))
    oblk = pl.BlockSpec((t, dv), lambda hh, i: (i, hh))
    vec = pl.BlockSpec((None, t, 1), lambda hh, i: (hh, i, 0))
    in_specs = [
        qblk,
        pl.BlockSpec((s, dqk), lambda hh, i: (0, k_blk(hh))),
        pl.BlockSpec((s, dv), lambda hh, i: (0, v_blk(hh))),
        pl.BlockSpec((t, dv), lambda hh, i: (i, gate_blk(hh))),
        oblk,
        oblk,
        vec,
    ]
    args = [q_arr, k_arr, v_arr, gate_arr, o, dg, lse]
    out_specs = [
        pl.BlockSpec((t, dqk), lambda hh, i: (i, hh)),
        pl.BlockSpec((s, dqk), lambda hh, i: (0, hh)),
        pl.BlockSpec((s, dv), lambda hh, i: (0, hh)),
        oblk,
    ]
    out_shape = [
        jax.ShapeDtypeStruct((s, h * dqk), F32),
        jax.ShapeDtypeStruct((s, h * dqk), F32),
        jax.ShapeDtypeStruct((s, h * dv), F32),
        jax.ShapeDtypeStruct((s, h * dv), BF16),
    ]
    if bias:
        ccs = pl.BlockSpec((None, s // tk, tk), lambda hh, i: (hh, 0, 0))
        in_specs += [vec, ccs]
        args += [crow, ccol]
        out_specs += [ccs, vec]
        out_shape += [jax.ShapeDtypeStruct((h, s // tk, tk), F32), jax.ShapeDtypeStruct((h, s, 1), F32)]
    scratch = [pltpu.VMEM((s, dqk), BF16), pltpu.VMEM((s, dv), BF16)]
    if exchange is not None:
        in_specs.append(ANY)
        args.append(exchange)
        out_specs.append(ANY)
        out_shape.append(jax.ShapeDtypeStruct(exchange.shape, exchange.dtype))
        scratch += EXCHANGE_SEMS
    return pl.pallas_call(
        body,
        grid=(h, nq),
        in_specs=in_specs,
        out_specs=out_specs,
        out_shape=out_shape,
        scratch_shapes=scratch,
        compiler_params=_cparams(("arbitrary", "arbitrary")),
        name=name,
    )(*args)


def _rope_tables(pos):
    half = MLA_ROPE // 2
    inv_freq = ROPE_BASE ** (-jnp.arange(0, MLA_ROPE, 2, dtype=F32) / MLA_ROPE)
    ang = pos.astype(F32)[:, None] * inv_freq
    cos, sin = jnp.cos(ang), jnp.sin(ang)
    z = lambda n: jnp.zeros((pos.shape[0], n), F32)
    tc = jnp.concatenate([cos, cos, z(LANES - 2 * half)], axis=1)
    ta = jnp.concatenate([-sin, z(LANES - half)], axis=1)
    tb = jnp.concatenate([z(half), sin, z(LANES - 2 * half)], axis=1)
    return tc, ta, tb


def _rot(v, tc, ta, tb, sign):
    half = MLA_ROPE // 2
    return v * tc + sign * (pltpu.roll(v, LANES - half, 1) * ta + pltpu.roll(v, half, 1) * tb)


def _mla_assemble(qpre, kv, proj1, kr_blk, tabs, n_heads, name):
    s = qpre.shape[0]
    tr = LANES
    wd = n_heads * MLA_QK

    def body(qp_ref, kv_ref, kr_ref, tc_ref, ta_ref, tb_ref, qc_ref, kc_ref):
        tc, ta, tb = tc_ref[...], ta_ref[...], tb_ref[...]
        kr = _rot(kr_ref[...], tc, ta, tb, 1.0)
        for hh in range(n_heads):
            lo, mid, hi = hh * MLA_QK, hh * MLA_QK + LANES, (hh + 1) * MLA_QK
            qc_ref[:, lo:mid] = qp_ref[:, lo:mid]
            qc_ref[:, mid:hi] = _rot(qp_ref[:, mid:hi], tc, ta, tb, 1.0)
            kc_ref[:, lo:mid] = kv_ref[:, lo:mid]
            kc_ref[:, mid:hi] = kr

    tab = pl.BlockSpec((tr, LANES), lambda i: (i, 0))
    wide = pl.BlockSpec((tr, wd), lambda i: (i, 0))
    shp = jax.ShapeDtypeStruct((s, wd), F32)
    return pl.pallas_call(
        body,
        grid=(s // tr,),
        in_specs=[wide, wide, pl.BlockSpec((tr, LANES), lambda i: (i, kr_blk)), tab, tab, tab],
        out_specs=[wide, wide],
        out_shape=[shp, shp],
        compiler_params=_cparams(("parallel",)),
        name=name,
    )(qpre, kv, proj1, *tabs)


def _mla_disassemble(dqcat, dkcat, dv, tabs, n_heads, name):
    s = dqcat.shape[0]
    tr = LANES
    wd = n_heads * MLA_QK

    def body(dq_ref, dk_ref, dv_ref, tc_ref, ta_ref, tb_ref, dqp_ref, dkv_ref, dkr_ref):
        tc, ta, tb = tc_ref[...], ta_ref[...], tb_ref[...]
        acc = jnp.zeros((tr, LANES), F32)
        for hh in range(n_heads):
            lo, mid, hi = hh * MLA_QK, hh * MLA_QK + LANES, (hh + 1) * MLA_QK
            dqp_ref[:, lo:mid] = dq_ref[:, lo:mid].astype(BF16)
            dqp_ref[:, mid:hi] = _rot(dq_ref[:, mid:hi], tc, ta, tb, -1.0).astype(BF16)
            dkv_ref[:, lo:mid] = dk_ref[:, lo:mid].astype(BF16)
            dkv_ref[:, mid:hi] = dv_ref[:, hh * LANES : (hh + 1) * LANES].astype(BF16)
            acc = acc + dk_ref[:, mid:hi]
        dkr_ref[...] = _rot(acc, tc, ta, tb, -1.0)

    tab = pl.BlockSpec((tr, LANES), lambda i: (i, 0))
    wide = pl.BlockSpec((tr, wd), lambda i: (i, 0))
    shp = jax.ShapeDtypeStruct((s, wd), BF16)
    return pl.pallas_call(
        body,
        grid=(s // tr,),
        in_specs=[wide, wide, pl.BlockSpec((tr, n_heads * LANES), lambda i: (i, 0)), tab, tab, tab],
        out_specs=[wide, wide, tab],
        out_shape=[shp, shp, jax.ShapeDtypeStruct((s, LANES), F32)],
        compiler_params=_cparams(("parallel",)),
        name=name,
    )(dqcat, dkcat, dv, *tabs)


def _forget_scan(proj2, f_blk, bias, name):
    s = proj2.shape[0]
    n = LANES

    def body(f_ref, b_ref, c_ref):
        rows, cols = _iotas(n, n)
        tri = (cols <= rows).astype(BF16)

        def step(j, carry):
            r0 = pl.multiple_of(j * n, n)
            f = f_ref[pl.ds(r0, n), :] + b_ref[...]
            lf = jnp.minimum(f, 0.0) - jnp.log1p(jnp.exp(-jnp.abs(f)))
            a, b, c = _split3(lf)
            cs = _dot(tri, a, "nn") + _dot(tri, b, "nn") + _dot(tri, c, "nn") + carry
            c_ref[pl.ds(r0, n), :] = cs
            return cs[n - 1 : n, :]

        lax.fori_loop(0, s // n, step, jnp.zeros((1, n), F32))

    return pl.pallas_call(
        body,
        grid=(1,),
        in_specs=[pl.BlockSpec((s, n), lambda i: (0, f_blk)), pl.BlockSpec((1, n), lambda i: (0, 0))],
        out_specs=pl.BlockSpec((s, n), lambda i: (0, 0)),
        out_shape=jax.ShapeDtypeStruct((s, n), F32),
        compiler_params=_cparams(("arbitrary",)),
        name=name,
    )(proj2, bias)


def _forget_scan_bwd(dc_col, dc_row, proj2, f_blk, bias, n_heads, name):
    s = proj2.shape[0]
    n = LANES
    nb = s // n

    def body(dcc_ref, dcr_ref, f_ref, b_ref, df_ref, db_ref):
        rows, cols = _iotas(n, n)
        tri = (cols >= rows).astype(BF16)
        live = cols < n_heads

        def step(j, carry):
            acc, dbv = carry
            r0 = pl.multiple_of((nb - 1 - j) * n, n)
            a, b, c = _split3(dcc_ref[pl.ds(r0, n), :] + dcr_ref[pl.ds(r0, n), :])
            dl = _dot(tri, a, "nn") + _dot(tri, b, "nn") + _dot(tri, c, "nn") + acc
            f = f_ref[pl.ds(r0, n), :] + b_ref[...]
            df = jnp.where(live, dl / (1.0 + jnp.exp(f)), 0.0)
            df_ref[pl.ds(r0, n), :] = df.astype(BF16)
            return dl[0:1, :], dbv + jnp.sum(df, axis=0, keepdims=True)

        z = jnp.zeros((1, n), F32)
        _, dbv = lax.fori_loop(0, nb, step, (z, z))
        db_ref[...] = dbv

    return pl.pallas_call(
        body,
        grid=(1,),
        in_specs=[
            pl.BlockSpec((s, n), lambda i: (0, 0)),
            pl.BlockSpec((s, n), lambda i: (0, 0)),
            pl.BlockSpec((s, n), lambda i: (0, f_blk)),
            pl.BlockSpec((1, n), lambda i: (0, 0)),
        ],
        out_specs=[pl.BlockSpec((s, n), lambda i: (0, 0)), pl.BlockSpec((1, n), lambda i: (0, 0))],
        out_shape=[jax.ShapeDtypeStruct((s, n), BF16), jax.ShapeDtypeStruct((1, n), F32)],
        compiler_params=_cparams(("arbitrary",)),
        name=name,
    )(dc_col, dc_row, proj2, bias)


def _adamw(w, g, m, v, name):
    r, c = w.shape
    tr = _pick(r, (128, 64, 32, 16, 8))
    c1 = 1.0 - ADAM_B1**ADAM_STEP
    c2 = 1.0 - ADAM_B2**ADAM_STEP

    def body(w_ref, g_ref, m_ref, v_ref, d_ref, mo_ref, vo_ref):
        gv = g_ref[...]
        mn = ADAM_B1 * m_ref[...] + (1.0 - ADAM_B1) * gv
        vn = ADAM_B2 * v_ref[...] + (1.0 - ADAM_B2) * (gv * gv)
        mo_ref[...] = mn
        vo_ref[...] = vn
        d_ref[...] = -ADAM_LR * ((mn / c1) / (jnp.sqrt(vn / c2) + ADAM_EPS) + ADAM_WD * w_ref[...])

    blk = pl.BlockSpec((tr, c), lambda i: (i, 0))
    shp = jax.ShapeDtypeStruct((r, c), F32)
    return pl.pallas_call(
        body,
        grid=(r // tr,),
        in_specs=[blk, blk, blk, blk],
        out_specs=[blk, blk, blk],
        out_shape=[shp, shp, shp],
        compiler_params=_cparams(("parallel",)),
        name=name,
    )(w, g, m, v)


def _mesh_pos():
    return lax.axis_index("x"), lax.axis_index("y"), lax.axis_index("c")


def _other_chips(x, y):
    return [(1 - x, y), (x, 1 - y), (1 - x, 1 - y)]


ANY = pl.BlockSpec(memory_space=pl.ANY)


GATHER_SEMS = [pltpu.SemaphoreType.DMA((6,)), pltpu.SemaphoreType.DMA((6,))]
EXCHANGE_SEMS = [pltpu.SemaphoreType.DMA((3,)), pltpu.SemaphoreType.DMA((3,))]


def _gather_steps(w_ref, out_ref, send_sems, recv_sems):
    half = w_ref.shape[0] // 2
    x, y, c = _mesh_pos()
    me = 2 * x + y
    chips = _other_chips(x, y)

    def region(chip, hc):
        return out_ref.at[chip, pl.ds(hc * half, half), :]

    def copy(k, src, dst, to):
        return pltpu.make_async_remote_copy(
            src_ref=src, dst_ref=dst, send_sem=send_sems.at[k], recv_sem=recv_sems.at[k], device_id=to, device_id_type=MESH
        )

    first = [copy(j, w_ref.at[pl.ds(c * half, half), :], region(me, c), (cx, cy, c)) for j, (cx, cy) in enumerate(chips)]
    passed = [copy(3 + j, region(2 * cx + cy, c), region(2 * cx + cy, c), (x, y, 1 - c)) for j, (cx, cy) in enumerate(chips)]

    def send():
        for cp in first:
            cp.start()

    def forward():
        for j, (cx, cy) in enumerate(chips):
            copy(j, region(2 * cx + cy, c), region(2 * cx + cy, c), (x, y, c)).wait_recv()
            passed[j].start()

    def finish():
        for j, (cx, cy) in enumerate(chips):
            copy(3 + j, region(2 * cx + cy, 1 - c), region(2 * cx + cy, 1 - c), (x, y, c)).wait_recv()
        for cp in first + passed:
            cp.wait_send()

    return send, forward, finish


def _gather_weights(wp, tag):
    rp, wd = wp.shape

    def body(w_ref, out_ref, send_sems, recv_sems):
        for step in _gather_steps(w_ref, out_ref, send_sems, recv_sems):
            step()

    return pl.pallas_call(
        body,
        in_specs=[ANY],
        out_specs=ANY,
        out_shape=jax.ShapeDtypeStruct((N_CHIPS, rp, wd), wp.dtype),
        scratch_shapes=GATHER_SEMS,
        name=f"gather_weights_{tag}",
    )(wp)


def _place_own(wall, wp, pos, tag):
    rp, wd = wp.shape

    def body(x_ref, y_ref, c_ref, wall_ref, w_ref, o_ref):
        o_ref[0] = w_ref[...]

    grid_spec = pltpu.PrefetchScalarGridSpec(
        num_scalar_prefetch=3,
        grid=(rp // PACK_TR,),
        in_specs=[ANY, pl.BlockSpec((PACK_TR, wd), lambda i, xr, yr, cr: (i, 0))],
        out_specs=pl.BlockSpec((1, PACK_TR, wd), lambda i, xr, yr, cr: (2 * xr[0] + yr[0], i, 0)),
    )
    return pl.pallas_call(
        body,
        grid_spec=grid_spec,
        out_shape=jax.ShapeDtypeStruct(wall.shape, wall.dtype),
        input_output_aliases={3: 0},
        compiler_params=_cparams(("parallel",)),
        name=f"place_own_shard_{tag}",
    )(*pos, wall, wp)


def _pair_exchange(g, tag):
    _, rp, wd = g.shape
    half = rp // 2

    def body(g_ref, out_ref, send_sem, recv_sem):
        x, y, c = _mesh_pos()
        cp = pltpu.make_async_remote_copy(
            src_ref=g_ref.at[:, pl.ds((1 - c) * half, half), :],
            dst_ref=out_ref,
            send_sem=send_sem,
            recv_sem=recv_sem,
            device_id=(x, y, 1 - c),
            device_id_type=MESH,
        )
        cp.start()
        cp.wait()

    return pl.pallas_call(
        body,
        in_specs=[ANY],
        out_specs=ANY,
        out_shape=jax.ShapeDtypeStruct((N_CHIPS, half, wd), g.dtype),
        scratch_shapes=[pltpu.SemaphoreType.DMA, pltpu.SemaphoreType.DMA],
        name=f"rs_pair_exchange_{tag}",
    )(g)


def _pair_add(g, recv, pos, tag):
    _, rp, wd = g.shape
    half = rp // 2
    nb = half // PACK_TR

    def body(x_ref, y_ref, c_ref, g_ref, r_ref, o_ref):
        o_ref[...] = (g_ref[...] + r_ref[...]).astype(BF16)

    blk = (1, PACK_TR, wd)
    grid_spec = pltpu.PrefetchScalarGridSpec(
        num_scalar_prefetch=3,
        grid=(N_CHIPS, nb),
        in_specs=[
            pl.BlockSpec(blk, lambda j, i, xr, yr, cr: (j, cr[0] * nb + i, 0)),
            pl.BlockSpec(blk, lambda j, i, xr, yr, cr: (j, i, 0)),
        ],
        out_specs=pl.BlockSpec(blk, lambda j, i, xr, yr, cr: (j, i, 0)),
    )
    return pl.pallas_call(
        body,
        grid_spec=grid_spec,
        out_shape=jax.ShapeDtypeStruct((N_CHIPS, half, wd), BF16),
        compiler_params=_cparams(("parallel", "parallel")),
        name=f"rs_pair_add_{tag}",
    )(*pos, g, recv)


def _exchange_steps(s_ref, out_ref, send_sems, recv_sems):
    x, y, c = _mesh_pos()
    me = 2 * x + y
    chips = _other_chips(x, y)

    def copy(j, src, dst, to):
        return pltpu.make_async_remote_copy(
            src_ref=src, dst_ref=dst, send_sem=send_sems.at[j], recv_sem=recv_sems.at[j], device_id=to, device_id_type=MESH
        )

    sends = [copy(j, s_ref.at[2 * cx + cy], out_ref.at[me], (cx, cy, c)) for j, (cx, cy) in enumerate(chips)]

    def send():
        for cp in sends:
            cp.start()

    def finish():
        for j, (cx, cy) in enumerate(chips):
            copy(j, s_ref.at[me], out_ref.at[2 * cx + cy], (x, y, c)).wait_recv()
        for cp in sends:
            cp.wait_send()

    return send, finish


def _chip_exchange(sp, tag):
    def body(s_ref, out_ref, send_sems, recv_sems):
        for step in _exchange_steps(s_ref, out_ref, send_sems, recv_sems):
            step()

    return pl.pallas_call(
        body,
        in_specs=[ANY],
        out_specs=ANY,
        out_shape=jax.ShapeDtypeStruct(sp.shape, sp.dtype),
        scratch_shapes=EXCHANGE_SEMS,
        name=f"rs_chip_exchange_{tag}",
    )(sp)


def _sum_slots(own, slots, pos, tag):
    _, rh, wd = slots.shape
    nb = rh // PACK_TR

    def body(x_ref, y_ref, c_ref, own_ref, a_ref, b_ref, d_ref, o_ref):
        f = lambda r: r[0].astype(F32)
        o_ref[...] = ((f(own_ref) + f(a_ref)) + f(b_ref)) + f(d_ref)

    blk = (1, PACK_TR, wd)

    def other(k):
        return pl.BlockSpec(blk, lambda i, xr, yr, cr: (k + (k >= 2 * xr[0] + yr[0]).astype(jnp.int32), i, 0))

    grid_spec = pltpu.PrefetchScalarGridSpec(
        num_scalar_prefetch=3,
        grid=(nb,),
        in_specs=[pl.BlockSpec(blk, lambda i, xr, yr, cr: (2 * xr[0] + yr[0], i, 0)), other(0), other(1), other(2)],
        out_specs=pl.BlockSpec((PACK_TR, wd), lambda i, xr, yr, cr: (cr[0] * nb + i, 0)),
    )
    return pl.pallas_call(
        body,
        grid_spec=grid_spec,
        out_shape=jax.ShapeDtypeStruct((2 * rh, wd), F32),
        compiler_params=_cparams(("parallel",)),
        name=f"rs_sum_slots_{tag}",
    )(*pos, own, slots, slots, slots)


def _pair_gather(t, tag):
    rh = t.shape[0] // 2

    def body(t_ref, out_ref, send_sem, recv_sem):
        x, y, c = _mesh_pos()
        cp = pltpu.make_async_remote_copy(
            src_ref=t_ref.at[pl.ds(c * rh, rh), :],
            dst_ref=out_ref.at[pl.ds(c * rh, rh), :],
            send_sem=send_sem,
            recv_sem=recv_sem,
            device_id=(x, y, 1 - c),
            device_id_type=MESH,
        )
        cp.start()
        cp.wait_send()
        pltpu.make_async_remote_copy(
            src_ref=t_ref.at[pl.ds((1 - c) * rh, rh), :],
            dst_ref=out_ref.at[pl.ds((1 - c) * rh, rh), :],
            send_sem=send_sem,
            recv_sem=recv_sem,
            device_id=(x, y, c),
            device_id_type=MESH,
        ).wait_recv()

    return pl.pallas_call(
        body,
        in_specs=[ANY],
        out_specs=ANY,
        out_shape=jax.ShapeDtypeStruct(t.shape, t.dtype),
        input_output_aliases={0: 0},
        scratch_shapes=[pltpu.SemaphoreType.DMA, pltpu.SemaphoreType.DMA],
        name=f"rs_pair_gather_{tag}",
    )(t)


def _allreduce_small(v):
    shape = v.shape
    n_dev = 8

    def body(v_ref, o_ref, slots, send_sems, recv_sems):
        x, y, c = _mesh_pos()
        me = 4 * x + 2 * y + c
        slots[me] = v_ref[...]
        sends = []
        for k in range(1, n_dev):
            fx, fy, fc = (k >> 2) & 1, (k >> 1) & 1, k & 1
            to = (x ^ fx, y ^ fy, c ^ fc)
            cp = pltpu.make_async_remote_copy(
                src_ref=v_ref,
                dst_ref=slots.at[me],
                send_sem=send_sems.at[k - 1],
                recv_sem=recv_sems.at[k - 1],
                device_id=to,
                device_id_type=MESH,
            )
            cp.start()
            sends.append(cp)
        for k in range(1, n_dev):
            fx, fy, fc = (k >> 2) & 1, (k >> 1) & 1, k & 1
            frm = 4 * (x ^ fx) + 2 * (y ^ fy) + (c ^ fc)
            pltpu.make_async_remote_copy(
                src_ref=v_ref,
                dst_ref=slots.at[frm],
                send_sem=send_sems.at[k - 1],
                recv_sem=recv_sems.at[k - 1],
                device_id=(x, y, c),
                device_id_type=MESH,
            ).wait_recv()
        for cp in sends:
            cp.wait_send()
        acc = slots[0]
        for k in range(1, n_dev):
            acc = acc + slots[k]
        o_ref[...] = acc

    vm = pl.BlockSpec(memory_space=pltpu.VMEM)
    return pl.pallas_call(
        body,
        in_specs=[vm],
        out_specs=vm,
        out_shape=jax.ShapeDtypeStruct(shape, F32),
        scratch_shapes=[pltpu.VMEM((n_dev,) + shape, F32), pltpu.SemaphoreType.DMA((n_dev - 1,)), pltpu.SemaphoreType.DMA((n_dev - 1,))],
        name="allreduce_small",
    )(v)


def _pack_layout(shard_shapes):
    offs, rows = [], []
    off = 0
    for r, c in shard_shapes:
        assert (r * c) % PACK_W == 0
        n = r * c // PACK_W
        offs.append(off)
        rows.append(n)
        off += -(-n // 16) * 16
    rp = -(-off // (2 * PACK_TR)) * (2 * PACK_TR)
    return offs, rows, rp


def _pack_rows(parts, offs, rows, rp, lead):
    ends = list(offs[1:]) + [rp]
    nolead = ((0, 0),) * len(lead)
    out = [jnp.pad(p, nolead + ((0, e - o - n), (0, 0))) for p, o, n, e in zip(parts, offs, rows, ends)]
    return jnp.concatenate(out, axis=len(lead))


def kernel(x, positions, ln0, w_in0, w_out0, ln1, w_in1, q_norm1, w_qb1, kv_norm1, w_kvb1, w_out1, ln2, w_in2, b_f2, w_out2, ln3, w_in3, w_out3, final_norm, loss_target, m_ln0, m_w_in0, m_w_out0, m_ln1, m_w_in1, m_q_norm1, m_w_qb1, m_kv_norm1, m_w_kvb1, m_w_out1, m_ln2, m_w_in2, m_b_f2, m_w_out2, m_ln3, m_w_in3, m_w_out3, m_final_norm, v_ln0, v_w_in0, v_w_out0, v_ln1, v_w_in1, v_q_norm1, v_w_qb1, v_kv_norm1, v_w_kvb1, v_w_out1, v_ln2, v_w_in2, v_b_f2, v_w_out2, v_ln3, v_w_in3, v_w_out3, v_final_norm):
    xs = x[0]
    s, d = xs.shape
    di = 4 * w_out0.shape[0]
    nh = di // HEAD_DIM
    idx = tuple(lax.axis_index(a).astype(jnp.int32).reshape(1) for a in ("x", "y", "c"))

    big = [w_in0, w_out0, w_in1, w_qb1, w_kvb1, w_out1, w_in2, w_out2, w_in3, w_out3]
    col_sharded = [True, False, True, True, True, False, True, False, True, False]
    shard_shapes = [w.shape for w in big]
    as_gathered = [True, False, False, False, False, False, False, False, True, False]
    n_first = 2

    def pack_weights(lo, hi):
        offs, rows, rp = _pack_layout(shard_shapes[lo:hi])
        return _pack_rows([w.astype(BF16).reshape(n, PACK_W) for w, n in zip(big[lo:hi], rows)], offs, rows, rp, ()), offs, rows

    def unpack_weights(wall, lo, hi, offs, rows):
        out = []
        for (r, c), o, n, cs, g4 in zip(shard_shapes[lo:hi], offs, rows, col_sharded[lo:hi], as_gathered[lo:hi]):
            slab = wall[:, o : o + n, :].reshape(N_CHIPS, r, c)
            if g4:
                out.append(slab)
            else:
                out.append(slab.transpose(1, 0, 2).reshape(r, N_CHIPS * c) if cs else slab.reshape(N_CHIPS * r, c))
        return out

    wp_a, offs_a, rows_a = pack_weights(0, n_first)
    wp_b, offs_b, rows_b = pack_weights(n_first, len(big))
    wall_a = _place_own(_gather_weights(wp_a, "l0"), wp_a, idx, "l0")
    f_in0, f_out0 = unpack_weights(wall_a, 0, n_first, offs_a, rows_a)

    row = lambda v: v.reshape(1, -1)

    def sb_layer_fwd(xin, ln, w_in, w_out, tag, gather=None):
        h = _rmsnorm_fwd(xin, row(ln), f"norm_fwd_{tag}")
        proj = _proj_w4(h, w_in, f"proj_in_{tag}")
        o, lt, g, *gathered = _sb_fwd(proj, nh, f"sb_fwd_{tag}", gather=gather)
        xout = _matmul(g, w_out, "nn", f"proj_out_{tag}", res=xin)
        return xout, (xin, h, proj, o, lt, g), gathered

    x1, sv0, (wall_b_raw,) = sb_layer_fwd(xs, ln0, f_in0, f_out0, "l0", gather=wp_b)
    wall_b = _place_own(wall_b_raw, wp_b, idx, "rest")
    f_in1, f_qb1, f_kvb1, f_out1, f_in2, f_out2, f_in3, f_out3 = unpack_weights(wall_b, n_first, len(big), offs_b, rows_b)

    i_kr = MLA_Q_RANK + MLA_KV_RANK + MLA_ROPE
    w1p = jnp.concatenate([f_in1[:, i_kr:], f_in1[:, :i_kr], jnp.zeros((d, LANES - MLA_ROPE), BF16)], axis=1)
    qlat_blk = di // MLA_Q_RANK
    kvlat_blk = (di + MLA_Q_RANK) // MLA_KV_RANK
    kr_blk = (di + MLA_Q_RANK + MLA_KV_RANK) // LANES
    qk_w = HEAD_DIM + MLA_ROPE
    wqbp = jnp.pad(f_qb1.reshape(MLA_Q_RANK, nh, qk_w), ((0, 0), (0, 0), (0, MLA_QK - qk_w))).reshape(MLA_Q_RANK, nh * MLA_QK)
    n2 = f_in2.shape[1]
    w2p = jnp.pad(f_in2, ((0, 0), (0, 4 * di + LANES - n2)))
    b2p = jnp.pad(b_f2, (0, LANES - nh)).reshape(1, LANES)

    tabs = _rope_tables(positions[0])

    def sb_layer_bwd(dxn, dxnb, saved, ln, w_in, w_out, tag):
        xin, h, proj, o, lt, g = saved
        dgf = _matmul(dxnb, w_out, "nt", f"dgate_in_{tag}")
        dw_out = _matmul(g, dxnb, "tn", f"dw_out_{tag}")
        dq, dk, dv, dgate = _sb_bwd(proj, lt, o, dgf, nh, f"sb_bwd_{tag}")
        dproj = [dq, dk, dv, dgate]
        dh = _dh_w4(dproj, w_in, f"dh_{tag}")
        dw_in = _dw_w4(h, dproj, f"dw_in_{tag}")
        dx, dxb, dln = _rmsnorm_bwd(xin, row(ln), dh, f"norm_bwd_{tag}", dres=dxn)
        return dx, dxb, dln, dw_in, dw_out

    h1 = _rmsnorm_fwd(x1, row(ln1), "norm_fwd_l1")
    proj1 = _matmul(h1, w1p, "nn", "proj_in_l1")
    qn = _rmsnorm_fwd(proj1, row(q_norm1), "qnorm_fwd_l1", col_block=qlat_blk)
    kvn = _rmsnorm_fwd(proj1, row(kv_norm1), "kvnorm_fwd_l1", col_block=kvlat_blk)
    qpre = _matmul(qn, wqbp, "nn", "q_up_l1")
    kv1 = _matmul(kvn, f_kvb1, "nn", "kv_up_l1")
    qcat, kcat = _mla_assemble(qpre, kv1, proj1, kr_blk, tabs, nh, "mla_assemble_l1")
    mla_blk = (lambda hh: hh, lambda hh: hh, lambda hh: 2 * hh + 1)
    gate1_blk = lambda hh: hh
    o1, lse1, g1 = _sm_fwd(qcat, kcat, kv1, proj1, nh, MLA_QK, *mla_blk, gate1_blk, "chunk", "mla_fwd_l1")
    x2 = _matmul(g1, f_out1, "nn", "proj_out_l1", res=x1)

    h2 = _rmsnorm_fwd(x2, row(ln2), "norm_fwd_l2")
    proj2 = _matmul(h2, w2p, "nn", "proj_in_l2")
    f_blk = 4 * di // LANES
    cum = _forget_scan(proj2, f_blk, b2p, "forget_scan_l2")
    cum_h = cum[:, :nh].T
    t_sm = min(SM_TK, s)
    crow = cum_h.reshape(nh, s, 1)
    ccol = cum_h.reshape(nh, s // t_sm, t_sm)
    fg_blk = (lambda hh: hh, lambda hh: nh + hh, lambda hh: 2 * nh + hh)
    gate2_blk = lambda hh: 3 * nh + hh
    o2, lse2, g2 = _sm_fwd(proj2, proj2, proj2, proj2, nh, HEAD_DIM, *fg_blk, gate2_blk, "causal", "forget_fwd_l2", crow=crow, ccol=ccol)
    x3 = _matmul(g2, f_out2, "nn", "proj_out_l2", res=x2)

    x4, sv3, _ = sb_layer_fwd(x3, ln3, f_in3, f_out3, "l3")

    dx, dxb, d_final, loss_part = _loss_head(x4, row(final_norm), loss_target[0], "loss_head")
    dx, dxb, d_ln3, dw_in3, dw_out3 = sb_layer_bwd(dx, dxb, sv3, ln3, f_in3, f_out3, "l3")

    dgf2 = _matmul(dxb, f_out2, "nt", "dgate_in_l2")
    dw_out2 = _matmul(g2, dxb, "tn", "dw_out_l2")
    dq2, dk2, dv2, dgate2, dcc2, dcr2 = _sm_bwd(proj2, proj2, proj2, proj2, o2, dgf2, lse2, nh, HEAD_DIM, *fg_blk, gate2_blk, "causal", "forget_bwd_l2", crow=crow, ccol=ccol)
    lanes_of = lambda a: jnp.pad(a.reshape(nh, s).T, ((0, 0), (0, LANES - nh)))
    df2, d_bf = _forget_scan_bwd(lanes_of(dcc2), lanes_of(dcr2), proj2, f_blk, b2p, nh, "forget_scan_bwd_l2")
    dproj2 = jnp.concatenate([dq2.astype(BF16), dk2.astype(BF16), dv2.astype(BF16), dgate2, df2], axis=1)
    dh2 = _matmul(dproj2, w2p, "nt", "dh_l2")
    dw_in2 = _matmul(h2, dproj2, "tn", "dw_in_l2")[:, :n2]
    dx, dxb, d_ln2 = _rmsnorm_bwd(x2, row(ln2), dh2, "norm_bwd_l2", dres=dx)

    n_late = 6

    def reduce_start(lo, hi, dws, tag):
        offs, rows, rp = _pack_layout(shard_shapes[lo:hi])
        parts = []
        for g, (r, c), n, cs, is4 in zip(dws, shard_shapes[lo:hi], rows, col_sharded[lo:hi], as_gathered[lo:hi]):
            g4 = g if is4 else (g.reshape(r, N_CHIPS, c).transpose(1, 0, 2) if cs else g.reshape(N_CHIPS, r, c))
            parts.append(g4.reshape(N_CHIPS, n, PACK_W))
        gp = _pack_rows(parts, offs, rows, rp, (N_CHIPS,))
        return _pair_add(gp, _pair_exchange(gp, tag), idx, tag), offs, rows

    def reduce_finish(pair, slots, lo, hi, offs, rows, tag):
        gred = _pair_gather(_sum_slots(pair, slots, idx, tag), tag)
        return [gred[o : o + n, :].reshape(r, c) for (r, c), o, n in zip(shard_shapes[lo:hi], offs, rows)]

    pair_b, offs_gb, rows_gb = reduce_start(n_late, len(big), [dw_in2, dw_out2, dw_in3, dw_out3], "l23")

    dgf1 = _matmul(dxb, f_out1, "nt", "dgate_in_l1")
    dw_out1 = _matmul(g1, dxb, "tn", "dw_out_l1")
    dqc, dkc, dv1, dgate1, slots_b = _sm_bwd(qcat, kcat, kv1, proj1, o1, dgf1, lse1, nh, MLA_QK, *mla_blk, gate1_blk, "chunk", "mla_bwd_l1", exchange=pair_b)
    dqpre, dkv1, dkr = _mla_disassemble(dqc, dkc, dv1, tabs, nh, "mla_disassemble_l1")
    dqn = _matmul(dqpre, wqbp, "nt", "dqn_l1")
    dw_qbp = _matmul(qn, dqpre, "tn", "dw_qb_l1")
    dw_qb1 = dw_qbp.reshape(MLA_Q_RANK, nh, MLA_QK)[:, :, :qk_w].reshape(MLA_Q_RANK, nh * qk_w)
    dkvn = _matmul(dkv1, f_kvb1, "nt", "dkvn_l1")
    dw_kvb1 = _matmul(kvn, dkv1, "tn", "dw_kvb_l1")
    _, dqlat_b, d_qnorm = _rmsnorm_bwd(proj1, row(q_norm1), dqn, "qnorm_bwd_l1", col_block=qlat_blk)
    _, dkvlat_b, d_kvnorm = _rmsnorm_bwd(proj1, row(kv_norm1), dkvn, "kvnorm_bwd_l1", col_block=kvlat_blk)
    dproj1 = jnp.concatenate([dgate1, dqlat_b, dkvlat_b, dkr.astype(BF16)], axis=1)
    dh1 = _matmul(dproj1, w1p, "nt", "dh_l1")
    dw1p = _matmul(h1, dproj1, "tn", "dw_in_l1")
    dw_in1 = jnp.concatenate([dw1p[:, di : di + i_kr], dw1p[:, :di]], axis=1)
    dx, dxb, d_ln1 = _rmsnorm_bwd(x1, row(ln1), dh1, "norm_bwd_l1", dres=dx)

    dx, dxb, d_ln0, dw_in0, dw_out0 = sb_layer_bwd(dx, dxb, sv0, ln0, f_in0, f_out0, "l0")
    grad_x = dx.reshape(x.shape)

    pair_a, offs_ga, rows_ga = reduce_start(0, n_late, [dw_in0, dw_out0, dw_in1, dw_qb1, dw_kvb1, dw_out1], "l01")
    slots_a = _chip_exchange(pair_a, "l01")
    big_grads = reduce_finish(pair_a, slots_a, 0, n_late, offs_ga, rows_ga, "l01")
    big_grads += reduce_finish(pair_b, slots_b, n_late, len(big), offs_gb, rows_gb, "l23")

    small = [ln0, ln1, q_norm1, kv_norm1, ln2, b_f2, ln3, final_norm]
    small_g = [d_ln0[0], d_ln1[0], d_qnorm[0], d_kvnorm[0], d_ln2[0], d_bf[0, :nh], d_ln3[0], d_final[0]]
    n_small = SMALL_SHAPE[0] * SMALL_SHAPE[1]
    used = sum(v.shape[0] for v in small) + 1
    assert used <= n_small

    def pack_small(vs, last):
        return jnp.concatenate(list(vs) + [last, jnp.zeros((n_small - used,), F32)]).reshape(SMALL_SHAPE)

    sm_sum = _allreduce_small(pack_small(small_g, loss_part[0, :1]))
    flat = sm_sum.reshape(-1)
    loss = flat[used - 1]

    big_m = [m_w_in0, m_w_out0, m_w_in1, m_w_qb1, m_w_kvb1, m_w_out1, m_w_in2, m_w_out2, m_w_in3, m_w_out3]
    big_v = [v_w_in0, v_w_out0, v_w_in1, v_w_qb1, v_w_kvb1, v_w_out1, v_w_in2, v_w_out2, v_w_in3, v_w_out3]
    big_names = ["w_in0", "w_out0", "w_in1", "w_qb1", "w_kvb1", "w_out1", "w_in2", "w_out2", "w_in3", "w_out3"]
    big_upd = [_adamw(w, g, m, v, f"adamw_{nm}") for w, g, m, v, nm in zip(big, big_grads, big_m, big_v, big_names)]

    small_m = [m_ln0, m_ln1, m_q_norm1, m_kv_norm1, m_ln2, m_b_f2, m_ln3, m_final_norm]
    small_v = [v_ln0, v_ln1, v_q_norm1, v_kv_norm1, v_ln2, v_b_f2, v_ln3, v_final_norm]
    one = jnp.ones((1,), F32)
    sd, smn, svn = _adamw(pack_small(small, one), sm_sum, pack_small(small_m, one), pack_small(small_v, one), "adamw_small")

    def unpack_small(p):
        out, at = [], 0
        fl = p.reshape(-1)
        for v in small:
            out.append(fl[at : at + v.shape[0]])
            at += v.shape[0]
        return out

    sg_l, sd_l, sm_l, sv_l = unpack_small(sm_sum), unpack_small(sd), unpack_small(smn), unpack_small(svn)

    order = ["ln0", "w_in0", "w_out0", "ln1", "w_in1", "q_norm1", "w_qb1", "kv_norm1", "w_kvb1", "w_out1", "ln2", "w_in2", "b_f2", "w_out2", "ln3", "w_in3", "w_out3", "final_norm"]
    small_names = ["ln0", "ln1", "q_norm1", "kv_norm1", "ln2", "b_f2", "ln3", "final_norm"]
    grads, deltas, new_m, new_v = {}, {}, {}, {}
    for nm, g, (dl, mn, vn) in zip(big_names, big_grads, big_upd):
        grads[nm], deltas[nm], new_m[nm], new_v[nm] = g, dl, mn, vn
    for nm, g, dl, mn, vn in zip(small_names, sg_l, sd_l, sm_l, sv_l):
        grads[nm], deltas[nm], new_m[nm], new_v[nm] = g, dl, mn, vn
    return (loss, grad_x, *[grads[n] for n in order], *[deltas[n] for n in order], *[new_m[n] for n in order], *[new_v[n] for n in order])
```

```python
import functools

import jax
import jax.numpy as jnp
from jax import lax
from jax.experimental import pallas as pl
from jax.experimental.pallas import tpu as pltpu

F32 = jnp.float32
BF16 = jnp.bfloat16
EPS = 1e-6
NEG = -1e30
HEAD_DIM = 128
CHUNK_SHIFT = 6
MLA_Q_RANK = 256
MLA_KV_RANK = 128
MLA_ROPE = 64
MLA_QK = 256
ROPE_BASE = 10000.0
ADAM_LR = 0.001
ADAM_B1 = 0.9
ADAM_B2 = 0.999
ADAM_EPS = 1e-08
ADAM_WD = 0.01
ADAM_STEP = 10
VMEM_LIMIT_BYTES = 56 * 2**20
LANES = 128
PACK_W = 1024
PACK_TR = 128
SMALL_SHAPE = (8, 768)
MESH = pl.DeviceIdType.MESH
N_CHIPS = 4


def _pick(n, cands):
    for c in cands:
        if n % c == 0:
            return c
    return n


def _cparams(sem):
    return pltpu.CompilerParams(dimension_semantics=sem, vmem_limit_bytes=VMEM_LIMIT_BYTES)


def _dot(a, b, dims):
    dn = {"nn": (((1,), (0,)), ((), ())), "nt": (((1,), (1,)), ((), ())), "tn": (((0,), (0,)), ((), ()))}[dims]
    return lax.dot_general(a, b, dn, preferred_element_type=F32)


def _matmul(a, b, dims, name, res=None):
    if dims == "nn":
        (m, k), (k2, n) = a.shape, b.shape
    elif dims == "nt":
        (m, k), (n, k2) = a.shape, b.shape
    else:
        (k, m), (k2, n) = a.shape, b.shape
    assert k == k2, (a.shape, b.shape, dims)
    tm = _pick(m, (1024, 512, 256, 128))
    tn = _pick(n, (1024, 1664, 640, 512, 384, 256, 128))
    tk = _pick(k, (1024, 1664, 640, 512, 256, 128))
    nk = k // tk

    def body(*refs):
        if res is None:
            a_ref, b_ref, o_ref = refs
            r_ref = None
        else:
            a_ref, b_ref, r_ref, o_ref = refs
        kk = pl.program_id(2)
        p = _dot(a_ref[...].astype(BF16), b_ref[...].astype(BF16), dims)

        @pl.when(kk == 0)
        def _():
            o_ref[...] = p if r_ref is None else p + r_ref[...]

        @pl.when(kk > 0)
        def _():
            o_ref[...] += p

    a_spec = pl.BlockSpec((tk, tm), lambda i, j, kk: (kk, i)) if dims == "tn" else pl.BlockSpec((tm, tk), lambda i, j, kk: (i, kk))
    b_spec = pl.BlockSpec((tn, tk), lambda i, j, kk: (j, kk)) if dims == "nt" else pl.BlockSpec((tk, tn), lambda i, j, kk: (kk, j))
    o_spec = pl.BlockSpec((tm, tn), lambda i, j, kk: (i, j))
    in_specs = [a_spec, b_spec] + ([] if res is None else [o_spec])
    args = (a, b) + (() if res is None else (res,))
    return pl.pallas_call(
        body,
        grid=(m // tm, n // tn, nk),
        in_specs=in_specs,
        out_specs=o_spec,
        out_shape=jax.ShapeDtypeStruct((m, n), F32),
        compiler_params=_cparams(("parallel", "parallel", "arbitrary")),
        name=name,
    )(*args)


def _proj_w4(h, w4, name):
    (s, d), (ns, d2, c) = h.shape, w4.shape
    assert d == d2
    tm = _pick(s, (1024, 512, 256, 128))
    tn = _pick(c, (1024, 512, 256, 128))
    nbs = c // tn

    def body(a_ref, b_ref, o_ref):
        o_ref[...] = _dot(a_ref[...].astype(BF16), b_ref[...].astype(BF16), "nn")

    return pl.pallas_call(
        body,
        grid=(s // tm, ns * nbs),
        in_specs=[pl.BlockSpec((tm, d), lambda i, j: (i, 0)), pl.BlockSpec((None, d, tn), lambda i, j: (j // nbs, 0, j % nbs))],
        out_specs=pl.BlockSpec((tm, tn), lambda i, j: (i, j)),
        out_shape=jax.ShapeDtypeStruct((s, ns * c), F32),
        compiler_params=_cparams(("parallel", "parallel")),
        name=name,
    )(h, w4)


def _dh_w4(parts, w4, name):
    ns, d, c = w4.shape
    s = parts[0].shape[0]
    assert len(parts) == ns and all(p.shape == (s, c) for p in parts)
    tm = _pick(s, (1024, 512, 256, 128))
    tk = _pick(c, (1024, 512, 256, 128))
    nkp = c // tk

    def body(*refs):
        a_refs, b_ref, o_ref = refs[:ns], refs[ns], refs[ns + 1]
        kk = pl.program_id(1)
        for p in range(ns):

            @pl.when(kk // nkp == p)
            def _(p=p):
                pv = _dot(a_refs[p][...].astype(BF16), b_ref[...].astype(BF16), "nt")

                @pl.when(kk == 0)
                def _():
                    o_ref[...] = pv

                @pl.when(kk > 0)
                def _():
                    o_ref[...] += pv

    def a_spec(p):
        return pl.BlockSpec((tm, tk), lambda i, kk: (i, jnp.clip(kk - p * nkp, 0, nkp - 1)))

    return pl.pallas_call(
        body,
        grid=(s // tm, ns * nkp),
        in_specs=[a_spec(p) for p in range(ns)] + [pl.BlockSpec((None, d, tk), lambda i, kk: (kk // nkp, 0, kk % nkp))],
        out_specs=pl.BlockSpec((tm, d), lambda i, kk: (i, 0)),
        out_shape=jax.ShapeDtypeStruct((s, d), F32),
        compiler_params=_cparams(("parallel", "arbitrary")),
        name=name,
    )(*parts, w4)


def _dw_w4(h, parts, name):
    s, d = h.shape
    ns = len(parts)
    c = parts[0].shape[1]
    tn = _pick(c, (1024, 512, 256, 128))
    tk = _pick(s, (1024, 512, 256, 128))
    nbp = c // tn

    def body(*refs):
        a_ref, b_refs, o_ref = refs[0], refs[1 : 1 + ns], refs[1 + ns]
        j, kk = pl.program_id(0), pl.program_id(1)
        for p in range(ns):

            @pl.when(j // nbp == p)
            def _(p=p):
                pv = _dot(a_ref[...].astype(BF16), b_refs[p][...].astype(BF16), "tn")

                @pl.when(kk == 0)
                def _():
                    o_ref[...] = pv

                @pl.when(kk > 0)
                def _():
                    o_ref[...] += pv

    def b_spec(p):
        return pl.BlockSpec((tk, tn), lambda j, kk: (kk, jnp.clip(j - p * nbp, 0, nbp - 1)))

    return pl.pallas_call(
        body,
        grid=(ns * nbp, s // tk),
        in_specs=[pl.BlockSpec((tk, d), lambda j, kk: (kk, 0))] + [b_spec(p) for p in range(ns)],
        out_specs=pl.BlockSpec((None, d, tn), lambda j, kk: (j // nbp, 0, j % nbp)),
        out_shape=jax.ShapeDtypeStruct((ns, d, c), F32),
        compiler_params=_cparams(("parallel", "arbitrary")),
        name=name,
    )(h, *parts)


def _rmsnorm_fwd(x, g, name, col_block=0):
    s = x.shape[0]
    w = g.shape[1]
    tr = _pick(s, (512, 256, 128))

    def body(x_ref, g_ref, h_ref):
        xv = x_ref[...]
        r = lax.rsqrt(jnp.mean(xv * xv, axis=-1, keepdims=True) + EPS)
        h_ref[...] = ((xv * r) * g_ref[...]).astype(BF16)

    return pl.pallas_call(
        body,
        grid=(s // tr,),
        in_specs=[pl.BlockSpec((tr, w), lambda i: (i, col_block)), pl.BlockSpec((1, w), lambda i: (0, 0))],
        out_specs=pl.BlockSpec((tr, w), lambda i: (i, 0)),
        out_shape=jax.ShapeDtypeStruct((s, w), BF16),
        compiler_params=_cparams(("parallel",)),
        name=name,
    )(x, g)


def _rmsnorm_bwd(x, g, dh, name, col_block=0, dres=None):
    s = x.shape[0]
    w = g.shape[1]
    tr = _pick(s, (512, 256, 128))

    def body(*refs):
        if dres is None:
            x_ref, g_ref, dh_ref, dx_ref, dxb_ref, dg_ref = refs
        else:
            x_ref, g_ref, dh_ref, dr_ref, dx_ref, dxb_ref, dg_ref = refs
        i = pl.program_id(0)
        xv = x_ref[...]
        r = lax.rsqrt(jnp.mean(xv * xv, axis=-1, keepdims=True) + EPS)
        xh = xv * r
        dhv = dh_ref[...]
        dyg = dhv * g_ref[...]
        dx = r * (dyg - xh * jnp.mean(dyg * xh, axis=-1, keepdims=True))
        if dres is not None:
            dx = dx + dr_ref[...]
        dx_ref[...] = dx
        dxb_ref[...] = dx.astype(BF16)
        part = jnp.sum(dhv * xh, axis=0, keepdims=True)

        @pl.when(i == 0)
        def _():
            dg_ref[...] = part

        @pl.when(i > 0)
        def _():
            dg_ref[...] += part

    row = pl.BlockSpec((tr, w), lambda i: (i, 0))
    in_specs = [pl.BlockSpec((tr, w), lambda i: (i, col_block)), pl.BlockSpec((1, w), lambda i: (0, 0)), row]
    args = [x, g, dh]
    if dres is not None:
        in_specs.append(row)
        args.append(dres)
    return pl.pallas_call(
        body,
        grid=(s // tr,),
        in_specs=in_specs,
        out_specs=[row, row, pl.BlockSpec((1, w), lambda i: (0, 0))],
        out_shape=[jax.ShapeDtypeStruct((s, w), F32), jax.ShapeDtypeStruct((s, w), BF16), jax.ShapeDtypeStruct((1, w), F32)],
        compiler_params=_cparams(("arbitrary",)),
        name=name,
    )(*args)


def _loss_head(x, g, target, name):
    s, d = x.shape
    tr = _pick(s, (512, 256, 128))

    def body(x_ref, g_ref, t_ref, dx_ref, dxb_ref, dg_ref, loss_ref):
        i = pl.program_id(0)
        xv = x_ref[...]
        gv = g_ref[...]
        r = lax.rsqrt(jnp.mean(xv * xv, axis=-1, keepdims=True) + EPS)
        xh = xv * r
        err = xh * gv - t_ref[...]
        lpart = 0.5 * jnp.sum(jnp.mean(err * err, axis=-1, keepdims=True), axis=0, keepdims=True)
        dy = err / d
        dyg = dy * gv
        dx = r * (dyg - xh * jnp.mean(dyg * xh, axis=-1, keepdims=True))
        dx_ref[...] = dx
        dxb_ref[...] = dx.astype(BF16)
        part = jnp.sum(dy * xh, axis=0, keepdims=True)
        lrow = jnp.broadcast_to(lpart, (1, LANES))

        @pl.when(i == 0)
        def _():
            dg_ref[...] = part
            loss_ref[...] = lrow

        @pl.when(i > 0)
        def _():
            dg_ref[...] += part
            loss_ref[...] += lrow

    row = pl.BlockSpec((tr, d), lambda i: (i, 0))
    vec = pl.BlockSpec((1, d), lambda i: (0, 0))
    return pl.pallas_call(
        body,
        grid=(s // tr,),
        in_specs=[row, vec, row],
        out_specs=[row, row, vec, pl.BlockSpec((1, LANES), lambda i: (0, 0))],
        out_shape=[
            jax.ShapeDtypeStruct((s, d), F32),
            jax.ShapeDtypeStruct((s, d), BF16),
            jax.ShapeDtypeStruct((1, d), F32),
            jax.ShapeDtypeStruct((1, LANES), F32),
        ],
        compiler_params=_cparams(("arbitrary",)),
        name=name,
    )(x, g, target)


def _sigmoid(x):
    return 1.0 / (1.0 + jnp.exp(-x))


def _gate_out(gate_ref, o):
    gt = gate_ref[...]
    return (o * (gt * _sigmoid(gt))).astype(BF16)


def _gate_grads(gate_ref, o_ref, dg_ref, dgate_ref):
    gt = gate_ref[...]
    sg = _sigmoid(gt)
    dgv = dg_ref[...]
    dgate_ref[...] = (dgv * o_ref[...] * (sg * (1.0 + gt * (1.0 - sg)))).astype(BF16)
    return dgv * (gt * sg)


def _iotas(tq, tk):
    return lax.broadcasted_iota(jnp.int32, (tq, tk), 0), lax.broadcasted_iota(jnp.int32, (tq, tk), 1)


def _softplus(s):
    return jnp.maximum(s, 0.0) + jnp.log(1.0 + jnp.exp(-jnp.abs(s)))


def _split2(v):
    hi = v.astype(BF16)
    lo = (v - hi.astype(F32)).astype(BF16)
    return hi, lo


def _cat2(v):
    return jnp.concatenate(_split2(v), axis=1)


def _tri2(keep):
    m = keep.astype(BF16)
    return jnp.concatenate([m, m], axis=0)


def _split3(v):
    a = v.astype(BF16)
    r1 = v - a.astype(F32)
    b = r1.astype(BF16)
    c = (r1 - b.astype(F32)).astype(BF16)
    return a, b, c


SB_TQ = 1024
SB_TK = 128
SB_UNROLL = 8


def _sb_fwd(proj, n_heads, name, gather=None):
    s = proj.shape[0]
    d = HEAD_DIM
    t = min(SB_TQ, s)
    tk = min(SB_TK, s)
    r = t // tk
    un = SB_UNROLL if r % SB_UNROLL == 0 else 1
    nq = s // t
    scale = d**-0.5

    def body(*refs):
        if gather is None:
            q_ref, k_ref, v_ref, gate_ref, o_ref, lt_ref, g_ref, kb_ref, vb_ref = refs
        else:
            q_ref, k_ref, v_ref, gate_ref, w_ref, o_ref, lt_ref, g_ref, wall_ref, kb_ref, vb_ref, send_sems, recv_sems = refs
        i = pl.program_id(1)

        if gather is not None:
            hh = pl.program_id(0)
            send, forward, finish = _gather_steps(w_ref, wall_ref, send_sems, recv_sems)
            pl.when((hh == 0) & (i == 0))(send)
            pl.when((hh == n_heads // 2) & (i == 0))(forward)

        @pl.when(i == 0)
        def _():
            kb_ref[...] = k_ref[...].astype(BF16)
            vb_ref[...] = v_ref[...].astype(BF16)

        q = (q_ref[...] * scale).astype(BF16)
        rows, cols = _iotas(t, tk)
        trows, tcols = _iotas(tk, tk)
        tri = (trows > tcols).astype(BF16)

        def block(kb, cl, acc, diag):
            k0 = pl.multiple_of(kb * tk, tk)
            sc = _dot(q, kb_ref[pl.ds(k0, tk), :], "nt")
            sp = _softplus(sc)
            ls = -sp
            if diag is not None:
                strict = cols + diag * tk < rows
                ls = jnp.where(strict, ls, 0.0)
            hi, lo = _split2(ls)
            later = _dot(hi, tri, "nn") + _dot(lo, tri, "nn")
            w = jnp.exp((sc - sp) + later + cl)
            if diag is not None:
                w = jnp.where(strict, w, 0.0)
            acc = acc + _dot(w.astype(BF16), vb_ref[pl.ds(k0, tk), :], "nn")
            return cl + jnp.sum(ls, axis=1, keepdims=True), acc

        cl, acc = jnp.zeros((t, 1), F32), jnp.zeros((t, d), F32)
        for dd in reversed(range(r)):
            cl, acc = block(i * r + dd, cl, acc, dd)

        def loop(j, carry):
            for u in range(un):
                carry = block(i * r - 1 - (un * j + u), carry[0], carry[1], None)
            return carry

        cl, acc = lax.fori_loop(0, (i * r) // un, loop, (cl, acc))
        o_ref[...] = acc
        lt_ref[...] = cl
        g_ref[...] = _gate_out(gate_ref, acc)

        if gather is not None:
            pl.when((hh == n_heads - 1) & (i == nq - 1))(finish)

    h = n_heads
    qblk = pl.BlockSpec((t, d), lambda hh, i: (i, hh))
    in_specs = [
        qblk,
        pl.BlockSpec((s, d), lambda hh, i: (0, h + hh)),
        pl.BlockSpec((s, d), lambda hh, i: (0, 2 * h + hh)),
        pl.BlockSpec((t, d), lambda hh, i: (i, 3 * h + hh)),
    ]
    out_specs = [qblk, pl.BlockSpec((None, t, 1), lambda hh, i: (hh, i, 0)), qblk]
    out_shape = [jax.ShapeDtypeStruct((s, h * d), F32), jax.ShapeDtypeStruct((h, s, 1), F32), jax.ShapeDtypeStruct((s, h * d), BF16)]
    scratch = [pltpu.VMEM((s, d), BF16), pltpu.VMEM((s, d), BF16)]
    args = [proj, proj, proj, proj]
    if gather is not None:
        in_specs.append(ANY)
        out_specs.append(ANY)
        out_shape.append(jax.ShapeDtypeStruct((N_CHIPS,) + gather.shape, gather.dtype))
        scratch += GATHER_SEMS
        args.append(gather)
    return pl.pallas_call(
        body,
        grid=(h, nq),
        in_specs=in_specs,
        out_specs=out_specs,
        out_shape=out_shape,
        scratch_shapes=scratch,
        compiler_params=_cparams(("arbitrary", "arbitrary")),
        name=name,
    )(*args)


def _sb_bwd(proj, ltot, o, dg, n_heads, name):
    s = proj.shape[0]
    d = HEAD_DIM
    t = min(SB_TQ, s)
    tk = min(SB_TK, s)
    r = t // tk
    un = SB_UNROLL if r % SB_UNROLL == 0 else 1
    nq = s // t
    scale = d**-0.5

    def body(q_ref, k_ref, v_ref, gate_ref, lt_ref, o_ref, dg_ref, dq_ref, dko_ref, dvo_ref, dgate_ref, kb_ref, vb_ref, dk_ref, dv_ref):
        i = pl.program_id(1)

        @pl.when(i == 0)
        def _():
            kb_ref[...] = k_ref[...].astype(BF16)
            vb_ref[...] = v_ref[...].astype(BF16)
            dk_ref[...] = jnp.zeros_like(dk_ref)
            dv_ref[...] = jnp.zeros_like(dv_ref)

        dob = _gate_grads(gate_ref, o_ref, dg_ref, dgate_ref).astype(BF16)
        q = (q_ref[...] * scale).astype(BF16)
        ltv = lt_ref[...]
        rows, cols = _iotas(t, tk)
        trows, tcols = _iotas(tk, tk)
        upto = _tri2(trows <= tcols)
        before = _tri2(trows < tcols)

        def block(kb, cp, cc, dq, diag):
            k0 = pl.multiple_of(kb * tk, tk)
            kk = kb_ref[pl.ds(k0, tk), :]
            sc = _dot(q, kk, "nt")
            sp = _softplus(sc)
            ls = -sp
            if diag is not None:
                strict = cols + diag * tk < rows
                ls = jnp.where(strict, ls, 0.0)
            prefix = _dot(_cat2(ls), upto, "nn") + cp
            lsig = sc - sp
            w = jnp.exp(lsig + (ltv - prefix))
            if diag is not None:
                w = jnp.where(strict, w, 0.0)
            da = _dot(dob, vb_ref[pl.ds(k0, tk), :], "nt") * w
            csum = _dot(_cat2(da), before, "nn") + cc
            beta = jnp.exp(lsig)
            dz = da * (1.0 - beta) - beta * csum
            if diag is not None:
                dz = jnp.where(strict, dz, 0.0)
            dzb = dz.astype(BF16)
            dq = dq + _dot(dzb, kk, "nn")
            dk_ref[pl.ds(k0, tk), :] += _dot(dzb, q, "tn")
            dv_ref[pl.ds(k0, tk), :] += _dot(w.astype(BF16), dob, "tn")
            return cp + jnp.sum(ls, axis=1, keepdims=True), cc + jnp.sum(da, axis=1, keepdims=True), dq

        def loop(j, carry):
            for u in range(un):
                carry = block(un * j + u, carry[0], carry[1], carry[2], None)
            return carry

        z1 = jnp.zeros((t, 1), F32)
        cp, cc, dq = lax.fori_loop(0, (i * r) // un, loop, (z1, z1, jnp.zeros((t, d), F32)))
        for dd in range(r):
            cp, cc, dq = block(i * r + dd, cp, cc, dq, dd)
        dq_ref[...] = (dq * scale).astype(BF16)

        @pl.when(i == nq - 1)
        def _():
            dko_ref[...] = dk_ref[...].astype(BF16)
            dvo_ref[...] = dv_ref[...].astype(BF16)

    h = n_heads
    qblk = pl.BlockSpec((t, d), lambda hh, i: (i, hh))
    full = pl.BlockSpec((s, d), lambda hh, i: (0, hh))
    shp = jax.ShapeDtypeStruct((s, h * d), BF16)
    return pl.pallas_call(
        body,
        grid=(h, nq),
        in_specs=[
            qblk,
            pl.BlockSpec((s, d), lambda hh, i: (0, h + hh)),
            pl.BlockSpec((s, d), lambda hh, i: (0, 2 * h + hh)),
            pl.BlockSpec((t, d), lambda hh, i: (i, 3 * h + hh)),
            pl.BlockSpec((None, t, 1), lambda hh, i: (hh, i, 0)),
            qblk,
            qblk,
        ],
        out_specs=[qblk, full, full, qblk],
        out_shape=[shp, shp, shp, shp],
        scratch_shapes=[pltpu.VMEM((s, d), BF16), pltpu.VMEM((s, d), BF16), pltpu.VMEM((s, d), F32), pltpu.VMEM((s, d), F32)],
        compiler_params=_cparams(("arbitrary", "arbitrary")),
        name=name,
    )(proj, proj, proj, proj, ltot, o, dg)


SM_TQ = 1024
SM_TK = 512
SM_UNROLL = 2
SM_TK_WIDE = 256
SM_UNROLL_WIDE = 4


def _allowed(mode, rows, cols, q0, k0):
    r = rows + q0
    c = cols + k0
    if mode == "causal":
        return c <= r
    return (c >> CHUNK_SHIFT) <= (r >> CHUNK_SHIFT)


def _sm_fwd(q_arr, k_arr, v_arr, gate_arr, n_heads, dqk, q_blk, k_blk, v_blk, gate_blk, mode, name, crow=None, ccol=None):
    s = q_arr.shape[0]
    dv = HEAD_DIM
    t = min(SM_TQ, s)
    tk = min(SM_TK, s)
    r = t // tk
    un = SM_UNROLL if r % SM_UNROLL == 0 else 1
    bias = crow is not None
    scale = (HEAD_DIM if mode == "causal" else HEAD_DIM + MLA_ROPE) ** -0.5

    def body(*refs):
        if bias:
            q_ref, k_ref, v_ref, gate_ref, cr_ref, cc_ref, o_ref, lse_ref, g_ref, kb_ref, vb_ref = refs
        else:
            q_ref, k_ref, v_ref, gate_ref, o_ref, lse_ref, g_ref, kb_ref, vb_ref = refs
        i = pl.program_id(1)

        @pl.when(i == 0)
        def _():
            kb_ref[...] = k_ref[...].astype(BF16)
            vb_ref[...] = v_ref[...].astype(BF16)

        q = (q_ref[...] * scale).astype(BF16)
        q0 = i * t
        rows, cols = _iotas(t, tk)
        crv = cr_ref[...] if bias else None

        def block(kb, m, l, acc, masked):
            k0 = pl.multiple_of(kb * tk, tk)
            sc = _dot(q, kb_ref[pl.ds(k0, tk), :], "nt")
            if bias:
                sc = sc + crv - cc_ref[pl.ds(kb, 1), :]
            if masked:
                sc = jnp.where(_allowed(mode, rows, cols, q0, k0), sc, NEG)
            m_new = jnp.maximum(m, jnp.max(sc, axis=1, keepdims=True))
            alpha = jnp.exp(m - m_new)
            p = jnp.exp(sc - m_new)
            l = alpha * l + jnp.sum(p, axis=1, keepdims=True)
            acc = alpha * acc + _dot(p.astype(BF16), vb_ref[pl.ds(k0, tk), :], "nn")
            return m_new, l, acc

        def loop(j, carry):
            for u in range(un):
                carry = block(un * j + u, carry[0], carry[1], carry[2], False)
            return carry

        init = (jnp.full((t, 1), NEG, F32), jnp.zeros((t, 1), F32), jnp.zeros((t, dv), F32))
        m, l, acc = lax.fori_loop(0, (i * r) // un, loop, init)
        for dd in range(r):
            m, l, acc = block(i * r + dd, m, l, acc, True)
        ov = acc / l
        o_ref[...] = ov
        lse_ref[...] = m + jnp.log(l)
        g_ref[...] = _gate_out(gate_ref, ov)

    h = n_heads
    oblk = pl.BlockSpec((t, dv), lambda hh, i: (i, hh))
    in_specs = [
        pl.BlockSpec((t, dqk), lambda hh, i: (i, q_blk(hh))),
        pl.BlockSpec((s, dqk), lambda hh, i: (0, k_blk(hh))),
        pl.BlockSpec((s, dv), lambda hh, i: (0, v_blk(hh))),
        pl.BlockSpec((t, dv), lambda hh, i: (i, gate_blk(hh))),
    ]
    args = [q_arr, k_arr, v_arr, gate_arr]
    if bias:
        in_specs += [pl.BlockSpec((None, t, 1), lambda hh, i: (hh, i, 0)), pl.BlockSpec((None, s // tk, tk), lambda hh, i: (hh, 0, 0))]
        args += [crow, ccol]
    return pl.pallas_call(
        body,
        grid=(h, s // t),
        in_specs=in_specs,
        out_specs=[oblk, pl.BlockSpec((None, t, 1), lambda hh, i: (hh, i, 0)), oblk],
        out_shape=[jax.ShapeDtypeStruct((s, h * dv), F32), jax.ShapeDtypeStruct((h, s, 1), F32), jax.ShapeDtypeStruct((s, h * dv), BF16)],
        scratch_shapes=[pltpu.VMEM((s, dqk), BF16), pltpu.VMEM((s, dv), BF16)],
        compiler_params=_cparams(("arbitrary", "arbitrary")),
        name=name,
    )(*args)


def _sm_bwd(q_arr, k_arr, v_arr, gate_arr, o, dg, lse, n_heads, dqk, q_blk, k_blk, v_blk, gate_blk, mode, name, crow=None, ccol=None, exchange=None):
    s = q_arr.shape[0]
    dv = HEAD_DIM
    t = min(SM_TQ, s)
    tk_cfg, un_cfg = (SM_TK_WIDE, SM_UNROLL_WIDE) if dqk > HEAD_DIM else (SM_TK, SM_UNROLL)
    tk = min(tk_cfg, s)
    r = t // tk
    un = un_cfg if r % un_cfg == 0 else 1
    nq = s // t
    bias = crow is not None
    assert not (bias and exchange is not None)
    scale = (HEAD_DIM if mode == "causal" else HEAD_DIM + MLA_ROPE) ** -0.5

    def body(*refs):
        if bias:
            q_ref, k_ref, v_ref, gate_ref, o_ref, dg_ref, lse_ref, cr_ref, cc_ref, dq_ref, dk_ref, dv_ref, dgate_ref, dcc_ref, dcr_ref, kb_ref, vb_ref = refs
        elif exchange is not None:
            q_ref, k_ref, v_ref, gate_ref, o_ref, dg_ref, lse_ref, sp_ref, dq_ref, dk_ref, dv_ref, dgate_ref, slots_ref, kb_ref, vb_ref, send_sems, recv_sems = refs
        else:
            q_ref, k_ref, v_ref, gate_ref, o_ref, dg_ref, lse_ref, dq_ref, dk_ref, dv_ref, dgate_ref, kb_ref, vb_ref = refs
        i = pl.program_id(1)

        if exchange is not None:
            hh = pl.program_id(0)
            send, finish = _exchange_steps(sp_ref, slots_ref, send_sems, recv_sems)
            pl.when((hh == 0) & (i == 0))(send)

        @pl.when(i == 0)
        def _():
            kb_ref[...] = k_ref[...].astype(BF16)
            vb_ref[...] = v_ref[...].astype(BF16)
            dk_ref[...] = jnp.zeros_like(dk_ref)
            dv_ref[...] = jnp.zeros_like(dv_ref)
            if bias:
                dcc_ref[...] = jnp.zeros_like(dcc_ref)

        q = (q_ref[...] * scale).astype(BF16)
        dov = _gate_grads(gate_ref, o_ref, dg_ref, dgate_ref)
        dob = dov.astype(BF16)
        dsum = jnp.sum(dov * o_ref[...], axis=1, keepdims=True)
        lsev = lse_ref[...]
        q0 = i * t
        rows, cols = _iotas(t, tk)
        crv = cr_ref[...] if bias else None

        def block(kb, dq, dr, masked):
            k0 = pl.multiple_of(kb * tk, tk)
            kk = kb_ref[pl.ds(k0, tk), :]
            sc = _dot(q, kk, "nt")
            if bias:
                sc = sc + crv - cc_ref[pl.ds(kb, 1), :]
            if masked:
                sc = jnp.where(_allowed(mode, rows, cols, q0, k0), sc, NEG)
            p = jnp.exp(sc - lsev)
            dz = p * (_dot(dob, vb_ref[pl.ds(k0, tk), :], "nt") - dsum)
            dzb = dz.astype(BF16)
            dk_ref[pl.ds(k0, tk), :] += _dot(dzb, q, "tn")
            dv_ref[pl.ds(k0, tk), :] += _dot(p.astype(BF16), dob, "tn")
            if bias:
                dcc_ref[pl.ds(kb, 1), :] -= jnp.sum(dz, axis=0, keepdims=True)
                dr = dr + jnp.sum(dz, axis=1, keepdims=True)
            return dq + _dot(dzb, kk, "nn"), dr

        def loop(j, carry):
            for u in range(un):
                carry = block(un * j + u, carry[0], carry[1], False)
            return carry

        dq, dr = lax.fori_loop(0, (i * r) // un, loop, (jnp.zeros((t, dqk), F32), jnp.zeros((t, 1), F32)))
        for dd in range(r):
            dq, dr = block(i * r + dd, dq, dr, True)
        dq_ref[...] = dq * scale
        if bias:
            dcr_ref[...] = dr
        if exchange is not None:
            pl.when((hh == n_heads - 1) & (i == nq - 1))(finish)

    h = n_heads
    qblk = pl.BlockSpec((t, dqk), lambda hh, i: (i, q_blk(hh)))
    oblk = pl.BlockSpec((t, dv), lambda hh, i: (i, hh))
    vec = pl.BlockSpec((None, t, 1), lambda hh, i: (hh, i, 0))
    in_specs = [
        qblk,
        pl.BlockSpec((s, dqk), lambda hh, i: (0, k_blk(hh))),
        pl.BlockSpec((s, dv), lambda hh, i: (0, v_blk(hh))),
        pl.BlockSpec((t, dv), lambda hh, i: (i, gate_blk(hh))),
        oblk,
        oblk,
        vec,
    ]
    args = [q_arr, k_arr, v_arr, gate_arr, o, dg, lse]
    out_specs = [
        pl.BlockSpec((t, dqk), lambda hh, i: (i, hh)),
        pl.BlockSpec((s, dqk), lambda hh, i: (0, hh)),
        pl.BlockSpec((s, dv), lambda hh, i: (0, hh)),
        oblk,
    ]
    out_shape = [
        jax.ShapeDtypeStruct((s, h * dqk), F32),
        jax.ShapeDtypeStruct((s, h * dqk), F32),
        jax.ShapeDtypeStruct((s, h * dv), F32),
        jax.ShapeDtypeStruct((s, h * dv), BF16),
    ]
    if bias:
        ccs = pl.BlockSpec((None, s // tk, tk), lambda hh, i: (hh, 0, 0))
        in_specs += [vec, ccs]
        args += [crow, ccol]
        out_specs += [ccs, vec]
        out_shape += [jax.ShapeDtypeStruct((h, s // tk, tk), F32), jax.ShapeDtypeStruct((h, s, 1), F32)]
    scratch = [pltpu.VMEM((s, dqk), BF16), pltpu.VMEM((s, dv), BF16)]
    if exchange is not None:
        in_specs.append(ANY)
        args.append(exchange)
        out_specs.append(ANY)
        out_shape.append(jax.ShapeDtypeStruct(exchange.shape, exchange.dtype))
        scratch += EXCHANGE_SEMS
    return pl.pallas_call(
        body,
        grid=(h, nq),
        in_specs=in_specs,
        out_specs=out_specs,
        out_shape=out_shape,
        scratch_shapes=scratch,
        compiler_params=_cparams(("arbitrary", "arbitrary")),
        name=name,
    )(*args)


def _rope_tables(pos):
    half = MLA_ROPE // 2
    inv_freq = ROPE_BASE ** (-jnp.arange(0, MLA_ROPE, 2, dtype=F32) / MLA_ROPE)
    ang = pos.astype(F32)[:, None] * inv_freq
    cos, sin = jnp.cos(ang), jnp.sin(ang)
    z = lambda n: jnp.zeros((pos.shape[0], n), F32)
    tc = jnp.concatenate([cos, cos, z(LANES - 2 * half)], axis=1)
    ta = jnp.concatenate([-sin, z(LANES - half)], axis=1)
    tb = jnp.concatenate([z(half), sin, z(LANES - 2 * half)], axis=1)
    return tc, ta, tb


def _rot(v, tc, ta, tb, sign):
    half = MLA_ROPE // 2
    return v * tc + sign * (pltpu.roll(v, LANES - half, 1) * ta + pltpu.roll(v, half, 1) * tb)


def _mla_assemble(qpre, kv, proj1, kr_blk, tabs, n_heads, name):
    s = qpre.shape[0]
    tr = LANES
    wd = n_heads * MLA_QK

    def body(qp_ref, kv_ref, kr_ref, tc_ref, ta_ref, tb_ref, qc_ref, kc_ref):
        tc, ta, tb = tc_ref[...], ta_ref[...], tb_ref[...]
        kr = _rot(kr_ref[...], tc, ta, tb, 1.0)
        for hh in range(n_heads):
            lo, mid, hi = hh * MLA_QK, hh * MLA_QK + LANES, (hh + 1) * MLA_QK
            qc_ref[:, lo:mid] = qp_ref[:, lo:mid]
            qc_ref[:, mid:hi] = _rot(qp_ref[:, mid:hi], tc, ta, tb, 1.0)
            kc_ref[:, lo:mid] = kv_ref[:, lo:mid]
            kc_ref[:, mid:hi] = kr

    tab = pl.BlockSpec((tr, LANES), lambda i: (i, 0))
    wide = pl.BlockSpec((tr, wd), lambda i: (i, 0))
    shp = jax.ShapeDtypeStruct((s, wd), F32)
    return pl.pallas_call(
        body,
        grid=(s // tr,),
        in_specs=[wide, wide, pl.BlockSpec((tr, LANES), lambda i: (i, kr_blk)), tab, tab, tab],
        out_specs=[wide, wide],
        out_shape=[shp, shp],
        compiler_params=_cparams(("parallel",)),
        name=name,
    )(qpre, kv, proj1, *tabs)


def _mla_disassemble(dqcat, dkcat, dv, tabs, n_heads, name):
    s = dqcat.shape[0]
    tr = LANES
    wd = n_heads * MLA_QK

    def body(dq_ref, dk_ref, dv_ref, tc_ref, ta_ref, tb_ref, dqp_ref, dkv_ref, dkr_ref):
        tc, ta, tb = tc_ref[...], ta_ref[...], tb_ref[...]
        acc = jnp.zeros((tr, LANES), F32)
        for hh in range(n_heads):
            lo, mid, hi = hh * MLA_QK, hh * MLA_QK + LANES, (hh + 1) * MLA_QK
            dqp_ref[:, lo:mid] = dq_ref[:, lo:mid].astype(BF16)
            dqp_ref[:, mid:hi] = _rot(dq_ref[:, mid:hi], tc, ta, tb, -1.0).astype(BF16)
            dkv_ref[:, lo:mid] = dk_ref[:, lo:mid].astype(BF16)
            dkv_ref[:, mid:hi] = dv_ref[:, hh * LANES : (hh + 1) * LANES].astype(BF16)
            acc = acc + dk_ref[:, mid:hi]
        dkr_ref[...] = _rot(acc, tc, ta, tb, -1.0)

    tab = pl.BlockSpec((tr, LANES), lambda i: (i, 0))
    wide = pl.BlockSpec((tr, wd), lambda i: (i, 0))
    shp = jax.ShapeDtypeStruct((s, wd), BF16)
    return pl.pallas_call(
        body,
        grid=(s // tr,),
        in_specs=[wide, wide, pl.BlockSpec((tr, n_heads * LANES), lambda i: (i, 0)), tab, tab, tab],
        out_specs=[wide, wide, tab],
        out_shape=[shp, shp, jax.ShapeDtypeStruct((s, LANES), F32)],
        compiler_params=_cparams(("parallel",)),
        name=name,
    )(dqcat, dkcat, dv, *tabs)


def _forget_scan(proj2, f_blk, bias, name):
    s = proj2.shape[0]
    n = LANES

    def body(f_ref, b_ref, c_ref):
        rows, cols = _iotas(n, n)
        tri = (cols <= rows).astype(BF16)

        def step(j, carry):
            r0 = pl.multiple_of(j * n, n)
            f = f_ref[pl.ds(r0, n), :] + b_ref[...]
            lf = jnp.minimum(f, 0.0) - jnp.log1p(jnp.exp(-jnp.abs(f)))
            a, b, c = _split3(lf)
            cs = _dot(tri, a, "nn") + _dot(tri, b, "nn") + _dot(tri, c, "nn") + carry
            c_ref[pl.ds(r0, n), :] = cs
            return cs[n - 1 : n, :]

        lax.fori_loop(0, s // n, step, jnp.zeros((1, n), F32))

    return pl.pallas_call(
        body,
        grid=(1,),
        in_specs=[pl.BlockSpec((s, n), lambda i: (0, f_blk)), pl.BlockSpec((1, n), lambda i: (0, 0))],
        out_specs=pl.BlockSpec((s, n), lambda i: (0, 0)),
        out_shape=jax.ShapeDtypeStruct((s, n), F32),
        compiler_params=_cparams(("arbitrary",)),
        name=name,
    )(proj2, bias)


def _forget_scan_bwd(dc_col, dc_row, proj2, f_blk, bias, n_heads, name):
    s = proj2.shape[0]
    n = LANES
    nb = s // n

    def body(dcc_ref, dcr_ref, f_ref, b_ref, df_ref, db_ref):
        rows, cols = _iotas(n, n)
        tri = (cols >= rows).astype(BF16)
        live = cols < n_heads

        def step(j, carry):
            acc, dbv = carry
            r0 = pl.multiple_of((nb - 1 - j) * n, n)
            a, b, c = _split3(dcc_ref[pl.ds(r0, n), :] + dcr_ref[pl.ds(r0, n), :])
            dl = _dot(tri, a, "nn") + _dot(tri, b, "nn") + _dot(tri, c, "nn") + acc
            f = f_ref[pl.ds(r0, n), :] + b_ref[...]
            df = jnp.where(live, dl / (1.0 + jnp.exp(f)), 0.0)
            df_ref[pl.ds(r0, n), :] = df.astype(BF16)
            return dl[0:1, :], dbv + jnp.sum(df, axis=0, keepdims=True)

        z = jnp.zeros((1, n), F32)
        _, dbv = lax.fori_loop(0, nb, step, (z, z))
        db_ref[...] = dbv

    return pl.pallas_call(
        body,
        grid=(1,),
        in_specs=[
            pl.BlockSpec((s, n), lambda i: (0, 0)),
            pl.BlockSpec((s, n), lambda i: (0, 0)),
            pl.BlockSpec((s, n), lambda i: (0, f_blk)),
            pl.BlockSpec((1, n), lambda i: (0, 0)),
        ],
        out_specs=[pl.BlockSpec((s, n), lambda i: (0, 0)), pl.BlockSpec((1, n), lambda i: (0, 0))],
        out_shape=[jax.ShapeDtypeStruct((s, n), BF16), jax.ShapeDtypeStruct((1, n), F32)],
        compiler_params=_cparams(("arbitrary",)),
        name=name,
    )(dc_col, dc_row, proj2, bias)


def _adamw(w, g, m, v, name):
    r, c = w.shape
    tr = _pick(r, (256, 128, 64, 32, 16, 8))
    c1 = 1.0 - ADAM_B1**ADAM_STEP
    c2 = 1.0 - ADAM_B2**ADAM_STEP

    def body(w_ref, g_ref, m_ref, v_ref, d_ref, mo_ref, vo_ref):
        gv = g_ref[...]
        mn = ADAM_B1 * m_ref[...] + (1.0 - ADAM_B1) * gv
        vn = ADAM_B2 * v_ref[...] + (1.0 - ADAM_B2) * (gv * gv)
        mo_ref[...] = mn
        vo_ref[...] = vn
        d_ref[...] = -ADAM_LR * ((mn / c1) / (jnp.sqrt(vn / c2) + ADAM_EPS) + ADAM_WD * w_ref[...])

    blk = pl.BlockSpec((tr, c), lambda i: (i, 0))
    shp = jax.ShapeDtypeStruct((r, c), F32)
    return pl.pallas_call(
        body,
        grid=(r // tr,),
        in_specs=[blk, blk, blk, blk],
        out_specs=[blk, blk, blk],
        out_shape=[shp, shp, shp],
        compiler_params=_cparams(("parallel",)),
        name=name,
    )(w, g, m, v)


def _mesh_pos():
    return lax.axis_index("x"), lax.axis_index("y"), lax.axis_index("c")


def _other_chips(x, y):
    return [(1 - x, y), (x, 1 - y), (1 - x, 1 - y)]


ANY = pl.BlockSpec(memory_space=pl.ANY)


GATHER_SEMS = [pltpu.SemaphoreType.DMA((6,)), pltpu.SemaphoreType.DMA((6,))]
EXCHANGE_SEMS = [pltpu.SemaphoreType.DMA((3,)), pltpu.SemaphoreType.DMA((3,))]


def _gather_steps(w_ref, out_ref, send_sems, recv_sems):
    half = w_ref.shape[0] // 2
    x, y, c = _mesh_pos()
    me = 2 * x + y
    chips = _other_chips(x, y)

    def region(chip, hc):
        return out_ref.at[chip, pl.ds(hc * half, half), :]

    def copy(k, src, dst, to):
        return pltpu.make_async_remote_copy(
            src_ref=src, dst_ref=dst, send_sem=send_sems.at[k], recv_sem=recv_sems.at[k], device_id=to, device_id_type=MESH
        )

    first = [copy(j, w_ref.at[pl.ds(c * half, half), :], region(me, c), (cx, cy, c)) for j, (cx, cy) in enumerate(chips)]
    passed = [copy(3 + j, region(2 * cx + cy, c), region(2 * cx + cy, c), (x, y, 1 - c)) for j, (cx, cy) in enumerate(chips)]

    def send():
        for cp in first:
            cp.start()

    def forward():
        for j, (cx, cy) in enumerate(chips):
            copy(j, region(2 * cx + cy, c), region(2 * cx + cy, c), (x, y, c)).wait_recv()
            passed[j].start()

    def finish():
        for j, (cx, cy) in enumerate(chips):
            copy(3 + j, region(2 * cx + cy, 1 - c), region(2 * cx + cy, 1 - c), (x, y, c)).wait_recv()
        for cp in first + passed:
            cp.wait_send()

    return send, forward, finish


def _gather_weights(wp, tag):
    rp, wd = wp.shape

    def body(w_ref, out_ref, send_sems, recv_sems):
        for step in _gather_steps(w_ref, out_ref, send_sems, recv_sems):
            step()

    return pl.pallas_call(
        body,
        in_specs=[ANY],
        out_specs=ANY,
        out_shape=jax.ShapeDtypeStruct((N_CHIPS, rp, wd), wp.dtype),
        scratch_shapes=GATHER_SEMS,
        name=f"gather_weights_{tag}",
    )(wp)


def _place_own(wall, wp, pos, tag):
    rp, wd = wp.shape

    def body(x_ref, y_ref, c_ref, wall_ref, w_ref, o_ref):
        o_ref[0] = w_ref[...]

    grid_spec = pltpu.PrefetchScalarGridSpec(
        num_scalar_prefetch=3,
        grid=(rp // PACK_TR,),
        in_specs=[ANY, pl.BlockSpec((PACK_TR, wd), lambda i, xr, yr, cr: (i, 0))],
        out_specs=pl.BlockSpec((1, PACK_TR, wd), lambda i, xr, yr, cr: (2 * xr[0] + yr[0], i, 0)),
    )
    return pl.pallas_call(
        body,
        grid_spec=grid_spec,
        out_shape=jax.ShapeDtypeStruct(wall.shape, wall.dtype),
        input_output_aliases={3: 0},
        compiler_params=_cparams(("parallel",)),
        name=f"place_own_shard_{tag}",
    )(*pos, wall, wp)


def _pair_exchange(g, tag):
    _, rp, wd = g.shape
    half = rp // 2

    def body(g_ref, out_ref, send_sem, recv_sem):
        x, y, c = _mesh_pos()
        cp = pltpu.make_async_remote_copy(
            src_ref=g_ref.at[:, pl.ds((1 - c) * half, half), :],
            dst_ref=out_ref,
            send_sem=send_sem,
            recv_sem=recv_sem,
            device_id=(x, y, 1 - c),
            device_id_type=MESH,
        )
        cp.start()
        cp.wait()

    return pl.pallas_call(
        body,
        in_specs=[ANY],
        out_specs=ANY,
        out_shape=jax.ShapeDtypeStruct((N_CHIPS, half, wd), g.dtype),
        scratch_shapes=[pltpu.SemaphoreType.DMA, pltpu.SemaphoreType.DMA],
        name=f"rs_pair_exchange_{tag}",
    )(g)


def _pair_add(g, recv, pos, tag):
    _, rp, wd = g.shape
    half = rp // 2
    nb = half // PACK_TR

    def body(x_ref, y_ref, c_ref, g_ref, r_ref, o_ref):
        o_ref[...] = (g_ref[...] + r_ref[...]).astype(BF16)

    blk = (1, PACK_TR, wd)
    grid_spec = pltpu.PrefetchScalarGridSpec(
        num_scalar_prefetch=3,
        grid=(N_CHIPS, nb),
        in_specs=[
            pl.BlockSpec(blk, lambda j, i, xr, yr, cr: (j, cr[0] * nb + i, 0)),
            pl.BlockSpec(blk, lambda j, i, xr, yr, cr: (j, i, 0)),
        ],
        out_specs=pl.BlockSpec(blk, lambda j, i, xr, yr, cr: (j, i, 0)),
    )
    return pl.pallas_call(
        body,
        grid_spec=grid_spec,
        out_shape=jax.ShapeDtypeStruct((N_CHIPS, half, wd), BF16),
        compiler_params=_cparams(("parallel", "parallel")),
        name=f"rs_pair_add_{tag}",
    )(*pos, g, recv)


def _exchange_steps(s_ref, out_ref, send_sems, recv_sems):
    x, y, c = _mesh_pos()
    me = 2 * x + y
    chips = _other_chips(x, y)

    def copy(j, src, dst, to):
        return pltpu.make_async_remote_copy(
            src_ref=src, dst_ref=dst, send_sem=send_sems.at[j], recv_sem=recv_sems.at[j], device_id=to, device_id_type=MESH
        )

    sends = [copy(j, s_ref.at[2 * cx + cy], out_ref.at[me], (cx, cy, c)) for j, (cx, cy) in enumerate(chips)]

    def send():
        for cp in sends:
            cp.start()

    def finish():
        for j, (cx, cy) in enumerate(chips):
            copy(j, s_ref.at[me], out_ref.at[2 * cx + cy], (x, y, c)).wait_recv()
        for cp in sends:
            cp.wait_send()

    return send, finish


def _chip_exchange(sp, tag):
    def body(s_ref, out_ref, send_sems, recv_sems):
        for step in _exchange_steps(s_ref, out_ref, send_sems, recv_sems):
            step()

    return pl.pallas_call(
        body,
        in_specs=[ANY],
        out_specs=ANY,
        out_shape=jax.ShapeDtypeStruct(sp.shape, sp.dtype),
        scratch_shapes=EXCHANGE_SEMS,
        name=f"rs_chip_exchange_{tag}",
    )(sp)


def _sum_slots(own, slots, pos, tag):
    _, rh, wd = slots.shape
    nb = rh // PACK_TR

    def body(x_ref, y_ref, c_ref, own_ref, a_ref, b_ref, d_ref, o_ref):
        f = lambda r: r[0].astype(F32)
        o_ref[...] = ((f(own_ref) + f(a_ref)) + f(b_ref)) + f(d_ref)

    blk = (1, PACK_TR, wd)

    def other(k):
        return pl.BlockSpec(blk, lambda i, xr, yr, cr: (k + (k >= 2 * xr[0] + yr[0]).astype(jnp.int32), i, 0))

    grid_spec = pltpu.PrefetchScalarGridSpec(
        num_scalar_prefetch=3,
        grid=(nb,),
        in_specs=[pl.BlockSpec(blk, lambda i, xr, yr, cr: (2 * xr[0] + yr[0], i, 0)), other(0), other(1), other(2)],
        out_specs=pl.BlockSpec((PACK_TR, wd), lambda i, xr, yr, cr: (cr[0] * nb + i, 0)),
    )
    return pl.pallas_call(
        body,
        grid_spec=grid_spec,
        out_shape=jax.ShapeDtypeStruct((2 * rh, wd), F32),
        compiler_params=_cparams(("parallel",)),
        name=f"rs_sum_slots_{tag}",
    )(*pos, own, slots, slots, slots)


def _pair_gather(t, tag):
    rh = t.shape[0] // 2

    def body(t_ref, out_ref, send_sem, recv_sem):
        x, y, c = _mesh_pos()
        cp = pltpu.make_async_remote_copy(
            src_ref=t_ref.at[pl.ds(c * rh, rh), :],
            dst_ref=out_ref.at[pl.ds(c * rh, rh), :],
            send_sem=send_sem,
            recv_sem=recv_sem,
            device_id=(x, y, 1 - c),
            device_id_type=MESH,
        )
        cp.start()
        cp.wait_send()
        pltpu.make_async_remote_copy(
            src_ref=t_ref.at[pl.ds((1 - c) * rh, rh), :],
            dst_ref=out_ref.at[pl.ds((1 - c) * rh, rh), :],
            send_sem=send_sem,
            recv_sem=recv_sem,
            device_id=(x, y, c),
            device_id_type=MESH,
        ).wait_recv()

    return pl.pallas_call(
        body,
        in_specs=[ANY],
        out_specs=ANY,
        out_shape=jax.ShapeDtypeStruct(t.shape, t.dtype),
        input_output_aliases={0: 0},
        scratch_shapes=[pltpu.SemaphoreType.DMA, pltpu.SemaphoreType.DMA],
        name=f"rs_pair_gather_{tag}",
    )(t)


def _allreduce_small(v):
    shape = v.shape
    n_dev = 8

    def body(v_ref, o_ref, slots, send_sems, recv_sems):
        x, y, c = _mesh_pos()
        me = 4 * x + 2 * y + c
        slots[me] = v_ref[...]
        sends = []
        for k in range(1, n_dev):
            fx, fy, fc = (k >> 2) & 1, (k >> 1) & 1, k & 1
            to = (x ^ fx, y ^ fy, c ^ fc)
            cp = pltpu.make_async_remote_copy(
                src_ref=v_ref,
                dst_ref=slots.at[me],
                send_sem=send_sems.at[k - 1],
                recv_sem=recv_sems.at[k - 1],
                device_id=to,
                device_id_type=MESH,
            )
            cp.start()
            sends.append(cp)
        for k in range(1, n_dev):
            fx, fy, fc = (k >> 2) & 1, (k >> 1) & 1, k & 1
            frm = 4 * (x ^ fx) + 2 * (y ^ fy) + (c ^ fc)
            pltpu.make_async_remote_copy(
                src_ref=v_ref,
                dst_ref=slots.at[frm],
                send_sem=send_sems.at[k - 1],
                recv_sem=recv_sems.at[k - 1],
                device_id=(x, y, c),
                device_id_type=MESH,
            ).wait_recv()
        for cp in sends:
            cp.wait_send()
        acc = slots[0]
        for k in range(1, n_dev):
            acc = acc + slots[k]
        o_ref[...] = acc

    vm = pl.BlockSpec(memory_space=pltpu.VMEM)
    return pl.pallas_call(
        body,
        in_specs=[vm],
        out_specs=vm,
        out_shape=jax.ShapeDtypeStruct(shape, F32),
        scratch_shapes=[pltpu.VMEM((n_dev,) + shape, F32), pltpu.SemaphoreType.DMA((n_dev - 1,)), pltpu.SemaphoreType.DMA((n_dev - 1,))],
        name="allreduce_small",
    )(v)


def _pack_layout(shard_shapes):
    offs, rows = [], []
    off = 0
    for r, c in shard_shapes:
        assert (r * c) % PACK_W == 0
        n = r * c // PACK_W
        offs.append(off)
        rows.append(n)
        off += -(-n // 16) * 16
    rp = -(-off // (2 * PACK_TR)) * (2 * PACK_TR)
    return offs, rows, rp


def _pack_rows(parts, offs, rows, rp, lead):
    ends = list(offs[1:]) + [rp]
    nolead = ((0, 0),) * len(lead)
    out = [jnp.pad(p, nolead + ((0, e - o - n), (0, 0))) for p, o, n, e in zip(parts, offs, rows, ends)]
    return jnp.concatenate(out, axis=len(lead))


def kernel(x, positions, ln0, w_in0, w_out0, ln1, w_in1, q_norm1, w_qb1, kv_norm1, w_kvb1, w_out1, ln2, w_in2, b_f2, w_out2, ln3, w_in3, w_out3, final_norm, loss_target, m_ln0, m_w_in0, m_w_out0, m_ln1, m_w_in1, m_q_norm1, m_w_qb1, m_kv_norm1, m_w_kvb1, m_w_out1, m_ln2, m_w_in2, m_b_f2, m_w_out2, m_ln3, m_w_in3, m_w_out3, m_final_norm, v_ln0, v_w_in0, v_w_out0, v_ln1, v_w_in1, v_q_norm1, v_w_qb1, v_kv_norm1, v_w_kvb1, v_w_out1, v_ln2, v_w_in2, v_b_f2, v_w_out2, v_ln3, v_w_in3, v_w_out3, v_final_norm):
    xs = x[0]
    s, d = xs.shape
    di = 4 * w_out0.shape[0]
    nh = di // HEAD_DIM
    idx = tuple(lax.axis_index(a).astype(jnp.int32).reshape(1) for a in ("x", "y", "c"))

    big = [w_in0, w_out0, w_in1, w_qb1, w_kvb1, w_out1, w_in2, w_out2, w_in3, w_out3]
    col_sharded = [True, False, True, True, True, False, True, False, True, False]
    shard_shapes = [w.shape for w in big]
    as_gathered = [True, False, False, False, False, False, False, False, True, False]
    n_first = 1

    def pack_weights(lo, hi):
        offs, rows, rp = _pack_layout(shard_shapes[lo:hi])
        return _pack_rows([w.astype(BF16).reshape(n, PACK_W) for w, n in zip(big[lo:hi], rows)], offs, rows, rp, ()), offs, rows

    def unpack_weights(wall, lo, hi, offs, rows):
        out = []
        for (r, c), o, n, cs, g4 in zip(shard_shapes[lo:hi], offs, rows, col_sharded[lo:hi], as_gathered[lo:hi]):
            slab = wall[:, o : o + n, :].reshape(N_CHIPS, r, c)
            if g4:
                out.append(slab)
            else:
                out.append(slab.transpose(1, 0, 2).reshape(r, N_CHIPS * c) if cs else slab.reshape(N_CHIPS * r, c))
        return out

    wp_a, offs_a, rows_a = pack_weights(0, n_first)
    wp_b, offs_b, rows_b = pack_weights(n_first, len(big))
    wall_a = _place_own(_gather_weights(wp_a, "l0"), wp_a, idx, "l0")
    (f_in0,) = unpack_weights(wall_a, 0, n_first, offs_a, rows_a)

    row = lambda v: v.reshape(1, -1)

    def sb_layer_attend(xin, ln, w_in, tag, gather=None):
        h = _rmsnorm_fwd(xin, row(ln), f"norm_fwd_{tag}")
        proj = _proj_w4(h, w_in, f"proj_in_{tag}")
        o, lt, g, *gathered = _sb_fwd(proj, nh, f"sb_fwd_{tag}", gather=gather)
        return (xin, h, proj, o, lt, g), gathered

    def sb_layer_out(saved, w_out, tag):
        return _matmul(saved[-1], w_out, "nn", f"proj_out_{tag}", res=saved[0])

    sv0, (wall_b_raw,) = sb_layer_attend(xs, ln0, f_in0, "l0", gather=wp_b)
    wall_b = _place_own(wall_b_raw, wp_b, idx, "rest")
    f_out0, f_in1, f_qb1, f_kvb1, f_out1, f_in2, f_out2, f_in3, f_out3 = unpack_weights(wall_b, n_first, len(big), offs_b, rows_b)
    x1 = sb_layer_out(sv0, f_out0, "l0")

    i_kr = MLA_Q_RANK + MLA_KV_RANK + MLA_ROPE
    w1p = jnp.concatenate([f_in1[:, i_kr:], f_in1[:, :i_kr], jnp.zeros((d, LANES - MLA_ROPE), BF16)], axis=1)
    qlat_blk = di // MLA_Q_RANK
    kvlat_blk = (di + MLA_Q_RANK) // MLA_KV_RANK
    kr_blk = (di + MLA_Q_RANK + MLA_KV_RANK) // LANES
    qk_w = HEAD_DIM + MLA_ROPE
    wqbp = jnp.pad(f_qb1.reshape(MLA_Q_RANK, nh, qk_w), ((0, 0), (0, 0), (0, MLA_QK - qk_w))).reshape(MLA_Q_RANK, nh * MLA_QK)
    n2 = f_in2.shape[1]
    w2p = jnp.pad(f_in2, ((0, 0), (0, 4 * di + LANES - n2)))
    b2p = jnp.pad(b_f2, (0, LANES - nh)).reshape(1, LANES)

    tabs = _rope_tables(positions[0])

    def sb_layer_bwd(dxn, dxnb, saved, ln, w_in, w_out, tag):
        xin, h, proj, o, lt, g = saved
        dgf = _matmul(dxnb, w_out, "nt", f"dgate_in_{tag}")
        dw_out = _matmul(g, dxnb, "tn", f"dw_out_{tag}")
        dq, dk, dv, dgate = _sb_bwd(proj, lt, o, dgf, nh, f"sb_bwd_{tag}")
        dproj = [dq, dk, dv, dgate]
        dh = _dh_w4(dproj, w_in, f"dh_{tag}")
        dw_in = _dw_w4(h, dproj, f"dw_in_{tag}")
        dx, dxb, dln = _rmsnorm_bwd(xin, row(ln), dh, f"norm_bwd_{tag}", dres=dxn)
        return dx, dxb, dln, dw_in, dw_out

    h1 = _rmsnorm_fwd(x1, row(ln1), "norm_fwd_l1")
    proj1 = _matmul(h1, w1p, "nn", "proj_in_l1")
    qn = _rmsnorm_fwd(proj1, row(q_norm1), "qnorm_fwd_l1", col_block=qlat_blk)
    kvn = _rmsnorm_fwd(proj1, row(kv_norm1), "kvnorm_fwd_l1", col_block=kvlat_blk)
    qpre = _matmul(qn, wqbp, "nn", "q_up_l1")
    kv1 = _matmul(kvn, f_kvb1, "nn", "kv_up_l1")
    qcat, kcat = _mla_assemble(qpre, kv1, proj1, kr_blk, tabs, nh, "mla_assemble_l1")
    mla_blk = (lambda hh: hh, lambda hh: hh, lambda hh: 2 * hh + 1)
    gate1_blk = lambda hh: hh
    o1, lse1, g1 = _sm_fwd(qcat, kcat, kv1, proj1, nh, MLA_QK, *mla_blk, gate1_blk, "chunk", "mla_fwd_l1")
    x2 = _matmul(g1, f_out1, "nn", "proj_out_l1", res=x1)

    h2 = _rmsnorm_fwd(x2, row(ln2), "norm_fwd_l2")
    proj2 = _matmul(h2, w2p, "nn", "proj_in_l2")
    f_blk = 4 * di // LANES
    cum = _forget_scan(proj2, f_blk, b2p, "forget_scan_l2")
    cum_h = cum[:, :nh].T
    t_sm = min(SM_TK, s)
    crow = cum_h.reshape(nh, s, 1)
    ccol = cum_h.reshape(nh, s // t_sm, t_sm)
    fg_blk = (lambda hh: hh, lambda hh: nh + hh, lambda hh: 2 * nh + hh)
    gate2_blk = lambda hh: 3 * nh + hh
    o2, lse2, g2 = _sm_fwd(proj2, proj2, proj2, proj2, nh, HEAD_DIM, *fg_blk, gate2_blk, "causal", "forget_fwd_l2", crow=crow, ccol=ccol)
    x3 = _matmul(g2, f_out2, "nn", "proj_out_l2", res=x2)

    sv3, _ = sb_layer_attend(x3, ln3, f_in3, "l3")
    x4 = sb_layer_out(sv3, f_out3, "l3")

    dx, dxb, d_final, loss_part = _loss_head(x4, row(final_norm), loss_target[0], "loss_head")
    dx, dxb, d_ln3, dw_in3, dw_out3 = sb_layer_bwd(dx, dxb, sv3, ln3, f_in3, f_out3, "l3")

    dgf2 = _matmul(dxb, f_out2, "nt", "dgate_in_l2")
    dw_out2 = _matmul(g2, dxb, "tn", "dw_out_l2")
    dq2, dk2, dv2, dgate2, dcc2, dcr2 = _sm_bwd(proj2, proj2, proj2, proj2, o2, dgf2, lse2, nh, HEAD_DIM, *fg_blk, gate2_blk, "causal", "forget_bwd_l2", crow=crow, ccol=ccol)
    lanes_of = lambda a: jnp.pad(a.reshape(nh, s).T, ((0, 0), (0, LANES - nh)))
    df2, d_bf = _forget_scan_bwd(lanes_of(dcc2), lanes_of(dcr2), proj2, f_blk, b2p, nh, "forget_scan_bwd_l2")
    dproj2 = jnp.concatenate([dq2.astype(BF16), dk2.astype(BF16), dv2.astype(BF16), dgate2, df2], axis=1)
    dh2 = _matmul(dproj2, w2p, "nt", "dh_l2")
    dw_in2 = _matmul(h2, dproj2, "tn", "dw_in_l2")[:, :n2]
    dx, dxb, d_ln2 = _rmsnorm_bwd(x2, row(ln2), dh2, "norm_bwd_l2", dres=dx)

    n_late = 6

    def reduce_start(lo, hi, dws, tag):
        offs, rows, rp = _pack_layout(shard_shapes[lo:hi])
        parts = []
        for g, (r, c), n, cs, is4 in zip(dws, shard_shapes[lo:hi], rows, col_sharded[lo:hi], as_gathered[lo:hi]):
            g4 = g if is4 else (g.reshape(r, N_CHIPS, c).transpose(1, 0, 2) if cs else g.reshape(N_CHIPS, r, c))
            parts.append(g4.reshape(N_CHIPS, n, PACK_W))
        gp = _pack_rows(parts, offs, rows, rp, (N_CHIPS,))
        return _pair_add(gp, _pair_exchange(gp, tag), idx, tag), offs, rows

    def reduce_finish(pair, slots, lo, hi, offs, rows, tag):
        gred = _pair_gather(_sum_slots(pair, slots, idx, tag), tag)
        return [gred[o : o + n, :].reshape(r, c) for (r, c), o, n in zip(shard_shapes[lo:hi], offs, rows)]

    pair_b, offs_gb, rows_gb = reduce_start(n_late, len(big), [dw_in2, dw_out2, dw_in3, dw_out3], "l23")

    dgf1 = _matmul(dxb, f_out1, "nt", "dgate_in_l1")
    dw_out1 = _matmul(g1, dxb, "tn", "dw_out_l1")
    dqc, dkc, dv1, dgate1, slots_b = _sm_bwd(qcat, kcat, kv1, proj1, o1, dgf1, lse1, nh, MLA_QK, *mla_blk, gate1_blk, "chunk", "mla_bwd_l1", exchange=pair_b)
    dqpre, dkv1, dkr = _mla_disassemble(dqc, dkc, dv1, tabs, nh, "mla_disassemble_l1")
    dqn = _matmul(dqpre, wqbp, "nt", "dqn_l1")
    dw_qbp = _matmul(qn, dqpre, "tn", "dw_qb_l1")
    dw_qb1 = dw_qbp.reshape(MLA_Q_RANK, nh, MLA_QK)[:, :, :qk_w].reshape(MLA_Q_RANK, nh * qk_w)
    dkvn = _matmul(dkv1, f_kvb1, "nt", "dkvn_l1")
    dw_kvb1 = _matmul(kvn, dkv1, "tn", "dw_kvb_l1")
    _, dqlat_b, d_qnorm = _rmsnorm_bwd(proj1, row(q_norm1), dqn, "qnorm_bwd_l1", col_block=qlat_blk)
    _, dkvlat_b, d_kvnorm = _rmsnorm_bwd(proj1, row(kv_norm1), dkvn, "kvnorm_bwd_l1", col_block=kvlat_blk)
    dproj1 = jnp.concatenate([dgate1, dqlat_b, dkvlat_b, dkr.astype(BF16)], axis=1)
    dh1 = _matmul(dproj1, w1p, "nt", "dh_l1")
    dw1p = _matmul(h1, dproj1, "tn", "dw_in_l1")
    dw_in1 = jnp.concatenate([dw1p[:, di : di + i_kr], dw1p[:, :di]], axis=1)
    dx, dxb, d_ln1 = _rmsnorm_bwd(x1, row(ln1), dh1, "norm_bwd_l1", dres=dx)

    dx, dxb, d_ln0, dw_in0, dw_out0 = sb_layer_bwd(dx, dxb, sv0, ln0, f_in0, f_out0, "l0")
    grad_x = dx.reshape(x.shape)

    pair_a, offs_ga, rows_ga = reduce_start(0, n_late, [dw_in0, dw_out0, dw_in1, dw_qb1, dw_kvb1, dw_out1], "l01")
    slots_a = _chip_exchange(pair_a, "l01")
    big_grads = reduce_finish(pair_a, slots_a, 0, n_late, offs_ga, rows_ga, "l01")
    big_grads += reduce_finish(pair_b, slots_b, n_late, len(big), offs_gb, rows_gb, "l23")

    small = [ln0, ln1, q_norm1, kv_norm1, ln2, b_f2, ln3, final_norm]
    small_g = [d_ln0[0], d_ln1[0], d_qnorm[0], d_kvnorm[0], d_ln2[0], d_bf[0, :nh], d_ln3[0], d_final[0]]
    n_small = SMALL_SHAPE[0] * SMALL_SHAPE[1]
    used = sum(v.shape[0] for v in small) + 1
    assert used <= n_small

    def pack_small(vs, last):
        return jnp.concatenate(list(vs) + [last, jnp.zeros((n_small - used,), F32)]).reshape(SMALL_SHAPE)

    sm_sum = _allreduce_small(pack_small(small_g, loss_part[0, :1]))
    flat = sm_sum.reshape(-1)
    loss = flat[used - 1]

    big_m = [m_w_in0, m_w_out0, m_w_in1, m_w_qb1, m_w_kvb1, m_w_out1, m_w_in2, m_w_out2, m_w_in3, m_w_out3]
    big_v = [v_w_in0, v_w_out0, v_w_in1, v_w_qb1, v_w_kvb1, v_w_out1, v_w_in2, v_w_out2, v_w_in3, v_w_out3]
    big_names = ["w_in0", "w_out0", "w_in1", "w_qb1", "w_kvb1", "w_out1", "w_in2", "w_out2", "w_in3", "w_out3"]
    big_upd = [_adamw(w, g, m, v, f"adamw_{nm}") for w, g, m, v, nm in zip(big, big_grads, big_m, big_v, big_names)]

    small_m = [m_ln0, m_ln1, m_q_norm1, m_kv_norm1, m_ln2, m_b_f2, m_ln3, m_final_norm]
    small_v = [v_ln0, v_ln1, v_q_norm1, v_kv_norm1, v_ln2, v_b_f2, v_ln3, v_final_norm]
    one = jnp.ones((1,), F32)
    sd, smn, svn = _adamw(pack_small(small, one), sm_sum, pack_small(small_m, one), pack_small(small_v, one), "adamw_small")

    def unpack_small(p):
        out, at = [], 0
        fl = p.reshape(-1)
        for v in small:
            out.append(fl[at : at + v.shape[0]])
            at += v.shape[0]
        return out

    sg_l, sd_l, sm_l, sv_l = unpack_small(sm_sum), unpack_small(sd), unpack_small(smn), unpack_small(svn)

    order = ["ln0", "w_in0", "w_out0", "ln1", "w_in1", "q_norm1", "w_qb1", "kv_norm1", "w_kvb1", "w_out1", "ln2", "w_in2", "b_f2", "w_out2", "ln3", "w_in3", "w_out3", "final_norm"]
    small_names = ["ln0", "ln1", "q_norm1", "kv_norm1", "ln2", "b_f2", "ln3", "final_norm"]
    grads, deltas, new_m, new_v = {}, {}, {}, {}
    for nm, g, (dl, mn, vn) in zip(big_names, big_grads, big_upd):
        grads[nm], deltas[nm], new_m[nm], new_v[nm] = g, dl, mn, vn
    for nm, g, dl, mn, vn in zip(small_names, sg_l, sd_l, sm_l, sv_l):
        grads[nm], deltas[nm], new_m[nm], new_v[nm] = g, dl, mn, vn
    return (loss, grad_x, *[grads[n] for n in order], *[deltas[n] for n in order], *[new_m[n] for n in order], *[new_v[n] for n in order])
```

```python
import functools

import jax
import jax.numpy as jnp
from jax import lax
from jax.experimental import pallas as pl
from jax.experimental.pallas import tpu as pltpu

F32 = jnp.float32
BF16 = jnp.bfloat16
EPS = 1e-6
NEG = -1e30
HEAD_DIM = 128
CHUNK_SHIFT = 6
MLA_Q_RANK = 256
MLA_KV_RANK = 128
MLA_ROPE = 64
MLA_QK = 256
ROPE_BASE = 10000.0
ADAM_LR = 0.001
ADAM_B1 = 0.9
ADAM_B2 = 0.999
ADAM_EPS = 1e-08
ADAM_WD = 0.01
ADAM_STEP = 10
VMEM_LIMIT_BYTES = 56 * 2**20
LANES = 128
PACK_W = 1024
PACK_TR = 128
SMALL_SHAPE = (8, 768)
MESH = pl.DeviceIdType.MESH
N_CHIPS = 4


def _pick(n, cands):
    for c in cands:
        if n % c == 0:
            return c
    return n


def _cparams(sem):
    return pltpu.CompilerParams(dimension_semantics=sem, vmem_limit_bytes=VMEM_LIMIT_BYTES)


def _dot(a, b, dims):
    dn = {"nn": (((1,), (0,)), ((), ())), "nt": (((1,), (1,)), ((), ())), "tn": (((0,), (0,)), ((), ()))}[dims]
    return lax.dot_general(a, b, dn, preferred_element_type=F32)


def _matmul(a, b, dims, name, res=None):
    if dims == "nn":
        (m, k), (k2, n) = a.shape, b.shape
    elif dims == "nt":
        (m, k), (n, k2) = a.shape, b.shape
    else:
        (k, m), (k2, n) = a.shape, b.shape
    assert k == k2, (a.shape, b.shape, dims)
    tm = _pick(m, (1024, 512, 256, 128))
    tn = _pick(n, (1024, 1664, 640, 512, 384, 256, 128))
    tk = _pick(k, (1024, 1664, 640, 512, 256, 128))
    nk = k // tk

    def body(*refs):
        if res is None:
            a_ref, b_ref, o_ref = refs
            r_ref = None
        else:
            a_ref, b_ref, r_ref, o_ref = refs
        kk = pl.program_id(2)
        p = _dot(a_ref[...].astype(BF16), b_ref[...].astype(BF16), dims)

        @pl.when(kk == 0)
        def _():
            o_ref[...] = p if r_ref is None else p + r_ref[...]

        @pl.when(kk > 0)
        def _():
            o_ref[...] += p

    a_spec = pl.BlockSpec((tk, tm), lambda i, j, kk: (kk, i)) if dims == "tn" else pl.BlockSpec((tm, tk), lambda i, j, kk: (i, kk))
    b_spec = pl.BlockSpec((tn, tk), lambda i, j, kk: (j, kk)) if dims == "nt" else pl.BlockSpec((tk, tn), lambda i, j, kk: (kk, j))
    o_spec = pl.BlockSpec((tm, tn), lambda i, j, kk: (i, j))
    in_specs = [a_spec, b_spec] + ([] if res is None else [o_spec])
    args = (a, b) + (() if res is None else (res,))
    return pl.pallas_call(
        body,
        grid=(m // tm, n // tn, nk),
        in_specs=in_specs,
        out_specs=o_spec,
        out_shape=jax.ShapeDtypeStruct((m, n), F32),
        compiler_params=_cparams(("parallel", "parallel", "arbitrary")),
        name=name,
    )(*args)


def _proj_w4(h, w4, name):
    (s, d), (ns, d2, c) = h.shape, w4.shape
    assert d == d2
    tm = _pick(s, (1024, 512, 256, 128))
    tn = _pick(c, (1024, 512, 256, 128))
    nbs = c // tn

    def body(a_ref, b_ref, o_ref):
        o_ref[...] = _dot(a_ref[...].astype(BF16), b_ref[...].astype(BF16), "nn")

    return pl.pallas_call(
        body,
        grid=(s // tm, ns * nbs),
        in_specs=[pl.BlockSpec((tm, d), lambda i, j: (i, 0)), pl.BlockSpec((None, d, tn), lambda i, j: (j // nbs, 0, j % nbs))],
        out_specs=pl.BlockSpec((tm, tn), lambda i, j: (i, j)),
        out_shape=jax.ShapeDtypeStruct((s, ns * c), F32),
        compiler_params=_cparams(("parallel", "parallel")),
        name=name,
    )(h, w4)


def _dh_w4(parts, w4, name):
    ns, d, c = w4.shape
    s = parts[0].shape[0]
    assert len(parts) == ns and all(p.shape == (s, c) for p in parts)
    tm = _pick(s, (1024, 512, 256, 128))
    tk = _pick(c, (1024, 512, 256, 128))
    nkp = c // tk

    def body(*refs):
        a_refs, b_ref, o_ref = refs[:ns], refs[ns], refs[ns + 1]
        kk = pl.program_id(1)
        for p in range(ns):

            @pl.when(kk // nkp == p)
            def _(p=p):
                pv = _dot(a_refs[p][...].astype(BF16), b_ref[...].astype(BF16), "nt")

                @pl.when(kk == 0)
                def _():
                    o_ref[...] = pv

                @pl.when(kk > 0)
                def _():
                    o_ref[...] += pv

    def a_spec(p):
        return pl.BlockSpec((tm, tk), lambda i, kk: (i, jnp.clip(kk - p * nkp, 0, nkp - 1)))

    return pl.pallas_call(
        body,
        grid=(s // tm, ns * nkp),
        in_specs=[a_spec(p) for p in range(ns)] + [pl.BlockSpec((None, d, tk), lambda i, kk: (kk // nkp, 0, kk % nkp))],
        out_specs=pl.BlockSpec((tm, d), lambda i, kk: (i, 0)),
        out_shape=jax.ShapeDtypeStruct((s, d), F32),
        compiler_params=_cparams(("parallel", "arbitrary")),
        name=name,
    )(*parts, w4)


def _dw_w4(h, parts, name):
    s, d = h.shape
    ns = len(parts)
    c = parts[0].shape[1]
    tn = _pick(c, (1024, 512, 256, 128))
    tk = _pick(s, (1024, 512, 256, 128))
    nbp = c // tn

    def body(*refs):
        a_ref, b_refs, o_ref = refs[0], refs[1 : 1 + ns], refs[1 + ns]
        j, kk = pl.program_id(0), pl.program_id(1)
        for p in range(ns):

            @pl.when(j // nbp == p)
            def _(p=p):
                pv = _dot(a_ref[...].astype(BF16), b_refs[p][...].astype(BF16), "tn")

                @pl.when(kk == 0)
                def _():
                    o_ref[...] = pv

                @pl.when(kk > 0)
                def _():
                    o_ref[...] += pv

    def b_spec(p):
        return pl.BlockSpec((tk, tn), lambda j, kk: (kk, jnp.clip(j - p * nbp, 0, nbp - 1)))

    return pl.pallas_call(
        body,
        grid=(ns * nbp, s // tk),
        in_specs=[pl.BlockSpec((tk, d), lambda j, kk: (kk, 0))] + [b_spec(p) for p in range(ns)],
        out_specs=pl.BlockSpec((None, d, tn), lambda j, kk: (j // nbp, 0, j % nbp)),
        out_shape=jax.ShapeDtypeStruct((ns, d, c), F32),
        compiler_params=_cparams(("parallel", "arbitrary")),
        name=name,
    )(h, *parts)


def _rmsnorm_fwd(x, g, name, col_block=0):
    s = x.shape[0]
    w = g.shape[1]
    tr = _pick(s, (512, 256, 128))

    def body(x_ref, g_ref, h_ref):
        xv = x_ref[...]
        r = lax.rsqrt(jnp.mean(xv * xv, axis=-1, keepdims=True) + EPS)
        h_ref[...] = ((xv * r) * g_ref[...]).astype(BF16)

    return pl.pallas_call(
        body,
        grid=(s // tr,),
        in_specs=[pl.BlockSpec((tr, w), lambda i: (i, col_block)), pl.BlockSpec((1, w), lambda i: (0, 0))],
        out_specs=pl.BlockSpec((tr, w), lambda i: (i, 0)),
        out_shape=jax.ShapeDtypeStruct((s, w), BF16),
        compiler_params=_cparams(("parallel",)),
        name=name,
    )(x, g)


def _rmsnorm_bwd(x, g, dh, name, col_block=0, dres=None):
    s = x.shape[0]
    w = g.shape[1]
    tr = _pick(s, (512, 256, 128))

    def body(*refs):
        if dres is None:
            x_ref, g_ref, dh_ref, dx_ref, dxb_ref, dg_ref = refs
        else:
            x_ref, g_ref, dh_ref, dr_ref, dx_ref, dxb_ref, dg_ref = refs
        i = pl.program_id(0)
        xv = x_ref[...]
        r = lax.rsqrt(jnp.mean(xv * xv, axis=-1, keepdims=True) + EPS)
        xh = xv * r
        dhv = dh_ref[...]
        dyg = dhv * g_ref[...]
        dx = r * (dyg - xh * jnp.mean(dyg * xh, axis=-1, keepdims=True))
        if dres is not None:
            dx = dx + dr_ref[...]
        dx_ref[...] = dx
        dxb_ref[...] = dx.astype(BF16)
        part = jnp.sum(dhv * xh, axis=0, keepdims=True)

        @pl.when(i == 0)
        def _():
            dg_ref[...] = part

        @pl.when(i > 0)
        def _():
            dg_ref[...] += part

    row = pl.BlockSpec((tr, w), lambda i: (i, 0))
    in_specs = [pl.BlockSpec((tr, w), lambda i: (i, col_block)), pl.BlockSpec((1, w), lambda i: (0, 0)), row]
    args = [x, g, dh]
    if dres is not None:
        in_specs.append(row)
        args.append(dres)
    return pl.pallas_call(
        body,
        grid=(s // tr,),
        in_specs=in_specs,
        out_specs=[row, row, pl.BlockSpec((1, w), lambda i: (0, 0))],
        out_shape=[jax.ShapeDtypeStruct((s, w), F32), jax.ShapeDtypeStruct((s, w), BF16), jax.ShapeDtypeStruct((1, w), F32)],
        compiler_params=_cparams(("arbitrary",)),
        name=name,
    )(*args)


def _loss_head(x, g, target, name):
    s, d = x.shape
    tr = _pick(s, (512, 256, 128))

    def body(x_ref, g_ref, t_ref, dx_ref, dxb_ref, dg_ref, loss_ref):
        i = pl.program_id(0)
        xv = x_ref[...]
        gv = g_ref[...]
        r = lax.rsqrt(jnp.mean(xv * xv, axis=-1, keepdims=True) + EPS)
        xh = xv * r
        err = xh * gv - t_ref[...]
        lpart = 0.5 * jnp.sum(jnp.mean(err * err, axis=-1, keepdims=True), axis=0, keepdims=True)
        dy = err / d
        dyg = dy * gv
        dx = r * (dyg - xh * jnp.mean(dyg * xh, axis=-1, keepdims=True))
        dx_ref[...] = dx
        dxb_ref[...] = dx.astype(BF16)
        part = jnp.sum(dy * xh, axis=0, keepdims=True)
        lrow = jnp.broadcast_to(lpart, (1, LANES))

        @pl.when(i == 0)
        def _():
            dg_ref[...] = part
            loss_ref[...] = lrow

        @pl.when(i > 0)
        def _():
            dg_ref[...] += part
            loss_ref[...] += lrow

    row = pl.BlockSpec((tr, d), lambda i: (i, 0))
    vec = pl.BlockSpec((1, d), lambda i: (0, 0))
    return pl.pallas_call(
        body,
        grid=(s // tr,),
        in_specs=[row, vec, row],
        out_specs=[row, row, vec, pl.BlockSpec((1, LANES), lambda i: (0, 0))],
        out_shape=[
            jax.ShapeDtypeStruct((s, d), F32),
            jax.ShapeDtypeStruct((s, d), BF16),
            jax.ShapeDtypeStruct((1, d), F32),
            jax.ShapeDtypeStruct((1, LANES), F32),
        ],
        compiler_params=_cparams(("arbitrary",)),
        name=name,
    )(x, g, target)


def _sigmoid(x):
    return 1.0 / (1.0 + jnp.exp(-x))


def _gate_out(gate_ref, o):
    gt = gate_ref[...]
    return (o * (gt * _sigmoid(gt))).astype(BF16)


def _gate_grads(gate_ref, o_ref, dg_ref, dgate_ref):
    gt = gate_ref[...]
    sg = _sigmoid(gt)
    dgv = dg_ref[...]
    dgate_ref[...] = (dgv * o_ref[...] * (sg * (1.0 + gt * (1.0 - sg)))).astype(BF16)
    return dgv * (gt * sg)


def _iotas(tq, tk):
    return lax.broadcasted_iota(jnp.int32, (tq, tk), 0), lax.broadcasted_iota(jnp.int32, (tq, tk), 1)


def _softplus(s):
    return jnp.maximum(s, 0.0) + jnp.log(1.0 + jnp.exp(-jnp.abs(s)))


def _split2(v):
    hi = v.astype(BF16)
    lo = (v - hi.astype(F32)).astype(BF16)
    return hi, lo


def _cat2(v):
    return jnp.concatenate(_split2(v), axis=1)


def _tri2(keep):
    m = keep.astype(BF16)
    return jnp.concatenate([m, m], axis=0)


def _split3(v):
    a = v.astype(BF16)
    r1 = v - a.astype(F32)
    b = r1.astype(BF16)
    c = (r1 - b.astype(F32)).astype(BF16)
    return a, b, c


SB_TQ = 1024
SB_TK = 128
SB_UNROLL = 8


def _sb_fwd(proj, n_heads, name, gather=None):
    s = proj.shape[0]
    d = HEAD_DIM
    t = min(SB_TQ, s)
    tk = min(SB_TK, s)
    r = t // tk
    un = SB_UNROLL if r % SB_UNROLL == 0 else 1
    nq = s // t
    scale = d**-0.5

    def body(*refs):
        if gather is None:
            q_ref, k_ref, v_ref, gate_ref, o_ref, lt_ref, g_ref, kb_ref, vb_ref = refs
        else:
            q_ref, k_ref, v_ref, gate_ref, w_ref, o_ref, lt_ref, g_ref, wall_ref, kb_ref, vb_ref, send_sems, recv_sems = refs
        i = pl.program_id(1)

        if gather is not None:
            hh = pl.program_id(0)
            send, forward, finish = _gather_steps(w_ref, wall_ref, send_sems, recv_sems)
            pl.when((hh == 0) & (i == 0))(send)
            pl.when((hh == n_heads // 2) & (i == 0))(forward)

        @pl.when(i == 0)
        def _():
            kb_ref[...] = k_ref[...].astype(BF16)
            vb_ref[...] = v_ref[...].astype(BF16)

        q = (q_ref[...] * scale).astype(BF16)
        rows, cols = _iotas(t, tk)
        trows, tcols = _iotas(tk, tk)
        tri = (trows > tcols).astype(BF16)

        def block(kb, cl, acc, diag):
            k0 = pl.multiple_of(kb * tk, tk)
            sc = _dot(q, kb_ref[pl.ds(k0, tk), :], "nt")
            sp = _softplus(sc)
            ls = -sp
            if diag is not None:
                strict = cols + diag * tk < rows
                ls = jnp.where(strict, ls, 0.0)
            hi, lo = _split2(ls)
            later = _dot(hi, tri, "nn") + _dot(lo, tri, "nn")
            w = jnp.exp((sc - sp) + later + cl)
            if diag is not None:
                w = jnp.where(strict, w, 0.0)
            acc = acc + _dot(w.astype(BF16), vb_ref[pl.ds(k0, tk), :], "nn")
            return cl + jnp.sum(ls, axis=1, keepdims=True), acc

        cl, acc = jnp.zeros((t, 1), F32), jnp.zeros((t, d), F32)
        for dd in reversed(range(r)):
            cl, acc = block(i * r + dd, cl, acc, dd)

        def loop(j, carry):
            for u in range(un):
                carry = block(i * r - 1 - (un * j + u), carry[0], carry[1], None)
            return carry

        cl, acc = lax.fori_loop(0, (i * r) // un, loop, (cl, acc))
        o_ref[...] = acc
        lt_ref[...] = cl
        g_ref[...] = _gate_out(gate_ref, acc)

        if gather is not None:
            pl.when((hh == n_heads - 1) & (i == nq - 1))(finish)

    h = n_heads
    qblk = pl.BlockSpec((t, d), lambda hh, i: (i, hh))
    in_specs = [
        qblk,
        pl.BlockSpec((s, d), lambda hh, i: (0, h + hh)),
        pl.BlockSpec((s, d), lambda hh, i: (0, 2 * h + hh)),
        pl.BlockSpec((t, d), lambda hh, i: (i, 3 * h + hh)),
    ]
    out_specs = [qblk, pl.BlockSpec((None, t, 1), lambda hh, i: (hh, i, 0)), qblk]
    out_shape = [jax.ShapeDtypeStruct((s, h * d), F32), jax.ShapeDtypeStruct((h, s, 1), F32), jax.ShapeDtypeStruct((s, h * d), BF16)]
    scratch = [pltpu.VMEM((s, d), BF16), pltpu.VMEM((s, d), BF16)]
    args = [proj, proj, proj, proj]
    if gather is not None:
        in_specs.append(ANY)
        out_specs.append(ANY)
        out_shape.append(jax.ShapeDtypeStruct((N_CHIPS,) + gather.shape, gather.dtype))
        scratch += GATHER_SEMS
        args.append(gather)
    return pl.pallas_call(
        body,
        grid=(h, nq),
        in_specs=in_specs,
        out_specs=out_specs,
        out_shape=out_shape,
        scratch_shapes=scratch,
        compiler_params=_cparams(("arbitrary", "arbitrary")),
        name=name,
    )(*args)


def _sb_bwd(proj, ltot, o, dg, n_heads, name, exchange=None):
    s = proj.shape[0]
    d = HEAD_DIM
    t = min(SB_TQ, s)
    tk = min(SB_TK, s)
    r = t // tk
    un = SB_UNROLL if r % SB_UNROLL == 0 else 1
    nq = s // t
    scale = d**-0.5

    def body(*refs):
        if exchange is None:
            q_ref, k_ref, v_ref, gate_ref, lt_ref, o_ref, dg_ref, dq_ref, dko_ref, dvo_ref, dgate_ref, kb_ref, vb_ref, dk_ref, dv_ref = refs
        else:
            (q_ref, k_ref, v_ref, gate_ref, lt_ref, o_ref, dg_ref, sp_ref, dq_ref, dko_ref, dvo_ref, dgate_ref, slots_ref,
             kb_ref, vb_ref, dk_ref, dv_ref, send_sems, recv_sems) = refs
        i = pl.program_id(1)

        if exchange is not None:
            hh = pl.program_id(0)
            send, finish = _exchange_steps(sp_ref, slots_ref, send_sems, recv_sems)
            pl.when((hh == 0) & (i == 0))(send)

        @pl.when(i == 0)
        def _():
            kb_ref[...] = k_ref[...].astype(BF16)
            vb_ref[...] = v_ref[...].astype(BF16)
            dk_ref[...] = jnp.zeros_like(dk_ref)
            dv_ref[...] = jnp.zeros_like(dv_ref)

        dob = _gate_grads(gate_ref, o_ref, dg_ref, dgate_ref).astype(BF16)
        q = (q_ref[...] * scale).astype(BF16)
        ltv = lt_ref[...]
        rows, cols = _iotas(t, tk)
        trows, tcols = _iotas(tk, tk)
        upto = _tri2(trows <= tcols)
        before = _tri2(trows < tcols)

        def block(kb, cp, cc, dq, diag):
            k0 = pl.multiple_of(kb * tk, tk)
            kk = kb_ref[pl.ds(k0, tk), :]
            sc = _dot(q, kk, "nt")
            sp = _softplus(sc)
            ls = -sp
            if diag is not None:
                strict = cols + diag * tk < rows
                ls = jnp.where(strict, ls, 0.0)
            prefix = _dot(_cat2(ls), upto, "nn") + cp
            lsig = sc - sp
            w = jnp.exp(lsig + (ltv - prefix))
            if diag is not None:
                w = jnp.where(strict, w, 0.0)
            da = _dot(dob, vb_ref[pl.ds(k0, tk), :], "nt") * w
            csum = _dot(_cat2(da), before, "nn") + cc
            beta = jnp.exp(lsig)
            dz = da * (1.0 - beta) - beta * csum
            if diag is not None:
                dz = jnp.where(strict, dz, 0.0)
            dzb = dz.astype(BF16)
            dq = dq + _dot(dzb, kk, "nn")
            dk_ref[pl.ds(k0, tk), :] += _dot(dzb, q, "tn")
            dv_ref[pl.ds(k0, tk), :] += _dot(w.astype(BF16), dob, "tn")
            return cp + jnp.sum(ls, axis=1, keepdims=True), cc + jnp.sum(da, axis=1, keepdims=True), dq

        def loop(j, carry):
            for u in range(un):
                carry = block(un * j + u, carry[0], carry[1], carry[2], None)
            return carry

        z1 = jnp.zeros((t, 1), F32)
        cp, cc, dq = lax.fori_loop(0, (i * r) // un, loop, (z1, z1, jnp.zeros((t, d), F32)))
        for dd in range(r):
            cp, cc, dq = block(i * r + dd, cp, cc, dq, dd)
        dq_ref[...] = (dq * scale).astype(BF16)

        @pl.when(i == nq - 1)
        def _():
            dko_ref[...] = dk_ref[...].astype(BF16)
            dvo_ref[...] = dv_ref[...].astype(BF16)

        if exchange is not None:
            pl.when((hh == n_heads - 1) & (i == nq - 1))(finish)

    h = n_heads
    qblk = pl.BlockSpec((t, d), lambda hh, i: (i, hh))
    full = pl.BlockSpec((s, d), lambda hh, i: (0, hh))
    shp = jax.ShapeDtypeStruct((s, h * d), BF16)
    in_specs = [
        qblk,
        pl.BlockSpec((s, d), lambda hh, i: (0, h + hh)),
        pl.BlockSpec((s, d), lambda hh, i: (0, 2 * h + hh)),
        pl.BlockSpec((t, d), lambda hh, i: (i, 3 * h + hh)),
        pl.BlockSpec((None, t, 1), lambda hh, i: (hh, i, 0)),
        qblk,
        qblk,
    ]
    args = [proj, proj, proj, proj, ltot, o, dg]
    out_specs = [qblk, full, full, qblk]
    out_shape = [shp, shp, shp, shp]
    scratch = [pltpu.VMEM((s, d), BF16), pltpu.VMEM((s, d), BF16), pltpu.VMEM((s, d), F32), pltpu.VMEM((s, d), F32)]
    if exchange is not None:
        in_specs.append(ANY)
        args.append(exchange)
        out_specs.append(ANY)
        out_shape.append(jax.ShapeDtypeStruct(exchange.shape, exchange.dtype))
        scratch += EXCHANGE_SEMS
    return pl.pallas_call(
        body,
        grid=(h, nq),
        in_specs=in_specs,
        out_specs=out_specs,
        out_shape=out_shape,
        scratch_shapes=scratch,
        compiler_params=_cparams(("arbitrary", "arbitrary")),
        name=name,
    )(*args)


SM_TQ = 1024
SM_TK = 512
SM_UNROLL = 2
SM_TK_WIDE = 256
SM_UNROLL_WIDE = 4


def _allowed(mode, rows, cols, q0, k0):
    r = rows + q0
    c = cols + k0
    if mode == "causal":
        return c <= r
    return (c >> CHUNK_SHIFT) <= (r >> CHUNK_SHIFT)


def _sm_fwd(q_arr, k_arr, v_arr, gate_arr, n_heads, dqk, q_blk, k_blk, v_blk, gate_blk, mode, name, crow=None, ccol=None):
    s = q_arr.shape[0]
    dv = HEAD_DIM
    t = min(SM_TQ, s)
    tk = min(SM_TK, s)
    r = t // tk
    un = SM_UNROLL if r % SM_UNROLL == 0 else 1
    bias = crow is not None
    scale = (HEAD_DIM if mode == "causal" else HEAD_DIM + MLA_ROPE) ** -0.5

    def body(*refs):
        if bias:
            q_ref, k_ref, v_ref, gate_ref, cr_ref, cc_ref, o_ref, lse_ref, g_ref, kb_ref, vb_ref = refs
        else:
            q_ref, k_ref, v_ref, gate_ref, o_ref, lse_ref, g_ref, kb_ref, vb_ref = refs
        i = pl.program_id(1)

        @pl.when(i == 0)
        def _():
            kb_ref[...] = k_ref[...].astype(BF16)
            vb_ref[...] = v_ref[...].astype(BF16)

        q = (q_ref[...] * scale).astype(BF16)
        q0 = i * t
        rows, cols = _iotas(t, tk)
        crv = cr_ref[...] if bias else None

        def block(kb, m, l, acc, masked):
            k0 = pl.multiple_of(kb * tk, tk)
            sc = _dot(q, kb_ref[pl.ds(k0, tk), :], "nt")
            if bias:
                sc = sc + crv - cc_ref[pl.ds(kb, 1), :]
            if masked:
                sc = jnp.where(_allowed(mode, rows, cols, q0, k0), sc, NEG)
            m_new = jnp.maximum(m, jnp.max(sc, axis=1, keepdims=True))
            alpha = jnp.exp(m - m_new)
            p = jnp.exp(sc - m_new)
            l = alpha * l + jnp.sum(p, axis=1, keepdims=True)
            acc = alpha * acc + _dot(p.astype(BF16), vb_ref[pl.ds(k0, tk), :], "nn")
            return m_new, l, acc

        def loop(j, carry):
            for u in range(un):
                carry = block(un * j + u, carry[0], carry[1], carry[2], False)
            return carry

        init = (jnp.full((t, 1), NEG, F32), jnp.zeros((t, 1), F32), jnp.zeros((t, dv), F32))
        m, l, acc = lax.fori_loop(0, (i * r) // un, loop, init)
        for dd in range(r):
            m, l, acc = block(i * r + dd, m, l, acc, True)
        ov = acc / l
        o_ref[...] = ov
        lse_ref[...] = m + jnp.log(l)
        g_ref[...] = _gate_out(gate_ref, ov)

    h = n_heads
    oblk = pl.BlockSpec((t, dv), lambda hh, i: (i, hh))
    in_specs = [
        pl.BlockSpec((t, dqk), lambda hh, i: (i, q_blk(hh))),
        pl.BlockSpec((s, dqk), lambda hh, i: (0, k_blk(hh))),
        pl.BlockSpec((s, dv), lambda hh, i: (0, v_blk(hh))),
        pl.BlockSpec((t, dv), lambda hh, i: (i, gate_blk(hh))),
    ]
    args = [q_arr, k_arr, v_arr, gate_arr]
    if bias:
        in_specs += [pl.BlockSpec((None, t, 1), lambda hh, i: (hh, i, 0)), pl.BlockSpec((None, s // tk, tk), lambda hh, i: (hh, 0, 0))]
        args += [crow, ccol]
    return pl.pallas_call(
        body,
        grid=(h, s // t),
        in_specs=in_specs,
        out_specs=[oblk, pl.BlockSpec((None, t, 1), lambda hh, i: (hh, i, 0)), oblk],
        out_shape=[jax.ShapeDtypeStruct((s, h * dv), F32), jax.ShapeDtypeStruct((h, s, 1), F32), jax.ShapeDtypeStruct((s, h * dv), BF16)],
        scratch_shapes=[pltpu.VMEM((s, dqk), BF16), pltpu.VMEM((s, dv), BF16)],
        compiler_params=_cparams(("arbitrary", "arbitrary")),
        name=name,
    )(*args)


def _sm_bwd(q_arr, k_arr, v_arr, gate_arr, o, dg, lse, n_heads, dqk, q_blk, k_blk, v_blk, gate_blk, mode, name, crow=None, ccol=None, exchange=None):
    s = q_arr.shape[0]
    dv = HEAD_DIM
    t = min(SM_TQ, s)
    tk_cfg, un_cfg = (SM_TK_WIDE, SM_UNROLL_WIDE) if dqk > HEAD_DIM else (SM_TK, SM_UNROLL)
    tk = min(tk_cfg, s)
    r = t // tk
    un = un_cfg if r % un_cfg == 0 else 1
    nq = s // t
    bias = crow is not None
    assert not (bias and exchange is not None)
    scale = (HEAD_DIM if mode == "causal" else HEAD_DIM + MLA_ROPE) ** -0.5

    def body(*refs):
        if bias:
            q_ref, k_ref, v_ref, gate_ref, o_ref, dg_ref, lse_ref, cr_ref, cc_ref, dq_ref, dk_ref, dv_ref, dgate_ref, dcc_ref, dcr_ref, kb_ref, vb_ref = refs
        elif exchange is not None:
            q_ref, k_ref, v_ref, gate_ref, o_ref, dg_ref, lse_ref, sp_ref, dq_ref, dk_ref, dv_ref, dgate_ref, slots_ref, kb_ref, vb_ref, send_sems, recv_sems = refs
        else:
            q_ref, k_ref, v_ref, gate_ref, o_ref, dg_ref, lse_ref, dq_ref, dk_ref, dv_ref, dgate_ref, kb_ref, vb_ref = refs
        i = pl.program_id(1)

        if exchange is not None:
            hh = pl.program_id(0)
            send, finish = _exchange_steps(sp_ref, slots_ref, send_sems, recv_sems)
            pl.when((hh == 0) & (i == 0))(send)

        @pl.when(i == 0)
        def _():
            kb_ref[...] = k_ref[...].astype(BF16)
            vb_ref[...] = v_ref[...].astype(BF16)
            dk_ref[...] = jnp.zeros_like(dk_ref)
            dv_ref[...] = jnp.zeros_like(dv_ref)
            if bias:
                dcc_ref[...] = jnp.zeros_like(dcc_ref)

        q = (q_ref[...] * scale).astype(BF16)
        dov = _gate_grads(gate_ref, o_ref, dg_ref, dgate_ref)
        dob = dov.astype(BF16)
        dsum = jnp.sum(dov * o_ref[...], axis=1, keepdims=True)
        lsev = lse_ref[...]
        q0 = i * t
        rows, cols = _iotas(t, tk)
        crv = cr_ref[...] if bias else None

        def block(kb, dq, dr, masked):
            k0 = pl.multiple_of(kb * tk, tk)
            kk = kb_ref[pl.ds(k0, tk), :]
            sc = _dot(q, kk, "nt")
            if bias:
                sc = sc + crv - cc_ref[pl.ds(kb, 1), :]
            if masked:
                sc = jnp.where(_allowed(mode, rows, cols, q0, k0), sc, NEG)
            p = jnp.exp(sc - lsev)
            dz = p * (_dot(dob, vb_ref[pl.ds(k0, tk), :], "nt") - dsum)
            dzb = dz.astype(BF16)
            dk_ref[pl.ds(k0, tk), :] += _dot(dzb, q, "tn")
            dv_ref[pl.ds(k0, tk), :] += _dot(p.astype(BF16), dob, "tn")
            if bias:
                dcc_ref[pl.ds(kb, 1), :] -= jnp.sum(dz, axis=0, keepdims=True)
                dr = dr + jnp.sum(dz, axis=1, keepdims=True)
            return dq + _dot(dzb, kk, "nn"), dr

        def loop(j, carry):
            for u in range(un):
                carry = block(un * j + u, carry[0], carry[1], False)
            return carry

        dq, dr = lax.fori_loop(0, (i * r) // un, loop, (jnp.zeros((t, dqk), F32), jnp.zeros((t, 1), F32)))
        for dd in range(r):
            dq, dr = block(i * r + dd, dq, dr, True)
        dq_ref[...] = dq * scale
        if bias:
            dcr_ref[...] = dr
        if exchange is not None:
            pl.when((hh == n_heads - 1) & (i == nq - 1))(finish)

    h = n_heads
    qblk = pl.BlockSpec((t, dqk), lambda hh, i: (i, q_blk(hh)))
    oblk = pl.BlockSpec((t, dv), lambda hh, i: (i, hh))
    vec = pl.BlockSpec((None, t, 1), lambda hh, i: (hh, i, 0))
    in_specs = [
        qblk,
        pl.BlockSpec((s, dqk), lambda hh, i: (0, k_blk(hh))),
        pl.BlockSpec((s, dv), lambda hh, i: (0, v_blk(hh))),
        pl.BlockSpec((t, dv), lambda hh, i: (i, gate_blk(hh))),
        oblk,
        oblk,
        vec,
    ]
    args = [q_arr, k_arr, v_arr, gate_arr, o, dg, lse]
    out_specs = [
        pl.BlockSpec((t, dqk), lambda hh, i: (i, hh)),
        pl.BlockSpec((s, dqk), lambda hh, i: (0, hh)),
        pl.BlockSpec((s, dv), lambda hh, i: (0, hh)),
        oblk,
    ]
    out_shape = [
        jax.ShapeDtypeStruct((s, h * dqk), F32),
        jax.ShapeDtypeStruct((s, h * dqk), F32),
        jax.ShapeDtypeStruct((s, h * dv), F32),
        jax.ShapeDtypeStruct((s, h * dv), BF16),
    ]
    if bias:
        ccs = pl.BlockSpec((None, s // tk, tk), lambda hh, i: (hh, 0, 0))
        in_specs += [vec, ccs]
        args += [crow, ccol]
        out_specs += [ccs, vec]
        out_shape += [jax.ShapeDtypeStruct((h, s // tk, tk), F32), jax.ShapeDtypeStruct((h, s, 1), F32)]
    scratch = [pltpu.VMEM((s, dqk), BF16), pltpu.VMEM((s, dv), BF16)]
    if exchange is not None:
        in_specs.append(ANY)
        args.append(exchange)
        out_specs.append(ANY)
        out_shape.append(jax.ShapeDtypeStruct(exchange.shape, exchange.dtype))
        scratch += EXCHANGE_SEMS
    return pl.pallas_call(
        body,
        grid=(h, nq),
        in_specs=in_specs,
        out_specs=out_specs,
        out_shape=out_shape,
        scratch_shapes=scratch,
        compiler_params=_cparams(("arbitrary", "arbitrary")),
        name=name,
    )(*args)


def _rope_tables(pos):
    half = MLA_ROPE // 2
    inv_freq = ROPE_BASE ** (-jnp.arange(0, MLA_ROPE, 2, dtype=F32) / MLA_ROPE)
    ang = pos.astype(F32)[:, None] * inv_freq
    cos, sin = jnp.cos(ang), jnp.sin(ang)
    z = lambda n: jnp.zeros((pos.shape[0], n), F32)
    tc = jnp.concatenate([cos, cos, z(LANES - 2 * half)], axis=1)
    ta = jnp.concatenate([-sin, z(LANES - half)], axis=1)
    tb = jnp.concatenate([z(half), sin, z(LANES - 2 * half)], axis=1)
    return tc, ta, tb


def _rot(v, tc, ta, tb, sign):
    half = MLA_ROPE // 2
    return v * tc + sign * (pltpu.roll(v, LANES - half, 1) * ta + pltpu.roll(v, half, 1) * tb)


def _mla_assemble(qpre, kv, proj1, kr_blk, tabs, n_heads, name):
    s = qpre.shape[0]
    tr = LANES
    wd = n_heads * MLA_QK

    def body(qp_ref, kv_ref, kr_ref, tc_ref, ta_ref, tb_ref, qc_ref, kc_ref):
        tc, ta, tb = tc_ref[...], ta_ref[...], tb_ref[...]
        kr = _rot(kr_ref[...], tc, ta, tb, 1.0)
        for hh in range(n_heads):
            lo, mid, hi = hh * MLA_QK, hh * MLA_QK + LANES, (hh + 1) * MLA_QK
            qc_ref[:, lo:mid] = qp_ref[:, lo:mid]
            qc_ref[:, mid:hi] = _rot(qp_ref[:, mid:hi], tc, ta, tb, 1.0)
            kc_ref[:, lo:mid] = kv_ref[:, lo:mid]
            kc_ref[:, mid:hi] = kr

    tab = pl.BlockSpec((tr, LANES), lambda i: (i, 0))
    wide = pl.BlockSpec((tr, wd), lambda i: (i, 0))
    shp = jax.ShapeDtypeStruct((s, wd), F32)
    return pl.pallas_call(
        body,
        grid=(s // tr,),
        in_specs=[wide, wide, pl.BlockSpec((tr, LANES), lambda i: (i, kr_blk)), tab, tab, tab],
        out_specs=[wide, wide],
        out_shape=[shp, shp],
        compiler_params=_cparams(("parallel",)),
        name=name,
    )(qpre, kv, proj1, *tabs)


def _mla_disassemble(dqcat, dkcat, dv, tabs, n_heads, name):
    s = dqcat.shape[0]
    tr = LANES
    wd = n_heads * MLA_QK

    def body(dq_ref, dk_ref, dv_ref, tc_ref, ta_ref, tb_ref, dqp_ref, dkv_ref, dkr_ref):
        tc, ta, tb = tc_ref[...], ta_ref[...], tb_ref[...]
        acc = jnp.zeros((tr, LANES), F32)
        for hh in range(n_heads):
            lo, mid, hi = hh * MLA_QK, hh * MLA_QK + LANES, (hh + 1) * MLA_QK
            dqp_ref[:, lo:mid] = dq_ref[:, lo:mid].astype(BF16)
            dqp_ref[:, mid:hi] = _rot(dq_ref[:, mid:hi], tc, ta, tb, -1.0).astype(BF16)
            dkv_ref[:, lo:mid] = dk_ref[:, lo:mid].astype(BF16)
            dkv_ref[:, mid:hi] = dv_ref[:, hh * LANES : (hh + 1) * LANES].astype(BF16)
            acc = acc + dk_ref[:, mid:hi]
        dkr_ref[...] = _rot(acc, tc, ta, tb, -1.0)

    tab = pl.BlockSpec((tr, LANES), lambda i: (i, 0))
    wide = pl.BlockSpec((tr, wd), lambda i: (i, 0))
    shp = jax.ShapeDtypeStruct((s, wd), BF16)
    return pl.pallas_call(
        body,
        grid=(s // tr,),
        in_specs=[wide, wide, pl.BlockSpec((tr, n_heads * LANES), lambda i: (i, 0)), tab, tab, tab],
        out_specs=[wide, wide, tab],
        out_shape=[shp, shp, jax.ShapeDtypeStruct((s, LANES), F32)],
        compiler_params=_cparams(("parallel",)),
        name=name,
    )(dqcat, dkcat, dv, *tabs)


def _forget_scan(proj2, f_blk, bias, name):
    s = proj2.shape[0]
    n = LANES

    def body(f_ref, b_ref, c_ref):
        rows, cols = _iotas(n, n)
        tri = (cols <= rows).astype(BF16)

        def step(j, carry):
            r0 = pl.multiple_of(j * n, n)
            f = f_ref[pl.ds(r0, n), :] + b_ref[...]
            lf = jnp.minimum(f, 0.0) - jnp.log1p(jnp.exp(-jnp.abs(f)))
            a, b, c = _split3(lf)
            cs = _dot(tri, a, "nn") + _dot(tri, b, "nn") + _dot(tri, c, "nn") + carry
            c_ref[pl.ds(r0, n), :] = cs
            return cs[n - 1 : n, :]

        lax.fori_loop(0, s // n, step, jnp.zeros((1, n), F32))

    return pl.pallas_call(
        body,
        grid=(1,),
        in_specs=[pl.BlockSpec((s, n), lambda i: (0, f_blk)), pl.BlockSpec((1, n), lambda i: (0, 0))],
        out_specs=pl.BlockSpec((s, n), lambda i: (0, 0)),
        out_shape=jax.ShapeDtypeStruct((s, n), F32),
        compiler_params=_cparams(("arbitrary",)),
        name=name,
    )(proj2, bias)


def _forget_scan_bwd(dc_col, dc_row, proj2, f_blk, bias, n_heads, name):
    s = proj2.shape[0]
    n = LANES
    nb = s // n

    def body(dcc_ref, dcr_ref, f_ref, b_ref, df_ref, db_ref):
        rows, cols = _iotas(n, n)
        tri = (cols >= rows).astype(BF16)
        live = cols < n_heads

        def step(j, carry):
            acc, dbv = carry
            r0 = pl.multiple_of((nb - 1 - j) * n, n)
            a, b, c = _split3(dcc_ref[pl.ds(r0, n), :] + dcr_ref[pl.ds(r0, n), :])
            dl = _dot(tri, a, "nn") + _dot(tri, b, "nn") + _dot(tri, c, "nn") + acc
            f = f_ref[pl.ds(r0, n), :] + b_ref[...]
            df = jnp.where(live, dl / (1.0 + jnp.exp(f)), 0.0)
            df_ref[pl.ds(r0, n), :] = df.astype(BF16)
            return dl[0:1, :], dbv + jnp.sum(df, axis=0, keepdims=True)

        z = jnp.zeros((1, n), F32)
        _, dbv = lax.fori_loop(0, nb, step, (z, z))
        db_ref[...] = dbv

    return pl.pallas_call(
        body,
        grid=(1,),
        in_specs=[
            pl.BlockSpec((s, n), lambda i: (0, 0)),
            pl.BlockSpec((s, n), lambda i: (0, 0)),
            pl.BlockSpec((s, n), lambda i: (0, f_blk)),
            pl.BlockSpec((1, n), lambda i: (0, 0)),
        ],
        out_specs=[pl.BlockSpec((s, n), lambda i: (0, 0)), pl.BlockSpec((1, n), lambda i: (0, 0))],
        out_shape=[jax.ShapeDtypeStruct((s, n), BF16), jax.ShapeDtypeStruct((1, n), F32)],
        compiler_params=_cparams(("arbitrary",)),
        name=name,
    )(dc_col, dc_row, proj2, bias)


def _adamw(w, g, m, v, name):
    r, c = w.shape
    tr = _pick(r, (256, 128, 64, 32, 16, 8))
    c1 = 1.0 - ADAM_B1**ADAM_STEP
    c2 = 1.0 - ADAM_B2**ADAM_STEP

    def body(w_ref, g_ref, m_ref, v_ref, d_ref, mo_ref, vo_ref):
        gv = g_ref[...]
        mn = ADAM_B1 * m_ref[...] + (1.0 - ADAM_B1) * gv
        vn = ADAM_B2 * v_ref[...] + (1.0 - ADAM_B2) * (gv * gv)
        mo_ref[...] = mn
        vo_ref[...] = vn
        d_ref[...] = -ADAM_LR * ((mn / c1) / (jnp.sqrt(vn / c2) + ADAM_EPS) + ADAM_WD * w_ref[...])

    blk = pl.BlockSpec((tr, c), lambda i: (i, 0))
    shp = jax.ShapeDtypeStruct((r, c), F32)
    return pl.pallas_call(
        body,
        grid=(r // tr,),
        in_specs=[blk, blk, blk, blk],
        out_specs=[blk, blk, blk],
        out_shape=[shp, shp, shp],
        compiler_params=_cparams(("parallel",)),
        name=name,
    )(w, g, m, v)


def _mesh_pos():
    return lax.axis_index("x"), lax.axis_index("y"), lax.axis_index("c")


def _other_chips(x, y):
    return [(1 - x, y), (x, 1 - y), (1 - x, 1 - y)]


ANY = pl.BlockSpec(memory_space=pl.ANY)


GATHER_SEMS = [pltpu.SemaphoreType.DMA((6,)), pltpu.SemaphoreType.DMA((6,))]
EXCHANGE_SEMS = [pltpu.SemaphoreType.DMA((3,)), pltpu.SemaphoreType.DMA((3,))]


def _gather_steps(w_ref, out_ref, send_sems, recv_sems):
    half = w_ref.shape[0] // 2
    x, y, c = _mesh_pos()
    me = 2 * x + y
    chips = _other_chips(x, y)

    def region(chip, hc):
        return out_ref.at[chip, pl.ds(hc * half, half), :]

    def copy(k, src, dst, to):
        return pltpu.make_async_remote_copy(
            src_ref=src, dst_ref=dst, send_sem=send_sems.at[k], recv_sem=recv_sems.at[k], device_id=to, device_id_type=MESH
        )

    first = [copy(j, w_ref.at[pl.ds(c * half, half), :], region(me, c), (cx, cy, c)) for j, (cx, cy) in enumerate(chips)]
    passed = [copy(3 + j, region(2 * cx + cy, c), region(2 * cx + cy, c), (x, y, 1 - c)) for j, (cx, cy) in enumerate(chips)]

    def send():
        for cp in first:
            cp.start()

    def forward():
        for j, (cx, cy) in enumerate(chips):
            copy(j, region(2 * cx + cy, c), region(2 * cx + cy, c), (x, y, c)).wait_recv()
            passed[j].start()

    def finish():
        for j, (cx, cy) in enumerate(chips):
            copy(3 + j, region(2 * cx + cy, 1 - c), region(2 * cx + cy, 1 - c), (x, y, c)).wait_recv()
        for cp in first + passed:
            cp.wait_send()

    return send, forward, finish


def _gather_weights(wp, tag):
    rp, wd = wp.shape

    def body(w_ref, out_ref, send_sems, recv_sems):
        for step in _gather_steps(w_ref, out_ref, send_sems, recv_sems):
            step()

    return pl.pallas_call(
        body,
        in_specs=[ANY],
        out_specs=ANY,
        out_shape=jax.ShapeDtypeStruct((N_CHIPS, rp, wd), wp.dtype),
        scratch_shapes=GATHER_SEMS,
        name=f"gather_weights_{tag}",
    )(wp)


def _place_own(wall, wp, pos, tag):
    rp, wd = wp.shape

    def body(x_ref, y_ref, c_ref, wall_ref, w_ref, o_ref):
        o_ref[0] = w_ref[...]

    grid_spec = pltpu.PrefetchScalarGridSpec(
        num_scalar_prefetch=3,
        grid=(rp // PACK_TR,),
        in_specs=[ANY, pl.BlockSpec((PACK_TR, wd), lambda i, xr, yr, cr: (i, 0))],
        out_specs=pl.BlockSpec((1, PACK_TR, wd), lambda i, xr, yr, cr: (2 * xr[0] + yr[0], i, 0)),
    )
    return pl.pallas_call(
        body,
        grid_spec=grid_spec,
        out_shape=jax.ShapeDtypeStruct(wall.shape, wall.dtype),
        input_output_aliases={3: 0},
        compiler_params=_cparams(("parallel",)),
        name=f"place_own_shard_{tag}",
    )(*pos, wall, wp)


def _pair_exchange(g, tag):
    _, rp, wd = g.shape
    half = rp // 2

    def body(g_ref, out_ref, send_sem, recv_sem):
        x, y, c = _mesh_pos()
        cp = pltpu.make_async_remote_copy(
            src_ref=g_ref.at[:, pl.ds((1 - c) * half, half), :],
            dst_ref=out_ref,
            send_sem=send_sem,
            recv_sem=recv_sem,
            device_id=(x, y, 1 - c),
            device_id_type=MESH,
        )
        cp.start()
        cp.wait()

    return pl.pallas_call(
        body,
        in_specs=[ANY],
        out_specs=ANY,
        out_shape=jax.ShapeDtypeStruct((N_CHIPS, half, wd), g.dtype),
        scratch_shapes=[pltpu.SemaphoreType.DMA, pltpu.SemaphoreType.DMA],
        name=f"rs_pair_exchange_{tag}",
    )(g)


def _pair_add(g, recv, pos, tag):
    _, rp, wd = g.shape
    half = rp // 2
    nb = half // PACK_TR

    def body(x_ref, y_ref, c_ref, g_ref, r_ref, o_ref):
        o_ref[...] = (g_ref[...] + r_ref[...]).astype(BF16)

    blk = (1, PACK_TR, wd)
    grid_spec = pltpu.PrefetchScalarGridSpec(
        num_scalar_prefetch=3,
        grid=(N_CHIPS, nb),
        in_specs=[
            pl.BlockSpec(blk, lambda j, i, xr, yr, cr: (j, cr[0] * nb + i, 0)),
            pl.BlockSpec(blk, lambda j, i, xr, yr, cr: (j, i, 0)),
        ],
        out_specs=pl.BlockSpec(blk, lambda j, i, xr, yr, cr: (j, i, 0)),
    )
    return pl.pallas_call(
        body,
        grid_spec=grid_spec,
        out_shape=jax.ShapeDtypeStruct((N_CHIPS, half, wd), BF16),
        compiler_params=_cparams(("parallel", "parallel")),
        name=f"rs_pair_add_{tag}",
    )(*pos, g, recv)


def _exchange_steps(s_ref, out_ref, send_sems, recv_sems):
    x, y, c = _mesh_pos()
    me = 2 * x + y
    chips = _other_chips(x, y)

    def copy(j, src, dst, to):
        return pltpu.make_async_remote_copy(
            src_ref=src, dst_ref=dst, send_sem=send_sems.at[j], recv_sem=recv_sems.at[j], device_id=to, device_id_type=MESH
        )

    sends = [copy(j, s_ref.at[2 * cx + cy], out_ref.at[me], (cx, cy, c)) for j, (cx, cy) in enumerate(chips)]

    def send():
        for cp in sends:
            cp.start()

    def finish():
        for j, (cx, cy) in enumerate(chips):
            copy(j, s_ref.at[me], out_ref.at[2 * cx + cy], (x, y, c)).wait_recv()
        for cp in sends:
            cp.wait_send()

    return send, finish


def _chip_exchange(sp, tag):
    def body(s_ref, out_ref, send_sems, recv_sems):
        for step in _exchange_steps(s_ref, out_ref, send_sems, recv_sems):
            step()

    return pl.pallas_call(
        body,
        in_specs=[ANY],
        out_specs=ANY,
        out_shape=jax.ShapeDtypeStruct(sp.shape, sp.dtype),
        scratch_shapes=EXCHANGE_SEMS,
        name=f"rs_chip_exchange_{tag}",
    )(sp)


def _sum_slots(own, slots, pos, tag):
    _, rh, wd = slots.shape
    nb = rh // PACK_TR

    def body(x_ref, y_ref, c_ref, own_ref, a_ref, b_ref, d_ref, o_ref):
        f = lambda r: r[0].astype(F32)
        o_ref[...] = ((f(own_ref) + f(a_ref)) + f(b_ref)) + f(d_ref)

    blk = (1, PACK_TR, wd)

    def other(k):
        return pl.BlockSpec(blk, lambda i, xr, yr, cr: (k + (k >= 2 * xr[0] + yr[0]).astype(jnp.int32), i, 0))

    grid_spec = pltpu.PrefetchScalarGridSpec(
        num_scalar_prefetch=3,
        grid=(nb,),
        in_specs=[pl.BlockSpec(blk, lambda i, xr, yr, cr: (2 * xr[0] + yr[0], i, 0)), other(0), other(1), other(2)],
        out_specs=pl.BlockSpec((PACK_TR, wd), lambda i, xr, yr, cr: (cr[0] * nb + i, 0)),
    )
    return pl.pallas_call(
        body,
        grid_spec=grid_spec,
        out_shape=jax.ShapeDtypeStruct((2 * rh, wd), F32),
        compiler_params=_cparams(("parallel",)),
        name=f"rs_sum_slots_{tag}",
    )(*pos, own, slots, slots, slots)


def _pair_gather(t, tag):
    rh = t.shape[0] // 2

    def body(t_ref, out_ref, send_sem, recv_sem):
        x, y, c = _mesh_pos()
        cp = pltpu.make_async_remote_copy(
            src_ref=t_ref.at[pl.ds(c * rh, rh), :],
            dst_ref=out_ref.at[pl.ds(c * rh, rh), :],
            send_sem=send_sem,
            recv_sem=recv_sem,
            device_id=(x, y, 1 - c),
            device_id_type=MESH,
        )
        cp.start()
        cp.wait_send()
        pltpu.make_async_remote_copy(
            src_ref=t_ref.at[pl.ds((1 - c) * rh, rh), :],
            dst_ref=out_ref.at[pl.ds((1 - c) * rh, rh), :],
            send_sem=send_sem,
            recv_sem=recv_sem,
            device_id=(x, y, c),
            device_id_type=MESH,
        ).wait_recv()

    return pl.pallas_call(
        body,
        in_specs=[ANY],
        out_specs=ANY,
        out_shape=jax.ShapeDtypeStruct(t.shape, t.dtype),
        input_output_aliases={0: 0},
        scratch_shapes=[pltpu.SemaphoreType.DMA, pltpu.SemaphoreType.DMA],
        name=f"rs_pair_gather_{tag}",
    )(t)


def _allreduce_small(v):
    shape = v.shape
    n_dev = 8

    def body(v_ref, o_ref, slots, send_sems, recv_sems):
        x, y, c = _mesh_pos()
        me = 4 * x + 2 * y + c
        slots[me] = v_ref[...]
        sends = []
        for k in range(1, n_dev):
            fx, fy, fc = (k >> 2) & 1, (k >> 1) & 1, k & 1
            to = (x ^ fx, y ^ fy, c ^ fc)
            cp = pltpu.make_async_remote_copy(
                src_ref=v_ref,
                dst_ref=slots.at[me],
                send_sem=send_sems.at[k - 1],
                recv_sem=recv_sems.at[k - 1],
                device_id=to,
                device_id_type=MESH,
            )
            cp.start()
            sends.append(cp)
        for k in range(1, n_dev):
            fx, fy, fc = (k >> 2) & 1, (k >> 1) & 1, k & 1
            frm = 4 * (x ^ fx) + 2 * (y ^ fy) + (c ^ fc)
            pltpu.make_async_remote_copy(
                src_ref=v_ref,
                dst_ref=slots.at[frm],
                send_sem=send_sems.at[k - 1],
                recv_sem=recv_sems.at[k - 1],
                device_id=(x, y, c),
                device_id_type=MESH,
            ).wait_recv()
        for cp in sends:
            cp.wait_send()
        acc = slots[0]
        for k in range(1, n_dev):
            acc = acc + slots[k]
        o_ref[...] = acc

    vm = pl.BlockSpec(memory_space=pltpu.VMEM)
    return pl.pallas_call(
        body,
        in_specs=[vm],
        out_specs=vm,
        out_shape=jax.ShapeDtypeStruct(shape, F32),
        scratch_shapes=[pltpu.VMEM((n_dev,) + shape, F32), pltpu.SemaphoreType.DMA((n_dev - 1,)), pltpu.SemaphoreType.DMA((n_dev - 1,))],
        name="allreduce_small",
    )(v)


def _pack_layout(shard_shapes):
    offs, rows = [], []
    off = 0
    for r, c in shard_shapes:
        assert (r * c) % PACK_W == 0
        n = r * c // PACK_W
        offs.append(off)
        rows.append(n)
        off += -(-n // 16) * 16
    rp = -(-off // (2 * PACK_TR)) * (2 * PACK_TR)
    return offs, rows, rp


def _pack_rows(parts, offs, rows, rp, lead):
    ends = list(offs[1:]) + [rp]
    nolead = ((0, 0),) * len(lead)
    out = [jnp.pad(p, nolead + ((0, e - o - n), (0, 0))) for p, o, n, e in zip(parts, offs, rows, ends)]
    return jnp.concatenate(out, axis=len(lead))


def kernel(x, positions, ln0, w_in0, w_out0, ln1, w_in1, q_norm1, w_qb1, kv_norm1, w_kvb1, w_out1, ln2, w_in2, b_f2, w_out2, ln3, w_in3, w_out3, final_norm, loss_target, m_ln0, m_w_in0, m_w_out0, m_ln1, m_w_in1, m_q_norm1, m_w_qb1, m_kv_norm1, m_w_kvb1, m_w_out1, m_ln2, m_w_in2, m_b_f2, m_w_out2, m_ln3, m_w_in3, m_w_out3, m_final_norm, v_ln0, v_w_in0, v_w_out0, v_ln1, v_w_in1, v_q_norm1, v_w_qb1, v_kv_norm1, v_w_kvb1, v_w_out1, v_ln2, v_w_in2, v_b_f2, v_w_out2, v_ln3, v_w_in3, v_w_out3, v_final_norm):
    xs = x[0]
    s, d = xs.shape
    di = 4 * w_out0.shape[0]
    nh = di // HEAD_DIM
    idx = tuple(lax.axis_index(a).astype(jnp.int32).reshape(1) for a in ("x", "y", "c"))

    big = [w_in0, w_out0, w_in1, w_qb1, w_kvb1, w_out1, w_in2, w_out2, w_in3, w_out3]
    col_sharded = [True, False, True, True, True, False, True, False, True, False]
    shard_shapes = [w.shape for w in big]
    as_gathered = [True, False, False, False, False, False, False, False, True, False]
    n_first = 1

    def pack_weights(lo, hi):
        offs, rows, rp = _pack_layout(shard_shapes[lo:hi])
        return _pack_rows([w.astype(BF16).reshape(n, PACK_W) for w, n in zip(big[lo:hi], rows)], offs, rows, rp, ()), offs, rows

    def unpack_weights(wall, lo, hi, offs, rows):
        out = []
        for (r, c), o, n, cs, g4 in zip(shard_shapes[lo:hi], offs, rows, col_sharded[lo:hi], as_gathered[lo:hi]):
            slab = wall[:, o : o + n, :].reshape(N_CHIPS, r, c)
            if g4:
                out.append(slab)
            else:
                out.append(slab.transpose(1, 0, 2).reshape(r, N_CHIPS * c) if cs else slab.reshape(N_CHIPS * r, c))
        return out

    wp_a, offs_a, rows_a = pack_weights(0, n_first)
    wp_b, offs_b, rows_b = pack_weights(n_first, len(big))
    wall_a = _place_own(_gather_weights(wp_a, "l0"), wp_a, idx, "l0")
    (f_in0,) = unpack_weights(wall_a, 0, n_first, offs_a, rows_a)

    row = lambda v: v.reshape(1, -1)

    def sb_layer_attend(xin, ln, w_in, tag, gather=None):
        h = _rmsnorm_fwd(xin, row(ln), f"norm_fwd_{tag}")
        proj = _proj_w4(h, w_in, f"proj_in_{tag}")
        o, lt, g, *gathered = _sb_fwd(proj, nh, f"sb_fwd_{tag}", gather=gather)
        return (xin, h, proj, o, lt, g), gathered

    def sb_layer_out(saved, w_out, tag):
        return _matmul(saved[-1], w_out, "nn", f"proj_out_{tag}", res=saved[0])

    sv0, (wall_b_raw,) = sb_layer_attend(xs, ln0, f_in0, "l0", gather=wp_b)
    wall_b = _place_own(wall_b_raw, wp_b, idx, "rest")
    f_out0, f_in1, f_qb1, f_kvb1, f_out1, f_in2, f_out2, f_in3, f_out3 = unpack_weights(wall_b, n_first, len(big), offs_b, rows_b)
    x1 = sb_layer_out(sv0, f_out0, "l0")

    i_kr = MLA_Q_RANK + MLA_KV_RANK + MLA_ROPE
    w1p = jnp.concatenate([f_in1[:, i_kr:], f_in1[:, :i_kr], jnp.zeros((d, LANES - MLA_ROPE), BF16)], axis=1)
    qlat_blk = di // MLA_Q_RANK
    kvlat_blk = (di + MLA_Q_RANK) // MLA_KV_RANK
    kr_blk = (di + MLA_Q_RANK + MLA_KV_RANK) // LANES
    qk_w = HEAD_DIM + MLA_ROPE
    wqbp = jnp.pad(f_qb1.reshape(MLA_Q_RANK, nh, qk_w), ((0, 0), (0, 0), (0, MLA_QK - qk_w))).reshape(MLA_Q_RANK, nh * MLA_QK)
    n2 = f_in2.shape[1]
    w2p = jnp.pad(f_in2, ((0, 0), (0, 4 * di + LANES - n2)))
    b2p = jnp.pad(b_f2, (0, LANES - nh)).reshape(1, LANES)

    tabs = _rope_tables(positions[0])

    def sb_layer_bwd(dxn, dxnb, saved, ln, w_in, w_out, tag, early=None):
        xin, h, proj, o, lt, g = saved
        dgf = _matmul(dxnb, w_out, "nt", f"dgate_in_{tag}")
        dw_out = _matmul(g, dxnb, "tn", f"dw_out_{tag}")
        pair = None if early is None else early(dw_out)
        dq, dk, dv, dgate, *slots = _sb_bwd(proj, lt, o, dgf, nh, f"sb_bwd_{tag}", exchange=pair)
        dproj = [dq, dk, dv, dgate]
        dh = _dh_w4(dproj, w_in, f"dh_{tag}")
        dw_in = _dw_w4(h, dproj, f"dw_in_{tag}")
        dx, dxb, dln = _rmsnorm_bwd(xin, row(ln), dh, f"norm_bwd_{tag}", dres=dxn)
        return dx, dxb, dln, dw_in, dw_out, (pair, slots)

    h1 = _rmsnorm_fwd(x1, row(ln1), "norm_fwd_l1")
    proj1 = _matmul(h1, w1p, "nn", "proj_in_l1")
    qn = _rmsnorm_fwd(proj1, row(q_norm1), "qnorm_fwd_l1", col_block=qlat_blk)
    kvn = _rmsnorm_fwd(proj1, row(kv_norm1), "kvnorm_fwd_l1", col_block=kvlat_blk)
    qpre = _matmul(qn, wqbp, "nn", "q_up_l1")
    kv1 = _matmul(kvn, f_kvb1, "nn", "kv_up_l1")
    qcat, kcat = _mla_assemble(qpre, kv1, proj1, kr_blk, tabs, nh, "mla_assemble_l1")
    mla_blk = (lambda hh: hh, lambda hh: hh, lambda hh: 2 * hh + 1)
    gate1_blk = lambda hh: hh
    o1, lse1, g1 = _sm_fwd(qcat, kcat, kv1, proj1, nh, MLA_QK, *mla_blk, gate1_blk, "chunk", "mla_fwd_l1")
    x2 = _matmul(g1, f_out1, "nn", "proj_out_l1", res=x1)

    h2 = _rmsnorm_fwd(x2, row(ln2), "norm_fwd_l2")
    proj2 = _matmul(h2, w2p, "nn", "proj_in_l2")
    f_blk = 4 * di // LANES
    cum = _forget_scan(proj2, f_blk, b2p, "forget_scan_l2")
    cum_h = cum[:, :nh].T
    t_sm = min(SM_TK, s)
    crow = cum_h.reshape(nh, s, 1)
    ccol = cum_h.reshape(nh, s // t_sm, t_sm)
    fg_blk = (lambda hh: hh, lambda hh: nh + hh, lambda hh: 2 * nh + hh)
    gate2_blk = lambda hh: 3 * nh + hh
    o2, lse2, g2 = _sm_fwd(proj2, proj2, proj2, proj2, nh, HEAD_DIM, *fg_blk, gate2_blk, "causal", "forget_fwd_l2", crow=crow, ccol=ccol)
    x3 = _matmul(g2, f_out2, "nn", "proj_out_l2", res=x2)

    sv3, _ = sb_layer_attend(x3, ln3, f_in3, "l3")
    x4 = sb_layer_out(sv3, f_out3, "l3")

    dx, dxb, d_final, loss_part = _loss_head(x4, row(final_norm), loss_target[0], "loss_head")
    dx, dxb, d_ln3, dw_in3, dw_out3, _ = sb_layer_bwd(dx, dxb, sv3, ln3, f_in3, f_out3, "l3")

    dgf2 = _matmul(dxb, f_out2, "nt", "dgate_in_l2")
    dw_out2 = _matmul(g2, dxb, "tn", "dw_out_l2")
    dq2, dk2, dv2, dgate2, dcc2, dcr2 = _sm_bwd(proj2, proj2, proj2, proj2, o2, dgf2, lse2, nh, HEAD_DIM, *fg_blk, gate2_blk, "causal", "forget_bwd_l2", crow=crow, ccol=ccol)
    lanes_of = lambda a: jnp.pad(a.reshape(nh, s).T, ((0, 0), (0, LANES - nh)))
    df2, d_bf = _forget_scan_bwd(lanes_of(dcc2), lanes_of(dcr2), proj2, f_blk, b2p, nh, "forget_scan_bwd_l2")
    dproj2 = jnp.concatenate([dq2.astype(BF16), dk2.astype(BF16), dv2.astype(BF16), dgate2, df2], axis=1)
    dh2 = _matmul(dproj2, w2p, "nt", "dh_l2")
    dw_in2 = _matmul(h2, dproj2, "tn", "dw_in_l2")[:, :n2]
    dx, dxb, d_ln2 = _rmsnorm_bwd(x2, row(ln2), dh2, "norm_bwd_l2", dres=dx)

    n_late = 6

    def reduce_start(ids, dws, tag):
        offs, rows, rp = _pack_layout([shard_shapes[k] for k in ids])
        parts = []
        for g, k, n in zip(dws, ids, rows):
            r, c = shard_shapes[k]
            g4 = g if as_gathered[k] else (g.reshape(r, N_CHIPS, c).transpose(1, 0, 2) if col_sharded[k] else g.reshape(N_CHIPS, r, c))
            parts.append(g4.reshape(N_CHIPS, n, PACK_W))
        gp = _pack_rows(parts, offs, rows, rp, (N_CHIPS,))
        return _pair_add(gp, _pair_exchange(gp, tag), idx, tag), offs, rows

    def reduce_finish(pair, slots, ids, offs, rows, tag):
        gred = _pair_gather(_sum_slots(pair, slots, idx, tag), tag)
        return {k: gred[o : o + n, :].reshape(shard_shapes[k]) for k, o, n in zip(ids, offs, rows)}

    ids_late = list(range(n_late, len(big)))
    pair_b, offs_gb, rows_gb = reduce_start(ids_late, [dw_in2, dw_out2, dw_in3, dw_out3], "l23")

    dgf1 = _matmul(dxb, f_out1, "nt", "dgate_in_l1")
    dw_out1 = _matmul(g1, dxb, "tn", "dw_out_l1")
    dqc, dkc, dv1, dgate1, slots_b = _sm_bwd(qcat, kcat, kv1, proj1, o1, dgf1, lse1, nh, MLA_QK, *mla_blk, gate1_blk, "chunk", "mla_bwd_l1", exchange=pair_b)
    dqpre, dkv1, dkr = _mla_disassemble(dqc, dkc, dv1, tabs, nh, "mla_disassemble_l1")
    dqn = _matmul(dqpre, wqbp, "nt", "dqn_l1")
    dw_qbp = _matmul(qn, dqpre, "tn", "dw_qb_l1")
    dw_qb1 = dw_qbp.reshape(MLA_Q_RANK, nh, MLA_QK)[:, :, :qk_w].reshape(MLA_Q_RANK, nh * qk_w)
    dkvn = _matmul(dkv1, f_kvb1, "nt", "dkvn_l1")
    dw_kvb1 = _matmul(kvn, dkv1, "tn", "dw_kvb_l1")
    _, dqlat_b, d_qnorm = _rmsnorm_bwd(proj1, row(q_norm1), dqn, "qnorm_bwd_l1", col_block=qlat_blk)
    _, dkvlat_b, d_kvnorm = _rmsnorm_bwd(proj1, row(kv_norm1), dkvn, "kvnorm_bwd_l1", col_block=kvlat_blk)
    dproj1 = jnp.concatenate([dgate1, dqlat_b, dkvlat_b, dkr.astype(BF16)], axis=1)
    dh1 = _matmul(dproj1, w1p, "nt", "dh_l1")
    dw1p = _matmul(h1, dproj1, "tn", "dw_in_l1")
    dw_in1 = jnp.concatenate([dw1p[:, di : di + i_kr], dw1p[:, :di]], axis=1)
    dx, dxb, d_ln1 = _rmsnorm_bwd(x1, row(ln1), dh1, "norm_bwd_l1", dres=dx)

    ids_mid = list(range(1, n_late))
    lay_mid = {}

    def early(dw_out0):
        pair, lay_mid["offs"], lay_mid["rows"] = reduce_start(ids_mid, [dw_out0, dw_in1, dw_qb1, dw_kvb1, dw_out1], "mid")
        return pair

    dx, dxb, d_ln0, dw_in0, dw_out0, (pair_m, (slots_m,)) = sb_layer_bwd(dx, dxb, sv0, ln0, f_in0, f_out0, "l0", early=early)
    grad_x = dx.reshape(x.shape)

    pair_a, offs_ga, rows_ga = reduce_start([0], [dw_in0], "l0")
    slots_a = _chip_exchange(pair_a, "l0")
    reduced = reduce_finish(pair_a, slots_a, [0], offs_ga, rows_ga, "l0")
    reduced.update(reduce_finish(pair_m, slots_m, ids_mid, lay_mid["offs"], lay_mid["rows"], "mid"))
    reduced.update(reduce_finish(pair_b, slots_b, ids_late, offs_gb, rows_gb, "l23"))
    big_grads = [reduced[k] for k in range(len(big))]

    small = [ln0, ln1, q_norm1, kv_norm1, ln2, b_f2, ln3, final_norm]
    small_g = [d_ln0[0], d_ln1[0], d_qnorm[0], d_kvnorm[0], d_ln2[0], d_bf[0, :nh], d_ln3[0], d_final[0]]
    n_small = SMALL_SHAPE[0] * SMALL_SHAPE[1]
    used = sum(v.shape[0] for v in small) + 1
    assert used <= n_small

    def pack_small(vs, last):
        return jnp.concatenate(list(vs) + [last, jnp.zeros((n_small - used,), F32)]).reshape(SMALL_SHAPE)

    sm_sum = _allreduce_small(pack_small(small_g, loss_part[0, :1]))
    flat = sm_sum.reshape(-1)
    loss = flat[used - 1]

    big_m = [m_w_in0, m_w_out0, m_w_in1, m_w_qb1, m_w_kvb1, m_w_out1, m_w_in2, m_w_out2, m_w_in3, m_w_out3]
    big_v = [v_w_in0, v_w_out0, v_w_in1, v_w_qb1, v_w_kvb1, v_w_out1, v_w_in2, v_w_out2, v_w_in3, v_w_out3]
    big_names = ["w_in0", "w_out0", "w_in1", "w_qb1", "w_kvb1", "w_out1", "w_in2", "w_out2", "w_in3", "w_out3"]
    big_upd = [_adamw(w, g, m, v, f"adamw_{nm}") for w, g, m, v, nm in zip(big, big_grads, big_m, big_v, big_names)]

    small_m = [m_ln0, m_ln1, m_q_norm1, m_kv_norm1, m_ln2, m_b_f2, m_ln3, m_final_norm]
    small_v = [v_ln0, v_ln1, v_q_norm1, v_kv_norm1, v_ln2, v_b_f2, v_ln3, v_final_norm]
    one = jnp.ones((1,), F32)
    sd, smn, svn = _adamw(pack_small(small, one), sm_sum, pack_small(small_m, one), pack_small(small_v, one), "adamw_small")

    def unpack_small(p):
        out, at = [], 0
        fl = p.reshape(-1)
        for v in small:
            out.append(fl[at : at + v.shape[0]])
            at += v.shape[0]
        return out

    sg_l, sd_l, sm_l, sv_l = unpack_small(sm_sum), unpack_small(sd), unpack_small(smn), unpack_small(svn)

    order = ["ln0", "w_in0", "w_out0", "ln1", "w_in1", "q_norm1", "w_qb1", "kv_norm1", "w_kvb1", "w_out1", "ln2", "w_in2", "b_f2", "w_out2", "ln3", "w_in3", "w_out3", "final_norm"]
    small_names = ["ln0", "ln1", "q_norm1", "kv_norm1", "ln2", "b_f2", "ln3", "final_norm"]
    grads, deltas, new_m, new_v = {}, {}, {}, {}
    for nm, g, (dl, mn, vn) in zip(big_names, big_grads, big_upd):
        grads[nm], deltas[nm], new_m[nm], new_v[nm] = g, dl, mn, vn
    for nm, g, dl, mn, vn in zip(small_names, sg_l, sd_l, sm_l, sv_l):
        grads[nm], deltas[nm], new_m[nm], new_v[nm] = g, dl, mn, vn
    return (loss, grad_x, *[grads[n] for n in order], *[deltas[n] for n in order], *[new_m[n] for n in order], *[new_v[n] for n in order])
```
